```python
import math
import jax, jax.numpy as jnp
from jax import lax
import numpy as np

D_MODEL = 1024
BATCH = 8
SEQ = 2048
DEPTH = 1

HEAD_DIM = 64
N_HEADS_SB = 8
N_HEADS_MOBA = 8
D_SB = N_HEADS_SB * HEAD_DIM
D_MOBA = N_HEADS_MOBA * HEAD_DIM
SB_Q_BLOCK = 128
MOBA_BLOCK = 256
MOBA_TOPK = 3
MOBA_Q_CHUNK = 16
REL_BUCKETS = 32
REL_MAX_DIST = 128
N_EXPERTS = 32
TOP_K = 4
D_EXPERT = D_MODEL
SWIGLU_LIMIT = 7.0
SWIGLU_ALPHA = 1.702
RMS_EPS = 1e-6
IN_SPLIT_POINTS = (D_SB, 2 * D_SB, 3 * D_SB, 3 * D_SB + D_MOBA, 3 * D_SB + 2 * D_MOBA, 3 * D_SB + 3 * D_MOBA, 3 * D_SB + 3 * D_MOBA + D_MODEL)
D_IN_PROJ = 3 * D_SB + 3 * D_MOBA + 2 * D_MODEL

kernel_name = "hybrid_stickbreak_moba_moe_block"


def rmsnorm(x, g):
    xf = x.astype(jnp.float32)
    var = jnp.mean(xf * xf, axis=-1, keepdims=True)
    return (xf * lax.rsqrt(var + RMS_EPS)).astype(x.dtype) * g


def split_heads(t, n_heads):
    B, S, _ = t.shape
    return t.reshape(B, S, n_heads, HEAD_DIM).transpose(0, 2, 1, 3)


def merge_heads(t):
    B, H, S, dh = t.shape
    return t.transpose(0, 2, 1, 3).reshape(B, S, H * dh)


def rel_bucket(dist):
    n = jnp.maximum(dist, 0)
    max_exact = REL_BUCKETS // 2
    nf = jnp.maximum(n, 1).astype(jnp.float32)
    large = max_exact + (jnp.log(nf / max_exact) / math.log(REL_MAX_DIST / max_exact)
                         * (REL_BUCKETS - max_exact)).astype(jnp.int32)
    large = jnp.minimum(large, REL_BUCKETS - 1)
    return jnp.where(n < max_exact, n, large)


def stick_breaking_attention(q, k, v):
    B, H, S, dh = q.shape
    nq = S // SB_Q_BLOCK
    qb = q.reshape(B, H, nq, SB_Q_BLOCK, dh).transpose(2, 0, 1, 3, 4)
    key_pos = jnp.arange(S)
    scale = 1.0 / math.sqrt(dh)

    def block(args):
        i, qi = args
        q_pos = i * SB_Q_BLOCK + jnp.arange(SB_Q_BLOCK)
        z = jnp.einsum('bhqd,bhkd->bhqk', qi, k).astype(jnp.float32) * scale
        causal = key_pos[None, :] < q_pos[:, None]
        log_keep = jnp.where(causal, -jax.nn.softplus(z), 0.0)
        tail = lax.cumsum(log_keep, axis=3, reverse=True) - log_keep
        a = jnp.where(causal, jnp.exp(jax.nn.log_sigmoid(z) + tail), 0.0)
        return jnp.einsum('bhqk,bhkd->bhqd', a.astype(v.dtype), v)

    out = lax.map(block, (jnp.arange(nq), qb))
    return out.transpose(1, 2, 0, 3, 4).reshape(B, H, S, dh)


_gather_blocks = jax.vmap(jax.vmap(lambda blk, idx: blk[idx]))


def moba_attention(q, k, v, rel_bias):
    B, H, S, dh = q.shape
    L = MOBA_BLOCK
    n_blk = -(-S // L)
    pad = n_blk * L - S
    kp = jnp.pad(k, ((0, 0), (0, 0), (0, pad), (0, 0)))
    vp = jnp.pad(v, ((0, 0), (0, 0), (0, pad), (0, 0)))
    k_blocks = kp.reshape(B, H, n_blk, L, dh)
    v_blocks = vp.reshape(B, H, n_blk, L, dh)
    k_mean = jnp.mean(k_blocks.astype(jnp.float32), axis=3)
    k_sel = min(MOBA_TOPK, n_blk)
    scale = 1.0 / math.sqrt(dh)
    C = MOBA_Q_CHUNK
    nq = S // C
    qc = q.reshape(B, H, nq, C, dh).transpose(2, 0, 1, 3, 4)
    offs = jnp.arange(L)
    rel_t = rel_bias.T
    head_idx = jnp.arange(H)[:, None, None, None]

    def chunk(args):
        i, qi = args
        q_pos = i * C + jnp.arange(C)
        own_blk = (i * C) // L
        gs = jnp.einsum('bhqd,bhnd->bhqn', qi.astype(jnp.float32), k_mean)
        past = jnp.arange(n_blk) < own_blk
        gs = jnp.where(past[None, None, None, :], gs, -jnp.inf)
        _, top = lax.top_k(gs, k_sel)
        sel_valid = top < own_blk
        kg = _gather_blocks(k_blocks, top)
        vg = _gather_blocks(v_blocks, top)
        s_past = jnp.einsum('bhqd,bhqmld->bhqml', qi, kg).astype(jnp.float32) * scale
        dist_past = q_pos[None, None, :, None, None] - (top[..., None] * L + offs)
        bias_past = rel_t[head_idx, rel_bucket(dist_past)]
        s_past = jnp.where(sel_valid[..., None], s_past + bias_past, -jnp.inf)
        k_own = lax.dynamic_slice_in_dim(k_blocks, own_blk, 1, axis=2)[:, :, 0]
        v_own = lax.dynamic_slice_in_dim(v_blocks, own_blk, 1, axis=2)[:, :, 0]
        s_own = jnp.einsum('bhqd,bhld->bhql', qi, k_own).astype(jnp.float32) * scale
        dist_own = q_pos[:, None] - (own_blk * L + offs)[None, :]
        bias_own = rel_t[:, rel_bucket(dist_own)]
        s_own = jnp.where((dist_own >= 0)[None, None], s_own + bias_own[None], -jnp.inf)
        scores = jnp.concatenate([s_past.reshape(B, H, C, k_sel * L), s_own], axis=-1)
        p = jax.nn.softmax(scores, axis=-1).astype(v.dtype)
        p_past = p[..., :k_sel * L].reshape(B, H, C, k_sel, L)
        p_own = p[..., k_sel * L:]
        return (jnp.einsum('bhqml,bhqmld->bhqd', p_past, vg)
                + jnp.einsum('bhql,bhld->bhqd', p_own, v_own))

    out = lax.map(chunk, (jnp.arange(nq), qc))
    return out.transpose(1, 2, 0, 3, 4).reshape(B, H, S, dh)


def moe_ffn(x, w_router, b_router, w_gate_up, b_gate_up, w_down, b_down):
    B, S, D = x.shape
    xt = x.reshape(B * S, D)
    logits = (xt @ w_router + b_router).astype(jnp.float32)
    top_val, top_idx = lax.top_k(logits, TOP_K)
    top_w = jax.nn.softmax(top_val, axis=-1)
    gates = jnp.sum(jax.nn.one_hot(top_idx, N_EXPERTS, dtype=jnp.float32) * top_w[..., None], axis=1)

    def expert_step(acc, params):
        wgu, bgu, wd, bd, g = params
        hgu = xt @ wgu + bgu
        gate = jnp.minimum(hgu[:, :D_EXPERT], SWIGLU_LIMIT)
        up = jnp.clip(hgu[:, D_EXPERT:], -SWIGLU_LIMIT, SWIGLU_LIMIT)
        y = ((up + 1.0) * (gate * jax.nn.sigmoid(SWIGLU_ALPHA * gate))) @ wd + bd
        return acc + g[:, None].astype(y.dtype) * y, None

    acc, _ = lax.scan(expert_step, jnp.zeros_like(xt), (w_gate_up, b_gate_up, w_down, b_down, gates.T))
    return acc.reshape(B, S, D)


def setup_inputs(seed: int = 0) -> dict:
    key = jax.random.key(seed)
    ks = jax.random.split(key, 15)
    f32 = jnp.float32
    nrm = lambda k, shape, s: jax.random.normal(k, shape, f32) * s
    return {
        "x": nrm(ks[0], (BATCH, SEQ, D_MODEL), 1.0),
        "norm_mix_g": 1.0 + nrm(ks[1], (DEPTH, D_MODEL), 0.02),
        "w_in": nrm(ks[2], (DEPTH, D_MODEL, D_IN_PROJ), D_MODEL ** -0.5),
        "w_proj_sb": nrm(ks[3], (DEPTH, D_SB, D_MODEL), D_SB ** -0.5),
        "w_proj_moba": nrm(ks[4], (DEPTH, D_MOBA, D_MODEL), D_MOBA ** -0.5),
        "w_out": nrm(ks[5], (DEPTH, D_MODEL, D_MODEL), D_MODEL ** -0.5),
        "rel_bias": nrm(ks[6], (REL_BUCKETS, N_HEADS_MOBA), 0.5),
        "norm_ffn_g": 1.0 + nrm(ks[7], (DEPTH, D_MODEL), 0.02),
        "w_router": nrm(ks[8], (DEPTH, D_MODEL, N_EXPERTS), D_MODEL ** -0.5),
        "b_router": nrm(ks[9], (DEPTH, N_EXPERTS), 0.01),
        "w_gate_up": nrm(ks[10], (DEPTH, N_EXPERTS, D_MODEL, 2 * D_EXPERT), D_MODEL ** -0.5),
        "b_gate_up": nrm(ks[11], (DEPTH, N_EXPERTS, 2 * D_EXPERT), 0.02),
        "w_down": nrm(ks[12], (DEPTH, N_EXPERTS, D_EXPERT, D_MODEL), D_EXPERT ** -0.5),
        "b_down": nrm(ks[13], (DEPTH, N_EXPERTS, D_MODEL), 0.02),
        "norm_final_g": 1.0 + nrm(ks[14], (D_MODEL,), 0.02),
    }


def reference(x, norm_mix_g, w_in, w_proj_sb, w_proj_moba, w_out, rel_bias, norm_ffn_g,
              w_router, b_router, w_gate_up, b_gate_up, w_down, b_down, norm_final_g):
    h = x
    for l in range(DEPTH):
        xn = rmsnorm(h, norm_mix_g[l])
        proj = xn @ w_in[l]
        q_sb, k_sb, v_sb, q_mb, k_mb, v_mb, g_sb, g_mb = jnp.split(proj, IN_SPLIT_POINTS, axis=-1)
        y_sb = stick_breaking_attention(split_heads(q_sb, N_HEADS_SB), split_heads(k_sb, N_HEADS_SB),
                                        split_heads(v_sb, N_HEADS_SB))
        y_mb = moba_attention(split_heads(q_mb, N_HEADS_MOBA), split_heads(k_mb, N_HEADS_MOBA),
                              split_heads(v_mb, N_HEADS_MOBA), rel_bias)
        merged = (jax.nn.sigmoid(g_sb) * (merge_heads(y_sb) @ w_proj_sb[l])
                  + jax.nn.sigmoid(g_mb) * (merge_heads(y_mb) @ w_proj_moba[l]))
        h = h + merged @ w_out[l]
        hn = rmsnorm(h, norm_ffn_g[l])
        h = h + moe_ffn(hn, w_router[l], b_router[l], w_gate_up[l], b_gate_up[l], w_down[l], b_down[l])
    return rmsnorm(h, norm_final_g)
```

```python
import functools
import math

import numpy as np
import jax
import jax.numpy as jnp
from jax import lax
from jax.experimental import pallas as pl
from jax.experimental.pallas import tpu as pltpu

HEAD_DIM = 64
N_HEADS = 8
D_MIX = N_HEADS * HEAD_DIM
MOBA_BLOCK = 256
MOBA_TOPK = 3
REL_BUCKETS = 32
REL_MAX_DIST = 128
N_EXPERTS = 32
TOP_K = 4
SWIGLU_LIMIT = 7.0
SWIGLU_ALPHA = 1.702
RMS_EPS = 1e-6

LANES = 128
NEG_BIG = -1e30
VMEM_LIMIT = 56 * 1024 * 1024

F32 = jnp.float32
BF16 = jnp.bfloat16


def _split_bf16(a):
    hi = a.astype(BF16)
    lo = (a - hi.astype(F32)).astype(BF16)
    return hi, lo


def _dot_nt(a, b):
    return lax.dot_general(a, b, (((1,), (1,)), ((), ())), preferred_element_type=F32)


def _in_proj_kernel(x_ref, g_ref, w_ref, o_ref, *, n_slab, slab):
    x = x_ref[...]
    var = jnp.mean(x * x, axis=-1, keepdims=True)
    xn = ((x * lax.rsqrt(var + RMS_EPS)) * g_ref[...]).astype(BF16)
    for j in range(n_slab):
        o_ref[j] = jnp.dot(xn, w_ref[:, j * slab:(j + 1) * slab],
                           preferred_element_type=F32).astype(o_ref.dtype)


def _in_proj(xt, g, w_bf16, *, tm=512, slab=D_MIX):
    T, D = xt.shape
    n_slab = w_bf16.shape[1] // slab
    return pl.pallas_call(
        functools.partial(_in_proj_kernel, n_slab=n_slab, slab=slab),
        grid=(T // tm,),
        in_specs=[pl.BlockSpec((tm, D), lambda i: (i, 0)),
                  pl.BlockSpec((1, D), lambda i: (0, 0)),
                  pl.BlockSpec(w_bf16.shape, lambda i: (0, 0))],
        out_specs=pl.BlockSpec((n_slab, tm, slab), lambda i: (0, i, 0)),
        out_shape=jax.ShapeDtypeStruct((n_slab, T, slab), BF16),
        compiler_params=pltpu.CompilerParams(dimension_semantics=("arbitrary",),
                                             vmem_limit_bytes=VMEM_LIMIT),
        name="in_proj",
    )(xt, g.reshape(1, D), w_bf16)


def _sb_kernel(q_ref, k_ref, v_ref, o_ref, *, tq, tk, scale):
    i = pl.program_id(2)
    q = q_ref[0]
    lane = lax.broadcasted_iota(jnp.int32, (1, LANES), 1)
    row = lax.broadcasted_iota(jnp.int32, (tq, tk), 0)
    col = lax.broadcasted_iota(jnp.int32, (tq, tk), 1)
    jj = lax.broadcasted_iota(jnp.int32, (2 * tk, tk), 0)
    ss = lax.broadcasted_iota(jnp.int32, (2 * tk, tk), 1)
    later = (jnp.where(jj >= tk, jj - tk, jj) > ss).astype(BF16)
    n_kt = (i + 1) * (tq // tk)
    zero = jnp.zeros((), q.dtype)
    outs = []
    for h in range(2):
        qh = jnp.where(lane // HEAD_DIM == h, q, zero)

        def body(it, carry, qh=qh):
            c, acc = carry
            kt = n_kt - 1 - it
            ks = pl.multiple_of(kt * tk, tk)
            kblk = k_ref[0, pl.ds(ks, tk), :]
            vblk = v_ref[0, pl.ds(ks, tk), :]
            z = _dot_nt(qh, kblk) * scale
            causal = (ks + col) < (i * tq + row)
            sp = jnp.maximum(z, 0.0) + jnp.log(1.0 + jnp.exp(-jnp.abs(z)))
            log_keep = jnp.where(causal, -sp, 0.0)
            hi, lo = _split_bf16(log_keep)
            tail = jnp.dot(jnp.concatenate([hi, lo], axis=1), later, preferred_element_type=F32) + c
            a = jnp.where(causal, jnp.exp((z - sp) + tail), 0.0)
            acc = acc + jnp.dot(a.astype(BF16), vblk, preferred_element_type=F32)
            c = c + jnp.sum(log_keep, axis=-1, keepdims=True)
            return c, acc

        _, acc = lax.fori_loop(0, n_kt, body, (jnp.zeros((tq, 1), F32), jnp.zeros((tq, LANES), F32)))
        outs.append(acc)
    o_ref[...] = jnp.where(lane < HEAD_DIM, outs[0], outs[1]).astype(o_ref.dtype)


def _sb_attention(proj, *, batch, seq, tq=256, tk=128):
    T = batch * seq
    nq = seq // tq
    n_hp = D_MIX // LANES
    return pl.pallas_call(
        functools.partial(_sb_kernel, tq=tq, tk=tk, scale=1.0 / math.sqrt(HEAD_DIM)),
        grid=(batch, n_hp, nq),
        in_specs=[pl.BlockSpec((1, tq, LANES), lambda b, p, i: (0, b * nq + i, p)),
                  pl.BlockSpec((1, seq, LANES), lambda b, p, i: (1, b, p)),
                  pl.BlockSpec((1, seq, LANES), lambda b, p, i: (2, b, p))],
        out_specs=pl.BlockSpec((tq, LANES), lambda b, p, i: (b * nq + i, p)),
        out_shape=jax.ShapeDtypeStruct((T, D_MIX), BF16),
        compiler_params=pltpu.CompilerParams(
            dimension_semantics=("arbitrary", "arbitrary", "arbitrary"), vmem_limit_bytes=VMEM_LIMIT),
        name="sb_attn",
    )(proj, proj, proj)


def _rel_bucket_np(dist):
    n = np.maximum(dist, 0)
    max_exact = REL_BUCKETS // 2
    nf = np.maximum(n, 1).astype(np.float64)
    large = max_exact + (np.log(nf / max_exact) / math.log(REL_MAX_DIST / max_exact)
                         * (REL_BUCKETS - max_exact)).astype(np.int32)
    large = np.minimum(large, REL_BUCKETS - 1)
    return np.where(n < max_exact, n, large).astype(np.int32)


def _bias_kernel(rel_ref, bkt_ref, o_ref):
    h = pl.program_id(0)
    for d in range(2):
        bkt = bkt_ref[d]
        acc = jnp.zeros(bkt.shape, F32)
        for b in range(REL_BUCKETS):
            acc = jnp.where(bkt == b, rel_ref[b, h], acc)
        o_ref[0, d] = acc


def _moba_bias(rel_bias):
    L = MOBA_BLOCK
    r = np.arange(L)[:, None]
    c = np.arange(L)[None, :]
    bkt = np.stack([_rel_bucket_np(r - c), _rel_bucket_np(L + r - c)])
    return pl.pallas_call(
        _bias_kernel,
        grid=(N_HEADS,),
        in_specs=[pl.BlockSpec(memory_space=pltpu.SMEM),
                  pl.BlockSpec((2, L, L), lambda h: (0, 0, 0))],
        out_specs=pl.BlockSpec((1, 2, L, L), lambda h: (h, 0, 0, 0)),
        out_shape=jax.ShapeDtypeStruct((N_HEADS, 2, L, L), F32),
        name="moba_bias",
    )(rel_bias, jnp.asarray(bkt))


def _moba_kernel(rel_ref, q_ref, k_ref, v_ref, bias_ref, o_ref, *, L, nblk, scale):
    p = pl.program_id(1)
    i = pl.program_id(2)
    q = q_ref[0]
    lane = lax.broadcasted_iota(jnp.int32, (1, LANES), 1)
    row = lax.broadcasted_iota(jnp.int32, (L, L), 0)
    col = lax.broadcasted_iota(jnp.int32, (L, L), 1)
    kmean = jnp.mean(k_ref[0].astype(F32).reshape(nblk, L, LANES), axis=1)
    kmean = jnp.concatenate([kmean, jnp.zeros((LANES - nblk, LANES), F32)], axis=0)
    km_hi, km_lo = _split_bf16(kmean)
    km_lo2 = (kmean - km_hi.astype(F32) - km_lo.astype(F32)).astype(BF16)
    past = lane < i
    zero = jnp.zeros((), q.dtype)
    outs = []
    for h in range(2):
        qh = jnp.where(lane // HEAD_DIM == h, q, zero)
        head = 2 * p + h
        gs = _dot_nt(qh, km_hi) + _dot_nt(qh, km_lo) + _dot_nt(qh, km_lo2)
        selm = jnp.zeros((L, LANES), F32)
        for n in range(nblk - 1):
            g_n = jnp.sum(jnp.where(lane == n, gs, 0.0), axis=-1, keepdims=True)
            beats = past & ((gs > g_n) | ((gs == g_n) & (lane < n)))
            cnt = jnp.sum(beats.astype(F32), axis=-1, keepdims=True)
            selm = jnp.where((lane == n) & (cnt < MOBA_TOPK), 1.0, selm)
        ks = pl.multiple_of(i * L, L)
        s = _dot_nt(qh, k_ref[0, pl.ds(ks, L), :]) * scale + bias_ref[h, 0]
        s = jnp.where(col <= row, s, NEG_BIG)
        m = jnp.max(s, axis=-1, keepdims=True)
        pr = jnp.exp(s - m)
        l = jnp.sum(pr, axis=-1, keepdims=True)
        acc = jnp.dot(pr.astype(BF16), v_ref[0, pl.ds(ks, L), :], preferred_element_type=F32)
        far = rel_ref[REL_BUCKETS - 1, head]

        def body(n, carry, qh=qh, h=h, selm=selm, far=far):
            m, l, acc = carry
            ks = pl.multiple_of(n * L, L)
            sel = jnp.sum(jnp.where(lane == n, selm, 0.0), axis=-1, keepdims=True) > 0.0
            bias = jnp.where(i - n == 1, bias_ref[h, 1], far)
            s = _dot_nt(qh, k_ref[0, pl.ds(ks, L), :]) * scale + bias
            s = jnp.where(sel, s, NEG_BIG)
            m_new = jnp.maximum(m, jnp.max(s, axis=-1, keepdims=True))
            alpha = jnp.exp(m - m_new)
            pr = jnp.exp(s - m_new)
            l = alpha * l + jnp.sum(pr, axis=-1, keepdims=True)
            acc = alpha * acc + jnp.dot(pr.astype(BF16), v_ref[0, pl.ds(ks, L), :],
                                        preferred_element_type=F32)
            return m_new, l, acc

        m, l, acc = lax.fori_loop(0, i, body, (m, l, acc))
        outs.append(acc / l)
    o_ref[...] = jnp.where(lane < HEAD_DIM, outs[0], outs[1]).astype(o_ref.dtype)


def _moba_attention(proj, bias, rel_bias, *, batch, seq):
    T = batch * seq
    L = MOBA_BLOCK
    nblk = seq // L
    n_hp = D_MIX // LANES
    return pl.pallas_call(
        functools.partial(_moba_kernel, L=L, nblk=nblk, scale=1.0 / math.sqrt(HEAD_DIM)),
        grid=(batch, n_hp, nblk),
        in_specs=[pl.BlockSpec(memory_space=pltpu.SMEM),
                  pl.BlockSpec((1, L, LANES), lambda b, p, i: (3, b * nblk + i, p)),
                  pl.BlockSpec((1, seq, LANES), lambda b, p, i: (4, b, p)),
                  pl.BlockSpec((1, seq, LANES), lambda b, p, i: (5, b, p)),
                  pl.BlockSpec((2, 2, L, L), lambda b, p, i: (p, 0, 0, 0))],
        out_specs=pl.BlockSpec((L, LANES), lambda b, p, i: (b * nblk + i, p)),
        out_shape=jax.ShapeDtypeStruct((T, D_MIX), BF16),
        compiler_params=pltpu.CompilerParams(
            dimension_semantics=("arbitrary", "arbitrary", "arbitrary"), vmem_limit_bytes=VMEM_LIMIT),
        name="moba_attn",
    )(rel_bias, proj, proj, proj, bias)


def _post_attn_kernel(ysb_ref, ymb_ref, gsb_ref, gmb_ref, x_ref, wsb_ref, wmb_ref, wo_ref, nrm_ref,
                      wr_ref, br_ref, h_ref, hn_ref, route_ref, cnt_ref, run_ref, *, tm):
    i = pl.program_id(0)

    @pl.when(i == 0)
    def _():
        run_ref[...] = jnp.zeros_like(run_ref)

    def gate(g_ref):
        g = jnp.concatenate([g_ref[0], g_ref[1]], axis=1).astype(F32)
        return 1.0 / (1.0 + jnp.exp(-g))

    merged = (gate(gsb_ref) * jnp.dot(ysb_ref[...], wsb_ref[...], preferred_element_type=F32)
              + gate(gmb_ref) * jnp.dot(ymb_ref[...], wmb_ref[...], preferred_element_type=F32))
    hres = x_ref[...] + jnp.dot(merged.astype(BF16), wo_ref[...], preferred_element_type=F32)
    h_ref[...] = hres
    var = jnp.mean(hres * hres, axis=-1, keepdims=True)
    hn = (hres * lax.rsqrt(var + RMS_EPS)) * nrm_ref[...]
    hn_ref[...] = hn
    a_hi, a_lo = _split_bf16(hn)
    w_hi, w_lo = _split_bf16(wr_ref[...])
    logits = (jnp.dot(a_hi, w_hi, preferred_element_type=F32)
              + jnp.dot(a_hi, w_lo, preferred_element_type=F32)
              + jnp.dot(a_lo, w_hi, preferred_element_type=F32)) + br_ref[...]
    lane = lax.broadcasted_iota(jnp.int32, (1, LANES), 1).astype(F32)
    work = jnp.where(lane < N_EXPERTS, logits, NEG_BIG)
    vals, hots = [], []
    for _ in range(TOP_K):
        mx = jnp.max(work, axis=-1, keepdims=True)
        idx = jnp.min(jnp.where(work == mx, lane, float(LANES)), axis=-1, keepdims=True)
        hot = lane == idx
        work = jnp.where(hot, NEG_BIG, work)
        vals.append(mx)
        hots.append((hot, idx))
    exps = [jnp.exp(v - vals[0]) for v in vals]
    denom = exps[0] + exps[1] + exps[2] + exps[3]
    sel = jnp.zeros((tm, LANES), F32)
    for hot, _ in hots:
        sel = jnp.where(hot, 1.0, sel)
    r = lax.broadcasted_iota(jnp.int32, (tm, tm), 0)
    c = lax.broadcasted_iota(jnp.int32, (tm, tm), 1)
    before = (c < r).astype(BF16)
    cum = jnp.dot(before, sel.astype(BF16), preferred_element_type=F32) + run_ref[0:1, :]
    slab = jnp.zeros((tm, LANES), F32)
    for k, (hot, idx) in enumerate(hots):
        rank = jnp.sum(jnp.where(hot, cum, 0.0), axis=-1, keepdims=True)
        slab = jnp.where(lane == k, idx, slab)
        slab = jnp.where(lane == TOP_K + k, exps[k] / denom, slab)
        slab = jnp.where(lane == 2 * TOP_K + k, rank, slab)
    route_ref[...] = slab
    run = run_ref[0:1, :] + jnp.sum(sel, axis=0, keepdims=True)
    run_ref[...] = jnp.broadcast_to(run, run_ref.shape)
    cnt_ref[...] = jnp.broadcast_to(run, cnt_ref.shape)


def _post_attn(ysb, ymb, proj, xt, wsb, wmb, wo, nrm_g, wr_pad, br_pad, *, tm=256):
    T, D = xt.shape
    half = D // 2
    const = lambda i: (0, 0)
    return pl.pallas_call(
        functools.partial(_post_attn_kernel, tm=tm),
        grid=(T // tm,),
        in_specs=[pl.BlockSpec((tm, D_MIX), lambda i: (i, 0)),
                  pl.BlockSpec((tm, D_MIX), lambda i: (i, 0)),
                  pl.BlockSpec((2, tm, half), lambda i: (3, i, 0)),
                  pl.BlockSpec((2, tm, half), lambda i: (4, i, 0)),
                  pl.BlockSpec((tm, D), lambda i: (i, 0)),
                  pl.BlockSpec(wsb.shape, const),
                  pl.BlockSpec(wmb.shape, const),
                  pl.BlockSpec(wo.shape, const),
                  pl.BlockSpec((1, D), const),
                  pl.BlockSpec(wr_pad.shape, const),
                  pl.BlockSpec((1, LANES), const)],
        out_specs=[pl.BlockSpec((tm, D), lambda i: (i, 0)),
                   pl.BlockSpec((tm, D), lambda i: (i, 0)),
                   pl.BlockSpec((tm, LANES), lambda i: (i, 0)),
                   pl.BlockSpec((8, LANES), const)],
        out_shape=[jax.ShapeDtypeStruct((T, D), F32),
                   jax.ShapeDtypeStruct((T, D), F32),
                   jax.ShapeDtypeStruct((T, LANES), F32),
                   jax.ShapeDtypeStruct((8, LANES), F32)],
        scratch_shapes=[pltpu.VMEM((8, LANES), F32)],
        compiler_params=pltpu.CompilerParams(dimension_semantics=("arbitrary",),
                                             vmem_limit_bytes=VMEM_LIMIT),
        name="post_attn",
    )(ysb, ymb, proj, proj, xt, wsb, wmb, wo, nrm_g.reshape(1, D), wr_pad, br_pad)


def _dispatch_kernel(pos_ref, src_ref, zero_ref, dst_ref, sem, *, tt):
    del zero_ref
    i = pl.program_id(0)

    def row_copy(t, p):
        return pltpu.make_async_copy(src_ref.at[pl.ds(t, 1)], dst_ref.at[pl.ds(p, 1)], sem)

    def body(j, _):
        for k in range(TOP_K):
            row_copy(i * tt + j, pos_ref[0, 0, j * TOP_K + k]).start()
        return 0

    lax.fori_loop(0, tt, body, 0)
    lax.fori_loop(0, tt * TOP_K, lambda j, _: (row_copy(0, 0).wait(), 0)[1], 0)


def _dispatch(hn, pos, n_rows, *, tt=256):
    T, D = hn.shape
    return pl.pallas_call(
        functools.partial(_dispatch_kernel, tt=tt),
        grid=(T // tt,),
        in_specs=[pl.BlockSpec((1, 1, tt * TOP_K), lambda i: (i, 0, 0), memory_space=pltpu.SMEM),
                  pl.BlockSpec(memory_space=pl.ANY),
                  pl.BlockSpec(memory_space=pl.ANY)],
        out_specs=pl.BlockSpec(memory_space=pl.ANY),
        out_shape=jax.ShapeDtypeStruct((n_rows, D), F32),
        input_output_aliases={2: 0},
        scratch_shapes=[pltpu.SemaphoreType.DMA(())],
        compiler_params=pltpu.CompilerParams(dimension_semantics=("arbitrary",)),
        name="dispatch",
    )(pos.reshape(T // tt, 1, tt * TOP_K), hn, jnp.zeros((n_rows, D), F32))


def _experts_kernel(te_ref, tv_ref, xs_ref, wgu_ref, bgu_ref, wd_ref, bd_ref, y_ref, wgu_s, wd_s, *, d_exp):
    j = pl.program_id(0)
    e = te_ref[j]
    prev = te_ref[jnp.maximum(j - 1, 0)]

    @pl.when((j == 0) | (e != prev))
    def _():
        wgu_s[...] = wgu_ref[0].astype(BF16)
        wd_s[...] = wd_ref[0].astype(BF16)

    @pl.when(tv_ref[j] > 0)
    def _():
        x = xs_ref[...].astype(BF16)
        hgu = jnp.dot(x, wgu_s[...], preferred_element_type=F32) + bgu_ref[0]
        gate = jnp.minimum(hgu[:, :d_exp], SWIGLU_LIMIT)
        up = jnp.clip(hgu[:, d_exp:], -SWIGLU_LIMIT, SWIGLU_LIMIT)
        act = (up + 1.0) * (gate * (1.0 / (1.0 + jnp.exp(-SWIGLU_ALPHA * gate))))
        y_ref[...] = jnp.dot(act.astype(BF16), wd_s[...], preferred_element_type=F32) + bd_ref[0]

    @pl.when(tv_ref[j] == 0)
    def _():
        y_ref[...] = jnp.zeros_like(y_ref)


def _experts(xs, tile_expert, tile_valid, wgu, bgu, wd, bd, *, tmm):
    n_rows, D = xs.shape
    E, _, d2 = wgu.shape
    d_exp = d2 // 2
    n_tiles = n_rows // tmm
    grid_spec = pltpu.PrefetchScalarGridSpec(
        num_scalar_prefetch=2,
        grid=(n_tiles,),
        in_specs=[pl.BlockSpec((tmm, D), lambda j, te, tv: (j, 0)),
                  pl.BlockSpec((1, D, d2), lambda j, te, tv: (te[j], 0, 0)),
                  pl.BlockSpec((1, 1, d2), lambda j, te, tv: (te[j], 0, 0)),
                  pl.BlockSpec((1, d_exp, D), lambda j, te, tv: (te[j], 0, 0)),
                  pl.BlockSpec((1, 1, D), lambda j, te, tv: (te[j], 0, 0))],
        out_specs=pl.BlockSpec((tmm, D), lambda j, te, tv: (j, 0)),
        scratch_shapes=[pltpu.VMEM((D, d2), BF16), pltpu.VMEM((d_exp, D), BF16)],
    )
    return pl.pallas_call(
        functools.partial(_experts_kernel, d_exp=d_exp),
        grid_spec=grid_spec,
        out_shape=jax.ShapeDtypeStruct((n_rows, D), F32),
        compiler_params=pltpu.CompilerParams(dimension_semantics=("arbitrary",),
                                             vmem_limit_bytes=VMEM_LIMIT),
        name="experts",
    )(tile_expert, tile_valid, xs, wgu, bgu.reshape(E, 1, d2), wd, bd.reshape(E, 1, D))


def _combine_kernel(pos_ref, y_ref, h_ref, w_ref, g_ref, o_ref, buf, sem, *, tt):
    def row_copy(p, k, t):
        return pltpu.make_async_copy(y_ref.at[pl.ds(p, 1)], buf.at[k, pl.ds(t, 1)], sem)

    def body(j, _):
        for k in range(TOP_K):
            row_copy(pos_ref[0, 0, j * TOP_K + k], k, j).start()
        return 0

    lax.fori_loop(0, tt, body, 0)
    lax.fori_loop(0, tt * TOP_K, lambda j, _: (row_copy(0, 0, 0).wait(), 0)[1], 0)
    w = w_ref[...]
    acc = h_ref[...]
    for k in range(TOP_K):
        acc = acc + w[:, TOP_K + k:TOP_K + k + 1] * buf[k]
    var = jnp.mean(acc * acc, axis=-1, keepdims=True)
    o_ref[...] = (acc * lax.rsqrt(var + RMS_EPS)) * g_ref[...]


def _combine(y, pos, h, route, g, *, tt=256):
    T, D = h.shape
    return pl.pallas_call(
        functools.partial(_combine_kernel, tt=tt),
        grid=(T // tt,),
        in_specs=[pl.BlockSpec((1, 1, tt * TOP_K), lambda i: (i, 0, 0), memory_space=pltpu.SMEM),
                  pl.BlockSpec(memory_space=pl.ANY),
                  pl.BlockSpec((tt, D), lambda i: (i, 0)),
                  pl.BlockSpec((tt, LANES), lambda i: (i, 0)),
                  pl.BlockSpec((1, D), lambda i: (0, 0))],
        out_specs=pl.BlockSpec((tt, D), lambda i: (i, 0)),
        out_shape=jax.ShapeDtypeStruct((T, D), F32),
        scratch_shapes=[pltpu.VMEM((TOP_K, tt, D), F32), pltpu.SemaphoreType.DMA(())],
        compiler_params=pltpu.CompilerParams(dimension_semantics=("arbitrary",),
                                             vmem_limit_bytes=VMEM_LIMIT),
        name="combine",
    )(pos.reshape(T // tt, 1, tt * TOP_K), y, h, route, g.reshape(1, D))


def _layer(xt, batch, seq, norm_mix_g, w_in, w_proj_sb, w_proj_moba, w_out, rel_bias, norm_ffn_g,
           w_router, b_router, w_gate_up, b_gate_up, w_down, b_down, *, tmm=512):
    T, D = xt.shape
    proj = _in_proj(xt, norm_mix_g, w_in.astype(BF16))
    ysb = _sb_attention(proj, batch=batch, seq=seq)
    ymb = _moba_attention(proj, _moba_bias(rel_bias), rel_bias, batch=batch, seq=seq)
    wr_pad = jnp.pad(w_router, ((0, 0), (0, LANES - N_EXPERTS)))
    br_pad = jnp.pad(b_router, (0, LANES - N_EXPERTS)).reshape(1, LANES)
    h, hn, route, cnt = _post_attn(ysb, ymb, proj, xt, w_proj_sb.astype(BF16), w_proj_moba.astype(BF16),
                                   w_out.astype(BF16), norm_ffn_g, wr_pad, br_pad)
    idx = route[:, 0:TOP_K].astype(jnp.int32)
    rank = route[:, 2 * TOP_K:3 * TOP_K].astype(jnp.int32)
    counts = cnt[0, :N_EXPERTS].astype(jnp.int32)
    padded = ((counts + tmm - 1) // tmm) * tmm
    ends = jnp.cumsum(padded)
    pos = ((ends - padded)[idx] + rank).reshape(-1)
    n_tiles = (T * TOP_K) // tmm + N_EXPERTS
    tile_start = jnp.arange(n_tiles, dtype=jnp.int32) * tmm
    tile_valid = (tile_start < ends[-1]).astype(jnp.int32)
    tile_expert = jnp.minimum(jnp.searchsorted(ends, tile_start, side="right"), N_EXPERTS - 1).astype(jnp.int32)
    tile_expert = jnp.where(tile_valid > 0, tile_expert, tile_expert[jnp.maximum(ends[-1] // tmm - 1, 0)])
    xs = _dispatch(hn, pos, n_tiles * tmm)
    y = _experts(xs, tile_expert, tile_valid, w_gate_up, b_gate_up, w_down, b_down, tmm=tmm)
    return h, y, pos, route


def kernel(x, norm_mix_g, w_in, w_proj_sb, w_proj_moba, w_out, rel_bias, norm_ffn_g, w_router, b_router,
           w_gate_up, b_gate_up, w_down, b_down, norm_final_g):
    batch, seq, D = x.shape
    assert norm_mix_g.shape[0] == 1, "single-layer trunk"
    xt = x.reshape(batch * seq, D)
    h, y, pos, route = _layer(xt, batch, seq, norm_mix_g[0], w_in[0], w_proj_sb[0], w_proj_moba[0], w_out[0],
                              rel_bias, norm_ffn_g[0], w_router[0], b_router[0], w_gate_up[0], b_gate_up[0],
                              w_down[0], b_down[0])
    out = _combine(y, pos, h, route, norm_final_g)
    return out.reshape(batch, seq, D)
```

```python
import functools
import math

import numpy as np
import jax
import jax.numpy as jnp
from jax import lax
from jax.experimental import pallas as pl
from jax.experimental.pallas import tpu as pltpu

HEAD_DIM = 64
N_HEADS = 8
D_MIX = N_HEADS * HEAD_DIM
MOBA_BLOCK = 256
MOBA_TOPK = 3
REL_BUCKETS = 32
REL_MAX_DIST = 128
N_EXPERTS = 32
TOP_K = 4
SWIGLU_LIMIT = 7.0
SWIGLU_ALPHA = 1.702
RMS_EPS = 1e-6

LANES = 128
NEG_BIG = -1e30
VMEM_LIMIT = 56 * 1024 * 1024

F32 = jnp.float32
BF16 = jnp.bfloat16


def _split_bf16(a):
    hi = a.astype(BF16)
    lo = (a - hi.astype(F32)).astype(BF16)
    return hi, lo


def _dot_nt(a, b):
    return lax.dot_general(a, b, (((1,), (1,)), ((), ())), preferred_element_type=F32)


def _in_proj_kernel(x_ref, g_ref, w_ref, o_ref, *, n_slab, slab):
    x = x_ref[...]
    var = jnp.mean(x * x, axis=-1, keepdims=True)
    xn = ((x * lax.rsqrt(var + RMS_EPS)) * g_ref[...]).astype(BF16)
    for j in range(n_slab):
        o_ref[j] = jnp.dot(xn, w_ref[:, j * slab:(j + 1) * slab],
                           preferred_element_type=F32).astype(o_ref.dtype)


def _in_proj(xt, g, w_bf16, *, tm=512, slab=D_MIX):
    T, D = xt.shape
    n_slab = w_bf16.shape[1] // slab
    return pl.pallas_call(
        functools.partial(_in_proj_kernel, n_slab=n_slab, slab=slab),
        grid=(T // tm,),
        in_specs=[pl.BlockSpec((tm, D), lambda i: (i, 0)),
                  pl.BlockSpec((1, D), lambda i: (0, 0)),
                  pl.BlockSpec(w_bf16.shape, lambda i: (0, 0))],
        out_specs=pl.BlockSpec((n_slab, tm, slab), lambda i: (0, i, 0)),
        out_shape=jax.ShapeDtypeStruct((n_slab, T, slab), BF16),
        compiler_params=pltpu.CompilerParams(dimension_semantics=("arbitrary",),
                                             vmem_limit_bytes=VMEM_LIMIT),
        name="in_proj",
    )(xt, g.reshape(1, D), w_bf16)


def _sb_kernel(q_ref, k_ref, v_ref, o_ref, *, tq, tk, scale):
    i = pl.program_id(2)
    q = q_ref[0]
    lane = lax.broadcasted_iota(jnp.int32, (1, LANES), 1)
    row = lax.broadcasted_iota(jnp.int32, (tq, tk), 0)
    col = lax.broadcasted_iota(jnp.int32, (tq, tk), 1)
    jj = lax.broadcasted_iota(jnp.int32, (2 * tk, tk), 0)
    ss = lax.broadcasted_iota(jnp.int32, (2 * tk, tk), 1)
    later = (jnp.where(jj >= tk, jj - tk, jj) > ss).astype(BF16)
    n_kt = (i + 1) * (tq // tk)
    zero = jnp.zeros((), q.dtype)
    outs = []
    for h in range(2):
        qh = jnp.where(lane // HEAD_DIM == h, q, zero)

        def body(it, carry, qh=qh):
            c, acc = carry
            kt = n_kt - 1 - it
            ks = pl.multiple_of(kt * tk, tk)
            kblk = k_ref[0, pl.ds(ks, tk), :]
            vblk = v_ref[0, pl.ds(ks, tk), :]
            z = _dot_nt(qh, kblk) * scale
            causal = (ks + col) < (i * tq + row)
            sp = jnp.maximum(z, 0.0) + jnp.log(1.0 + jnp.exp(-jnp.abs(z)))
            log_keep = jnp.where(causal, -sp, 0.0)
            hi, lo = _split_bf16(log_keep)
            tail = jnp.dot(jnp.concatenate([hi, lo], axis=1), later, preferred_element_type=F32) + c
            a = jnp.where(causal, jnp.exp((z - sp) + tail), 0.0)
            acc = acc + jnp.dot(a.astype(BF16), vblk, preferred_element_type=F32)
            c = c + jnp.sum(log_keep, axis=-1, keepdims=True)
            return c, acc

        _, acc = lax.fori_loop(0, n_kt, body, (jnp.zeros((tq, 1), F32), jnp.zeros((tq, LANES), F32)))
        outs.append(acc)
    o_ref[...] = jnp.where(lane < HEAD_DIM, outs[0], outs[1]).astype(o_ref.dtype)


def _sb_attention(proj, *, batch, seq, tq=256, tk=128):
    T = batch * seq
    nq = seq // tq
    n_hp = D_MIX // LANES
    return pl.pallas_call(
        functools.partial(_sb_kernel, tq=tq, tk=tk, scale=1.0 / math.sqrt(HEAD_DIM)),
        grid=(batch, n_hp, nq),
        in_specs=[pl.BlockSpec((1, tq, LANES), lambda b, p, i: (0, b * nq + i, p)),
                  pl.BlockSpec((1, seq, LANES), lambda b, p, i: (1, b, p)),
                  pl.BlockSpec((1, seq, LANES), lambda b, p, i: (2, b, p))],
        out_specs=pl.BlockSpec((tq, LANES), lambda b, p, i: (b * nq + i, p)),
        out_shape=jax.ShapeDtypeStruct((T, D_MIX), BF16),
        compiler_params=pltpu.CompilerParams(
            dimension_semantics=("arbitrary", "arbitrary", "arbitrary"), vmem_limit_bytes=VMEM_LIMIT),
        name="sb_attn",
    )(proj, proj, proj)


def _rel_bucket_np(dist):
    n = np.maximum(dist, 0)
    max_exact = REL_BUCKETS // 2
    nf = np.maximum(n, 1).astype(np.float64)
    large = max_exact + (np.log(nf / max_exact) / math.log(REL_MAX_DIST / max_exact)
                         * (REL_BUCKETS - max_exact)).astype(np.int32)
    large = np.minimum(large, REL_BUCKETS - 1)
    return np.where(n < max_exact, n, large).astype(np.int32)


def _bias_kernel(rel_ref, bkt_ref, o_ref):
    h = pl.program_id(0)
    for d in range(2):
        bkt = bkt_ref[d]
        acc = jnp.zeros(bkt.shape, F32)
        for b in range(REL_BUCKETS):
            acc = jnp.where(bkt == b, rel_ref[b, h], acc)
        o_ref[0, d] = acc


def _moba_bias(rel_bias):
    L = MOBA_BLOCK
    r = np.arange(L)[:, None]
    c = np.arange(L)[None, :]
    bkt = np.stack([_rel_bucket_np(r - c), _rel_bucket_np(L + r - c)])
    return pl.pallas_call(
        _bias_kernel,
        grid=(N_HEADS,),
        in_specs=[pl.BlockSpec(memory_space=pltpu.SMEM),
                  pl.BlockSpec((2, L, L), lambda h: (0, 0, 0))],
        out_specs=pl.BlockSpec((1, 2, L, L), lambda h: (h, 0, 0, 0)),
        out_shape=jax.ShapeDtypeStruct((N_HEADS, 2, L, L), F32),
        name="moba_bias",
    )(rel_bias, jnp.asarray(bkt))


def _moba_kernel(rel_ref, q_ref, k_ref, v_ref, bias_ref, o_ref, *, L, nblk, scale):
    p = pl.program_id(1)
    i = pl.program_id(2)
    q = q_ref[0]
    lane = lax.broadcasted_iota(jnp.int32, (1, LANES), 1)
    row = lax.broadcasted_iota(jnp.int32, (L, L), 0)
    col = lax.broadcasted_iota(jnp.int32, (L, L), 1)
    kmean = jnp.mean(k_ref[0].astype(F32).reshape(nblk, L, LANES), axis=1)
    kmean = jnp.concatenate([kmean, jnp.zeros((LANES - nblk, LANES), F32)], axis=0)
    km_hi, km_lo = _split_bf16(kmean)
    km_lo2 = (kmean - km_hi.astype(F32) - km_lo.astype(F32)).astype(BF16)
    past = lane < i
    zero = jnp.zeros((), q.dtype)
    outs = []
    for h in range(2):
        qh = jnp.where(lane // HEAD_DIM == h, q, zero)
        head = 2 * p + h
        gs = _dot_nt(qh, km_hi) + _dot_nt(qh, km_lo) + _dot_nt(qh, km_lo2)
        selm = jnp.zeros((L, LANES), F32)
        for n in range(nblk - 1):
            g_n = jnp.sum(jnp.where(lane == n, gs, 0.0), axis=-1, keepdims=True)
            beats = past & ((gs > g_n) | ((gs == g_n) & (lane < n)))
            cnt = jnp.sum(beats.astype(F32), axis=-1, keepdims=True)
            selm = jnp.where((lane == n) & (cnt < MOBA_TOPK), 1.0, selm)
        ks = pl.multiple_of(i * L, L)
        s = _dot_nt(qh, k_ref[0, pl.ds(ks, L), :]) * scale + bias_ref[h, 0]
        s = jnp.where(col <= row, s, NEG_BIG)
        m = jnp.max(s, axis=-1, keepdims=True)
        pr = jnp.exp(s - m)
        l = jnp.sum(pr, axis=-1, keepdims=True)
        acc = jnp.dot(pr.astype(BF16), v_ref[0, pl.ds(ks, L), :], preferred_element_type=F32)
        far = rel_ref[REL_BUCKETS - 1, head]

        def body(n, carry, qh=qh, h=h, selm=selm, far=far):
            m, l, acc = carry
            ks = pl.multiple_of(n * L, L)
            sel = jnp.sum(jnp.where(lane == n, selm, 0.0), axis=-1, keepdims=True) > 0.0
            bias = jnp.where(i - n == 1, bias_ref[h, 1], far)
            s = _dot_nt(qh, k_ref[0, pl.ds(ks, L), :]) * scale + bias
            s = jnp.where(sel, s, NEG_BIG)
            m_new = jnp.maximum(m, jnp.max(s, axis=-1, keepdims=True))
            alpha = jnp.exp(m - m_new)
            pr = jnp.exp(s - m_new)
            l = alpha * l + jnp.sum(pr, axis=-1, keepdims=True)
            acc = alpha * acc + jnp.dot(pr.astype(BF16), v_ref[0, pl.ds(ks, L), :],
                                        preferred_element_type=F32)
            return m_new, l, acc

        m, l, acc = lax.fori_loop(0, i, body, (m, l, acc))
        outs.append(acc / l)
    o_ref[...] = jnp.where(lane < HEAD_DIM, outs[0], outs[1]).astype(o_ref.dtype)


def _moba_attention(proj, bias, rel_bias, *, batch, seq):
    T = batch * seq
    L = MOBA_BLOCK
    nblk = seq // L
    n_hp = D_MIX // LANES
    return pl.pallas_call(
        functools.partial(_moba_kernel, L=L, nblk=nblk, scale=1.0 / math.sqrt(HEAD_DIM)),
        grid=(batch, n_hp, nblk),
        in_specs=[pl.BlockSpec(memory_space=pltpu.SMEM),
                  pl.BlockSpec((1, L, LANES), lambda b, p, i: (3, b * nblk + i, p)),
                  pl.BlockSpec((1, seq, LANES), lambda b, p, i: (4, b, p)),
                  pl.BlockSpec((1, seq, LANES), lambda b, p, i: (5, b, p)),
                  pl.BlockSpec((2, 2, L, L), lambda b, p, i: (p, 0, 0, 0))],
        out_specs=pl.BlockSpec((L, LANES), lambda b, p, i: (b * nblk + i, p)),
        out_shape=jax.ShapeDtypeStruct((T, D_MIX), BF16),
        compiler_params=pltpu.CompilerParams(
            dimension_semantics=("arbitrary", "arbitrary", "arbitrary"), vmem_limit_bytes=VMEM_LIMIT),
        name="moba_attn",
    )(rel_bias, proj, proj, proj, bias)


def _post_attn_kernel(ysb_ref, ymb_ref, gsb_ref, gmb_ref, x_ref, wsb_ref, wmb_ref, wo_ref, nrm_ref,
                      wr_ref, br_ref, h_ref, hn_ref, route_ref, cnt_ref, run_ref, *, tm):
    i = pl.program_id(0)

    @pl.when(i == 0)
    def _():
        run_ref[...] = jnp.zeros_like(run_ref)

    def gate(g_ref):
        g = jnp.concatenate([g_ref[0], g_ref[1]], axis=1).astype(F32)
        return 1.0 / (1.0 + jnp.exp(-g))

    merged = (gate(gsb_ref) * jnp.dot(ysb_ref[...], wsb_ref[...], preferred_element_type=F32)
              + gate(gmb_ref) * jnp.dot(ymb_ref[...], wmb_ref[...], preferred_element_type=F32))
    hres = x_ref[...] + jnp.dot(merged.astype(BF16), wo_ref[...], preferred_element_type=F32)
    h_ref[...] = hres
    var = jnp.mean(hres * hres, axis=-1, keepdims=True)
    hn = (hres * lax.rsqrt(var + RMS_EPS)) * nrm_ref[...]
    hn_ref[...] = hn
    a_hi, a_lo = _split_bf16(hn)
    w_hi, w_lo = _split_bf16(wr_ref[...])
    logits = (jnp.dot(a_hi, w_hi, preferred_element_type=F32)
              + jnp.dot(a_hi, w_lo, preferred_element_type=F32)
              + jnp.dot(a_lo, w_hi, preferred_element_type=F32)) + br_ref[...]
    lane = lax.broadcasted_iota(jnp.int32, (1, LANES), 1).astype(F32)
    work = jnp.where(lane < N_EXPERTS, logits, NEG_BIG)
    vals, hots = [], []
    for _ in range(TOP_K):
        mx = jnp.max(work, axis=-1, keepdims=True)
        idx = jnp.min(jnp.where(work == mx, lane, float(LANES)), axis=-1, keepdims=True)
        hot = lane == idx
        work = jnp.where(hot, NEG_BIG, work)
        vals.append(mx)
        hots.append((hot, idx))
    exps = [jnp.exp(v - vals[0]) for v in vals]
    denom = exps[0] + exps[1] + exps[2] + exps[3]
    sel = jnp.zeros((tm, LANES), F32)
    for hot, _ in hots:
        sel = jnp.where(hot, 1.0, sel)
    r = lax.broadcasted_iota(jnp.int32, (tm, tm), 0)
    c = lax.broadcasted_iota(jnp.int32, (tm, tm), 1)
    before = (c < r).astype(BF16)
    cum = jnp.dot(before, sel.astype(BF16), preferred_element_type=F32) + run_ref[0:1, :]
    slab = jnp.zeros((tm, LANES), F32)
    for k, (hot, idx) in enumerate(hots):
        rank = jnp.sum(jnp.where(hot, cum, 0.0), axis=-1, keepdims=True)
        slab = jnp.where(lane == k, idx, slab)
        slab = jnp.where(lane == TOP_K + k, exps[k] / denom, slab)
        slab = jnp.where(lane == 2 * TOP_K + k, rank, slab)
    route_ref[...] = slab
    run = run_ref[0:1, :] + jnp.sum(sel, axis=0, keepdims=True)
    run_ref[...] = jnp.broadcast_to(run, run_ref.shape)
    cnt_ref[...] = jnp.broadcast_to(run, cnt_ref.shape)


def _post_attn(ysb, ymb, proj, xt, wsb, wmb, wo, nrm_g, wr_pad, br_pad, *, tm=256):
    T, D = xt.shape
    half = D // 2
    const = lambda i: (0, 0)
    return pl.pallas_call(
        functools.partial(_post_attn_kernel, tm=tm),
        grid=(T // tm,),
        in_specs=[pl.BlockSpec((tm, D_MIX), lambda i: (i, 0)),
                  pl.BlockSpec((tm, D_MIX), lambda i: (i, 0)),
                  pl.BlockSpec((2, tm, half), lambda i: (3, i, 0)),
                  pl.BlockSpec((2, tm, half), lambda i: (4, i, 0)),
                  pl.BlockSpec((tm, D), lambda i: (i, 0)),
                  pl.BlockSpec(wsb.shape, const),
                  pl.BlockSpec(wmb.shape, const),
                  pl.BlockSpec(wo.shape, const),
                  pl.BlockSpec((1, D), const),
                  pl.BlockSpec(wr_pad.shape, const),
                  pl.BlockSpec((1, LANES), const)],
        out_specs=[pl.BlockSpec((tm, D), lambda i: (i, 0)),
                   pl.BlockSpec((tm, D), lambda i: (i, 0)),
                   pl.BlockSpec((tm, LANES), lambda i: (i, 0)),
                   pl.BlockSpec((8, LANES), const)],
        out_shape=[jax.ShapeDtypeStruct((T, D), F32),
                   jax.ShapeDtypeStruct((T, D), F32),
                   jax.ShapeDtypeStruct((T, LANES), F32),
                   jax.ShapeDtypeStruct((8, LANES), F32)],
        scratch_shapes=[pltpu.VMEM((8, LANES), F32)],
        compiler_params=pltpu.CompilerParams(dimension_semantics=("arbitrary",),
                                             vmem_limit_bytes=VMEM_LIMIT),
        name="post_attn",
    )(ysb, ymb, proj, proj, xt, wsb, wmb, wo, nrm_g.reshape(1, D), wr_pad, br_pad)


GATHER_UNROLL = 8


def _experts_kernel(te_ref, tv_ref, src_ref, nxt_ref, hn_ref, wgu_ref, bgu_ref, wd_ref, bd_ref, y_ref,
                    xbuf, wgu_s, wd_s, sem, *, d_exp, tmm):
    j = pl.program_id(0)
    n_tiles = pl.num_programs(0)
    slot = j % 2
    e = te_ref[j]
    prev = te_ref[jnp.maximum(j - 1, 0)]

    def start_gather(idx_ref, dst_slot):
        def body(r, _):
            for u in range(GATHER_UNROLL):
                rr = r * GATHER_UNROLL + u
                pltpu.make_async_copy(hn_ref.at[pl.ds(idx_ref[0, 0, rr], 1)],
                                      xbuf.at[dst_slot, pl.ds(rr, 1)], sem.at[dst_slot]).start()
            return 0
        lax.fori_loop(0, tmm // GATHER_UNROLL, body, 0)

    @pl.when((j == 0) & (tv_ref[0] > 0))
    def _():
        start_gather(src_ref, 0)

    @pl.when((j + 1 < n_tiles) & (tv_ref[jnp.minimum(j + 1, n_tiles - 1)] > 0))
    def _():
        start_gather(nxt_ref, 1 - slot)

    @pl.when((j == 0) | (e != prev))
    def _():
        wgu_s[...] = wgu_ref[0].astype(BF16)
        wd_s[...] = wd_ref[0].astype(BF16)

    @pl.when(tv_ref[j] > 0)
    def _():
        pltpu.make_async_copy(hn_ref.at[pl.ds(0, tmm)], xbuf.at[slot], sem.at[slot]).wait()
        x = xbuf[slot].astype(BF16)
        hgu = jnp.dot(x, wgu_s[...], preferred_element_type=F32) + bgu_ref[0]
        gate = jnp.minimum(hgu[:, :d_exp], SWIGLU_LIMIT)
        up = jnp.clip(hgu[:, d_exp:], -SWIGLU_LIMIT, SWIGLU_LIMIT)
        act = (up + 1.0) * (gate * (1.0 / (1.0 + jnp.exp(-SWIGLU_ALPHA * gate))))
        y_ref[...] = jnp.dot(act.astype(BF16), wd_s[...], preferred_element_type=F32) + bd_ref[0]

    @pl.when(tv_ref[j] == 0)
    def _():
        y_ref[...] = jnp.zeros_like(y_ref)


def _experts(hn, src, tile_expert, tile_valid, wgu, bgu, wd, bd, *, tmm):
    T, D = hn.shape
    E, _, d2 = wgu.shape
    d_exp = d2 // 2
    n_tiles = src.shape[0] // tmm
    src3 = src.reshape(n_tiles, 1, tmm)
    grid_spec = pltpu.PrefetchScalarGridSpec(
        num_scalar_prefetch=2,
        grid=(n_tiles,),
        in_specs=[pl.BlockSpec((1, 1, tmm), lambda j, te, tv: (j, 0, 0), memory_space=pltpu.SMEM),
                  pl.BlockSpec((1, 1, tmm), lambda j, te, tv: (jnp.minimum(j + 1, n_tiles - 1), 0, 0),
                               memory_space=pltpu.SMEM),
                  pl.BlockSpec(memory_space=pl.ANY),
                  pl.BlockSpec((1, D, d2), lambda j, te, tv: (te[j], 0, 0)),
                  pl.BlockSpec((1, 1, d2), lambda j, te, tv: (te[j], 0, 0)),
                  pl.BlockSpec((1, d_exp, D), lambda j, te, tv: (te[j], 0, 0)),
                  pl.BlockSpec((1, 1, D), lambda j, te, tv: (te[j], 0, 0))],
        out_specs=pl.BlockSpec((tmm, D), lambda j, te, tv: (j, 0)),
        scratch_shapes=[pltpu.VMEM((2, tmm, D), F32), pltpu.VMEM((D, d2), BF16), pltpu.VMEM((d_exp, D), BF16),
                        pltpu.SemaphoreType.DMA((2,))],
    )
    return pl.pallas_call(
        functools.partial(_experts_kernel, d_exp=d_exp, tmm=tmm),
        grid_spec=grid_spec,
        out_shape=jax.ShapeDtypeStruct((n_tiles * tmm, D), F32),
        compiler_params=pltpu.CompilerParams(dimension_semantics=("arbitrary",),
                                             vmem_limit_bytes=VMEM_LIMIT),
        name="experts",
    )(tile_expert, tile_valid, src3, src3, hn, wgu, bgu.reshape(E, 1, d2), wd, bd.reshape(E, 1, D))


def _combine_kernel(pos_ref, y_ref, h_ref, w_ref, g_ref, o_ref, buf, sem, *, tt):
    def row_copy(p, k, t):
        return pltpu.make_async_copy(y_ref.at[pl.ds(p, 1)], buf.at[k, pl.ds(t, 1)], sem)

    def body(j, _):
        for k in range(TOP_K):
            row_copy(pos_ref[0, 0, j * TOP_K + k], k, j).start()
        return 0

    lax.fori_loop(0, tt, body, 0)
    for k in range(TOP_K):
        pltpu.make_async_copy(y_ref.at[pl.ds(0, tt)], buf.at[k], sem).wait()
    w = w_ref[...]
    acc = h_ref[...]
    for k in range(TOP_K):
        acc = acc + w[:, TOP_K + k:TOP_K + k + 1] * buf[k]
    var = jnp.mean(acc * acc, axis=-1, keepdims=True)
    o_ref[...] = (acc * lax.rsqrt(var + RMS_EPS)) * g_ref[...]


def _combine(y, pos, h, route, g, *, tt=256):
    T, D = h.shape
    return pl.pallas_call(
        functools.partial(_combine_kernel, tt=tt),
        grid=(T // tt,),
        in_specs=[pl.BlockSpec((1, 1, tt * TOP_K), lambda i: (i, 0, 0), memory_space=pltpu.SMEM),
                  pl.BlockSpec(memory_space=pl.ANY),
                  pl.BlockSpec((tt, D), lambda i: (i, 0)),
                  pl.BlockSpec((tt, LANES), lambda i: (i, 0)),
                  pl.BlockSpec((1, D), lambda i: (0, 0))],
        out_specs=pl.BlockSpec((tt, D), lambda i: (i, 0)),
        out_shape=jax.ShapeDtypeStruct((T, D), F32),
        scratch_shapes=[pltpu.VMEM((TOP_K, tt, D), F32), pltpu.SemaphoreType.DMA(())],
        compiler_params=pltpu.CompilerParams(dimension_semantics=("arbitrary",),
                                             vmem_limit_bytes=VMEM_LIMIT),
        name="combine",
    )(pos.reshape(T // tt, 1, tt * TOP_K), y, h, route, g.reshape(1, D))


def _layer(xt, batch, seq, norm_mix_g, w_in, w_proj_sb, w_proj_moba, w_out, rel_bias, norm_ffn_g,
           w_router, b_router, w_gate_up, b_gate_up, w_down, b_down, *, tmm=512):
    T, D = xt.shape
    proj = _in_proj(xt, norm_mix_g, w_in.astype(BF16))
    ysb = _sb_attention(proj, batch=batch, seq=seq)
    ymb = _moba_attention(proj, _moba_bias(rel_bias), rel_bias, batch=batch, seq=seq)
    wr_pad = jnp.pad(w_router, ((0, 0), (0, LANES - N_EXPERTS)))
    br_pad = jnp.pad(b_router, (0, LANES - N_EXPERTS)).reshape(1, LANES)
    h, hn, route, cnt = _post_attn(ysb, ymb, proj, xt, w_proj_sb.astype(BF16), w_proj_moba.astype(BF16),
                                   w_out.astype(BF16), norm_ffn_g, wr_pad, br_pad)
    idx = route[:, 0:TOP_K].astype(jnp.int32)
    rank = route[:, 2 * TOP_K:3 * TOP_K].astype(jnp.int32)
    counts = cnt[0, :N_EXPERTS].astype(jnp.int32)
    padded = ((counts + tmm - 1) // tmm) * tmm
    ends = jnp.cumsum(padded)
    pos = ((ends - padded)[idx] + rank).reshape(-1)
    n_tiles = (T * TOP_K) // tmm + N_EXPERTS
    tile_start = jnp.arange(n_tiles, dtype=jnp.int32) * tmm
    tile_valid = (tile_start < ends[-1]).astype(jnp.int32)
    tile_expert = jnp.minimum(jnp.searchsorted(ends, tile_start, side="right"), N_EXPERTS - 1).astype(jnp.int32)
    tile_expert = jnp.where(tile_valid > 0, tile_expert, tile_expert[jnp.maximum(ends[-1] // tmm - 1, 0)])
    src = jnp.zeros((n_tiles * tmm,), jnp.int32).at[pos].set(jnp.arange(T * TOP_K, dtype=jnp.int32) // TOP_K)
    y = _experts(hn, src, tile_expert, tile_valid, w_gate_up, b_gate_up, w_down, b_down, tmm=tmm)
    return h, y, pos, route


def kernel(x, norm_mix_g, w_in, w_proj_sb, w_proj_moba, w_out, rel_bias, norm_ffn_g, w_router, b_router,
           w_gate_up, b_gate_up, w_down, b_down, norm_final_g):
    batch, seq, D = x.shape
    assert norm_mix_g.shape[0] == 1, "single-layer trunk"
    xt = x.reshape(batch * seq, D)
    h, y, pos, route = _layer(xt, batch, seq, norm_mix_g[0], w_in[0], w_proj_sb[0], w_proj_moba[0], w_out[0],
                              rel_bias, norm_ffn_g[0], w_router[0], b_router[0], w_gate_up[0], b_gate_up[0],
                              w_down[0], b_down[0])
    out = _combine(y, pos, h, route, norm_final_g)
    return out.reshape(batch, seq, D)
```

```python
import functools
import math

import numpy as np
import jax
import jax.numpy as jnp
from jax import lax
from jax.experimental import pallas as pl
from jax.experimental.pallas import tpu as pltpu

HEAD_DIM = 64
N_HEADS = 8
D_MIX = N_HEADS * HEAD_DIM
MOBA_BLOCK = 256
MOBA_TOPK = 3
REL_BUCKETS = 32
REL_MAX_DIST = 128
N_EXPERTS = 32
TOP_K = 4
SWIGLU_LIMIT = 7.0
SWIGLU_ALPHA = 1.702
RMS_EPS = 1e-6

LANES = 128
NEG_BIG = -1e30
VMEM_LIMIT = 56 * 1024 * 1024

F32 = jnp.float32
BF16 = jnp.bfloat16


def _split_bf16(a):
    hi = a.astype(BF16)
    lo = (a - hi.astype(F32)).astype(BF16)
    return hi, lo


def _dot_nt(a, b):
    return lax.dot_general(a, b, (((1,), (1,)), ((), ())), preferred_element_type=F32)


def _in_proj_kernel(x_ref, g_ref, w_ref, o_ref, *, n_slab, slab):
    x = x_ref[...]
    var = jnp.mean(x * x, axis=-1, keepdims=True)
    xn = ((x * lax.rsqrt(var + RMS_EPS)) * g_ref[...]).astype(BF16)
    for j in range(n_slab):
        o_ref[j] = jnp.dot(xn, w_ref[:, j * slab:(j + 1) * slab],
                           preferred_element_type=F32).astype(o_ref.dtype)


def _in_proj(xt, g, w_bf16, *, tm=512, slab=D_MIX):
    T, D = xt.shape
    n_slab = w_bf16.shape[1] // slab
    return pl.pallas_call(
        functools.partial(_in_proj_kernel, n_slab=n_slab, slab=slab),
        grid=(T // tm,),
        in_specs=[pl.BlockSpec((tm, D), lambda i: (i, 0)),
                  pl.BlockSpec((1, D), lambda i: (0, 0)),
                  pl.BlockSpec(w_bf16.shape, lambda i: (0, 0))],
        out_specs=pl.BlockSpec((n_slab, tm, slab), lambda i: (0, i, 0)),
        out_shape=jax.ShapeDtypeStruct((n_slab, T, slab), BF16),
        compiler_params=pltpu.CompilerParams(dimension_semantics=("arbitrary",),
                                             vmem_limit_bytes=VMEM_LIMIT),
        name="in_proj",
    )(xt, g.reshape(1, D), w_bf16)


def _sb_kernel(q_ref, k_ref, v_ref, o_ref, *, tq, tk, zscale):
    i = pl.program_id(2)
    q = q_ref[0]
    lane = lax.broadcasted_iota(jnp.int32, (1, LANES), 1)
    row = lax.broadcasted_iota(jnp.int32, (tq, tk), 0)
    col = lax.broadcasted_iota(jnp.int32, (tq, tk), 1)
    jj = lax.broadcasted_iota(jnp.int32, (2 * tk, tk), 0)
    ss = lax.broadcasted_iota(jnp.int32, (2 * tk, tk), 1)
    later = (jnp.where(jj >= tk, jj - tk, jj) >= ss).astype(BF16)
    zero = jnp.zeros((), q.dtype)
    qhs = [jnp.where(lane // HEAD_DIM == h, q, zero) for h in range(2)]
    n_sub = tq // tk

    def q_tile_keys(base, carry, diagonal):
        tiles = [(h, sub) for h in range(2) for sub in reversed(range(n_sub))]
        starts = {sub: pl.multiple_of(base + sub * tk, tk) for sub in range(n_sub)}
        z2 = {t: _dot_nt(qhs[t[0]], k_ref[0, pl.ds(starts[t[1]], tk), :]) * zscale for t in tiles}
        drop, parts = {}, {}
        for t in tiles:
            neg_abs = lax.bitcast_convert_type(
                lax.bitcast_convert_type(z2[t], jnp.uint32) | jnp.uint32(0x80000000), F32)
            d = jnp.maximum(z2[t], 0.0) + jnp.log2(1.0 + jnp.exp2(neg_abs))
            if diagonal:
                d = jnp.where((t[1] * tk + col) < row, d, 0.0)
            drop[t] = d
            parts[t] = jnp.concatenate(_split_bf16(d), axis=1)
        inner = {t: jnp.dot(parts[t], later, preferred_element_type=F32) for t in tiles}
        c = [carry[h][0] for h in range(2)]
        a = {}
        for t in tiles:
            h = t[0]
            w = jnp.exp2((z2[t] - c[h]) - inner[t])
            if diagonal:
                w = jnp.where((t[1] * tk + col) < row, w, 0.0)
            a[t] = w.astype(BF16)
            c[h] = c[h] + jnp.sum(drop[t], axis=-1, keepdims=True)
        acc = [carry[h][1] for h in range(2)]
        for t in tiles:
            acc[t[0]] = acc[t[0]] + jnp.dot(a[t], v_ref[0, pl.ds(starts[t[1]], tk), :],
                                            preferred_element_type=F32)
        return tuple((c[h], acc[h]) for h in range(2))

    init = tuple((jnp.zeros((tq, 1), F32), jnp.zeros((tq, LANES), F32)) for _ in range(2))
    carry = q_tile_keys(i * tq, init, True)
    carry = lax.fori_loop(0, i, lambda it, cr: q_tile_keys((i - 1 - it) * tq, cr, False), carry)
    o_ref[...] = jnp.where(lane < HEAD_DIM, carry[0][1], carry[1][1]).astype(o_ref.dtype)


def _sb_attention(proj, *, batch, seq, tq=512, tk=128):
    T = batch * seq
    nq = seq // tq
    n_hp = D_MIX // LANES
    return pl.pallas_call(
        functools.partial(_sb_kernel, tq=tq, tk=tk, zscale=math.log2(math.e) / math.sqrt(HEAD_DIM)),
        grid=(batch, n_hp, nq),
        in_specs=[pl.BlockSpec((1, tq, LANES), lambda b, p, i: (0, b * nq + i, p)),
                  pl.BlockSpec((1, seq, LANES), lambda b, p, i: (1, b, p)),
                  pl.BlockSpec((1, seq, LANES), lambda b, p, i: (2, b, p))],
        out_specs=pl.BlockSpec((tq, LANES), lambda b, p, i: (b * nq + i, p)),
        out_shape=jax.ShapeDtypeStruct((T, D_MIX), BF16),
        compiler_params=pltpu.CompilerParams(
            dimension_semantics=("arbitrary", "arbitrary", "arbitrary"), vmem_limit_bytes=VMEM_LIMIT),
        name="sb_attn",
    )(proj, proj, proj)


def _rel_bucket_np(dist):
    n = np.maximum(dist, 0)
    max_exact = REL_BUCKETS // 2
    nf = np.maximum(n, 1).astype(np.float64)
    large = max_exact + (np.log(nf / max_exact) / math.log(REL_MAX_DIST / max_exact)
                         * (REL_BUCKETS - max_exact)).astype(np.int32)
    large = np.minimum(large, REL_BUCKETS - 1)
    return np.where(n < max_exact, n, large).astype(np.int32)


def _bias_kernel(rel_ref, bkt_ref, o_ref):
    h = pl.program_id(0)
    for d in range(2):
        bkt = bkt_ref[d]
        acc = jnp.zeros(bkt.shape, F32)
        for b in range(REL_BUCKETS):
            acc = jnp.where(bkt == b, rel_ref[b, h], acc)
        o_ref[0, d] = acc


def _moba_bias(rel_bias):
    L = MOBA_BLOCK
    r = np.arange(L)[:, None]
    c = np.arange(L)[None, :]
    bkt = np.stack([_rel_bucket_np(r - c), _rel_bucket_np(L + r - c)])
    return pl.pallas_call(
        _bias_kernel,
        grid=(N_HEADS,),
        in_specs=[pl.BlockSpec(memory_space=pltpu.SMEM),
                  pl.BlockSpec((2, L, L), lambda h: (0, 0, 0))],
        out_specs=pl.BlockSpec((1, 2, L, L), lambda h: (h, 0, 0, 0)),
        out_shape=jax.ShapeDtypeStruct((N_HEADS, 2, L, L), F32),
        name="moba_bias",
    )(rel_bias, jnp.asarray(bkt))


def _moba_kernel(rel_ref, q_ref, k_ref, v_ref, bias_ref, o_ref, km_ref, *, L, nblk, scale):
    p = pl.program_id(1)
    i = pl.program_id(2)
    lane = lax.broadcasted_iota(jnp.int32, (1, LANES), 1)
    row = lax.broadcasted_iota(jnp.int32, (L, L), 0)
    col = lax.broadcasted_iota(jnp.int32, (L, L), 1)

    @pl.when(i == 0)
    def _():
        km_ref[...] = jnp.mean(k_ref[0].astype(F32).reshape(nblk, L, LANES), axis=1)

    def attend(n_off):
        rr = lax.broadcasted_iota(jnp.int32, (n_off, 1), 0)
        km = jnp.concatenate([km_ref[pl.ds(jnp.maximum(i - r, 0), 1), :] for r in range(n_off)], axis=0)
        km = jnp.concatenate([km, jnp.zeros((16 - n_off, LANES), F32)], axis=0)
        km_hi, km_lo = _split_bf16(km)
        km_lo2 = (km - km_hi.astype(F32) - km_lo.astype(F32)).astype(BF16)
        q = q_ref[0] * jnp.asarray(scale, q_ref.dtype)
        past = (rr >= 1) & (rr <= i)
        zero = jnp.zeros((), q.dtype)
        outs = []
        for h in range(2):
            own = lane // HEAD_DIM == h
            aux0 = (1 - h) * HEAD_DIM
            qh = jnp.where(own, q, zero)
            gs = (_dot_nt(km_hi, qh) + _dot_nt(km_lo, qh) + _dot_nt(km_lo2, qh))[:n_off]
            pen = jnp.zeros((n_off, L), F32)
            for r in range(1, n_off):
                g_r = gs[r:r + 1, :]
                beats = past & ((gs > g_r) | ((gs == g_r) & (rr > r)))
                cnt = jnp.sum(beats.astype(F32), axis=0, keepdims=True)
                keep = (cnt < MOBA_TOPK) & (r <= i)
                pen = jnp.where((rr == r) & jnp.logical_not(keep), NEG_BIG, pen)
            far = jnp.full((n_off, L), rel_ref[REL_BUCKETS - 1, 2 * p + h], F32)
            f_hi, f_lo = _split_bf16(far)
            f_lo2 = far - f_hi.astype(F32) - f_lo.astype(F32)
            far3 = jnp.where(rr == 0, f_hi.astype(F32), jnp.where(rr == 1, f_lo.astype(F32),
                                                                  jnp.where(rr == 2, f_lo2, 0.0)))
            aux = [pen, far3, jnp.zeros((LANES - aux0 - 2 * n_off, L), F32)]
            aux = jnp.concatenate(([jnp.zeros((aux0, L), F32)] if aux0 else []) + aux, axis=0)
            q_aug = jnp.where(own, q, aux.T.astype(q.dtype))
            scores, values = [], []
            for r in range(n_off):
                ks = pl.multiple_of(jnp.maximum(i - r, 0) * L, L)
                kblk = k_ref[0, pl.ds(ks, L), :]
                values.append(v_ref[0, pl.ds(ks, L), :])
                if r == 0:
                    s = _dot_nt(qh, kblk) + bias_ref[h, 0]
                    s = jnp.where(col <= row, s, NEG_BIG)
                else:
                    hot = (lane == aux0 + r)
                    if r >= 2:
                        hot = hot | ((lane >= aux0 + n_off) & (lane < aux0 + n_off + 3))
                    k_aug = jnp.where(own, kblk, jnp.where(hot, 1.0, 0.0).astype(kblk.dtype))
                    s = _dot_nt(q_aug, k_aug)
                    if r == 1:
                        s = s + bias_ref[h, 1]
                scores.append(s)
            m = scores[0]
            for s in scores[1:]:
                m = jnp.maximum(m, s)
            m = jnp.max(m, axis=-1, keepdims=True)
            probs = [jnp.exp(s - m) for s in scores]
            l = probs[0]
            for pr in probs[1:]:
                l = l + pr
            l = jnp.sum(l, axis=-1, keepdims=True)
            acc = jnp.dot(probs[0].astype(BF16), values[0], preferred_element_type=F32)
            for pr, vblk in zip(probs[1:], values[1:]):
                acc = acc + jnp.dot(pr.astype(BF16), vblk, preferred_element_type=F32)
            outs.append(acc / l)
        o_ref[...] = jnp.where(lane < HEAD_DIM, outs[0], outs[1]).astype(o_ref.dtype)

    half = nblk // 2
    if half >= MOBA_TOPK:
        pl.when(i < half)(lambda: attend(half))
        pl.when(i >= half)(lambda: attend(nblk))
    else:
        attend(nblk)


def _moba_attention(proj, bias, rel_bias, *, batch, seq):
    T = batch * seq
    L = MOBA_BLOCK
    nblk = seq // L
    n_hp = D_MIX // LANES
    return pl.pallas_call(
        functools.partial(_moba_kernel, L=L, nblk=nblk, scale=1.0 / math.sqrt(HEAD_DIM)),
        grid=(batch, n_hp, nblk),
        in_specs=[pl.BlockSpec(memory_space=pltpu.SMEM),
                  pl.BlockSpec((1, L, LANES), lambda b, p, i: (3, b * nblk + i, p)),
                  pl.BlockSpec((1, seq, LANES), lambda b, p, i: (4, b, p)),
                  pl.BlockSpec((1, seq, LANES), lambda b, p, i: (5, b, p)),
                  pl.BlockSpec((2, 2, L, L), lambda b, p, i: (p, 0, 0, 0))],
        out_specs=pl.BlockSpec((L, LANES), lambda b, p, i: (b * nblk + i, p)),
        out_shape=jax.ShapeDtypeStruct((T, D_MIX), BF16),
        scratch_shapes=[pltpu.VMEM((nblk, LANES), F32)],
        compiler_params=pltpu.CompilerParams(
            dimension_semantics=("arbitrary", "arbitrary", "arbitrary"), vmem_limit_bytes=VMEM_LIMIT),
        name="moba_attn",
    )(rel_bias, proj, proj, proj, bias)


def _post_attn_kernel(ysb_ref, ymb_ref, gsb_ref, gmb_ref, x_ref, wsb_ref, wmb_ref, wo_ref, nrm_ref,
                      wr_ref, br_ref, h_ref, hn_ref, route_ref, cnt_ref, run_ref, *, tm):
    i = pl.program_id(0)

    @pl.when(i == 0)
    def _():
        run_ref[...] = jnp.zeros_like(run_ref)

    def gate(g_ref):
        g = jnp.concatenate([g_ref[0], g_ref[1]], axis=1).astype(F32)
        return 1.0 / (1.0 + jnp.exp(-g))

    merged = (gate(gsb_ref) * jnp.dot(ysb_ref[...], wsb_ref[...], preferred_element_type=F32)
              + gate(gmb_ref) * jnp.dot(ymb_ref[...], wmb_ref[...], preferred_element_type=F32))
    hres = x_ref[...] + jnp.dot(merged.astype(BF16), wo_ref[...], preferred_element_type=F32)
    h_ref[...] = hres
    var = jnp.mean(hres * hres, axis=-1, keepdims=True)
    hn = (hres * lax.rsqrt(var + RMS_EPS)) * nrm_ref[...]
    hn_ref[...] = hn
    a_hi, a_lo = _split_bf16(hn)
    w_hi, w_lo = _split_bf16(wr_ref[...])
    logits = (jnp.dot(a_hi, w_hi, preferred_element_type=F32)
              + jnp.dot(a_hi, w_lo, preferred_element_type=F32)
              + jnp.dot(a_lo, w_hi, preferred_element_type=F32)) + br_ref[...]
    lane = lax.broadcasted_iota(jnp.int32, (1, LANES), 1).astype(F32)
    work = jnp.where(lane < N_EXPERTS, logits, NEG_BIG)
    vals, hots = [], []
    for _ in range(TOP_K):
        mx = jnp.max(work, axis=-1, keepdims=True)
        idx = jnp.min(jnp.where(work == mx, lane, float(LANES)), axis=-1, keepdims=True)
        hot = lane == idx
        work = jnp.where(hot, NEG_BIG, work)
        vals.append(mx)
        hots.append((hot, idx))
    exps = [jnp.exp(v - vals[0]) for v in vals]
    denom = exps[0] + exps[1] + exps[2] + exps[3]
    sel = jnp.zeros((tm, LANES), F32)
    for hot, _ in hots:
        sel = jnp.where(hot, 1.0, sel)
    r = lax.broadcasted_iota(jnp.int32, (tm, tm), 0)
    c = lax.broadcasted_iota(jnp.int32, (tm, tm), 1)
    before = (c < r).astype(BF16)
    cum = jnp.dot(before, sel.astype(BF16), preferred_element_type=F32) + run_ref[0:1, :]
    slab = jnp.zeros((tm, LANES), F32)
    for k, (hot, idx) in enumerate(hots):
        rank = jnp.sum(jnp.where(hot, cum, 0.0), axis=-1, keepdims=True)
        slab = jnp.where(lane == k, idx, slab)
        slab = jnp.where(lane == TOP_K + k, exps[k] / denom, slab)
        slab = jnp.where(lane == 2 * TOP_K + k, rank, slab)
    route_ref[...] = slab
    run = run_ref[0:1, :] + jnp.sum(sel, axis=0, keepdims=True)
    run_ref[...] = jnp.broadcast_to(run, run_ref.shape)
    cnt_ref[...] = jnp.broadcast_to(run, cnt_ref.shape)


def _post_attn(ysb, ymb, proj, xt, wsb, wmb, wo, nrm_g, wr_pad, br_pad, *, tm=256):
    T, D = xt.shape
    half = D // 2
    const = lambda i: (0, 0)
    return pl.pallas_call(
        functools.partial(_post_attn_kernel, tm=tm),
        grid=(T // tm,),
        in_specs=[pl.BlockSpec((tm, D_MIX), lambda i: (i, 0)),
                  pl.BlockSpec((tm, D_MIX), lambda i: (i, 0)),
                  pl.BlockSpec((2, tm, half), lambda i: (3, i, 0)),
                  pl.BlockSpec((2, tm, half), lambda i: (4, i, 0)),
                  pl.BlockSpec((tm, D), lambda i: (i, 0)),
                  pl.BlockSpec(wsb.shape, const),
                  pl.BlockSpec(wmb.shape, const),
                  pl.BlockSpec(wo.shape, const),
                  pl.BlockSpec((1, D), const),
                  pl.BlockSpec(wr_pad.shape, const),
                  pl.BlockSpec((1, LANES), const)],
        out_specs=[pl.BlockSpec((tm, D), lambda i: (i, 0)),
                   pl.BlockSpec((tm, D), lambda i: (i, 0)),
                   pl.BlockSpec((tm, LANES), lambda i: (i, 0)),
                   pl.BlockSpec((8, LANES), const)],
        out_shape=[jax.ShapeDtypeStruct((T, D), F32),
                   jax.ShapeDtypeStruct((T, D), F32),
                   jax.ShapeDtypeStruct((T, LANES), F32),
                   jax.ShapeDtypeStruct((8, LANES), F32)],
        scratch_shapes=[pltpu.VMEM((8, LANES), F32)],
        compiler_params=pltpu.CompilerParams(dimension_semantics=("arbitrary",),
                                             vmem_limit_bytes=VMEM_LIMIT),
        name="post_attn",
    )(ysb, ymb, proj, proj, xt, wsb, wmb, wo, nrm_g.reshape(1, D), wr_pad, br_pad)


GATHER_UNROLL = 8


def _experts_kernel(te_ref, tv_ref, src_ref, nxt_ref, hn_ref, wgu_ref, bgu_ref, wd_ref, bd_ref, y_ref,
                    xbuf, wgu_s, wd_s, sem, *, d_exp, tmm):
    j = pl.program_id(0)
    n_tiles = pl.num_programs(0)
    slot = j % 2
    e = te_ref[j]
    prev = te_ref[jnp.maximum(j - 1, 0)]

    def start_gather(idx_ref, dst_slot):
        def body(r, _):
            for u in range(GATHER_UNROLL):
                rr = r * GATHER_UNROLL + u
                pltpu.make_async_copy(hn_ref.at[pl.ds(idx_ref[0, 0, rr], 1)],
                                      xbuf.at[dst_slot, pl.ds(rr, 1)], sem.at[dst_slot]).start()
            return 0
        lax.fori_loop(0, tmm // GATHER_UNROLL, body, 0)

    @pl.when((j == 0) & (tv_ref[0] > 0))
    def _():
        start_gather(src_ref, 0)

    @pl.when((j + 1 < n_tiles) & (tv_ref[jnp.minimum(j + 1, n_tiles - 1)] > 0))
    def _():
        start_gather(nxt_ref, 1 - slot)

    @pl.when((j == 0) | (e != prev))
    def _():
        wgu_s[...] = wgu_ref[0].astype(BF16)
        wd_s[...] = wd_ref[0].astype(BF16)

    @pl.when(tv_ref[j] > 0)
    def _():
        pltpu.make_async_copy(hn_ref.at[pl.ds(0, tmm)], xbuf.at[slot], sem.at[slot]).wait()
        x = xbuf[slot].astype(BF16)
        hgu = jnp.dot(x, wgu_s[...], preferred_element_type=F32) + bgu_ref[0]
        gate = jnp.minimum(hgu[:, :d_exp], SWIGLU_LIMIT)
        up = jnp.clip(hgu[:, d_exp:], -SWIGLU_LIMIT, SWIGLU_LIMIT)
        act = (up + 1.0) * (gate * (1.0 / (1.0 + jnp.exp(-SWIGLU_ALPHA * gate))))
        y_ref[...] = jnp.dot(act.astype(BF16), wd_s[...], preferred_element_type=F32) + bd_ref[0]

    @pl.when(tv_ref[j] == 0)
    def _():
        y_ref[...] = jnp.zeros_like(y_ref)


def _experts(hn, src, tile_expert, tile_valid, wgu, bgu, wd, bd, *, tmm):
    T, D = hn.shape
    E, _, d2 = wgu.shape
    d_exp = d2 // 2
    n_tiles = src.shape[0] // tmm
    src3 = src.reshape(n_tiles, 1, tmm)
    grid_spec = pltpu.PrefetchScalarGridSpec(
        num_scalar_prefetch=2,
        grid=(n_tiles,),
        in_specs=[pl.BlockSpec((1, 1, tmm), lambda j, te, tv: (j, 0, 0), memory_space=pltpu.SMEM),
                  pl.BlockSpec((1, 1, tmm), lambda j, te, tv: (jnp.minimum(j + 1, n_tiles - 1), 0, 0),
                               memory_space=pltpu.SMEM),
                  pl.BlockSpec(memory_space=pl.ANY),
                  pl.BlockSpec((1, D, d2), lambda j, te, tv: (te[j], 0, 0)),
                  pl.BlockSpec((1, 1, d2), lambda j, te, tv: (te[j], 0, 0)),
                  pl.BlockSpec((1, d_exp, D), lambda j, te, tv: (te[j], 0, 0)),
                  pl.BlockSpec((1, 1, D), lambda j, te, tv: (te[j], 0, 0))],
        out_specs=pl.BlockSpec((tmm, D), lambda j, te, tv: (j, 0)),
        scratch_shapes=[pltpu.VMEM((2, tmm, D), F32), pltpu.VMEM((D, d2), BF16), pltpu.VMEM((d_exp, D), BF16),
                        pltpu.SemaphoreType.DMA((2,))],
    )
    return pl.pallas_call(
        functools.partial(_experts_kernel, d_exp=d_exp, tmm=tmm),
        grid_spec=grid_spec,
        out_shape=jax.ShapeDtypeStruct((n_tiles * tmm, D), F32),
        compiler_params=pltpu.CompilerParams(dimension_semantics=("arbitrary",),
                                             vmem_limit_bytes=VMEM_LIMIT),
        name="experts",
    )(tile_expert, tile_valid, src3, src3, hn, wgu, bgu.reshape(E, 1, d2), wd, bd.reshape(E, 1, D))


def _combine_kernel(pos_ref, y_ref, h_ref, w_ref, g_ref, o_ref, buf, sem, *, tt):
    def row_copy(p, k, t):
        return pltpu.make_async_copy(y_ref.at[pl.ds(p, 1)], buf.at[k, pl.ds(t, 1)], sem)

    def body(j, _):
        for k in range(TOP_K):
            row_copy(pos_ref[0, 0, j * TOP_K + k], k, j).start()
        return 0

    lax.fori_loop(0, tt, body, 0)
    for k in range(TOP_K):
        pltpu.make_async_copy(y_ref.at[pl.ds(0, tt)], buf.at[k], sem).wait()
    w = w_ref[...]
    acc = h_ref[...]
    for k in range(TOP_K):
        acc = acc + w[:, TOP_K + k:TOP_K + k + 1] * buf[k]
    var = jnp.mean(acc * acc, axis=-1, keepdims=True)
    o_ref[...] = (acc * lax.rsqrt(var + RMS_EPS)) * g_ref[...]


def _combine(y, pos, h, route, g, *, tt=256):
    T, D = h.shape
    return pl.pallas_call(
        functools.partial(_combine_kernel, tt=tt),
        grid=(T // tt,),
        in_specs=[pl.BlockSpec((1, 1, tt * TOP_K), lambda i: (i, 0, 0), memory_space=pltpu.SMEM),
                  pl.BlockSpec(memory_space=pl.ANY),
                  pl.BlockSpec((tt, D), lambda i: (i, 0)),
                  pl.BlockSpec((tt, LANES), lambda i: (i, 0)),
                  pl.BlockSpec((1, D), lambda i: (0, 0))],
        out_specs=pl.BlockSpec((tt, D), lambda i: (i, 0)),
        out_shape=jax.ShapeDtypeStruct((T, D), F32),
        scratch_shapes=[pltpu.VMEM((TOP_K, tt, D), F32), pltpu.SemaphoreType.DMA(())],
        compiler_params=pltpu.CompilerParams(dimension_semantics=("arbitrary",),
                                             vmem_limit_bytes=VMEM_LIMIT),
        name="combine",
    )(pos.reshape(T // tt, 1, tt * TOP_K), y, h, route, g.reshape(1, D))


def _layer(xt, batch, seq, norm_mix_g, w_in, w_proj_sb, w_proj_moba, w_out, rel_bias, norm_ffn_g,
           w_router, b_router, w_gate_up, b_gate_up, w_down, b_down, *, tmm=512):
    T, D = xt.shape
    proj = _in_proj(xt, norm_mix_g, w_in.astype(BF16))
    ysb = _sb_attention(proj, batch=batch, seq=seq)
    ymb = _moba_attention(proj, _moba_bias(rel_bias), rel_bias, batch=batch, seq=seq)
    wr_pad = jnp.pad(w_router, ((0, 0), (0, LANES - N_EXPERTS)))
    br_pad = jnp.pad(b_router, (0, LANES - N_EXPERTS)).reshape(1, LANES)
    h, hn, route, cnt = _post_attn(ysb, ymb, proj, xt, w_proj_sb.astype(BF16), w_proj_moba.astype(BF16),
                                   w_out.astype(BF16), norm_ffn_g, wr_pad, br_pad)
    idx = route[:, 0:TOP_K].astype(jnp.int32)
    rank = route[:, 2 * TOP_K:3 * TOP_K].astype(jnp.int32)
    counts = cnt[0, :N_EXPERTS].astype(jnp.int32)
    padded = ((counts + tmm - 1) // tmm) * tmm
    ends = jnp.cumsum(padded)
    pos = ((ends - padded)[idx] + rank).reshape(-1)
    n_tiles = (T * TOP_K) // tmm + N_EXPERTS
    tile_start = jnp.arange(n_tiles, dtype=jnp.int32) * tmm
    tile_valid = (tile_start < ends[-1]).astype(jnp.int32)
    tile_expert = jnp.minimum(jnp.searchsorted(ends, tile_start, side="right"), N_EXPERTS - 1).astype(jnp.int32)
    tile_expert = jnp.where(tile_valid > 0, tile_expert, tile_expert[jnp.maximum(ends[-1] // tmm - 1, 0)])
    src = jnp.zeros((n_tiles * tmm,), jnp.int32).at[pos].set(jnp.arange(T * TOP_K, dtype=jnp.int32) // TOP_K)
    y = _experts(hn, src, tile_expert, tile_valid, w_gate_up, b_gate_up, w_down, b_down, tmm=tmm)
    return h, y, pos, route


def kernel(x, norm_mix_g, w_in, w_proj_sb, w_proj_moba, w_out, rel_bias, norm_ffn_g, w_router, b_router,
           w_gate_up, b_gate_up, w_down, b_down, norm_final_g):
    batch, seq, D = x.shape
    assert norm_mix_g.shape[0] == 1, "single-layer trunk"
    xt = x.reshape(batch * seq, D)
    h, y, pos, route = _layer(xt, batch, seq, norm_mix_g[0], w_in[0], w_proj_sb[0], w_proj_moba[0], w_out[0],
                              rel_bias, norm_ffn_g[0], w_router[0], b_router[0], w_gate_up[0], b_gate_up[0],
                              w_down[0], b_down[0])
    out = _combine(y, pos, h, route, norm_final_g)
    return out.reshape(batch, seq, D)
```

```python
import functools
import math

import numpy as np
import jax
import jax.numpy as jnp
from jax import lax
from jax.experimental import pallas as pl
from jax.experimental.pallas import tpu as pltpu

HEAD_DIM = 64
N_HEADS = 8
D_MIX = N_HEADS * HEAD_DIM
MOBA_BLOCK = 256
MOBA_TOPK = 3
REL_BUCKETS = 32
REL_MAX_DIST = 128
N_EXPERTS = 32
TOP_K = 4
SWIGLU_LIMIT = 7.0
SWIGLU_ALPHA = 1.702
RMS_EPS = 1e-6

LANES = 128
NEG_BIG = -1e30
VMEM_LIMIT = 56 * 1024 * 1024

F32 = jnp.float32
BF16 = jnp.bfloat16


def _split_bf16(a):
    hi = a.astype(BF16)
    lo = (a - hi.astype(F32)).astype(BF16)
    return hi, lo


def _dot_nt(a, b):
    return lax.dot_general(a, b, (((1,), (1,)), ((), ())), preferred_element_type=F32)


def _in_proj_kernel(x_ref, g_ref, w_ref, o_ref, *, n_slab, slab):
    x = x_ref[...]
    var = jnp.mean(x * x, axis=-1, keepdims=True)
    xn = ((x * lax.rsqrt(var + RMS_EPS)) * g_ref[...]).astype(BF16)
    for j in range(n_slab):
        o_ref[j] = jnp.dot(xn, w_ref[:, j * slab:(j + 1) * slab],
                           preferred_element_type=F32).astype(o_ref.dtype)


def _in_proj(xt, g, w_bf16, *, tm=512, slab=D_MIX):
    T, D = xt.shape
    n_slab = w_bf16.shape[1] // slab
    return pl.pallas_call(
        functools.partial(_in_proj_kernel, n_slab=n_slab, slab=slab),
        grid=(T // tm,),
        in_specs=[pl.BlockSpec((tm, D), lambda i: (i, 0)),
                  pl.BlockSpec((1, D), lambda i: (0, 0)),
                  pl.BlockSpec(w_bf16.shape, lambda i: (0, 0))],
        out_specs=pl.BlockSpec((n_slab, tm, slab), lambda i: (0, i, 0)),
        out_shape=jax.ShapeDtypeStruct((n_slab, T, slab), BF16),
        compiler_params=pltpu.CompilerParams(dimension_semantics=("arbitrary",),
                                             vmem_limit_bytes=VMEM_LIMIT),
        name="in_proj",
    )(xt, g.reshape(1, D), w_bf16)


def _sb_kernel(q_ref, k_ref, v_ref, o_ref, *, tq, tk, zscale):
    i = pl.program_id(2)
    q = q_ref[0]
    lane = lax.broadcasted_iota(jnp.int32, (1, LANES), 1)
    row = lax.broadcasted_iota(jnp.int32, (tq, tk), 0)
    col = lax.broadcasted_iota(jnp.int32, (tq, tk), 1)
    jj = lax.broadcasted_iota(jnp.int32, (2 * tk, tk), 0)
    ss = lax.broadcasted_iota(jnp.int32, (2 * tk, tk), 1)
    later = (jnp.where(jj >= tk, jj - tk, jj) >= ss).astype(BF16)
    zero = jnp.zeros((), q.dtype)
    qhs = [jnp.where(lane // HEAD_DIM == h, q, zero) for h in range(2)]
    n_sub = tq // tk

    def q_tile_keys(base, carry, diagonal):
        tiles = [(h, sub) for h in range(2) for sub in reversed(range(n_sub))]
        starts = {sub: pl.multiple_of(base + sub * tk, tk) for sub in range(n_sub)}
        z2 = {t: _dot_nt(qhs[t[0]], k_ref[0, pl.ds(starts[t[1]], tk), :]) * zscale for t in tiles}
        drop, parts = {}, {}
        for t in tiles:
            neg_abs = lax.bitcast_convert_type(
                lax.bitcast_convert_type(z2[t], jnp.uint32) | jnp.uint32(0x80000000), F32)
            d = jnp.maximum(z2[t], 0.0) + jnp.log2(1.0 + jnp.exp2(neg_abs))
            if diagonal:
                d = jnp.where((t[1] * tk + col) < row, d, 0.0)
            drop[t] = d
            parts[t] = jnp.concatenate(_split_bf16(d), axis=1)
        inner = {t: jnp.dot(parts[t], later, preferred_element_type=F32) for t in tiles}
        c = [carry[h][0] for h in range(2)]
        a = {}
        for t in tiles:
            h = t[0]
            w = jnp.exp2((z2[t] - c[h]) - inner[t])
            if diagonal:
                w = jnp.where((t[1] * tk + col) < row, w, 0.0)
            a[t] = w.astype(BF16)
            c[h] = c[h] + jnp.sum(drop[t], axis=-1, keepdims=True)
        acc = [carry[h][1] for h in range(2)]
        for t in tiles:
            acc[t[0]] = acc[t[0]] + jnp.dot(a[t], v_ref[0, pl.ds(starts[t[1]], tk), :],
                                            preferred_element_type=F32)
        return tuple((c[h], acc[h]) for h in range(2))

    init = tuple((jnp.zeros((tq, 1), F32), jnp.zeros((tq, LANES), F32)) for _ in range(2))
    carry = q_tile_keys(i * tq, init, True)
    carry = lax.fori_loop(0, i, lambda it, cr: q_tile_keys((i - 1 - it) * tq, cr, False), carry)
    o_ref[...] = jnp.where(lane < HEAD_DIM, carry[0][1], carry[1][1]).astype(o_ref.dtype)


def _sb_attention(proj, *, batch, seq, tq=512, tk=128):
    T = batch * seq
    nq = seq // tq
    n_hp = D_MIX // LANES
    return pl.pallas_call(
        functools.partial(_sb_kernel, tq=tq, tk=tk, zscale=math.log2(math.e) / math.sqrt(HEAD_DIM)),
        grid=(batch, n_hp, nq),
        in_specs=[pl.BlockSpec((1, tq, LANES), lambda b, p, i: (0, b * nq + i, p)),
                  pl.BlockSpec((1, seq, LANES), lambda b, p, i: (1, b, p)),
                  pl.BlockSpec((1, seq, LANES), lambda b, p, i: (2, b, p))],
        out_specs=pl.BlockSpec((tq, LANES), lambda b, p, i: (b * nq + i, p)),
        out_shape=jax.ShapeDtypeStruct((T, D_MIX), BF16),
        compiler_params=pltpu.CompilerParams(
            dimension_semantics=("arbitrary", "arbitrary", "arbitrary"), vmem_limit_bytes=VMEM_LIMIT),
        name="sb_attn",
    )(proj, proj, proj)


def _rel_bucket_np(dist):
    n = np.maximum(dist, 0)
    max_exact = REL_BUCKETS // 2
    nf = np.maximum(n, 1).astype(np.float64)
    large = max_exact + (np.log(nf / max_exact) / math.log(REL_MAX_DIST / max_exact)
                         * (REL_BUCKETS - max_exact)).astype(np.int32)
    large = np.minimum(large, REL_BUCKETS - 1)
    return np.where(n < max_exact, n, large).astype(np.int32)


def _bias_kernel(rel_ref, bkt_ref, o_ref):
    h = pl.program_id(0)
    for d in range(2):
        bkt = bkt_ref[d]
        acc = jnp.zeros(bkt.shape, F32)
        for b in range(REL_BUCKETS):
            acc = jnp.where(bkt == b, rel_ref[b, h], acc)
        o_ref[0, d] = acc


def _moba_bias(rel_bias):
    L = MOBA_BLOCK
    r = np.arange(L)[:, None]
    c = np.arange(L)[None, :]
    bkt = np.stack([_rel_bucket_np(r - c), _rel_bucket_np(L + r - c)])
    return pl.pallas_call(
        _bias_kernel,
        grid=(N_HEADS,),
        in_specs=[pl.BlockSpec(memory_space=pltpu.SMEM),
                  pl.BlockSpec((2, L, L), lambda h: (0, 0, 0))],
        out_specs=pl.BlockSpec((1, 2, L, L), lambda h: (h, 0, 0, 0)),
        out_shape=jax.ShapeDtypeStruct((N_HEADS, 2, L, L), F32),
        name="moba_bias",
    )(rel_bias, jnp.asarray(bkt))


def _moba_kernel(rel_ref, q_ref, k_ref, v_ref, bias_ref, o_ref, km_ref, *, L, nblk, scale):
    p = pl.program_id(1)
    i = pl.program_id(2)
    lane = lax.broadcasted_iota(jnp.int32, (1, LANES), 1)
    row = lax.broadcasted_iota(jnp.int32, (L, L), 0)
    col = lax.broadcasted_iota(jnp.int32, (L, L), 1)

    @pl.when(i == 0)
    def _():
        km_ref[...] = jnp.mean(k_ref[0].astype(F32).reshape(nblk, L, LANES), axis=1)

    def attend(n_off):
        rr = lax.broadcasted_iota(jnp.int32, (n_off, 1), 0)
        km = jnp.concatenate([km_ref[pl.ds(jnp.maximum(i - r, 0), 1), :] for r in range(n_off)], axis=0)
        km = jnp.concatenate([km, jnp.zeros((16 - n_off, LANES), F32)], axis=0)
        km_hi, km_lo = _split_bf16(km)
        km_lo2 = (km - km_hi.astype(F32) - km_lo.astype(F32)).astype(BF16)
        q = q_ref[0] * jnp.asarray(scale, q_ref.dtype)
        past = (rr >= 1) & (rr <= i)
        zero = jnp.zeros((), q.dtype)
        outs = []
        for h in range(2):
            own = lane // HEAD_DIM == h
            aux0 = (1 - h) * HEAD_DIM
            qh = jnp.where(own, q, zero)
            gs = (_dot_nt(km_hi, qh) + _dot_nt(km_lo, qh) + _dot_nt(km_lo2, qh))[:n_off]
            pen = jnp.zeros((n_off, L), F32)
            for r in range(1, n_off):
                g_r = gs[r:r + 1, :]
                beats = past & ((gs > g_r) | ((gs == g_r) & (rr > r)))
                cnt = jnp.sum(beats.astype(F32), axis=0, keepdims=True)
                keep = (cnt < MOBA_TOPK) & (r <= i)
                pen = jnp.where((rr == r) & jnp.logical_not(keep), NEG_BIG, pen)
            far = jnp.full((n_off, L), rel_ref[REL_BUCKETS - 1, 2 * p + h], F32)
            f_hi, f_lo = _split_bf16(far)
            f_lo2 = far - f_hi.astype(F32) - f_lo.astype(F32)
            far3 = jnp.where(rr == 0, f_hi.astype(F32), jnp.where(rr == 1, f_lo.astype(F32),
                                                                  jnp.where(rr == 2, f_lo2, 0.0)))
            aux = [pen, far3, jnp.zeros((LANES - aux0 - 2 * n_off, L), F32)]
            aux = jnp.concatenate(([jnp.zeros((aux0, L), F32)] if aux0 else []) + aux, axis=0)
            q_aug = jnp.where(own, q, aux.T.astype(q.dtype))
            scores, values = [], []
            for r in range(n_off):
                ks = pl.multiple_of(jnp.maximum(i - r, 0) * L, L)
                kblk = k_ref[0, pl.ds(ks, L), :]
                values.append(v_ref[0, pl.ds(ks, L), :])
                if r == 0:
                    s = _dot_nt(qh, kblk) + bias_ref[h, 0]
                    s = jnp.where(col <= row, s, NEG_BIG)
                else:
                    hot = (lane == aux0 + r)
                    if r >= 2:
                        hot = hot | ((lane >= aux0 + n_off) & (lane < aux0 + n_off + 3))
                    k_aug = jnp.where(own, kblk, jnp.where(hot, 1.0, 0.0).astype(kblk.dtype))
                    s = _dot_nt(q_aug, k_aug)
                    if r == 1:
                        s = s + bias_ref[h, 1]
                scores.append(s)
            m = scores[0]
            for s in scores[1:]:
                m = jnp.maximum(m, s)
            m = jnp.max(m, axis=-1, keepdims=True)
            probs = [jnp.exp(s - m) for s in scores]
            l = probs[0]
            for pr in probs[1:]:
                l = l + pr
            l = jnp.sum(l, axis=-1, keepdims=True)
            acc = jnp.dot(probs[0].astype(BF16), values[0], preferred_element_type=F32)
            for pr, vblk in zip(probs[1:], values[1:]):
                acc = acc + jnp.dot(pr.astype(BF16), vblk, preferred_element_type=F32)
            outs.append(acc / l)
        o_ref[...] = jnp.where(lane < HEAD_DIM, outs[0], outs[1]).astype(o_ref.dtype)

    half = nblk // 2
    if half >= MOBA_TOPK:
        pl.when(i < half)(lambda: attend(half))
        pl.when(i >= half)(lambda: attend(nblk))
    else:
        attend(nblk)


def _moba_attention(proj, bias, rel_bias, *, batch, seq):
    T = batch * seq
    L = MOBA_BLOCK
    nblk = seq // L
    n_hp = D_MIX // LANES
    return pl.pallas_call(
        functools.partial(_moba_kernel, L=L, nblk=nblk, scale=1.0 / math.sqrt(HEAD_DIM)),
        grid=(batch, n_hp, nblk),
        in_specs=[pl.BlockSpec(memory_space=pltpu.SMEM),
                  pl.BlockSpec((1, L, LANES), lambda b, p, i: (3, b * nblk + i, p)),
                  pl.BlockSpec((1, seq, LANES), lambda b, p, i: (4, b, p)),
                  pl.BlockSpec((1, seq, LANES), lambda b, p, i: (5, b, p)),
                  pl.BlockSpec((2, 2, L, L), lambda b, p, i: (p, 0, 0, 0))],
        out_specs=pl.BlockSpec((L, LANES), lambda b, p, i: (b * nblk + i, p)),
        out_shape=jax.ShapeDtypeStruct((T, D_MIX), BF16),
        scratch_shapes=[pltpu.VMEM((nblk, LANES), F32)],
        compiler_params=pltpu.CompilerParams(
            dimension_semantics=("arbitrary", "arbitrary", "arbitrary"), vmem_limit_bytes=VMEM_LIMIT),
        name="moba_attn",
    )(rel_bias, proj, proj, proj, bias)


WIN_ROWS = 16
WINS_PER_TILE = 32


def _local_slots(tm):
    return tm * TOP_K + N_EXPERTS * WIN_ROWS


def _post_attn_kernel(ysb_ref, ymb_ref, gsb_ref, gmb_ref, x_ref, wsb_ref, wmb_ref, wo_ref, nrm_ref,
                      wr_ref, br_ref, h_ref, xl_ref, route_ref, cnt_ref, *, tm, chunk):
    def gate(g_ref):
        g = jnp.concatenate([g_ref[0], g_ref[1]], axis=1).astype(F32)
        return 1.0 / (1.0 + jnp.exp(-g))

    merged = (gate(gsb_ref) * jnp.dot(ysb_ref[...], wsb_ref[...], preferred_element_type=F32)
              + gate(gmb_ref) * jnp.dot(ymb_ref[...], wmb_ref[...], preferred_element_type=F32))
    hres = x_ref[...] + jnp.dot(merged.astype(BF16), wo_ref[...], preferred_element_type=F32)
    h_ref[...] = hres
    var = jnp.mean(hres * hres, axis=-1, keepdims=True)
    hn = (hres * lax.rsqrt(var + RMS_EPS)) * nrm_ref[...]
    a_hi, a_lo = _split_bf16(hn)
    w_hi, w_lo = _split_bf16(wr_ref[...])
    logits = (jnp.dot(a_hi, w_hi, preferred_element_type=F32)
              + jnp.dot(a_hi, w_lo, preferred_element_type=F32)
              + jnp.dot(a_lo, w_hi, preferred_element_type=F32)) + br_ref[...]
    lane = lax.broadcasted_iota(jnp.int32, (1, LANES), 1).astype(F32)
    work = jnp.where(lane < N_EXPERTS, logits, NEG_BIG)
    vals, hots = [], []
    for _ in range(TOP_K):
        mx = jnp.max(work, axis=-1, keepdims=True)
        idx = jnp.min(jnp.where(work == mx, lane, float(LANES)), axis=-1, keepdims=True)
        hot = lane == idx
        work = jnp.where(hot, NEG_BIG, work)
        vals.append(mx)
        hots.append(hot)
    exps = [jnp.exp(v - vals[0]) for v in vals]
    denom = exps[0] + exps[1] + exps[2] + exps[3]
    sel = jnp.zeros((tm, LANES), F32)
    for hot in hots:
        sel = jnp.where(hot, 1.0, sel)
    r = lax.broadcasted_iota(jnp.int32, (tm, tm), 0)
    c = lax.broadcasted_iota(jnp.int32, (tm, tm), 1)
    earlier = jnp.dot((c < r).astype(BF16), sel.astype(BF16), preferred_element_type=F32)
    cnt = jnp.sum(sel, axis=0, keepdims=True)
    padded = jnp.floor((cnt + (WIN_ROWS - 1)) * (1.0 / WIN_ROWS)) * WIN_ROWS
    er = lax.broadcasted_iota(jnp.int32, (LANES, LANES), 0)
    ec = lax.broadcasted_iota(jnp.int32, (LANES, LANES), 1)
    lower = jnp.dot(jnp.broadcast_to(padded, (8, LANES)).astype(BF16), (er < ec).astype(BF16),
                    preferred_element_type=F32)[0:1]
    slot = lower + earlier
    slab = jnp.zeros((tm, LANES), F32)
    for k, hot in enumerate(hots):
        slab = jnp.where(lane == k, exps[k] / denom, slab)
        slab = jnp.where(lane == TOP_K + k, jnp.sum(jnp.where(hot, slot, 0.0), axis=-1, keepdims=True), slab)
    route_ref[...] = slab
    cnt_ref[0] = jnp.broadcast_to(cnt, (8, LANES))
    slots_t = slab.T
    hn_bf = hn.astype(BF16)
    for ci in range(_local_slots(tm) // chunk):
        rid = (lax.broadcasted_iota(jnp.int32, (chunk, 1), 0) + ci * chunk).astype(F32)
        onehot = rid == slots_t[TOP_K:TOP_K + 1, :]
        for k in range(1, TOP_K):
            onehot = onehot | (rid == slots_t[TOP_K + k:TOP_K + k + 1, :])
        xl_ref[ci * chunk:(ci + 1) * chunk, :] = jnp.dot(
            jnp.where(onehot, 1.0, 0.0).astype(BF16), hn_bf, preferred_element_type=F32).astype(xl_ref.dtype)


def _post_attn(ysb, ymb, proj, xt, wsb, wmb, wo, nrm_g, wr_pad, br_pad, *, tm):
    T, D = xt.shape
    half = D // 2
    n_tt = T // tm
    ls = _local_slots(tm)
    const = lambda i: (0, 0)
    return pl.pallas_call(
        functools.partial(_post_attn_kernel, tm=tm, chunk=256),
        grid=(n_tt,),
        in_specs=[pl.BlockSpec((tm, D_MIX), lambda i: (i, 0)),
                  pl.BlockSpec((tm, D_MIX), lambda i: (i, 0)),
                  pl.BlockSpec((2, tm, half), lambda i: (3, i, 0)),
                  pl.BlockSpec((2, tm, half), lambda i: (4, i, 0)),
                  pl.BlockSpec((tm, D), lambda i: (i, 0)),
                  pl.BlockSpec(wsb.shape, const),
                  pl.BlockSpec(wmb.shape, const),
                  pl.BlockSpec(wo.shape, const),
                  pl.BlockSpec((1, D), const),
                  pl.BlockSpec(wr_pad.shape, const),
                  pl.BlockSpec((1, LANES), const)],
        out_specs=[pl.BlockSpec((tm, D), lambda i: (i, 0)),
                   pl.BlockSpec((ls, D), lambda i: (i, 0)),
                   pl.BlockSpec((tm, LANES), lambda i: (i, 0)),
                   pl.BlockSpec((1, 8, LANES), lambda i: (i, 0, 0))],
        out_shape=[jax.ShapeDtypeStruct((T, D), F32),
                   jax.ShapeDtypeStruct((n_tt * ls, D), BF16),
                   jax.ShapeDtypeStruct((T, LANES), F32),
                   jax.ShapeDtypeStruct((n_tt, 8, LANES), F32)],
        compiler_params=pltpu.CompilerParams(dimension_semantics=("arbitrary",),
                                             vmem_limit_bytes=VMEM_LIMIT),
        name="post_attn",
    )(ysb, ymb, proj, proj, xt, wsb, wmb, wo, nrm_g.reshape(1, D), wr_pad, br_pad)


def _window_plan(counts, tm):
    n_tt, n_exp = counts.shape
    lw_tile = _local_slots(tm) // WIN_ROWS
    n_tiles = -(-(n_tt * (tm * TOP_K // WIN_ROWS + n_exp)) // WINS_PER_TILE) + n_exp
    nw = (counts + WIN_ROWS - 1) // WIN_ROWS
    local_start = jnp.cumsum(nw, axis=1) - nw
    per_expert = jnp.sum(nw, axis=0)
    per_expert_pad = -(-per_expert // WINS_PER_TILE) * WINS_PER_TILE
    expert_end = jnp.cumsum(per_expert_pad)
    sorted_start = (expert_end - per_expert_pad)[None, :] + jnp.cumsum(nw, axis=0) - nw
    run_start = sorted_start.T.reshape(-1)
    run_len = nw.T.reshape(-1)
    run_src = (jnp.arange(n_tt, dtype=jnp.int32)[:, None] * lw_tile + local_start).T.reshape(-1)
    g = jnp.arange(n_tiles * WINS_PER_TILE, dtype=jnp.int32)
    run = jnp.minimum(jnp.sum((run_start + run_len)[None, :] <= g[:, None], axis=1), run_start.shape[0] - 1)
    off = g - run_start[run]
    src_win = jnp.where((off >= 0) & (off < run_len[run]), run_src[run] + off, 0)
    lw = jnp.arange(lw_tile, dtype=jnp.int32)
    e_of = jnp.sum((local_start + nw)[:, None, :] <= lw[None, :, None], axis=2)
    e_idx = jnp.minimum(e_of, n_exp - 1)
    off_l = lw[None, :] - jnp.take_along_axis(local_start, e_idx, axis=1)
    ok = (e_of < n_exp) & (off_l >= 0) & (off_l < jnp.take_along_axis(nw, e_idx, axis=1))
    back_win = jnp.where(ok, jnp.take_along_axis(sorted_start, e_idx, axis=1) + off_l, 0)
    tile_first = jnp.arange(n_tiles, dtype=jnp.int32) * WINS_PER_TILE
    tile_valid = tile_first < expert_end[-1]
    tile_expert = jnp.minimum(jnp.sum(expert_end[None, :] <= tile_first[:, None], axis=1), n_exp - 1)
    last = tile_expert[jnp.maximum(expert_end[-1] // WINS_PER_TILE - 1, 0)]
    tile_expert = jnp.where(tile_valid, tile_expert, last)
    return (src_win.astype(jnp.int32), back_win.astype(jnp.int32), tile_expert.astype(jnp.int32),
            tile_valid.astype(jnp.int32))


def _experts_kernel(te_ref, tv_ref, src_ref, nxt_ref, xl_ref, wgu_ref, bgu_ref, wd_ref, bd_ref, y_ref,
                    xbuf, wgu_s, wd_s, sem, *, d_exp):
    j = pl.program_id(0)
    n_tiles = pl.num_programs(0)
    slot = j % 2
    e = te_ref[j]
    prev = te_ref[jnp.maximum(j - 1, 0)]

    def window_copy(win, dst_slot, w):
        return pltpu.make_async_copy(xl_ref.at[win], xbuf.at[dst_slot, w], sem.at[dst_slot])

    def start_gather(idx_ref, dst_slot):
        for w in range(WINS_PER_TILE):
            window_copy(idx_ref[0, 0, w], dst_slot, w).start()

    @pl.when((j == 0) & (tv_ref[0] > 0))
    def _():
        start_gather(src_ref, 0)

    @pl.when((j + 1 < n_tiles) & (tv_ref[jnp.minimum(j + 1, n_tiles - 1)] > 0))
    def _():
        start_gather(nxt_ref, 1 - slot)

    @pl.when((j == 0) | (e != prev))
    def _():
        wgu_s[...] = wgu_ref[0].astype(BF16)
        wd_s[...] = wd_ref[0].astype(BF16)

    @pl.when(tv_ref[j] > 0)
    def _():
        for w in range(WINS_PER_TILE):
            window_copy(0, slot, w).wait()
        x = xbuf[slot].reshape(WINS_PER_TILE * WIN_ROWS, xbuf.shape[-1])
        hgu = jnp.dot(x, wgu_s[...], preferred_element_type=F32) + bgu_ref[0]
        gate = jnp.minimum(hgu[:, :d_exp], SWIGLU_LIMIT)
        up = jnp.clip(hgu[:, d_exp:], -SWIGLU_LIMIT, SWIGLU_LIMIT)
        act = (up + 1.0) * (gate * (1.0 / (1.0 + jnp.exp(-SWIGLU_ALPHA * gate))))
        y = jnp.dot(act.astype(BF16), wd_s[...], preferred_element_type=F32) + bd_ref[0]
        y_ref[...] = y.astype(y_ref.dtype)

    @pl.when(tv_ref[j] == 0)
    def _():
        y_ref[...] = jnp.zeros_like(y_ref)


def _experts(xl, src_win, tile_expert, tile_valid, wgu, bgu, wd, bd):
    D = xl.shape[1]
    E, _, d2 = wgu.shape
    d_exp = d2 // 2
    n_tiles = tile_expert.shape[0]
    tmm = WINS_PER_TILE * WIN_ROWS
    src3 = src_win.reshape(n_tiles, 1, WINS_PER_TILE)
    grid_spec = pltpu.PrefetchScalarGridSpec(
        num_scalar_prefetch=2,
        grid=(n_tiles,),
        in_specs=[pl.BlockSpec((1, 1, WINS_PER_TILE), lambda j, te, tv: (j, 0, 0), memory_space=pltpu.SMEM),
                  pl.BlockSpec((1, 1, WINS_PER_TILE), lambda j, te, tv: (jnp.minimum(j + 1, n_tiles - 1), 0, 0),
                               memory_space=pltpu.SMEM),
                  pl.BlockSpec(memory_space=pl.ANY),
                  pl.BlockSpec((1, D, d2), lambda j, te, tv: (te[j], 0, 0)),
                  pl.BlockSpec((1, 1, d2), lambda j, te, tv: (te[j], 0, 0)),
                  pl.BlockSpec((1, d_exp, D), lambda j, te, tv: (te[j], 0, 0)),
                  pl.BlockSpec((1, 1, D), lambda j, te, tv: (te[j], 0, 0))],
        out_specs=pl.BlockSpec((tmm, D), lambda j, te, tv: (j, 0)),
        scratch_shapes=[pltpu.VMEM((2, WINS_PER_TILE, WIN_ROWS, D), BF16), pltpu.VMEM((D, d2), BF16),
                        pltpu.VMEM((d_exp, D), BF16), pltpu.SemaphoreType.DMA((2,))],
    )
    return pl.pallas_call(
        functools.partial(_experts_kernel, d_exp=d_exp),
        grid_spec=grid_spec,
        out_shape=jax.ShapeDtypeStruct((n_tiles * tmm, D), BF16),
        compiler_params=pltpu.CompilerParams(dimension_semantics=("arbitrary",),
                                             vmem_limit_bytes=VMEM_LIMIT),
        name="experts",
    )(tile_expert, tile_valid, src3, src3, xl.reshape(-1, WIN_ROWS, D), wgu, bgu.reshape(E, 1, d2), wd,
      bd.reshape(E, 1, D))


def _combine_kernel(back_ref, y_ref, h_ref, route_ref, g_ref, o_ref, ybuf, sem, *, tm, chunk):
    n_win = ybuf.shape[0]

    def window_copy(win, w):
        return pltpu.make_async_copy(y_ref.at[win], ybuf.at[w], sem)

    for w in range(n_win):
        window_copy(back_ref[0, 0, w], w).start()
    route = route_ref[...]
    weights = [route[:, k:k + 1] for k in range(TOP_K)]
    slots = [route[:, TOP_K + k:TOP_K + k + 1] for k in range(TOP_K)]
    for w in range(n_win):
        window_copy(0, w).wait()
    acc = h_ref[...]
    for ci in range(n_win * WIN_ROWS // chunk):
        sid = (lax.broadcasted_iota(jnp.int32, (1, chunk), 1) + ci * chunk).astype(F32)
        wmat = jnp.zeros((tm, chunk), F32)
        for k in range(TOP_K):
            wmat = jnp.where(sid == slots[k], weights[k], wmat)
        hi, lo = _split_bf16(wmat)
        rows = ybuf[ci * chunk // WIN_ROWS:(ci + 1) * chunk // WIN_ROWS].reshape(chunk, ybuf.shape[-1])
        acc = acc + jnp.dot(hi, rows, preferred_element_type=F32) + jnp.dot(lo, rows, preferred_element_type=F32)
    var = jnp.mean(acc * acc, axis=-1, keepdims=True)
    o_ref[...] = (acc * lax.rsqrt(var + RMS_EPS)) * g_ref[...]


def _combine(y, back_win, h, route, g, *, tm):
    T, D = h.shape
    n_tt, n_win = back_win.shape
    return pl.pallas_call(
        functools.partial(_combine_kernel, tm=tm, chunk=256),
        grid=(n_tt,),
        in_specs=[pl.BlockSpec((1, 1, n_win), lambda i: (i, 0, 0), memory_space=pltpu.SMEM),
                  pl.BlockSpec(memory_space=pl.ANY),
                  pl.BlockSpec((tm, D), lambda i: (i, 0)),
                  pl.BlockSpec((tm, LANES), lambda i: (i, 0)),
                  pl.BlockSpec((1, D), lambda i: (0, 0))],
        out_specs=pl.BlockSpec((tm, D), lambda i: (i, 0)),
        out_shape=jax.ShapeDtypeStruct((T, D), F32),
        scratch_shapes=[pltpu.VMEM((n_win, WIN_ROWS, D), BF16), pltpu.SemaphoreType.DMA(())],
        compiler_params=pltpu.CompilerParams(dimension_semantics=("arbitrary",),
                                             vmem_limit_bytes=VMEM_LIMIT),
        name="combine",
    )(back_win.reshape(n_tt, 1, n_win), y.reshape(-1, WIN_ROWS, D), h, route, g.reshape(1, D))


def kernel(x, norm_mix_g, w_in, w_proj_sb, w_proj_moba, w_out, rel_bias, norm_ffn_g, w_router, b_router,
           w_gate_up, b_gate_up, w_down, b_down, norm_final_g):
    batch, seq, D = x.shape
    assert norm_mix_g.shape[0] == 1, "single-layer trunk"
    tm = 256
    xt = x.reshape(batch * seq, D)
    proj = _in_proj(xt, norm_mix_g[0], w_in[0].astype(BF16))
    ysb = _sb_attention(proj, batch=batch, seq=seq)
    ymb = _moba_attention(proj, _moba_bias(rel_bias), rel_bias, batch=batch, seq=seq)
    wr_pad = jnp.pad(w_router[0], ((0, 0), (0, LANES - N_EXPERTS)))
    br_pad = jnp.pad(b_router[0], (0, LANES - N_EXPERTS)).reshape(1, LANES)
    h, xl, route, cnt = _post_attn(ysb, ymb, proj, xt, w_proj_sb[0].astype(BF16), w_proj_moba[0].astype(BF16),
                                   w_out[0].astype(BF16), norm_ffn_g[0], wr_pad, br_pad, tm=tm)
    src_win, back_win, tile_expert, tile_valid = _window_plan(cnt[:, 0, :N_EXPERTS].astype(jnp.int32), tm)
    y = _experts(xl, src_win, tile_expert, tile_valid, w_gate_up[0], b_gate_up[0], w_down[0], b_down[0])
    out = _combine(y, back_win, h, route, norm_final_g, tm=tm)
    return out.reshape(batch, seq, D)
```

```python
import functools
import math

import numpy as np
import jax
import jax.numpy as jnp
from jax import lax
from jax.experimental import pallas as pl
from jax.experimental.pallas import tpu as pltpu

HEAD_DIM = 64
N_HEADS = 8
D_MIX = N_HEADS * HEAD_DIM
MOBA_BLOCK = 256
MOBA_TOPK = 3
REL_BUCKETS = 32
REL_MAX_DIST = 128
N_EXPERTS = 32
TOP_K = 4
SWIGLU_LIMIT = 7.0
SWIGLU_ALPHA = 1.702
RMS_EPS = 1e-6

LANES = 128
NEG_BIG = -1e30
VMEM_LIMIT = 56 * 1024 * 1024

F32 = jnp.float32
BF16 = jnp.bfloat16


def _split_bf16(a):
    hi = a.astype(BF16)
    lo = (a - hi.astype(F32)).astype(BF16)
    return hi, lo


def _dot_nt(a, b):
    return lax.dot_general(a, b, (((1,), (1,)), ((), ())), preferred_element_type=F32)


def _in_proj_kernel(x_ref, g_ref, w_ref, o_ref, *, n_slab, slab):
    x = x_ref[...]
    var = jnp.mean(x * x, axis=-1, keepdims=True)
    xn = ((x * lax.rsqrt(var + RMS_EPS)) * g_ref[...]).astype(BF16)
    for j in range(n_slab):
        o_ref[j] = jnp.dot(xn, w_ref[:, j * slab:(j + 1) * slab],
                           preferred_element_type=F32).astype(o_ref.dtype)


def _in_proj(xt, g, w_bf16, *, tm=512, slab=D_MIX):
    T, D = xt.shape
    n_slab = w_bf16.shape[1] // slab
    return pl.pallas_call(
        functools.partial(_in_proj_kernel, n_slab=n_slab, slab=slab),
        grid=(T // tm,),
        in_specs=[pl.BlockSpec((tm, D), lambda i: (i, 0)),
                  pl.BlockSpec((1, D), lambda i: (0, 0)),
                  pl.BlockSpec(w_bf16.shape, lambda i: (0, 0))],
        out_specs=pl.BlockSpec((n_slab, tm, slab), lambda i: (0, i, 0)),
        out_shape=jax.ShapeDtypeStruct((n_slab, T, slab), BF16),
        compiler_params=pltpu.CompilerParams(dimension_semantics=("arbitrary",),
                                             vmem_limit_bytes=VMEM_LIMIT),
        name="in_proj",
    )(xt, g.reshape(1, D), w_bf16)


def _sb_kernel(q_ref, k_ref, v_ref, o_ref, *, tq, tk, zscale):
    i = pl.program_id(2)
    q = q_ref[0]
    lane = lax.broadcasted_iota(jnp.int32, (1, LANES), 1)
    row = lax.broadcasted_iota(jnp.int32, (tq, tk), 0)
    col = lax.broadcasted_iota(jnp.int32, (tq, tk), 1)
    jj = lax.broadcasted_iota(jnp.int32, (2 * tk, tk), 0)
    ss = lax.broadcasted_iota(jnp.int32, (2 * tk, tk), 1)
    later = (jnp.where(jj >= tk, jj - tk, jj) >= ss).astype(BF16)
    zero = jnp.zeros((), q.dtype)
    qhs = [jnp.where(lane // HEAD_DIM == h, q, zero) for h in range(2)]
    n_sub = tq // tk

    def q_tile_keys(base, carry, diagonal):
        tiles = [(h, sub) for h in range(2) for sub in reversed(range(n_sub))]
        starts = {sub: pl.multiple_of(base + sub * tk, tk) for sub in range(n_sub)}
        z2 = {t: _dot_nt(qhs[t[0]], k_ref[0, pl.ds(starts[t[1]], tk), :]) * zscale for t in tiles}
        drop, parts = {}, {}
        for t in tiles:
            neg_abs = lax.bitcast_convert_type(
                lax.bitcast_convert_type(z2[t], jnp.uint32) | jnp.uint32(0x80000000), F32)
            d = jnp.maximum(z2[t], 0.0) + jnp.log2(1.0 + jnp.exp2(neg_abs))
            if diagonal:
                d = jnp.where((t[1] * tk + col) < row, d, 0.0)
            drop[t] = d
            parts[t] = jnp.concatenate(_split_bf16(d), axis=1)
        inner = {t: jnp.dot(parts[t], later, preferred_element_type=F32) for t in tiles}
        c = [carry[h][0] for h in range(2)]
        a = {}
        for t in tiles:
            h = t[0]
            w = jnp.exp2((z2[t] - c[h]) - inner[t])
            if diagonal:
                w = jnp.where((t[1] * tk + col) < row, w, 0.0)
            a[t] = w.astype(BF16)
            c[h] = c[h] + jnp.sum(drop[t], axis=-1, keepdims=True)
        acc = [carry[h][1] for h in range(2)]
        for t in tiles:
            acc[t[0]] = acc[t[0]] + jnp.dot(a[t], v_ref[0, pl.ds(starts[t[1]], tk), :],
                                            preferred_element_type=F32)
        return tuple((c[h], acc[h]) for h in range(2))

    init = tuple((jnp.zeros((tq, 1), F32), jnp.zeros((tq, LANES), F32)) for _ in range(2))
    carry = q_tile_keys(i * tq, init, True)
    carry = lax.fori_loop(0, i, lambda it, cr: q_tile_keys((i - 1 - it) * tq, cr, False), carry)
    o_ref[...] = jnp.where(lane < HEAD_DIM, carry[0][1], carry[1][1]).astype(o_ref.dtype)


def _sb_attention(proj, *, batch, seq, tq=512, tk=128):
    T = batch * seq
    nq = seq // tq
    n_hp = D_MIX // LANES
    return pl.pallas_call(
        functools.partial(_sb_kernel, tq=tq, tk=tk, zscale=math.log2(math.e) / math.sqrt(HEAD_DIM)),
        grid=(batch, n_hp, nq),
        in_specs=[pl.BlockSpec((1, tq, LANES), lambda b, p, i: (0, b * nq + i, p)),
                  pl.BlockSpec((1, seq, LANES), lambda b, p, i: (1, b, p)),
                  pl.BlockSpec((1, seq, LANES), lambda b, p, i: (2, b, p))],
        out_specs=pl.BlockSpec((tq, LANES), lambda b, p, i: (b * nq + i, p)),
        out_shape=jax.ShapeDtypeStruct((T, D_MIX), BF16),
        compiler_params=pltpu.CompilerParams(
            dimension_semantics=("arbitrary", "arbitrary", "arbitrary"), vmem_limit_bytes=VMEM_LIMIT),
        name="sb_attn",
    )(proj, proj, proj)


def _rel_bucket_np(dist):
    n = np.maximum(dist, 0)
    max_exact = REL_BUCKETS // 2
    nf = np.maximum(n, 1).astype(np.float64)
    large = max_exact + (np.log(nf / max_exact) / math.log(REL_MAX_DIST / max_exact)
                         * (REL_BUCKETS - max_exact)).astype(np.int32)
    large = np.minimum(large, REL_BUCKETS - 1)
    return np.where(n < max_exact, n, large).astype(np.int32)


def _bias_kernel(rel_ref, bkt_ref, o_ref):
    h = pl.program_id(0)
    for d in range(2):
        bkt = bkt_ref[d]
        acc = jnp.zeros(bkt.shape, F32)
        for b in range(REL_BUCKETS):
            acc = jnp.where(bkt == b, rel_ref[b, h], acc)
        o_ref[0, d] = acc


def _moba_bias(rel_bias):
    L = MOBA_BLOCK
    r = np.arange(L)[:, None]
    c = np.arange(L)[None, :]
    bkt = np.stack([_rel_bucket_np(r - c), _rel_bucket_np(L + r - c)])
    return pl.pallas_call(
        _bias_kernel,
        grid=(N_HEADS,),
        in_specs=[pl.BlockSpec(memory_space=pltpu.SMEM),
                  pl.BlockSpec((2, L, L), lambda h: (0, 0, 0))],
        out_specs=pl.BlockSpec((1, 2, L, L), lambda h: (h, 0, 0, 0)),
        out_shape=jax.ShapeDtypeStruct((N_HEADS, 2, L, L), F32),
        name="moba_bias",
    )(rel_bias, jnp.asarray(bkt))


def _moba_kernel(rel_ref, q_ref, k_ref, v_ref, bias_ref, o_ref, km_ref, *, L, nblk, scale):
    p = pl.program_id(1)
    i = pl.program_id(2)
    lane = lax.broadcasted_iota(jnp.int32, (1, LANES), 1)
    row = lax.broadcasted_iota(jnp.int32, (L, L), 0)
    col = lax.broadcasted_iota(jnp.int32, (L, L), 1)

    @pl.when(i == 0)
    def _():
        km_ref[...] = jnp.mean(k_ref[0].astype(F32).reshape(nblk, L, LANES), axis=1)

    def attend(n_off):
        rr = lax.broadcasted_iota(jnp.int32, (n_off, 1), 0)
        km = jnp.concatenate([km_ref[pl.ds(jnp.maximum(i - r, 0), 1), :] for r in range(n_off)], axis=0)
        km = jnp.concatenate([km, jnp.zeros((16 - n_off, LANES), F32)], axis=0)
        km_hi, km_lo = _split_bf16(km)
        km_lo2 = (km - km_hi.astype(F32) - km_lo.astype(F32)).astype(BF16)
        q = q_ref[0] * jnp.asarray(scale, q_ref.dtype)
        past = (rr >= 1) & (rr <= i)
        zero = jnp.zeros((), q.dtype)
        outs = []
        for h in range(2):
            own = lane // HEAD_DIM == h
            aux0 = (1 - h) * HEAD_DIM
            qh = jnp.where(own, q, zero)
            gs = (_dot_nt(km_hi, qh) + _dot_nt(km_lo, qh) + _dot_nt(km_lo2, qh))[:n_off]
            pen = jnp.zeros((n_off, L), F32)
            for r in range(1, n_off):
                g_r = gs[r:r + 1, :]
                beats = past & ((gs > g_r) | ((gs == g_r) & (rr > r)))
                cnt = jnp.sum(beats.astype(F32), axis=0, keepdims=True)
                keep = (cnt < MOBA_TOPK) & (r <= i)
                pen = jnp.where((rr == r) & jnp.logical_not(keep), NEG_BIG, pen)
            far = jnp.full((n_off, L), rel_ref[REL_BUCKETS - 1, 2 * p + h], F32)
            f_hi, f_lo = _split_bf16(far)
            f_lo2 = far - f_hi.astype(F32) - f_lo.astype(F32)
            far3 = jnp.where(rr == 0, f_hi.astype(F32), jnp.where(rr == 1, f_lo.astype(F32),
                                                                  jnp.where(rr == 2, f_lo2, 0.0)))
            aux = [pen, far3, jnp.zeros((LANES - aux0 - 2 * n_off, L), F32)]
            aux = jnp.concatenate(([jnp.zeros((aux0, L), F32)] if aux0 else []) + aux, axis=0)
            q_aug = jnp.where(own, q, aux.T.astype(q.dtype))
            scores, values = [], []
            for r in range(n_off):
                ks = pl.multiple_of(jnp.maximum(i - r, 0) * L, L)
                kblk = k_ref[0, pl.ds(ks, L), :]
                values.append(v_ref[0, pl.ds(ks, L), :])
                if r == 0:
                    s = _dot_nt(qh, kblk) + bias_ref[h, 0]
                    s = jnp.where(col <= row, s, NEG_BIG)
                else:
                    hot = (lane == aux0 + r)
                    if r >= 2:
                        hot = hot | ((lane >= aux0 + n_off) & (lane < aux0 + n_off + 3))
                    k_aug = jnp.where(own, kblk, jnp.where(hot, 1.0, 0.0).astype(kblk.dtype))
                    s = _dot_nt(q_aug, k_aug)
                    if r == 1:
                        s = s + bias_ref[h, 1]
                scores.append(s)
            m = scores[0]
            for s in scores[1:]:
                m = jnp.maximum(m, s)
            m = jnp.max(m, axis=-1, keepdims=True)
            probs = [jnp.exp(s - m) for s in scores]
            l = probs[0]
            for pr in probs[1:]:
                l = l + pr
            l = jnp.sum(l, axis=-1, keepdims=True)
            acc = jnp.dot(probs[0].astype(BF16), values[0], preferred_element_type=F32)
            for pr, vblk in zip(probs[1:], values[1:]):
                acc = acc + jnp.dot(pr.astype(BF16), vblk, preferred_element_type=F32)
            outs.append(acc / l)
        o_ref[...] = jnp.where(lane < HEAD_DIM, outs[0], outs[1]).astype(o_ref.dtype)

    half = nblk // 2
    if half >= MOBA_TOPK:
        pl.when(i < half)(lambda: attend(half))
        pl.when(i >= half)(lambda: attend(nblk))
    else:
        attend(nblk)


def _moba_attention(proj, bias, rel_bias, *, batch, seq):
    T = batch * seq
    L = MOBA_BLOCK
    nblk = seq // L
    n_hp = D_MIX // LANES
    return pl.pallas_call(
        functools.partial(_moba_kernel, L=L, nblk=nblk, scale=1.0 / math.sqrt(HEAD_DIM)),
        grid=(batch, n_hp, nblk),
        in_specs=[pl.BlockSpec(memory_space=pltpu.SMEM),
                  pl.BlockSpec((1, L, LANES), lambda b, p, i: (3, b * nblk + i, p)),
                  pl.BlockSpec((1, seq, LANES), lambda b, p, i: (4, b, p)),
                  pl.BlockSpec((1, seq, LANES), lambda b, p, i: (5, b, p)),
                  pl.BlockSpec((2, 2, L, L), lambda b, p, i: (p, 0, 0, 0))],
        out_specs=pl.BlockSpec((L, LANES), lambda b, p, i: (b * nblk + i, p)),
        out_shape=jax.ShapeDtypeStruct((T, D_MIX), BF16),
        scratch_shapes=[pltpu.VMEM((nblk, LANES), F32)],
        compiler_params=pltpu.CompilerParams(
            dimension_semantics=("arbitrary", "arbitrary", "arbitrary"), vmem_limit_bytes=VMEM_LIMIT),
        name="moba_attn",
    )(rel_bias, proj, proj, proj, bias)


WIN_ROWS = 16
WINS_PER_TILE = 32


def _local_slots(tm):
    return tm * TOP_K + N_EXPERTS * WIN_ROWS


def _post_attn_kernel(ysb_ref, ymb_ref, gsb_ref, gmb_ref, x_ref, wsb_ref, wmb_ref, wo_ref, nrm_ref,
                      wr_ref, br_ref, h_ref, xl_ref, route_ref, cnt_ref, *, tm, chunk):
    def gate(g_ref):
        g = jnp.concatenate([g_ref[0], g_ref[1]], axis=1).astype(F32)
        return 1.0 / (1.0 + jnp.exp(-g))

    merged = (gate(gsb_ref) * jnp.dot(ysb_ref[...], wsb_ref[...], preferred_element_type=F32)
              + gate(gmb_ref) * jnp.dot(ymb_ref[...], wmb_ref[...], preferred_element_type=F32))
    hres = x_ref[...] + jnp.dot(merged.astype(BF16), wo_ref[...], preferred_element_type=F32)
    h_ref[...] = hres
    var = jnp.mean(hres * hres, axis=-1, keepdims=True)
    hn = (hres * lax.rsqrt(var + RMS_EPS)) * nrm_ref[...]
    a_hi, a_lo = _split_bf16(hn)
    w_hi, w_lo = _split_bf16(wr_ref[...])
    logits = (jnp.dot(a_hi, w_hi, preferred_element_type=F32)
              + jnp.dot(a_hi, w_lo, preferred_element_type=F32)
              + jnp.dot(a_lo, w_hi, preferred_element_type=F32)) + br_ref[...]
    lane = lax.broadcasted_iota(jnp.int32, (1, LANES), 1).astype(F32)
    work = jnp.where(lane < N_EXPERTS, logits, NEG_BIG)
    vals, hots = [], []
    for _ in range(TOP_K):
        mx = jnp.max(work, axis=-1, keepdims=True)
        idx = jnp.min(jnp.where(work == mx, lane, float(LANES)), axis=-1, keepdims=True)
        hot = lane == idx
        work = jnp.where(hot, NEG_BIG, work)
        vals.append(mx)
        hots.append(hot)
    exps = [jnp.exp(v - vals[0]) for v in vals]
    denom = exps[0] + exps[1] + exps[2] + exps[3]
    sel = jnp.zeros((tm, LANES), F32)
    for hot in hots:
        sel = jnp.where(hot, 1.0, sel)
    r = lax.broadcasted_iota(jnp.int32, (tm, tm), 0)
    c = lax.broadcasted_iota(jnp.int32, (tm, tm), 1)
    earlier = jnp.dot((c < r).astype(BF16), sel.astype(BF16), preferred_element_type=F32)
    cnt = jnp.sum(sel, axis=0, keepdims=True)
    padded = jnp.floor((cnt + (WIN_ROWS - 1)) * (1.0 / WIN_ROWS)) * WIN_ROWS
    er = lax.broadcasted_iota(jnp.int32, (LANES, LANES), 0)
    ec = lax.broadcasted_iota(jnp.int32, (LANES, LANES), 1)
    lower = jnp.dot(jnp.broadcast_to(padded, (8, LANES)).astype(BF16), (er < ec).astype(BF16),
                    preferred_element_type=F32)[0:1]
    slot = lower + earlier
    slab = jnp.zeros((tm, LANES), F32)
    for k, hot in enumerate(hots):
        slab = jnp.where(lane == k, exps[k] / denom, slab)
        slab = jnp.where(lane == TOP_K + k, jnp.sum(jnp.where(hot, slot, 0.0), axis=-1, keepdims=True), slab)
    route_ref[...] = slab
    cnt_ref[0] = jnp.broadcast_to(cnt, (8, LANES))
    slots_t = slab.T
    hn_bf = hn.astype(BF16)
    for ci in range(_local_slots(tm) // chunk):
        rid = (lax.broadcasted_iota(jnp.int32, (chunk, 1), 0) + ci * chunk).astype(F32)
        onehot = rid == slots_t[TOP_K:TOP_K + 1, :]
        for k in range(1, TOP_K):
            onehot = onehot | (rid == slots_t[TOP_K + k:TOP_K + k + 1, :])
        xl_ref[ci * chunk:(ci + 1) * chunk, :] = jnp.dot(
            jnp.where(onehot, 1.0, 0.0).astype(BF16), hn_bf, preferred_element_type=F32).astype(xl_ref.dtype)


def _post_attn(ysb, ymb, proj, xt, wsb, wmb, wo, nrm_g, wr_pad, br_pad, *, tm):
    T, D = xt.shape
    half = D // 2
    n_tt = T // tm
    ls = _local_slots(tm)
    const = lambda i: (0, 0)
    return pl.pallas_call(
        functools.partial(_post_attn_kernel, tm=tm, chunk=256),
        grid=(n_tt,),
        in_specs=[pl.BlockSpec((tm, D_MIX), lambda i: (i, 0)),
                  pl.BlockSpec((tm, D_MIX), lambda i: (i, 0)),
                  pl.BlockSpec((2, tm, half), lambda i: (3, i, 0)),
                  pl.BlockSpec((2, tm, half), lambda i: (4, i, 0)),
                  pl.BlockSpec((tm, D), lambda i: (i, 0)),
                  pl.BlockSpec(wsb.shape, const),
                  pl.BlockSpec(wmb.shape, const),
                  pl.BlockSpec(wo.shape, const),
                  pl.BlockSpec((1, D), const),
                  pl.BlockSpec(wr_pad.shape, const),
                  pl.BlockSpec((1, LANES), const)],
        out_specs=[pl.BlockSpec((tm, D), lambda i: (i, 0)),
                   pl.BlockSpec((ls, D), lambda i: (i, 0)),
                   pl.BlockSpec((tm, LANES), lambda i: (i, 0)),
                   pl.BlockSpec((1, 8, LANES), lambda i: (i, 0, 0))],
        out_shape=[jax.ShapeDtypeStruct((T, D), F32),
                   jax.ShapeDtypeStruct((n_tt * ls, D), BF16),
                   jax.ShapeDtypeStruct((T, LANES), F32),
                   jax.ShapeDtypeStruct((n_tt, 8, LANES), F32)],
        compiler_params=pltpu.CompilerParams(dimension_semantics=("arbitrary",),
                                             vmem_limit_bytes=VMEM_LIMIT),
        name="post_attn",
    )(ysb, ymb, proj, proj, xt, wsb, wmb, wo, nrm_g.reshape(1, D), wr_pad, br_pad)


def _window_plan(counts, tm):
    n_tt, n_exp = counts.shape
    lw_tile = _local_slots(tm) // WIN_ROWS
    n_tiles = -(-(n_tt * (tm * TOP_K // WIN_ROWS + n_exp)) // WINS_PER_TILE) + n_exp
    nw = (counts + WIN_ROWS - 1) // WIN_ROWS
    local_start = jnp.cumsum(nw, axis=1) - nw
    per_expert = jnp.sum(nw, axis=0)
    per_expert_pad = -(-per_expert // WINS_PER_TILE) * WINS_PER_TILE
    expert_end = jnp.cumsum(per_expert_pad)
    sorted_start = (expert_end - per_expert_pad)[None, :] + jnp.cumsum(nw, axis=0) - nw
    run_start = sorted_start.T.reshape(-1)
    run_len = nw.T.reshape(-1)
    run_src = (jnp.arange(n_tt, dtype=jnp.int32)[:, None] * lw_tile + local_start).T.reshape(-1)
    g = jnp.arange(n_tiles * WINS_PER_TILE, dtype=jnp.int32)[:, None]
    off = g - run_start[None, :]
    src_win = jnp.sum(jnp.where((off >= 0) & (off < run_len[None, :]), run_src[None, :] + off, 0), axis=1)
    off_l = jnp.arange(lw_tile, dtype=jnp.int32)[None, :, None] - local_start[:, None, :]
    back_win = jnp.sum(jnp.where((off_l >= 0) & (off_l < nw[:, None, :]), sorted_start[:, None, :] + off_l, 0),
                       axis=2)
    tile_first = jnp.arange(n_tiles, dtype=jnp.int32) * WINS_PER_TILE
    tile_valid = tile_first < expert_end[-1]
    tile_expert = jnp.sum(expert_end[None, :] <= tile_first[:, None], axis=1)
    last = jnp.max(jnp.where(per_expert > 0, jnp.arange(n_exp, dtype=jnp.int32), 0))
    tile_expert = jnp.where(tile_valid, tile_expert, last)
    return (src_win.astype(jnp.int32), back_win.astype(jnp.int32), tile_expert.astype(jnp.int32),
            tile_valid.astype(jnp.int32))


def _experts_kernel(te_ref, tv_ref, src_ref, nxt_ref, xl_ref, wgu_ref, bgu_ref, wd_ref, bd_ref, y_ref,
                    xbuf, wgu_s, wd_s, sem, *, d_exp):
    j = pl.program_id(0)
    n_tiles = pl.num_programs(0)
    slot = j % 2
    e = te_ref[j]
    prev = te_ref[jnp.maximum(j - 1, 0)]

    def window_copy(win, dst_slot, w):
        return pltpu.make_async_copy(xl_ref.at[win], xbuf.at[dst_slot, w], sem.at[dst_slot])

    def start_gather(idx_ref, dst_slot):
        for w in range(WINS_PER_TILE):
            window_copy(idx_ref[0, 0, w], dst_slot, w).start()

    @pl.when((j == 0) & (tv_ref[0] > 0))
    def _():
        start_gather(src_ref, 0)

    @pl.when((j + 1 < n_tiles) & (tv_ref[jnp.minimum(j + 1, n_tiles - 1)] > 0))
    def _():
        start_gather(nxt_ref, 1 - slot)

    @pl.when((j == 0) | (e != prev))
    def _():
        wgu_s[...] = wgu_ref[0].astype(BF16)
        wd_s[...] = wd_ref[0].astype(BF16)

    @pl.when(tv_ref[j] > 0)
    def _():
        for w in range(WINS_PER_TILE):
            window_copy(0, slot, w).wait()
        x = xbuf[slot].reshape(WINS_PER_TILE * WIN_ROWS, xbuf.shape[-1])
        hgu = jnp.dot(x, wgu_s[...], preferred_element_type=F32) + bgu_ref[0]
        gate = jnp.minimum(hgu[:, :d_exp], SWIGLU_LIMIT)
        up = jnp.clip(hgu[:, d_exp:], -SWIGLU_LIMIT, SWIGLU_LIMIT)
        act = (up + 1.0) * (gate * (1.0 / (1.0 + jnp.exp(-SWIGLU_ALPHA * gate))))
        y = jnp.dot(act.astype(BF16), wd_s[...], preferred_element_type=F32) + bd_ref[0]
        y_ref[...] = y.astype(y_ref.dtype)

    @pl.when(tv_ref[j] == 0)
    def _():
        y_ref[...] = jnp.zeros_like(y_ref)


def _experts(xl, src_win, tile_expert, tile_valid, wgu, bgu, wd, bd):
    D = xl.shape[1]
    E, _, d2 = wgu.shape
    d_exp = d2 // 2
    n_tiles = tile_expert.shape[0]
    tmm = WINS_PER_TILE * WIN_ROWS
    src3 = src_win.reshape(n_tiles, 1, WINS_PER_TILE)
    grid_spec = pltpu.PrefetchScalarGridSpec(
        num_scalar_prefetch=2,
        grid=(n_tiles,),
        in_specs=[pl.BlockSpec((1, 1, WINS_PER_TILE), lambda j, te, tv: (j, 0, 0), memory_space=pltpu.SMEM),
                  pl.BlockSpec((1, 1, WINS_PER_TILE), lambda j, te, tv: (jnp.minimum(j + 1, n_tiles - 1), 0, 0),
                               memory_space=pltpu.SMEM),
                  pl.BlockSpec(memory_space=pl.ANY),
                  pl.BlockSpec((1, D, d2), lambda j, te, tv: (te[j], 0, 0)),
                  pl.BlockSpec((1, 1, d2), lambda j, te, tv: (te[j], 0, 0)),
                  pl.BlockSpec((1, d_exp, D), lambda j, te, tv: (te[j], 0, 0)),
                  pl.BlockSpec((1, 1, D), lambda j, te, tv: (te[j], 0, 0))],
        out_specs=pl.BlockSpec((tmm, D), lambda j, te, tv: (j, 0)),
        scratch_shapes=[pltpu.VMEM((2, WINS_PER_TILE, WIN_ROWS, D), BF16), pltpu.VMEM((D, d2), BF16),
                        pltpu.VMEM((d_exp, D), BF16), pltpu.SemaphoreType.DMA((2,))],
    )
    return pl.pallas_call(
        functools.partial(_experts_kernel, d_exp=d_exp),
        grid_spec=grid_spec,
        out_shape=jax.ShapeDtypeStruct((n_tiles * tmm, D), BF16),
        compiler_params=pltpu.CompilerParams(dimension_semantics=("arbitrary",),
                                             vmem_limit_bytes=VMEM_LIMIT),
        name="experts",
    )(tile_expert, tile_valid, src3, src3, xl.reshape(-1, WIN_ROWS, D), wgu, bgu.reshape(E, 1, d2), wd,
      bd.reshape(E, 1, D))


def _combine_kernel(back_ref, nxt_ref, y_ref, h_ref, route_ref, g_ref, o_ref, ybuf, sem, *, tm, chunk):
    i = pl.program_id(0)
    slot = i % 2
    n_win = ybuf.shape[1]

    def window_copy(win, dst_slot, w):
        return pltpu.make_async_copy(y_ref.at[win], ybuf.at[dst_slot, w], sem.at[dst_slot])

    def start_fetch(idx_ref, dst_slot):
        for w in range(n_win):
            window_copy(idx_ref[0, 0, w], dst_slot, w).start()

    @pl.when(i == 0)
    def _():
        start_fetch(back_ref, 0)

    @pl.when(i + 1 < pl.num_programs(0))
    def _():
        start_fetch(nxt_ref, 1 - slot)

    route = route_ref[...]
    weights = [route[:, k:k + 1] for k in range(TOP_K)]
    slots = [route[:, TOP_K + k:TOP_K + k + 1] for k in range(TOP_K)]
    for w in range(n_win):
        window_copy(0, slot, w).wait()
    acc = h_ref[...]
    for ci in range(n_win * WIN_ROWS // chunk):
        sid = (lax.broadcasted_iota(jnp.int32, (1, chunk), 1) + ci * chunk).astype(F32)
        wmat = jnp.zeros((tm, chunk), F32)
        for k in range(TOP_K):
            wmat = jnp.where(sid == slots[k], weights[k], wmat)
        hi, lo = _split_bf16(wmat)
        rows = ybuf[slot, ci * chunk // WIN_ROWS:(ci + 1) * chunk // WIN_ROWS].reshape(chunk, ybuf.shape[-1])
        acc = acc + jnp.dot(hi, rows, preferred_element_type=F32) + jnp.dot(lo, rows, preferred_element_type=F32)
    var = jnp.mean(acc * acc, axis=-1, keepdims=True)
    o_ref[...] = (acc * lax.rsqrt(var + RMS_EPS)) * g_ref[...]


def _combine(y, back_win, h, route, g, *, tm):
    T, D = h.shape
    n_tt, n_win = back_win.shape
    back3 = back_win.reshape(n_tt, 1, n_win)
    return pl.pallas_call(
        functools.partial(_combine_kernel, tm=tm, chunk=256),
        grid=(n_tt,),
        in_specs=[pl.BlockSpec((1, 1, n_win), lambda i: (i, 0, 0), memory_space=pltpu.SMEM),
                  pl.BlockSpec((1, 1, n_win), lambda i: (jnp.minimum(i + 1, n_tt - 1), 0, 0),
                               memory_space=pltpu.SMEM),
                  pl.BlockSpec(memory_space=pl.ANY),
                  pl.BlockSpec((tm, D), lambda i: (i, 0)),
                  pl.BlockSpec((tm, LANES), lambda i: (i, 0)),
                  pl.BlockSpec((1, D), lambda i: (0, 0))],
        out_specs=pl.BlockSpec((tm, D), lambda i: (i, 0)),
        out_shape=jax.ShapeDtypeStruct((T, D), F32),
        scratch_shapes=[pltpu.VMEM((2, n_win, WIN_ROWS, D), BF16), pltpu.SemaphoreType.DMA((2,))],
        compiler_params=pltpu.CompilerParams(dimension_semantics=("arbitrary",),
                                             vmem_limit_bytes=VMEM_LIMIT),
        name="combine",
    )(back3, back3, y.reshape(-1, WIN_ROWS, D), h, route, g.reshape(1, D))


def kernel(x, norm_mix_g, w_in, w_proj_sb, w_proj_moba, w_out, rel_bias, norm_ffn_g, w_router, b_router,
           w_gate_up, b_gate_up, w_down, b_down, norm_final_g):
    batch, seq, D = x.shape
    assert norm_mix_g.shape[0] == 1, "single-layer trunk"
    tm = 256
    xt = x.reshape(batch * seq, D)
    proj = _in_proj(xt, norm_mix_g[0], w_in[0].astype(BF16))
    ysb = _sb_attention(proj, batch=batch, seq=seq)
    ymb = _moba_attention(proj, _moba_bias(rel_bias), rel_bias, batch=batch, seq=seq)
    wr_pad = jnp.pad(w_router[0], ((0, 0), (0, LANES - N_EXPERTS)))
    br_pad = jnp.pad(b_router[0], (0, LANES - N_EXPERTS)).reshape(1, LANES)
    h, xl, route, cnt = _post_attn(ysb, ymb, proj, xt, w_proj_sb[0].astype(BF16), w_proj_moba[0].astype(BF16),
                                   w_out[0].astype(BF16), norm_ffn_g[0], wr_pad, br_pad, tm=tm)
    src_win, back_win, tile_expert, tile_valid = _window_plan(cnt[:, 0, :N_EXPERTS].astype(jnp.int32), tm)
    y = _experts(xl, src_win, tile_expert, tile_valid, w_gate_up[0], b_gate_up[0], w_down[0], b_down[0])
    out = _combine(y, back_win, h, route, norm_final_g, tm=tm)
    return out.reshape(batch, seq, D)
```

```python
import functools
import math

import numpy as np
import jax
import jax.numpy as jnp
from jax import lax
from jax.experimental import pallas as pl
from jax.experimental.pallas import tpu as pltpu

HEAD_DIM = 64
N_HEADS = 8
D_MIX = N_HEADS * HEAD_DIM
MOBA_BLOCK = 256
MOBA_TOPK = 3
REL_BUCKETS = 32
REL_MAX_DIST = 128
N_EXPERTS = 32
TOP_K = 4
SWIGLU_LIMIT = 7.0
SWIGLU_ALPHA = 1.702
RMS_EPS = 1e-6

LANES = 128
NEG_BIG = -1e30
VMEM_LIMIT = 56 * 1024 * 1024

F32 = jnp.float32
BF16 = jnp.bfloat16


def _split_bf16(a):
    hi = a.astype(BF16)
    lo = (a - hi.astype(F32)).astype(BF16)
    return hi, lo


def _dot_nt(a, b):
    return lax.dot_general(a, b, (((1,), (1,)), ((), ())), preferred_element_type=F32)


def _in_proj_kernel(x_ref, g_ref, w_ref, o_ref, *, n_slab, slab):
    x = x_ref[...]
    var = jnp.mean(x * x, axis=-1, keepdims=True)
    xn = ((x * lax.rsqrt(var + RMS_EPS)) * g_ref[...]).astype(BF16)
    for j in range(n_slab):
        o_ref[j] = jnp.dot(xn, w_ref[:, j * slab:(j + 1) * slab],
                           preferred_element_type=F32).astype(o_ref.dtype)


def _in_proj(xt, g, w_bf16, *, tm=512, slab=D_MIX):
    T, D = xt.shape
    n_slab = w_bf16.shape[1] // slab
    return pl.pallas_call(
        functools.partial(_in_proj_kernel, n_slab=n_slab, slab=slab),
        grid=(T // tm,),
        in_specs=[pl.BlockSpec((tm, D), lambda i: (i, 0)),
                  pl.BlockSpec((1, D), lambda i: (0, 0)),
                  pl.BlockSpec(w_bf16.shape, lambda i: (0, 0))],
        out_specs=pl.BlockSpec((n_slab, tm, slab), lambda i: (0, i, 0)),
        out_shape=jax.ShapeDtypeStruct((n_slab, T, slab), BF16),
        compiler_params=pltpu.CompilerParams(dimension_semantics=("arbitrary",),
                                             vmem_limit_bytes=VMEM_LIMIT),
        name="in_proj",
    )(xt, g.reshape(1, D), w_bf16)


def _sb_kernel(q_ref, k_ref, v_ref, o_ref, *, tq, tk, zscale):
    i = pl.program_id(2)
    q = q_ref[0]
    lane = lax.broadcasted_iota(jnp.int32, (1, LANES), 1)
    row = lax.broadcasted_iota(jnp.int32, (tq, tk), 0)
    col = lax.broadcasted_iota(jnp.int32, (tq, tk), 1)
    jj = lax.broadcasted_iota(jnp.int32, (tk, tk), 0)
    ss = lax.broadcasted_iota(jnp.int32, (tk, tk), 1)
    later = (jj >= ss).astype(BF16)
    zero = jnp.zeros((), q.dtype)
    qhs = [jnp.where(lane // HEAD_DIM == h, q, zero) for h in range(2)]
    n_sub = tq // tk

    def q_tile_keys(base, carry, diagonal):
        tiles = [(h, sub) for h in range(2) for sub in reversed(range(n_sub))]
        starts = {sub: pl.multiple_of(base + sub * tk, tk) for sub in range(n_sub)}
        z2 = {t: _dot_nt(qhs[t[0]], k_ref[0, pl.ds(starts[t[1]], tk), :]) * zscale for t in tiles}
        drop, parts = {}, {}
        for t in tiles:
            neg_abs = lax.bitcast_convert_type(
                lax.bitcast_convert_type(z2[t], jnp.uint32) | jnp.uint32(0x80000000), F32)
            d = jnp.maximum(z2[t], 0.0) + jnp.log2(1.0 + jnp.exp2(neg_abs))
            if diagonal:
                d = jnp.where((t[1] * tk + col) < row, d, 0.0)
            drop[t] = d
            parts[t] = d.astype(BF16)
        inner = {t: jnp.dot(parts[t], later, preferred_element_type=F32) for t in tiles}
        c = [carry[h][0] for h in range(2)]
        a = {}
        for t in tiles:
            h = t[0]
            w = jnp.exp2((z2[t] - c[h]) - inner[t])
            if diagonal:
                w = jnp.where((t[1] * tk + col) < row, w, 0.0)
            a[t] = w.astype(BF16)
            c[h] = c[h] + jnp.sum(drop[t], axis=-1, keepdims=True)
        acc = [carry[h][1] for h in range(2)]
        for t in tiles:
            acc[t[0]] = acc[t[0]] + jnp.dot(a[t], v_ref[0, pl.ds(starts[t[1]], tk), :],
                                            preferred_element_type=F32)
        return tuple((c[h], acc[h]) for h in range(2))

    init = tuple((jnp.zeros((tq, 1), F32), jnp.zeros((tq, LANES), F32)) for _ in range(2))
    carry = q_tile_keys(i * tq, init, True)
    carry = lax.fori_loop(0, i, lambda it, cr: q_tile_keys((i - 1 - it) * tq, cr, False), carry)
    o_ref[...] = jnp.where(lane < HEAD_DIM, carry[0][1], carry[1][1]).astype(o_ref.dtype)


def _sb_attention(proj, *, batch, seq, tq=512, tk=256):
    T = batch * seq
    nq = seq // tq
    n_hp = D_MIX // LANES
    return pl.pallas_call(
        functools.partial(_sb_kernel, tq=tq, tk=tk, zscale=math.log2(math.e) / math.sqrt(HEAD_DIM)),
        grid=(batch, n_hp, nq),
        in_specs=[pl.BlockSpec((1, tq, LANES), lambda b, p, i: (0, b * nq + i, p)),
                  pl.BlockSpec((1, seq, LANES), lambda b, p, i: (1, b, p)),
                  pl.BlockSpec((1, seq, LANES), lambda b, p, i: (2, b, p))],
        out_specs=pl.BlockSpec((tq, LANES), lambda b, p, i: (b * nq + i, p)),
        out_shape=jax.ShapeDtypeStruct((T, D_MIX), BF16),
        compiler_params=pltpu.CompilerParams(
            dimension_semantics=("arbitrary", "arbitrary", "arbitrary"), vmem_limit_bytes=VMEM_LIMIT),
        name="sb_attn",
    )(proj, proj, proj)


def _rel_bucket_np(dist):
    n = np.maximum(dist, 0)
    max_exact = REL_BUCKETS // 2
    nf = np.maximum(n, 1).astype(np.float64)
    large = max_exact + (np.log(nf / max_exact) / math.log(REL_MAX_DIST / max_exact)
                         * (REL_BUCKETS - max_exact)).astype(np.int32)
    large = np.minimum(large, REL_BUCKETS - 1)
    return np.where(n < max_exact, n, large).astype(np.int32)


def _bias_kernel(rel_ref, bkt_ref, o_ref):
    h = pl.program_id(0)
    for d in range(2):
        bkt = bkt_ref[d]
        acc = jnp.zeros(bkt.shape, F32)
        for b in range(REL_BUCKETS):
            acc = jnp.where(bkt == b, rel_ref[b, h], acc)
        o_ref[0, d] = acc


def _moba_bias(rel_bias):
    L = MOBA_BLOCK
    r = np.arange(L)[:, None]
    c = np.arange(L)[None, :]
    bkt = np.stack([_rel_bucket_np(r - c), _rel_bucket_np(L + r - c)])
    return pl.pallas_call(
        _bias_kernel,
        grid=(N_HEADS,),
        in_specs=[pl.BlockSpec(memory_space=pltpu.SMEM),
                  pl.BlockSpec((2, L, L), lambda h: (0, 0, 0))],
        out_specs=pl.BlockSpec((1, 2, L, L), lambda h: (h, 0, 0, 0)),
        out_shape=jax.ShapeDtypeStruct((N_HEADS, 2, L, L), F32),
        name="moba_bias",
    )(rel_bias, jnp.asarray(bkt))


def _moba_kernel(rel_ref, q_ref, k_ref, v_ref, bias_ref, o_ref, km_ref, *, L, nblk, scale):
    p = pl.program_id(1)
    i = pl.program_id(2)
    lane = lax.broadcasted_iota(jnp.int32, (1, LANES), 1)
    row = lax.broadcasted_iota(jnp.int32, (L, L), 0)
    col = lax.broadcasted_iota(jnp.int32, (L, L), 1)

    @pl.when(i == 0)
    def _():
        km_ref[...] = jnp.mean(k_ref[0].astype(F32).reshape(nblk, L, LANES), axis=1)

    def attend(n_off):
        rr = lax.broadcasted_iota(jnp.int32, (n_off, 1), 0)
        km = jnp.concatenate([km_ref[pl.ds(jnp.maximum(i - r, 0), 1), :] for r in range(n_off)], axis=0)
        km = jnp.concatenate([km, jnp.zeros((16 - n_off, LANES), F32)], axis=0)
        km_hi, km_lo = _split_bf16(km)
        km_lo2 = (km - km_hi.astype(F32) - km_lo.astype(F32)).astype(BF16)
        q = q_ref[0] * jnp.asarray(scale, q_ref.dtype)
        past = (rr >= 1) & (rr <= i)
        zero = jnp.zeros((), q.dtype)
        outs, all_scores = [], []
        for h in range(2):
            own = lane // HEAD_DIM == h
            aux0 = (1 - h) * HEAD_DIM
            qh = jnp.where(own, q, zero)
            gs = (_dot_nt(km_hi, qh) + _dot_nt(km_lo, qh) + _dot_nt(km_lo2, qh))[:n_off]
            pen = jnp.zeros((n_off, L), F32)
            for r in range(1, n_off):
                g_r = gs[r:r + 1, :]
                beats = past & ((gs > g_r) | ((gs == g_r) & (rr > r)))
                cnt = jnp.sum(beats.astype(F32), axis=0, keepdims=True)
                keep = (cnt < MOBA_TOPK) & (r <= i)
                pen = jnp.where((rr == r) & jnp.logical_not(keep), NEG_BIG, pen)
            far = jnp.full((n_off, L), rel_ref[REL_BUCKETS - 1, 2 * p + h], F32)
            f_hi, f_lo = _split_bf16(far)
            f_lo2 = far - f_hi.astype(F32) - f_lo.astype(F32)
            far3 = jnp.where(rr == 0, f_hi.astype(F32), jnp.where(rr == 1, f_lo.astype(F32),
                                                                  jnp.where(rr == 2, f_lo2, 0.0)))
            aux = [pen, far3, jnp.zeros((LANES - aux0 - 2 * n_off, L), F32)]
            aux = jnp.concatenate(([jnp.zeros((aux0, L), F32)] if aux0 else []) + aux, axis=0)
            q_aug = jnp.where(own, q, aux.T.astype(q.dtype))
            scores, values = [], []
            for r in range(n_off):
                ks = pl.multiple_of(jnp.maximum(i - r, 0) * L, L)
                kblk = k_ref[0, pl.ds(ks, L), :]
                values.append(v_ref[0, pl.ds(ks, L), :])
                if r == 0:
                    s = _dot_nt(qh, kblk) + bias_ref[h, 0]
                    s = jnp.where(col <= row, s, NEG_BIG)
                else:
                    hot = (lane == aux0 + r)
                    if r >= 2:
                        hot = hot | ((lane >= aux0 + n_off) & (lane < aux0 + n_off + 3))
                    k_aug = jnp.where(own, kblk, jnp.where(hot, 1.0, 0.0).astype(kblk.dtype))
                    s = _dot_nt(q_aug, k_aug)
                    if r == 1:
                        s = s + bias_ref[h, 1]
                scores.append(s)
            all_scores.append(scores)
        for scores in all_scores:
            m = scores[0]
            for s in scores[1:]:
                m = jnp.maximum(m, s)
            m = jnp.max(m, axis=-1, keepdims=True)
            probs = [jnp.exp(s - m) for s in scores]
            l = probs[0]
            for pr in probs[1:]:
                l = l + pr
            l = jnp.sum(l, axis=-1, keepdims=True)
            acc = jnp.dot(probs[0].astype(BF16), values[0], preferred_element_type=F32)
            for pr, vblk in zip(probs[1:], values[1:]):
                acc = acc + jnp.dot(pr.astype(BF16), vblk, preferred_element_type=F32)
            outs.append(acc / l)
        o_ref[...] = jnp.where(lane < HEAD_DIM, outs[0], outs[1]).astype(o_ref.dtype)

    half = nblk // 2
    if half >= MOBA_TOPK:
        pl.when(i < half)(lambda: attend(half))
        pl.when(i >= half)(lambda: attend(nblk))
    else:
        attend(nblk)


def _moba_attention(proj, bias, rel_bias, *, batch, seq):
    T = batch * seq
    L = MOBA_BLOCK
    nblk = seq // L
    n_hp = D_MIX // LANES
    return pl.pallas_call(
        functools.partial(_moba_kernel, L=L, nblk=nblk, scale=1.0 / math.sqrt(HEAD_DIM)),
        grid=(batch, n_hp, nblk),
        in_specs=[pl.BlockSpec(memory_space=pltpu.SMEM),
                  pl.BlockSpec((1, L, LANES), lambda b, p, i: (3, b * nblk + i, p)),
                  pl.BlockSpec((1, seq, LANES), lambda b, p, i: (4, b, p)),
                  pl.BlockSpec((1, seq, LANES), lambda b, p, i: (5, b, p)),
                  pl.BlockSpec((2, 2, L, L), lambda b, p, i: (p, 0, 0, 0))],
        out_specs=pl.BlockSpec((L, LANES), lambda b, p, i: (b * nblk + i, p)),
        out_shape=jax.ShapeDtypeStruct((T, D_MIX), BF16),
        scratch_shapes=[pltpu.VMEM((nblk, LANES), F32)],
        compiler_params=pltpu.CompilerParams(
            dimension_semantics=("arbitrary", "arbitrary", "arbitrary"), vmem_limit_bytes=VMEM_LIMIT),
        name="moba_attn",
    )(rel_bias, proj, proj, proj, bias)


WIN_ROWS = 16
WINS_PER_TILE = 32


def _local_slots(tm):
    return tm * TOP_K + N_EXPERTS * WIN_ROWS


def _post_attn_kernel(ysb_ref, ymb_ref, gsb_ref, gmb_ref, x_ref, wsb_ref, wmb_ref, wo_ref, nrm_ref,
                      wr_ref, br_ref, h_ref, xl_ref, route_ref, cnt_ref, *, tm, chunk):
    def gate(g_ref):
        g = jnp.concatenate([g_ref[0], g_ref[1]], axis=1).astype(F32)
        return 1.0 / (1.0 + jnp.exp(-g))

    merged = (gate(gsb_ref) * jnp.dot(ysb_ref[...], wsb_ref[...], preferred_element_type=F32)
              + gate(gmb_ref) * jnp.dot(ymb_ref[...], wmb_ref[...], preferred_element_type=F32))
    hres = x_ref[...] + jnp.dot(merged.astype(BF16), wo_ref[...], preferred_element_type=F32)
    h_ref[...] = hres
    var = jnp.mean(hres * hres, axis=-1, keepdims=True)
    hn = (hres * lax.rsqrt(var + RMS_EPS)) * nrm_ref[...]
    a_hi, a_lo = _split_bf16(hn)
    w_hi, w_lo = _split_bf16(wr_ref[...])
    logits = (jnp.dot(a_hi, w_hi, preferred_element_type=F32)
              + jnp.dot(a_hi, w_lo, preferred_element_type=F32)
              + jnp.dot(a_lo, w_hi, preferred_element_type=F32)) + br_ref[...]
    lane = lax.broadcasted_iota(jnp.int32, (1, LANES), 1).astype(F32)
    work = jnp.where(lane < N_EXPERTS, logits, NEG_BIG)
    vals, hots = [], []
    for _ in range(TOP_K):
        mx = jnp.max(work, axis=-1, keepdims=True)
        idx = jnp.min(jnp.where(work == mx, lane, float(LANES)), axis=-1, keepdims=True)
        hot = lane == idx
        work = jnp.where(hot, NEG_BIG, work)
        vals.append(mx)
        hots.append(hot)
    exps = [jnp.exp(v - vals[0]) for v in vals]
    denom = exps[0] + exps[1] + exps[2] + exps[3]
    sel = jnp.zeros((tm, LANES), F32)
    for hot in hots:
        sel = jnp.where(hot, 1.0, sel)
    r = lax.broadcasted_iota(jnp.int32, (tm, tm), 0)
    c = lax.broadcasted_iota(jnp.int32, (tm, tm), 1)
    earlier = jnp.dot((c < r).astype(BF16), sel.astype(BF16), preferred_element_type=F32)
    cnt = jnp.sum(sel, axis=0, keepdims=True)
    padded = jnp.floor((cnt + (WIN_ROWS - 1)) * (1.0 / WIN_ROWS)) * WIN_ROWS
    er = lax.broadcasted_iota(jnp.int32, (LANES, LANES), 0)
    ec = lax.broadcasted_iota(jnp.int32, (LANES, LANES), 1)
    lower = jnp.dot(jnp.broadcast_to(padded, (8, LANES)).astype(BF16), (er < ec).astype(BF16),
                    preferred_element_type=F32)[0:1]
    slot = lower + earlier
    slab = jnp.zeros((tm, LANES), F32)
    for k, hot in enumerate(hots):
        slab = jnp.where(lane == k, exps[k] / denom, slab)
        slab = jnp.where(lane == TOP_K + k, jnp.sum(jnp.where(hot, slot, 0.0), axis=-1, keepdims=True), slab)
    route_ref[...] = slab
    cnt_ref[0] = jnp.broadcast_to(cnt, (8, LANES))
    slots_t = slab.T
    hn_bf = hn.astype(BF16)
    for ci in range(_local_slots(tm) // chunk):
        rid = (lax.broadcasted_iota(jnp.int32, (chunk, 1), 0) + ci * chunk).astype(F32)
        onehot = rid == slots_t[TOP_K:TOP_K + 1, :]
        for k in range(1, TOP_K):
            onehot = onehot | (rid == slots_t[TOP_K + k:TOP_K + k + 1, :])
        xl_ref[ci * chunk:(ci + 1) * chunk, :] = jnp.dot(
            jnp.where(onehot, 1.0, 0.0).astype(BF16), hn_bf, preferred_element_type=F32).astype(xl_ref.dtype)


def _post_attn(ysb, ymb, proj, xt, wsb, wmb, wo, nrm_g, wr_pad, br_pad, *, tm):
    T, D = xt.shape
    half = D // 2
    n_tt = T // tm
    ls = _local_slots(tm)
    const = lambda i: (0, 0)
    return pl.pallas_call(
        functools.partial(_post_attn_kernel, tm=tm, chunk=256),
        grid=(n_tt,),
        in_specs=[pl.BlockSpec((tm, D_MIX), lambda i: (i, 0)),
                  pl.BlockSpec((tm, D_MIX), lambda i: (i, 0)),
                  pl.BlockSpec((2, tm, half), lambda i: (3, i, 0)),
                  pl.BlockSpec((2, tm, half), lambda i: (4, i, 0)),
                  pl.BlockSpec((tm, D), lambda i: (i, 0)),
                  pl.BlockSpec(wsb.shape, const),
                  pl.BlockSpec(wmb.shape, const),
                  pl.BlockSpec(wo.shape, const),
                  pl.BlockSpec((1, D), const),
                  pl.BlockSpec(wr_pad.shape, const),
                  pl.BlockSpec((1, LANES), const)],
        out_specs=[pl.BlockSpec((tm, D), lambda i: (i, 0)),
                   pl.BlockSpec((ls, D), lambda i: (i, 0)),
                   pl.BlockSpec((tm, LANES), lambda i: (i, 0)),
                   pl.BlockSpec((1, 8, LANES), lambda i: (i, 0, 0))],
        out_shape=[jax.ShapeDtypeStruct((T, D), F32),
                   jax.ShapeDtypeStruct((n_tt * ls, D), BF16),
                   jax.ShapeDtypeStruct((T, LANES), F32),
                   jax.ShapeDtypeStruct((n_tt, 8, LANES), F32)],
        compiler_params=pltpu.CompilerParams(dimension_semantics=("arbitrary",),
                                             vmem_limit_bytes=VMEM_LIMIT),
        name="post_attn",
    )(ysb, ymb, proj, proj, xt, wsb, wmb, wo, nrm_g.reshape(1, D), wr_pad, br_pad)


def _window_plan(counts, tm):
    n_tt, n_exp = counts.shape
    lw_tile = _local_slots(tm) // WIN_ROWS
    n_tiles = -(-(n_tt * (tm * TOP_K // WIN_ROWS + n_exp)) // WINS_PER_TILE) + n_exp
    nw = (counts + WIN_ROWS - 1) // WIN_ROWS
    local_start = jnp.cumsum(nw, axis=1) - nw
    per_expert = jnp.sum(nw, axis=0)
    per_expert_pad = -(-per_expert // WINS_PER_TILE) * WINS_PER_TILE
    expert_end = jnp.cumsum(per_expert_pad)
    sorted_start = (expert_end - per_expert_pad)[None, :] + jnp.cumsum(nw, axis=0) - nw
    run_start = sorted_start.T.reshape(-1)
    run_len = nw.T.reshape(-1)
    run_src = (jnp.arange(n_tt, dtype=jnp.int32)[:, None] * lw_tile + local_start).T.reshape(-1)
    g = jnp.arange(n_tiles * WINS_PER_TILE, dtype=jnp.int32)[:, None]
    off = g - run_start[None, :]
    src_win = jnp.sum(jnp.where((off >= 0) & (off < run_len[None, :]), run_src[None, :] + off, 0), axis=1)
    off_l = jnp.arange(lw_tile, dtype=jnp.int32)[None, :, None] - local_start[:, None, :]
    back_win = jnp.sum(jnp.where((off_l >= 0) & (off_l < nw[:, None, :]), sorted_start[:, None, :] + off_l, 0),
                       axis=2)
    tile_first = jnp.arange(n_tiles, dtype=jnp.int32) * WINS_PER_TILE
    tile_valid = tile_first < expert_end[-1]
    tile_expert = jnp.sum(expert_end[None, :] <= tile_first[:, None], axis=1)
    last = jnp.max(jnp.where(per_expert > 0, jnp.arange(n_exp, dtype=jnp.int32), 0))
    tile_expert = jnp.where(tile_valid, tile_expert, last)
    return (src_win.astype(jnp.int32), back_win.astype(jnp.int32), tile_expert.astype(jnp.int32),
            tile_valid.astype(jnp.int32))


def _experts_kernel(te_ref, tv_ref, src_ref, nxt_ref, xl_ref, wgu_ref, bgu_ref, wd_ref, bd_ref, y_ref,
                    xbuf, wgu_s, wd_s, sem, *, d_exp):
    j = pl.program_id(0)
    n_tiles = pl.num_programs(0)
    slot = j % 2
    e = te_ref[j]
    prev = te_ref[jnp.maximum(j - 1, 0)]

    def window_copy(win, dst_slot, w):
        return pltpu.make_async_copy(xl_ref.at[win], xbuf.at[dst_slot, w], sem.at[dst_slot])

    def start_gather(idx_ref, dst_slot):
        for w in range(WINS_PER_TILE):
            window_copy(idx_ref[0, 0, w], dst_slot, w).start()

    @pl.when((j == 0) & (tv_ref[0] > 0))
    def _():
        start_gather(src_ref, 0)

    @pl.when((j + 1 < n_tiles) & (tv_ref[jnp.minimum(j + 1, n_tiles - 1)] > 0))
    def _():
        start_gather(nxt_ref, 1 - slot)

    @pl.when((j == 0) | (e != prev))
    def _():
        wgu_s[...] = wgu_ref[0].astype(BF16)
        wd_s[...] = wd_ref[0].astype(BF16)

    @pl.when(tv_ref[j] > 0)
    def _():
        for w in range(WINS_PER_TILE):
            window_copy(0, slot, w).wait()
        x = xbuf[slot].reshape(WINS_PER_TILE * WIN_ROWS, xbuf.shape[-1])
        hgu = jnp.dot(x, wgu_s[...], preferred_element_type=F32) + bgu_ref[0]
        gate = jnp.minimum(hgu[:, :d_exp], SWIGLU_LIMIT)
        up = jnp.clip(hgu[:, d_exp:], -SWIGLU_LIMIT, SWIGLU_LIMIT)
        act = (up + 1.0) * (gate * (1.0 / (1.0 + jnp.exp(-SWIGLU_ALPHA * gate))))
        y = jnp.dot(act.astype(BF16), wd_s[...], preferred_element_type=F32) + bd_ref[0]
        y_ref[...] = y.astype(y_ref.dtype)

    @pl.when(tv_ref[j] == 0)
    def _():
        y_ref[...] = jnp.zeros_like(y_ref)


def _experts(xl, src_win, tile_expert, tile_valid, wgu, bgu, wd, bd):
    D = xl.shape[1]
    E, _, d2 = wgu.shape
    d_exp = d2 // 2
    n_tiles = tile_expert.shape[0]
    tmm = WINS_PER_TILE * WIN_ROWS
    src3 = src_win.reshape(n_tiles, 1, WINS_PER_TILE)
    grid_spec = pltpu.PrefetchScalarGridSpec(
        num_scalar_prefetch=2,
        grid=(n_tiles,),
        in_specs=[pl.BlockSpec((1, 1, WINS_PER_TILE), lambda j, te, tv: (j, 0, 0), memory_space=pltpu.SMEM),
                  pl.BlockSpec((1, 1, WINS_PER_TILE), lambda j, te, tv: (jnp.minimum(j + 1, n_tiles - 1), 0, 0),
                               memory_space=pltpu.SMEM),
                  pl.BlockSpec(memory_space=pl.ANY),
                  pl.BlockSpec((1, D, d2), lambda j, te, tv: (te[j], 0, 0)),
                  pl.BlockSpec((1, 1, d2), lambda j, te, tv: (te[j], 0, 0)),
                  pl.BlockSpec((1, d_exp, D), lambda j, te, tv: (te[j], 0, 0)),
                  pl.BlockSpec((1, 1, D), lambda j, te, tv: (te[j], 0, 0))],
        out_specs=pl.BlockSpec((tmm, D), lambda j, te, tv: (j, 0)),
        scratch_shapes=[pltpu.VMEM((2, WINS_PER_TILE, WIN_ROWS, D), BF16), pltpu.VMEM((D, d2), BF16),
                        pltpu.VMEM((d_exp, D), BF16), pltpu.SemaphoreType.DMA((2,))],
    )
    return pl.pallas_call(
        functools.partial(_experts_kernel, d_exp=d_exp),
        grid_spec=grid_spec,
        out_shape=jax.ShapeDtypeStruct((n_tiles * tmm, D), BF16),
        compiler_params=pltpu.CompilerParams(dimension_semantics=("arbitrary",),
                                             vmem_limit_bytes=VMEM_LIMIT),
        name="experts",
    )(tile_expert, tile_valid, src3, src3, xl.reshape(-1, WIN_ROWS, D), wgu, bgu.reshape(E, 1, d2), wd,
      bd.reshape(E, 1, D))


def _combine_kernel(back_ref, nxt_ref, y_ref, h_ref, route_ref, g_ref, o_ref, ybuf, sem, *, tm, chunk):
    i = pl.program_id(0)
    slot = i % 2
    n_win = ybuf.shape[1]

    def window_copy(win, dst_slot, w):
        return pltpu.make_async_copy(y_ref.at[win], ybuf.at[dst_slot, w], sem.at[dst_slot])

    def start_fetch(idx_ref, dst_slot):
        for w in range(n_win):
            window_copy(idx_ref[0, 0, w], dst_slot, w).start()

    @pl.when(i == 0)
    def _():
        start_fetch(back_ref, 0)

    @pl.when(i + 1 < pl.num_programs(0))
    def _():
        start_fetch(nxt_ref, 1 - slot)

    route = route_ref[...]
    weights = [route[:, k:k + 1] for k in range(TOP_K)]
    slots = [route[:, TOP_K + k:TOP_K + k + 1] for k in range(TOP_K)]
    for w in range(n_win):
        window_copy(0, slot, w).wait()
    acc = h_ref[...]
    for ci in range(n_win * WIN_ROWS // chunk):
        sid = (lax.broadcasted_iota(jnp.int32, (1, chunk), 1) + ci * chunk).astype(F32)
        wmat = jnp.zeros((tm, chunk), F32)
        for k in range(TOP_K):
            wmat = jnp.where(sid == slots[k], weights[k], wmat)
        hi, lo = _split_bf16(wmat)
        rows = ybuf[slot, ci * chunk // WIN_ROWS:(ci + 1) * chunk // WIN_ROWS].reshape(chunk, ybuf.shape[-1])
        acc = acc + jnp.dot(hi, rows, preferred_element_type=F32) + jnp.dot(lo, rows, preferred_element_type=F32)
    var = jnp.mean(acc * acc, axis=-1, keepdims=True)
    o_ref[...] = (acc * lax.rsqrt(var + RMS_EPS)) * g_ref[...]


def _combine(y, back_win, h, route, g, *, tm):
    T, D = h.shape
    n_tt, n_win = back_win.shape
    back3 = back_win.reshape(n_tt, 1, n_win)
    return pl.pallas_call(
        functools.partial(_combine_kernel, tm=tm, chunk=256),
        grid=(n_tt,),
        in_specs=[pl.BlockSpec((1, 1, n_win), lambda i: (i, 0, 0), memory_space=pltpu.SMEM),
                  pl.BlockSpec((1, 1, n_win), lambda i: (jnp.minimum(i + 1, n_tt - 1), 0, 0),
                               memory_space=pltpu.SMEM),
                  pl.BlockSpec(memory_space=pl.ANY),
                  pl.BlockSpec((tm, D), lambda i: (i, 0)),
                  pl.BlockSpec((tm, LANES), lambda i: (i, 0)),
                  pl.BlockSpec((1, D), lambda i: (0, 0))],
        out_specs=pl.BlockSpec((tm, D), lambda i: (i, 0)),
        out_shape=jax.ShapeDtypeStruct((T, D), F32),
        scratch_shapes=[pltpu.VMEM((2, n_win, WIN_ROWS, D), BF16), pltpu.SemaphoreType.DMA((2,))],
        compiler_params=pltpu.CompilerParams(dimension_semantics=("arbitrary",),
                                             vmem_limit_bytes=VMEM_LIMIT),
        name="combine",
    )(back3, back3, y.reshape(-1, WIN_ROWS, D), h, route, g.reshape(1, D))


def kernel(x, norm_mix_g, w_in, w_proj_sb, w_proj_moba, w_out, rel_bias, norm_ffn_g, w_router, b_router,
           w_gate_up, b_gate_up, w_down, b_down, norm_final_g):
    batch, seq, D = x.shape
    assert norm_mix_g.shape[0] == 1, "single-layer trunk"
    tm = 256
    xt = x.reshape(batch * seq, D)
    proj = _in_proj(xt, norm_mix_g[0], w_in[0].astype(BF16))
    ysb = _sb_attention(proj, batch=batch, seq=seq)
    ymb = _moba_attention(proj, _moba_bias(rel_bias), rel_bias, batch=batch, seq=seq)
    wr_pad = jnp.pad(w_router[0], ((0, 0), (0, LANES - N_EXPERTS)))
    br_pad = jnp.pad(b_router[0], (0, LANES - N_EXPERTS)).reshape(1, LANES)
    h, xl, route, cnt = _post_attn(ysb, ymb, proj, xt, w_proj_sb[0].astype(BF16), w_proj_moba[0].astype(BF16),
                                   w_out[0].astype(BF16), norm_ffn_g[0], wr_pad, br_pad, tm=tm)
    src_win, back_win, tile_expert, tile_valid = _window_plan(cnt[:, 0, :N_EXPERTS].astype(jnp.int32), tm)
    y = _experts(xl, src_win, tile_expert, tile_valid, w_gate_up[0], b_gate_up[0], w_down[0], b_down[0])
    out = _combine(y, back_win, h, route, norm_final_g, tm=tm)
    return out.reshape(batch, seq, D)
```

```python
import functools
import math

import numpy as np
import jax
import jax.numpy as jnp
from jax import lax
from jax.experimental import pallas as pl
from jax.experimental.pallas import tpu as pltpu

HEAD_DIM = 64
N_HEADS = 8
D_MIX = N_HEADS * HEAD_DIM
MOBA_BLOCK = 256
MOBA_TOPK = 3
REL_BUCKETS = 32
REL_MAX_DIST = 128
N_EXPERTS = 32
TOP_K = 4
SWIGLU_LIMIT = 7.0
SWIGLU_ALPHA = 1.702
RMS_EPS = 1e-6

LANES = 128
NEG_BIG = -1e30
VMEM_LIMIT = 56 * 1024 * 1024

F32 = jnp.float32
BF16 = jnp.bfloat16


def _split_bf16(a):
    hi = a.astype(BF16)
    lo = (a - hi.astype(F32)).astype(BF16)
    return hi, lo


def _dot_nt(a, b):
    return lax.dot_general(a, b, (((1,), (1,)), ((), ())), preferred_element_type=F32)


def _in_proj_kernel(x_ref, g_ref, w_ref, o_ref, *, n_slab, slab):
    x = x_ref[...]
    var = jnp.mean(x * x, axis=-1, keepdims=True)
    xn = ((x * lax.rsqrt(var + RMS_EPS)) * g_ref[...]).astype(BF16)
    for j in range(n_slab):
        o_ref[j] = jnp.dot(xn, w_ref[:, j * slab:(j + 1) * slab],
                           preferred_element_type=F32).astype(o_ref.dtype)


def _in_proj(xt, g, w_bf16, *, tm=512, slab=D_MIX):
    T, D = xt.shape
    n_slab = w_bf16.shape[1] // slab
    return pl.pallas_call(
        functools.partial(_in_proj_kernel, n_slab=n_slab, slab=slab),
        grid=(T // tm,),
        in_specs=[pl.BlockSpec((tm, D), lambda i: (i, 0)),
                  pl.BlockSpec((1, D), lambda i: (0, 0)),
                  pl.BlockSpec(w_bf16.shape, lambda i: (0, 0))],
        out_specs=pl.BlockSpec((n_slab, tm, slab), lambda i: (0, i, 0)),
        out_shape=jax.ShapeDtypeStruct((n_slab, T, slab), BF16),
        compiler_params=pltpu.CompilerParams(dimension_semantics=("arbitrary",),
                                             vmem_limit_bytes=VMEM_LIMIT),
        name="in_proj",
    )(xt, g.reshape(1, D), w_bf16)


def _sb_kernel(q_ref, k_ref, v_ref, o_ref, *, tq, tk, zscale):
    i = pl.program_id(2)
    q = q_ref[0]
    lane = lax.broadcasted_iota(jnp.int32, (1, LANES), 1)
    row = lax.broadcasted_iota(jnp.int32, (tq, tk), 0)
    col = lax.broadcasted_iota(jnp.int32, (tq, tk), 1)
    jj = lax.broadcasted_iota(jnp.int32, (tk, tk), 0)
    ss = lax.broadcasted_iota(jnp.int32, (tk, tk), 1)
    later = (jj >= ss).astype(BF16)
    zero = jnp.zeros((), q.dtype)
    qhs = [jnp.where(lane // HEAD_DIM == h, q, zero) for h in range(2)]
    n_sub = tq // tk

    def q_tile_keys(base, carry, diagonal):
        tiles = [(h, sub) for h in range(2) for sub in reversed(range(n_sub))]
        starts = {sub: pl.multiple_of(base + sub * tk, tk) for sub in range(n_sub)}
        z2 = {t: _dot_nt(qhs[t[0]], k_ref[0, pl.ds(starts[t[1]], tk), :]) * zscale for t in tiles}
        drop, parts = {}, {}
        for t in tiles:
            neg_abs = lax.bitcast_convert_type(
                lax.bitcast_convert_type(z2[t], jnp.uint32) | jnp.uint32(0x80000000), F32)
            d = jnp.maximum(z2[t], 0.0) + jnp.log2(1.0 + jnp.exp2(neg_abs))
            if diagonal:
                d = jnp.where((t[1] * tk + col) < row, d, 0.0)
            drop[t] = d
            parts[t] = d.astype(BF16)
        inner = {t: jnp.dot(parts[t], later, preferred_element_type=F32) for t in tiles}
        c = [carry[h][0] for h in range(2)]
        a = {}
        for t in tiles:
            h = t[0]
            w = jnp.exp2((z2[t] - c[h]) - inner[t])
            if diagonal:
                w = jnp.where((t[1] * tk + col) < row, w, 0.0)
            a[t] = w.astype(BF16)
            c[h] = c[h] + jnp.sum(drop[t], axis=-1, keepdims=True)
        acc = [carry[h][1] for h in range(2)]
        for t in tiles:
            acc[t[0]] = acc[t[0]] + jnp.dot(a[t], v_ref[0, pl.ds(starts[t[1]], tk), :],
                                            preferred_element_type=F32)
        return tuple((c[h], acc[h]) for h in range(2))

    init = tuple((jnp.zeros((tq, 1), F32), jnp.zeros((tq, LANES), F32)) for _ in range(2))
    carry = q_tile_keys(i * tq, init, True)
    carry = lax.fori_loop(0, i, lambda it, cr: q_tile_keys((i - 1 - it) * tq, cr, False), carry)
    o_ref[...] = jnp.where(lane < HEAD_DIM, carry[0][1], carry[1][1]).astype(o_ref.dtype)


def _sb_attention(proj, *, batch, seq, tq=512, tk=256):
    T = batch * seq
    nq = seq // tq
    n_hp = D_MIX // LANES
    return pl.pallas_call(
        functools.partial(_sb_kernel, tq=tq, tk=tk, zscale=math.log2(math.e) / math.sqrt(HEAD_DIM)),
        grid=(batch, n_hp, nq),
        in_specs=[pl.BlockSpec((1, tq, LANES), lambda b, p, i: (0, b * nq + i, p)),
                  pl.BlockSpec((1, seq, LANES), lambda b, p, i: (1, b, p)),
                  pl.BlockSpec((1, seq, LANES), lambda b, p, i: (2, b, p))],
        out_specs=pl.BlockSpec((tq, LANES), lambda b, p, i: (b * nq + i, p)),
        out_shape=jax.ShapeDtypeStruct((T, D_MIX), BF16),
        compiler_params=pltpu.CompilerParams(
            dimension_semantics=("arbitrary", "arbitrary", "arbitrary"), vmem_limit_bytes=VMEM_LIMIT),
        name="sb_attn",
    )(proj, proj, proj)


def _rel_bucket_np(dist):
    n = np.maximum(dist, 0)
    max_exact = REL_BUCKETS // 2
    nf = np.maximum(n, 1).astype(np.float64)
    large = max_exact + (np.log(nf / max_exact) / math.log(REL_MAX_DIST / max_exact)
                         * (REL_BUCKETS - max_exact)).astype(np.int32)
    large = np.minimum(large, REL_BUCKETS - 1)
    return np.where(n < max_exact, n, large).astype(np.int32)


def _bias_kernel(rel_ref, bkt_ref, o_ref):
    h = pl.program_id(0)
    for d in range(2):
        bkt = bkt_ref[d]
        acc = jnp.zeros(bkt.shape, F32)
        for b in range(REL_BUCKETS):
            acc = jnp.where(bkt == b, rel_ref[b, h], acc)
        o_ref[0, d] = acc


def _moba_bias(rel_bias):
    L = MOBA_BLOCK
    r = np.arange(L)[:, None]
    c = np.arange(L)[None, :]
    bkt = np.stack([_rel_bucket_np(r - c), _rel_bucket_np(L + r - c)])
    return pl.pallas_call(
        _bias_kernel,
        grid=(N_HEADS,),
        in_specs=[pl.BlockSpec(memory_space=pltpu.SMEM),
                  pl.BlockSpec((2, L, L), lambda h: (0, 0, 0))],
        out_specs=pl.BlockSpec((1, 2, L, L), lambda h: (h, 0, 0, 0)),
        out_shape=jax.ShapeDtypeStruct((N_HEADS, 2, L, L), F32),
        name="moba_bias",
    )(rel_bias, jnp.asarray(bkt))


def _moba_kernel(rel_ref, q_ref, k_ref, v_ref, bias_ref, o_ref, km_ref, *, L, nblk, scale):
    p = pl.program_id(1)
    i = pl.program_id(2)
    lane = lax.broadcasted_iota(jnp.int32, (1, LANES), 1)
    row = lax.broadcasted_iota(jnp.int32, (L, L), 0)
    col = lax.broadcasted_iota(jnp.int32, (L, L), 1)

    @pl.when(i == 0)
    def _():
        km_ref[...] = jnp.mean(k_ref[0].astype(F32).reshape(nblk, L, LANES), axis=1)

    def attend(n_off):
        rr = lax.broadcasted_iota(jnp.int32, (n_off, 1), 0)
        km = jnp.concatenate([km_ref[pl.ds(jnp.maximum(i - r, 0), 1), :] for r in range(n_off)], axis=0)
        km = jnp.concatenate([km, jnp.zeros((16 - n_off, LANES), F32)], axis=0)
        km_hi, km_lo = _split_bf16(km)
        km_lo2 = (km - km_hi.astype(F32) - km_lo.astype(F32)).astype(BF16)
        q = q_ref[0] * jnp.asarray(scale, q_ref.dtype)
        past = (rr >= 1) & (rr <= i)
        zero = jnp.zeros((), q.dtype)
        outs, all_scores = [], []
        for h in range(2):
            own = lane // HEAD_DIM == h
            aux0 = (1 - h) * HEAD_DIM
            qh = jnp.where(own, q, zero)
            gs = (_dot_nt(km_hi, qh) + _dot_nt(km_lo, qh) + _dot_nt(km_lo2, qh))[:n_off]
            pen = jnp.zeros((n_off, L), F32)
            for r in range(1, n_off):
                g_r = gs[r:r + 1, :]
                beats = past & ((gs > g_r) | ((gs == g_r) & (rr > r)))
                cnt = jnp.sum(beats.astype(F32), axis=0, keepdims=True)
                keep = (cnt < MOBA_TOPK) & (r <= i)
                pen = jnp.where((rr == r) & jnp.logical_not(keep), NEG_BIG, pen)
            far = jnp.full((n_off, L), rel_ref[REL_BUCKETS - 1, 2 * p + h], F32)
            f_hi, f_lo = _split_bf16(far)
            f_lo2 = far - f_hi.astype(F32) - f_lo.astype(F32)
            far3 = jnp.where(rr == 0, f_hi.astype(F32), jnp.where(rr == 1, f_lo.astype(F32),
                                                                  jnp.where(rr == 2, f_lo2, 0.0)))
            aux = [pen, far3, jnp.zeros((LANES - aux0 - 2 * n_off, L), F32)]
            aux = jnp.concatenate(([jnp.zeros((aux0, L), F32)] if aux0 else []) + aux, axis=0)
            q_aug = jnp.where(own, q, aux.T.astype(q.dtype))
            scores, values = [], []
            for r in range(n_off):
                ks = pl.multiple_of(jnp.maximum(i - r, 0) * L, L)
                kblk = k_ref[0, pl.ds(ks, L), :]
                values.append(v_ref[0, pl.ds(ks, L), :])
                if r == 0:
                    s = _dot_nt(qh, kblk) + bias_ref[h, 0]
                    s = jnp.where(col <= row, s, NEG_BIG)
                else:
                    hot = (lane == aux0 + r)
                    if r >= 2:
                        hot = hot | ((lane >= aux0 + n_off) & (lane < aux0 + n_off + 3))
                    k_aug = jnp.where(own, kblk, jnp.where(hot, 1.0, 0.0).astype(kblk.dtype))
                    s = _dot_nt(q_aug, k_aug)
                    if r == 1:
                        s = s + bias_ref[h, 1]
                scores.append(s)
            all_scores.append(scores)
        for scores in all_scores:
            m = scores[0]
            for s in scores[1:]:
                m = jnp.maximum(m, s)
            m = jnp.max(m, axis=-1, keepdims=True)
            probs = [jnp.exp(s - m) for s in scores]
            l = probs[0]
            for pr in probs[1:]:
                l = l + pr
            l = jnp.sum(l, axis=-1, keepdims=True)
            acc = jnp.dot(probs[0].astype(BF16), values[0], preferred_element_type=F32)
            for pr, vblk in zip(probs[1:], values[1:]):
                acc = acc + jnp.dot(pr.astype(BF16), vblk, preferred_element_type=F32)
            outs.append(acc / l)
        o_ref[...] = jnp.where(lane < HEAD_DIM, outs[0], outs[1]).astype(o_ref.dtype)

    half = nblk // 2
    if half >= MOBA_TOPK:
        pl.when(i < half)(lambda: attend(half))
        pl.when(i >= half)(lambda: attend(nblk))
    else:
        attend(nblk)


def _moba_attention(proj, bias, rel_bias, *, batch, seq):
    T = batch * seq
    L = MOBA_BLOCK
    nblk = seq // L
    n_hp = D_MIX // LANES
    return pl.pallas_call(
        functools.partial(_moba_kernel, L=L, nblk=nblk, scale=1.0 / math.sqrt(HEAD_DIM)),
        grid=(batch, n_hp, nblk),
        in_specs=[pl.BlockSpec(memory_space=pltpu.SMEM),
                  pl.BlockSpec((1, L, LANES), lambda b, p, i: (3, b * nblk + i, p)),
                  pl.BlockSpec((1, seq, LANES), lambda b, p, i: (4, b, p)),
                  pl.BlockSpec((1, seq, LANES), lambda b, p, i: (5, b, p)),
                  pl.BlockSpec((2, 2, L, L), lambda b, p, i: (p, 0, 0, 0))],
        out_specs=pl.BlockSpec((L, LANES), lambda b, p, i: (b * nblk + i, p)),
        out_shape=jax.ShapeDtypeStruct((T, D_MIX), BF16),
        scratch_shapes=[pltpu.VMEM((nblk, LANES), F32)],
        compiler_params=pltpu.CompilerParams(
            dimension_semantics=("arbitrary", "arbitrary", "arbitrary"), vmem_limit_bytes=VMEM_LIMIT),
        name="moba_attn",
    )(rel_bias, proj, proj, proj, bias)


WIN_ROWS = 16
WINS_PER_TILE = 32


def _local_slots(tm):
    return tm * TOP_K + N_EXPERTS * WIN_ROWS


def _post_attn_kernel(ysb_ref, ymb_ref, gsb_ref, gmb_ref, x_ref, wsb_ref, wmb_ref, wo_ref, nrm_ref,
                      wr_ref, br_ref, h_ref, xl_ref, route_ref, cnt_ref, *, tm, chunk):
    def gate(g_ref):
        g = jnp.concatenate([g_ref[0], g_ref[1]], axis=1).astype(F32)
        return 1.0 / (1.0 + jnp.exp(-g))

    merged = (gate(gsb_ref) * jnp.dot(ysb_ref[...], wsb_ref[...], preferred_element_type=F32)
              + gate(gmb_ref) * jnp.dot(ymb_ref[...], wmb_ref[...], preferred_element_type=F32))
    hres = x_ref[...] + jnp.dot(merged.astype(BF16), wo_ref[...], preferred_element_type=F32)
    h_ref[...] = hres
    var = jnp.mean(hres * hres, axis=-1, keepdims=True)
    hn = (hres * lax.rsqrt(var + RMS_EPS)) * nrm_ref[...]
    a_hi, a_lo = _split_bf16(hn)
    w_hi, w_lo = _split_bf16(wr_ref[...])
    logits = (jnp.dot(a_hi, w_hi, preferred_element_type=F32)
              + jnp.dot(a_hi, w_lo, preferred_element_type=F32)
              + jnp.dot(a_lo, w_hi, preferred_element_type=F32)) + br_ref[...]
    lane = lax.broadcasted_iota(jnp.int32, (1, LANES), 1).astype(F32)
    work = jnp.where(lane < N_EXPERTS, logits, NEG_BIG)
    vals, hots = [], []
    for _ in range(TOP_K):
        mx = jnp.max(work, axis=-1, keepdims=True)
        idx = jnp.min(jnp.where(work == mx, lane, float(LANES)), axis=-1, keepdims=True)
        hot = lane == idx
        work = jnp.where(hot, NEG_BIG, work)
        vals.append(mx)
        hots.append(hot)
    exps = [jnp.exp(v - vals[0]) for v in vals]
    denom = exps[0] + exps[1] + exps[2] + exps[3]
    sel = jnp.zeros((tm, LANES), F32)
    for hot in hots:
        sel = jnp.where(hot, 1.0, sel)
    r = lax.broadcasted_iota(jnp.int32, (tm, tm), 0)
    c = lax.broadcasted_iota(jnp.int32, (tm, tm), 1)
    earlier = jnp.dot((c < r).astype(BF16), sel.astype(BF16), preferred_element_type=F32)
    cnt = jnp.sum(sel, axis=0, keepdims=True)
    padded = jnp.floor((cnt + (WIN_ROWS - 1)) * (1.0 / WIN_ROWS)) * WIN_ROWS
    er = lax.broadcasted_iota(jnp.int32, (LANES, LANES), 0)
    ec = lax.broadcasted_iota(jnp.int32, (LANES, LANES), 1)
    lower = jnp.dot(jnp.broadcast_to(padded, (8, LANES)).astype(BF16), (er < ec).astype(BF16),
                    preferred_element_type=F32)[0:1]
    slot = lower + earlier
    slab = jnp.zeros((tm, LANES), F32)
    for k, hot in enumerate(hots):
        slab = jnp.where(lane == k, exps[k] / denom, slab)
        slab = jnp.where(lane == TOP_K + k, jnp.sum(jnp.where(hot, slot, 0.0), axis=-1, keepdims=True), slab)
    route_ref[...] = slab
    cnt_ref[0] = jnp.broadcast_to(cnt, (8, LANES))
    slots_t = slab.T
    hn_bf = hn.astype(BF16)
    for ci in range(_local_slots(tm) // chunk):
        rid = (lax.broadcasted_iota(jnp.int32, (chunk, 1), 0) + ci * chunk).astype(F32)
        onehot = rid == slots_t[TOP_K:TOP_K + 1, :]
        for k in range(1, TOP_K):
            onehot = onehot | (rid == slots_t[TOP_K + k:TOP_K + k + 1, :])
        xl_ref[ci * chunk:(ci + 1) * chunk, :] = jnp.dot(
            jnp.where(onehot, 1.0, 0.0).astype(BF16), hn_bf, preferred_element_type=F32).astype(xl_ref.dtype)


def _post_attn(ysb, ymb, proj, xt, wsb, wmb, wo, nrm_g, wr_pad, br_pad, *, tm):
    T, D = xt.shape
    half = D // 2
    n_tt = T // tm
    ls = _local_slots(tm)
    const = lambda i: (0, 0)
    return pl.pallas_call(
        functools.partial(_post_attn_kernel, tm=tm, chunk=256),
        grid=(n_tt,),
        in_specs=[pl.BlockSpec((tm, D_MIX), lambda i: (i, 0)),
                  pl.BlockSpec((tm, D_MIX), lambda i: (i, 0)),
                  pl.BlockSpec((2, tm, half), lambda i: (3, i, 0)),
                  pl.BlockSpec((2, tm, half), lambda i: (4, i, 0)),
                  pl.BlockSpec((tm, D), lambda i: (i, 0)),
                  pl.BlockSpec(wsb.shape, const),
                  pl.BlockSpec(wmb.shape, const),
                  pl.BlockSpec(wo.shape, const),
                  pl.BlockSpec((1, D), const),
                  pl.BlockSpec(wr_pad.shape, const),
                  pl.BlockSpec((1, LANES), const)],
        out_specs=[pl.BlockSpec((tm, D), lambda i: (i, 0)),
                   pl.BlockSpec((ls, D), lambda i: (i, 0)),
                   pl.BlockSpec((tm, LANES), lambda i: (i, 0)),
                   pl.BlockSpec((1, 8, LANES), lambda i: (i, 0, 0))],
        out_shape=[jax.ShapeDtypeStruct((T, D), F32),
                   jax.ShapeDtypeStruct((n_tt * ls, D), BF16),
                   jax.ShapeDtypeStruct((T, LANES), F32),
                   jax.ShapeDtypeStruct((n_tt, 8, LANES), F32)],
        compiler_params=pltpu.CompilerParams(dimension_semantics=("arbitrary",),
                                             vmem_limit_bytes=VMEM_LIMIT),
        name="post_attn",
    )(ysb, ymb, proj, proj, xt, wsb, wmb, wo, nrm_g.reshape(1, D), wr_pad, br_pad)


def _window_plan(counts, tm):
    n_tt, n_exp = counts.shape
    lw_tile = _local_slots(tm) // WIN_ROWS
    n_tiles = -(-(n_tt * (tm * TOP_K // WIN_ROWS + n_exp)) // WINS_PER_TILE) + n_exp
    nw = (counts + WIN_ROWS - 1) // WIN_ROWS
    local_start = jnp.cumsum(nw, axis=1) - nw
    per_expert = jnp.sum(nw, axis=0)
    per_expert_pad = -(-per_expert // WINS_PER_TILE) * WINS_PER_TILE
    expert_end = jnp.cumsum(per_expert_pad)
    sorted_start = (expert_end - per_expert_pad)[None, :] + jnp.cumsum(nw, axis=0) - nw
    run_start = sorted_start.T.reshape(-1)
    run_len = nw.T.reshape(-1)
    run_src = (jnp.arange(n_tt, dtype=jnp.int32)[:, None] * lw_tile + local_start).T.reshape(-1)
    g = jnp.arange(n_tiles * WINS_PER_TILE, dtype=jnp.int32)[:, None]
    off = g - run_start[None, :]
    src_win = jnp.sum(jnp.where((off >= 0) & (off < run_len[None, :]), run_src[None, :] + off, 0), axis=1)
    off_l = jnp.arange(lw_tile, dtype=jnp.int32)[None, :, None] - local_start[:, None, :]
    back_win = jnp.sum(jnp.where((off_l >= 0) & (off_l < nw[:, None, :]), sorted_start[:, None, :] + off_l, 0),
                       axis=2)
    tile_first = (jnp.arange(n_tiles, dtype=jnp.int32) * WINS_PER_TILE)[:, None]
    tile_valid = tile_first[:, 0] < expert_end[-1]
    used = (per_expert > 0)[None, :]
    expert_start = (expert_end - per_expert_pad)[None, :]
    ids = jnp.arange(n_exp, dtype=jnp.int32)[None, :]
    tile_expert = jnp.minimum(jnp.sum(expert_end[None, :] <= tile_first, axis=1), n_exp - 1)
    opens = jnp.sum(used & (expert_start == tile_first), axis=1) > 0
    buffer = (jnp.sum(used & (expert_start <= tile_first), axis=1) - 1) % 2
    next_expert = jnp.min(jnp.where(used & (expert_start > tile_first), ids, n_exp), axis=1)
    next_expert = jnp.where(next_expert == n_exp, -1, next_expert)
    tile_plan = tuple(a.astype(jnp.int32) for a in (tile_expert, tile_valid, opens, buffer, next_expert))
    return src_win.astype(jnp.int32), back_win.astype(jnp.int32), tile_plan


def _experts_kernel(te_ref, tv_ref, first_ref, par_ref, ne_ref, src_ref, nxt_ref, xl_ref, wgu_ref, bgu_ref, wd_ref,
                    bd_ref, y_ref, xbuf, wgu_f, wd_f, wgu_s, wd_s, sem, wsem, *, d_exp):
    j = pl.program_id(0)
    n_tiles = pl.num_programs(0)
    slot = j % 2

    def window_copy(win, dst_slot, w):
        return pltpu.make_async_copy(xl_ref.at[win], xbuf.at[dst_slot, w], sem.at[dst_slot])

    def start_gather(idx_ref, dst_slot):
        for w in range(WINS_PER_TILE):
            window_copy(idx_ref[0, 0, w], dst_slot, w).start()

    def weight_copies(expert, buf):
        return (pltpu.make_async_copy(wgu_ref.at[expert], wgu_f.at[buf], wsem.at[buf, 0]),
                pltpu.make_async_copy(wd_ref.at[expert], wd_f.at[buf], wsem.at[buf, 1]))

    @pl.when((j == 0) & (tv_ref[0] > 0))
    def _():
        for cp in weight_copies(te_ref[0], par_ref[0]):
            cp.start()
        start_gather(src_ref, 0)

    @pl.when((j + 1 < n_tiles) & (tv_ref[jnp.minimum(j + 1, n_tiles - 1)] > 0))
    def _():
        start_gather(nxt_ref, 1 - slot)

    @pl.when((first_ref[j] > 0) & (tv_ref[j] > 0))
    def _():
        buf = par_ref[j]

        @pl.when(ne_ref[j] >= 0)
        def _():
            for cp in weight_copies(ne_ref[j], 1 - buf):
                cp.start()

        for cp in weight_copies(te_ref[j], buf):
            cp.wait()
        wgu_s[...] = wgu_f[buf].astype(BF16)
        wd_s[...] = wd_f[buf].astype(BF16)

    @pl.when(tv_ref[j] > 0)
    def _():
        for w in range(WINS_PER_TILE):
            window_copy(0, slot, w).wait()
        x = xbuf[slot].reshape(WINS_PER_TILE * WIN_ROWS, xbuf.shape[-1])
        hgu = jnp.dot(x, wgu_s[...], preferred_element_type=F32) + bgu_ref[0]
        gate = jnp.minimum(hgu[:, :d_exp], SWIGLU_LIMIT)
        up = jnp.clip(hgu[:, d_exp:], -SWIGLU_LIMIT, SWIGLU_LIMIT)
        act = (up + 1.0) * (gate * (1.0 / (1.0 + jnp.exp(-SWIGLU_ALPHA * gate))))
        y = jnp.dot(act.astype(BF16), wd_s[...], preferred_element_type=F32) + bd_ref[0]
        y_ref[...] = y.astype(y_ref.dtype)

    @pl.when(tv_ref[j] == 0)
    def _():
        y_ref[...] = jnp.zeros_like(y_ref)


def _experts(xl, src_win, tile_plan, wgu, bgu, wd, bd):
    D = xl.shape[1]
    E, _, d2 = wgu.shape
    d_exp = d2 // 2
    n_tiles = tile_plan[0].shape[0]
    tmm = WINS_PER_TILE * WIN_ROWS
    src3 = src_win.reshape(n_tiles, 1, WINS_PER_TILE)
    grid_spec = pltpu.PrefetchScalarGridSpec(
        num_scalar_prefetch=len(tile_plan),
        grid=(n_tiles,),
        in_specs=[pl.BlockSpec((1, 1, WINS_PER_TILE), lambda j, *_: (j, 0, 0), memory_space=pltpu.SMEM),
                  pl.BlockSpec((1, 1, WINS_PER_TILE), lambda j, *_: (jnp.minimum(j + 1, n_tiles - 1), 0, 0),
                               memory_space=pltpu.SMEM),
                  pl.BlockSpec(memory_space=pl.ANY),
                  pl.BlockSpec(memory_space=pl.ANY),
                  pl.BlockSpec((1, 1, d2), lambda j, te, *_: (te[j], 0, 0)),
                  pl.BlockSpec(memory_space=pl.ANY),
                  pl.BlockSpec((1, 1, D), lambda j, te, *_: (te[j], 0, 0))],
        out_specs=pl.BlockSpec((tmm, D), lambda j, *_: (j, 0)),
        scratch_shapes=[pltpu.VMEM((2, WINS_PER_TILE, WIN_ROWS, D), BF16),
                        pltpu.VMEM((2, D, d2), F32), pltpu.VMEM((2, d_exp, D), F32),
                        pltpu.VMEM((D, d2), BF16), pltpu.VMEM((d_exp, D), BF16),
                        pltpu.SemaphoreType.DMA((2,)), pltpu.SemaphoreType.DMA((2, 2))],
    )
    return pl.pallas_call(
        functools.partial(_experts_kernel, d_exp=d_exp),
        grid_spec=grid_spec,
        out_shape=jax.ShapeDtypeStruct((n_tiles * tmm, D), BF16),
        compiler_params=pltpu.CompilerParams(dimension_semantics=("arbitrary",),
                                             vmem_limit_bytes=VMEM_LIMIT),
        name="experts",
    )(*tile_plan, src3, src3, xl.reshape(-1, WIN_ROWS, D), wgu, bgu.reshape(E, 1, d2), wd, bd.reshape(E, 1, D))


def _combine_kernel(back_ref, nxt_ref, y_ref, h_ref, route_ref, g_ref, o_ref, ybuf, sem, *, tm, chunk):
    i = pl.program_id(0)
    slot = i % 2
    n_win = ybuf.shape[1]

    def window_copy(win, dst_slot, w):
        return pltpu.make_async_copy(y_ref.at[win], ybuf.at[dst_slot, w], sem.at[dst_slot])

    def start_fetch(idx_ref, dst_slot):
        for w in range(n_win):
            window_copy(idx_ref[0, 0, w], dst_slot, w).start()

    @pl.when(i == 0)
    def _():
        start_fetch(back_ref, 0)

    @pl.when(i + 1 < pl.num_programs(0))
    def _():
        start_fetch(nxt_ref, 1 - slot)

    route = route_ref[...]
    weights = [route[:, k:k + 1] for k in range(TOP_K)]
    slots = [route[:, TOP_K + k:TOP_K + k + 1] for k in range(TOP_K)]
    for w in range(n_win):
        window_copy(0, slot, w).wait()
    acc = h_ref[...]
    for ci in range(n_win * WIN_ROWS // chunk):
        sid = (lax.broadcasted_iota(jnp.int32, (1, chunk), 1) + ci * chunk).astype(F32)
        wmat = jnp.zeros((tm, chunk), F32)
        for k in range(TOP_K):
            wmat = jnp.where(sid == slots[k], weights[k], wmat)
        hi, lo = _split_bf16(wmat)
        rows = ybuf[slot, ci * chunk // WIN_ROWS:(ci + 1) * chunk // WIN_ROWS].reshape(chunk, ybuf.shape[-1])
        acc = acc + jnp.dot(hi, rows, preferred_element_type=F32) + jnp.dot(lo, rows, preferred_element_type=F32)
    var = jnp.mean(acc * acc, axis=-1, keepdims=True)
    o_ref[...] = (acc * lax.rsqrt(var + RMS_EPS)) * g_ref[...]


def _combine(y, back_win, h, route, g, *, tm):
    T, D = h.shape
    n_tt, n_win = back_win.shape
    back3 = back_win.reshape(n_tt, 1, n_win)
    return pl.pallas_call(
        functools.partial(_combine_kernel, tm=tm, chunk=256),
        grid=(n_tt,),
        in_specs=[pl.BlockSpec((1, 1, n_win), lambda i: (i, 0, 0), memory_space=pltpu.SMEM),
                  pl.BlockSpec((1, 1, n_win), lambda i: (jnp.minimum(i + 1, n_tt - 1), 0, 0),
                               memory_space=pltpu.SMEM),
                  pl.BlockSpec(memory_space=pl.ANY),
                  pl.BlockSpec((tm, D), lambda i: (i, 0)),
                  pl.BlockSpec((tm, LANES), lambda i: (i, 0)),
                  pl.BlockSpec((1, D), lambda i: (0, 0))],
        out_specs=pl.BlockSpec((tm, D), lambda i: (i, 0)),
        out_shape=jax.ShapeDtypeStruct((T, D), F32),
        scratch_shapes=[pltpu.VMEM((2, n_win, WIN_ROWS, D), BF16), pltpu.SemaphoreType.DMA((2,))],
        compiler_params=pltpu.CompilerParams(dimension_semantics=("arbitrary",),
                                             vmem_limit_bytes=VMEM_LIMIT),
        name="combine",
    )(back3, back3, y.reshape(-1, WIN_ROWS, D), h, route, g.reshape(1, D))


def kernel(x, norm_mix_g, w_in, w_proj_sb, w_proj_moba, w_out, rel_bias, norm_ffn_g, w_router, b_router,
           w_gate_up, b_gate_up, w_down, b_down, norm_final_g):
    batch, seq, D = x.shape
    assert norm_mix_g.shape[0] == 1, "single-layer trunk"
    tm = 256
    xt = x.reshape(batch * seq, D)
    proj = _in_proj(xt, norm_mix_g[0], w_in[0].astype(BF16))
    ysb = _sb_attention(proj, batch=batch, seq=seq)
    ymb = _moba_attention(proj, _moba_bias(rel_bias), rel_bias, batch=batch, seq=seq)
    wr_pad = jnp.pad(w_router[0], ((0, 0), (0, LANES - N_EXPERTS)))
    br_pad = jnp.pad(b_router[0], (0, LANES - N_EXPERTS)).reshape(1, LANES)
    h, xl, route, cnt = _post_attn(ysb, ymb, proj, xt, w_proj_sb[0].astype(BF16), w_proj_moba[0].astype(BF16),
                                   w_out[0].astype(BF16), norm_ffn_g[0], wr_pad, br_pad, tm=tm)
    src_win, back_win, tile_plan = _window_plan(cnt[:, 0, :N_EXPERTS].astype(jnp.int32), tm)
    y = _experts(xl, src_win, tile_plan, w_gate_up[0], b_gate_up[0], w_down[0], b_down[0])
    out = _combine(y, back_win, h, route, norm_final_g, tm=tm)
    return out.reshape(batch, seq, D)
```

```python
import functools
import math

import numpy as np
import jax
import jax.numpy as jnp
from jax import lax
from jax.experimental import pallas as pl
from jax.experimental.pallas import tpu as pltpu

HEAD_DIM = 64
N_HEADS = 8
D_MIX = N_HEADS * HEAD_DIM
MOBA_BLOCK = 256
MOBA_TOPK = 3
REL_BUCKETS = 32
REL_MAX_DIST = 128
N_EXPERTS = 32
TOP_K = 4
SWIGLU_LIMIT = 7.0
SWIGLU_ALPHA = 1.702
RMS_EPS = 1e-6

LANES = 128
NEG_BIG = -1e30
VMEM_LIMIT = 56 * 1024 * 1024

F32 = jnp.float32
BF16 = jnp.bfloat16


def _split_bf16(a):
    hi = a.astype(BF16)
    lo = (a - hi.astype(F32)).astype(BF16)
    return hi, lo


def _dot_nt(a, b):
    return lax.dot_general(a, b, (((1,), (1,)), ((), ())), preferred_element_type=F32)


def _in_proj_kernel(x_ref, g_ref, w_ref, o_ref, *, n_slab, slab):
    x = x_ref[...]
    var = jnp.mean(x * x, axis=-1, keepdims=True)
    xn = ((x * lax.rsqrt(var + RMS_EPS)) * g_ref[...]).astype(BF16)
    for j in range(n_slab):
        o_ref[j] = jnp.dot(xn, w_ref[:, j * slab:(j + 1) * slab],
                           preferred_element_type=F32).astype(o_ref.dtype)


def _in_proj(xt, g, w_bf16, *, tm=512, slab=D_MIX):
    T, D = xt.shape
    n_slab = w_bf16.shape[1] // slab
    return pl.pallas_call(
        functools.partial(_in_proj_kernel, n_slab=n_slab, slab=slab),
        grid=(T // tm,),
        in_specs=[pl.BlockSpec((tm, D), lambda i: (i, 0)),
                  pl.BlockSpec((1, D), lambda i: (0, 0)),
                  pl.BlockSpec(w_bf16.shape, lambda i: (0, 0))],
        out_specs=pl.BlockSpec((n_slab, tm, slab), lambda i: (0, i, 0)),
        out_shape=jax.ShapeDtypeStruct((n_slab, T, slab), BF16),
        compiler_params=pltpu.CompilerParams(dimension_semantics=("arbitrary",),
                                             vmem_limit_bytes=VMEM_LIMIT),
        name="in_proj",
    )(xt, g.reshape(1, D), w_bf16)


def _sb_kernel(q_ref, k_ref, v_ref, o_ref, *, tq, tk, zscale):
    i = pl.program_id(2)
    q = q_ref[0]
    lane = lax.broadcasted_iota(jnp.int32, (1, LANES), 1)
    jj = lax.broadcasted_iota(jnp.int32, (tk, tk), 0)
    ss = lax.broadcasted_iota(jnp.int32, (tk, tk), 1)
    later = (jj >= ss).astype(BF16)
    zero = jnp.zeros((), q.dtype)
    qhs = [jnp.where(lane // HEAD_DIM == h, q, zero) for h in range(2)]
    n_sub = tq // tk

    def q_tile_keys(base, carry, diagonal):
        tiles = [(h, sub) for sub in reversed(range(n_sub)) for h in range(2)]
        starts = {sub: pl.multiple_of(base + sub * tk, tk) for sub in range(n_sub)}
        c = [list(carry[h][0]) for h in range(2)]
        acc = [carry[h][1] for h in range(2)]
        z2, drop, inner, pv = {}, {}, {}, {}

        def first_row(t):
            return t[1] * tk if diagonal else 0

        def causal(t):
            shape = (tq - first_row(t), tk)
            return lax.broadcasted_iota(jnp.int32, shape, 1) < lax.broadcasted_iota(jnp.int32, shape, 0)

        def scores(t):
            z2[t] = _dot_nt(qhs[t[0]][first_row(t):], k_ref[0, pl.ds(starts[t[1]], tk), :]) * zscale

        def drops(t):
            d = jnp.maximum(z2[t], 0.0) + jnp.log2(1.0 + jnp.exp2(-jnp.abs(z2[t])))
            if diagonal:
                d = jnp.where(causal(t), d, 0.0)
            drop[t] = d
            inner[t] = jnp.dot(d.astype(BF16), later, preferred_element_type=F32)

        def weights(t):
            h, b0 = t[0], first_row(t) // tk
            blocks = [slice((b - b0) * tk, (b - b0 + 1) * tk) for b in range(b0, n_sub)]
            w = [jnp.exp2((z2[t][rows] - c[h][b0 + n]) - inner[t][rows]) for n, rows in enumerate(blocks)]
            w = w[0] if len(w) == 1 else jnp.concatenate(w, axis=0)
            if diagonal:
                w = jnp.where(causal(t), w, 0.0)
            for n, rows in enumerate(blocks):
                c[h][b0 + n] = c[h][b0 + n] + jnp.sum(drop[t][rows], axis=-1, keepdims=True)
            pv[t] = jnp.dot(w.astype(BF16), v_ref[0, pl.ds(starts[t[1]], tk), :], preferred_element_type=F32)

        def accumulate(t):
            h, r0 = t[0], first_row(t)
            acc[h] = acc[h] + pv[t] if r0 == 0 else jnp.concatenate([acc[h][:r0], acc[h][r0:] + pv[t]], axis=0)

        stages = (scores, drops, weights, accumulate)
        for step in range(len(tiles) + len(stages) - 1):
            for k, stage in enumerate(stages):
                if 0 <= step - k < len(tiles):
                    stage(tiles[step - k])
        return tuple((tuple(c[h]), acc[h]) for h in range(2))

    init = tuple((tuple(jnp.zeros((tk, 1), F32) for _ in range(n_sub)), jnp.zeros((tq, LANES), F32))
                 for _ in range(2))
    carry = q_tile_keys(i * tq, init, True)
    carry = lax.fori_loop(0, i, lambda it, cr: q_tile_keys((i - 1 - it) * tq, cr, False), carry)
    o_ref[...] = jnp.where(lane < HEAD_DIM, carry[0][1], carry[1][1]).astype(o_ref.dtype)


def _sb_attention(proj, *, batch, seq, tq=512, tk=256):
    T = batch * seq
    nq = seq // tq
    n_hp = D_MIX // LANES
    return pl.pallas_call(
        functools.partial(_sb_kernel, tq=tq, tk=tk, zscale=math.log2(math.e) / math.sqrt(HEAD_DIM)),
        grid=(batch, n_hp, nq),
        in_specs=[pl.BlockSpec((1, tq, LANES), lambda b, p, i: (0, b * nq + i, p)),
                  pl.BlockSpec((1, seq, LANES), lambda b, p, i: (1, b, p)),
                  pl.BlockSpec((1, seq, LANES), lambda b, p, i: (2, b, p))],
        out_specs=pl.BlockSpec((tq, LANES), lambda b, p, i: (b * nq + i, p)),
        out_shape=jax.ShapeDtypeStruct((T, D_MIX), BF16),
        compiler_params=pltpu.CompilerParams(
            dimension_semantics=("arbitrary", "arbitrary", "arbitrary"), vmem_limit_bytes=VMEM_LIMIT),
        name="sb_attn",
    )(proj, proj, proj)


def _rel_bucket_np(dist):
    n = np.maximum(dist, 0)
    max_exact = REL_BUCKETS // 2
    nf = np.maximum(n, 1).astype(np.float64)
    large = max_exact + (np.log(nf / max_exact) / math.log(REL_MAX_DIST / max_exact)
                         * (REL_BUCKETS - max_exact)).astype(np.int32)
    large = np.minimum(large, REL_BUCKETS - 1)
    return np.where(n < max_exact, n, large).astype(np.int32)


def _bias_kernel(rel_ref, bkt_ref, o_ref):
    h = pl.program_id(0)
    for d in range(2):
        bkt = bkt_ref[d]
        acc = jnp.zeros(bkt.shape, F32)
        for b in range(REL_BUCKETS):
            acc = jnp.where(bkt == b, rel_ref[b, h], acc)
        o_ref[0, d] = acc


def _moba_bias(rel_bias):
    L = MOBA_BLOCK
    r = np.arange(L)[:, None]
    c = np.arange(L)[None, :]
    bkt = np.stack([_rel_bucket_np(r - c), _rel_bucket_np(L + r - c)])
    return pl.pallas_call(
        _bias_kernel,
        grid=(N_HEADS,),
        in_specs=[pl.BlockSpec(memory_space=pltpu.SMEM),
                  pl.BlockSpec((2, L, L), lambda h: (0, 0, 0))],
        out_specs=pl.BlockSpec((1, 2, L, L), lambda h: (h, 0, 0, 0)),
        out_shape=jax.ShapeDtypeStruct((N_HEADS, 2, L, L), F32),
        name="moba_bias",
    )(rel_bias, jnp.asarray(bkt))


def _moba_kernel(rel_ref, q_ref, k_ref, v_ref, bias_ref, o_ref, km_ref, *, L, nblk, scale):
    p = pl.program_id(1)
    i = pl.program_id(2)
    lane = lax.broadcasted_iota(jnp.int32, (1, LANES), 1)
    row = lax.broadcasted_iota(jnp.int32, (L, L), 0)
    col = lax.broadcasted_iota(jnp.int32, (L, L), 1)

    @pl.when(i == 0)
    def _():
        km_ref[...] = jnp.mean(k_ref[0].astype(F32).reshape(nblk, L, LANES), axis=1)

    def attend(n_off):
        rr = lax.broadcasted_iota(jnp.int32, (n_off, 1), 0)
        km = jnp.concatenate([km_ref[pl.ds(jnp.maximum(i - r, 0), 1), :] for r in range(n_off)], axis=0)
        km = jnp.concatenate([km, jnp.zeros((16 - n_off, LANES), F32)], axis=0)
        km_hi, km_lo = _split_bf16(km)
        km_lo2 = (km - km_hi.astype(F32) - km_lo.astype(F32)).astype(BF16)
        q = q_ref[0] * jnp.asarray(scale, q_ref.dtype)
        past = (rr >= 1) & (rr <= i)
        zero = jnp.zeros((), q.dtype)
        outs, all_scores = [], []
        for h in range(2):
            own = lane // HEAD_DIM == h
            aux0 = (1 - h) * HEAD_DIM
            qh = jnp.where(own, q, zero)
            gs = (_dot_nt(km_hi, qh) + _dot_nt(km_lo, qh) + _dot_nt(km_lo2, qh))[:n_off]
            pen = jnp.zeros((n_off, L), F32)
            for r in range(1, n_off):
                g_r = gs[r:r + 1, :]
                beats = past & ((gs > g_r) | ((gs == g_r) & (rr > r)))
                cnt = jnp.sum(beats.astype(F32), axis=0, keepdims=True)
                keep = (cnt < MOBA_TOPK) & (r <= i)
                pen = jnp.where((rr == r) & jnp.logical_not(keep), NEG_BIG, pen)
            far = jnp.full((n_off, L), rel_ref[REL_BUCKETS - 1, 2 * p + h], F32)
            f_hi, f_lo = _split_bf16(far)
            f_lo2 = far - f_hi.astype(F32) - f_lo.astype(F32)
            far3 = jnp.where(rr == 0, f_hi.astype(F32), jnp.where(rr == 1, f_lo.astype(F32),
                                                                  jnp.where(rr == 2, f_lo2, 0.0)))
            aux = [pen, far3, jnp.zeros((LANES - aux0 - 2 * n_off, L), F32)]
            aux = jnp.concatenate(([jnp.zeros((aux0, L), F32)] if aux0 else []) + aux, axis=0)
            q_aug = jnp.where(own, q, aux.T.astype(q.dtype))
            scores, values = [], []
            for r in range(n_off):
                ks = pl.multiple_of(jnp.maximum(i - r, 0) * L, L)
                kblk = k_ref[0, pl.ds(ks, L), :]
                values.append(v_ref[0, pl.ds(ks, L), :])
                if r == 0:
                    s = _dot_nt(qh, kblk) + bias_ref[h, 0]
                    s = jnp.where(col <= row, s, NEG_BIG)
                else:
                    hot = (lane == aux0 + r)
                    if r >= 2:
                        hot = hot | ((lane >= aux0 + n_off) & (lane < aux0 + n_off + 3))
                    k_aug = jnp.where(own, kblk, jnp.where(hot, 1.0, 0.0).astype(kblk.dtype))
                    s = _dot_nt(q_aug, k_aug)
                    if r == 1:
                        s = s + bias_ref[h, 1]
                scores.append(s)
            all_scores.append(scores)
        for scores in all_scores:
            m = scores[0]
            for s in scores[1:]:
                m = jnp.maximum(m, s)
            m = jnp.max(m, axis=-1, keepdims=True)
            probs = [jnp.exp(s - m) for s in scores]
            l = probs[0]
            for pr in probs[1:]:
                l = l + pr
            l = jnp.sum(l, axis=-1, keepdims=True)
            acc = jnp.dot(probs[0].astype(BF16), values[0], preferred_element_type=F32)
            for pr, vblk in zip(probs[1:], values[1:]):
                acc = acc + jnp.dot(pr.astype(BF16), vblk, preferred_element_type=F32)
            outs.append(acc / l)
        o_ref[...] = jnp.where(lane < HEAD_DIM, outs[0], outs[1]).astype(o_ref.dtype)

    half = nblk // 2
    if half >= MOBA_TOPK:
        pl.when(i < half)(lambda: attend(half))
        pl.when(i >= half)(lambda: attend(nblk))
    else:
        attend(nblk)


def _moba_attention(proj, bias, rel_bias, *, batch, seq):
    T = batch * seq
    L = MOBA_BLOCK
    nblk = seq // L
    n_hp = D_MIX // LANES
    return pl.pallas_call(
        functools.partial(_moba_kernel, L=L, nblk=nblk, scale=1.0 / math.sqrt(HEAD_DIM)),
        grid=(batch, n_hp, nblk),
        in_specs=[pl.BlockSpec(memory_space=pltpu.SMEM),
                  pl.BlockSpec((1, L, LANES), lambda b, p, i: (3, b * nblk + i, p)),
                  pl.BlockSpec((1, seq, LANES), lambda b, p, i: (4, b, p)),
                  pl.BlockSpec((1, seq, LANES), lambda b, p, i: (5, b, p)),
                  pl.BlockSpec((2, 2, L, L), lambda b, p, i: (p, 0, 0, 0))],
        out_specs=pl.BlockSpec((L, LANES), lambda b, p, i: (b * nblk + i, p)),
        out_shape=jax.ShapeDtypeStruct((T, D_MIX), BF16),
        scratch_shapes=[pltpu.VMEM((nblk, LANES), F32)],
        compiler_params=pltpu.CompilerParams(
            dimension_semantics=("arbitrary", "arbitrary", "arbitrary"), vmem_limit_bytes=VMEM_LIMIT),
        name="moba_attn",
    )(rel_bias, proj, proj, proj, bias)


WIN_ROWS = 16
WINS_PER_TILE = 32


def _local_slots(tm):
    return tm * TOP_K + N_EXPERTS * WIN_ROWS


def _post_attn_kernel(ysb_ref, ymb_ref, gsb_ref, gmb_ref, x_ref, wsb_ref, wmb_ref, wo_ref, nrm_ref,
                      wr_ref, br_ref, h_ref, xl_ref, route_ref, cnt_ref, *, tm, chunk):
    def gate(g_ref):
        g = jnp.concatenate([g_ref[0], g_ref[1]], axis=1).astype(F32)
        return 1.0 / (1.0 + jnp.exp(-g))

    merged = (gate(gsb_ref) * jnp.dot(ysb_ref[...], wsb_ref[...], preferred_element_type=F32)
              + gate(gmb_ref) * jnp.dot(ymb_ref[...], wmb_ref[...], preferred_element_type=F32))
    hres = x_ref[...] + jnp.dot(merged.astype(BF16), wo_ref[...], preferred_element_type=F32)
    h_ref[...] = hres
    var = jnp.mean(hres * hres, axis=-1, keepdims=True)
    hn = (hres * lax.rsqrt(var + RMS_EPS)) * nrm_ref[...]
    a_hi, a_lo = _split_bf16(hn)
    w_hi, w_lo = _split_bf16(wr_ref[...])
    both = jnp.dot(a_hi, jnp.concatenate([w_hi, w_lo], axis=1), preferred_element_type=F32)
    logits = (both[:, :LANES] + both[:, LANES:] + jnp.dot(a_lo, w_hi, preferred_element_type=F32)) + br_ref[...]
    lane = lax.broadcasted_iota(jnp.int32, (1, LANES), 1).astype(F32)
    work = jnp.where(lane < N_EXPERTS, logits, NEG_BIG)
    vals, hots = [], []
    for _ in range(TOP_K):
        mx = jnp.max(work, axis=-1, keepdims=True)
        idx = jnp.min(jnp.where(work == mx, lane, float(LANES)), axis=-1, keepdims=True)
        hot = lane == idx
        work = jnp.where(hot, NEG_BIG, work)
        vals.append(mx)
        hots.append(hot)
    exps = [jnp.exp(v - vals[0]) for v in vals]
    denom = exps[0] + exps[1] + exps[2] + exps[3]
    sel = jnp.zeros((tm, LANES), F32)
    for hot in hots:
        sel = jnp.where(hot, 1.0, sel)
    r = lax.broadcasted_iota(jnp.int32, (tm, tm), 0)
    c = lax.broadcasted_iota(jnp.int32, (tm, tm), 1)
    earlier = jnp.dot((c < r).astype(BF16), sel.astype(BF16), preferred_element_type=F32)
    cnt = jnp.sum(sel, axis=0, keepdims=True)
    padded = jnp.floor((cnt + (WIN_ROWS - 1)) * (1.0 / WIN_ROWS)) * WIN_ROWS
    er = lax.broadcasted_iota(jnp.int32, (LANES, LANES), 0)
    ec = lax.broadcasted_iota(jnp.int32, (LANES, LANES), 1)
    lower = jnp.dot(jnp.broadcast_to(padded, (8, LANES)).astype(BF16), (er < ec).astype(BF16),
                    preferred_element_type=F32)[0:1]
    slot = lower + earlier
    slab = jnp.zeros((tm, LANES), F32)
    for k, hot in enumerate(hots):
        slab = jnp.where(lane == k, exps[k] / denom, slab)
        slab = jnp.where(lane == TOP_K + k, jnp.sum(jnp.where(hot, slot, 0.0), axis=-1, keepdims=True), slab)
    route_ref[...] = slab
    cnt_ref[0] = jnp.broadcast_to(cnt, (8, LANES))
    slots_t = slab.T
    hn_bf = hn.astype(BF16)
    for ci in range(_local_slots(tm) // chunk):
        rid = (lax.broadcasted_iota(jnp.int32, (chunk, 1), 0) + ci * chunk).astype(F32)
        onehot = rid == slots_t[TOP_K:TOP_K + 1, :]
        for k in range(1, TOP_K):
            onehot = onehot | (rid == slots_t[TOP_K + k:TOP_K + k + 1, :])
        xl_ref[ci * chunk:(ci + 1) * chunk, :] = jnp.dot(
            jnp.where(onehot, 1.0, 0.0).astype(BF16), hn_bf, preferred_element_type=F32).astype(xl_ref.dtype)


def _post_attn(ysb, ymb, proj, xt, wsb, wmb, wo, nrm_g, wr_pad, br_pad, *, tm):
    T, D = xt.shape
    half = D // 2
    n_tt = T // tm
    ls = _local_slots(tm)
    const = lambda i: (0, 0)
    return pl.pallas_call(
        functools.partial(_post_attn_kernel, tm=tm, chunk=256),
        grid=(n_tt,),
        in_specs=[pl.BlockSpec((tm, D_MIX), lambda i: (i, 0)),
                  pl.BlockSpec((tm, D_MIX), lambda i: (i, 0)),
                  pl.BlockSpec((2, tm, half), lambda i: (3, i, 0)),
                  pl.BlockSpec((2, tm, half), lambda i: (4, i, 0)),
                  pl.BlockSpec((tm, D), lambda i: (i, 0)),
                  pl.BlockSpec(wsb.shape, const),
                  pl.BlockSpec(wmb.shape, const),
                  pl.BlockSpec(wo.shape, const),
                  pl.BlockSpec((1, D), const),
                  pl.BlockSpec(wr_pad.shape, const),
                  pl.BlockSpec((1, LANES), const)],
        out_specs=[pl.BlockSpec((tm, D), lambda i: (i, 0)),
                   pl.BlockSpec((ls, D), lambda i: (i, 0)),
                   pl.BlockSpec((tm, LANES), lambda i: (i, 0)),
                   pl.BlockSpec((1, 8, LANES), lambda i: (i, 0, 0))],
        out_shape=[jax.ShapeDtypeStruct((T, D), F32),
                   jax.ShapeDtypeStruct((n_tt * ls, D), BF16),
                   jax.ShapeDtypeStruct((T, LANES), F32),
                   jax.ShapeDtypeStruct((n_tt, 8, LANES), F32)],
        compiler_params=pltpu.CompilerParams(dimension_semantics=("arbitrary",),
                                             vmem_limit_bytes=VMEM_LIMIT),
        name="post_attn",
    )(ysb, ymb, proj, proj, xt, wsb, wmb, wo, nrm_g.reshape(1, D), wr_pad, br_pad)


def _window_plan(counts, tm):
    n_tt, n_exp = counts.shape
    lw_tile = _local_slots(tm) // WIN_ROWS
    n_tiles = -(-(n_tt * (tm * TOP_K // WIN_ROWS + n_exp)) // WINS_PER_TILE) + n_exp
    nw = (counts + WIN_ROWS - 1) // WIN_ROWS
    local_start = jnp.cumsum(nw, axis=1) - nw
    per_expert = jnp.sum(nw, axis=0)
    per_expert_pad = -(-per_expert // WINS_PER_TILE) * WINS_PER_TILE
    expert_end = jnp.cumsum(per_expert_pad)
    sorted_start = (expert_end - per_expert_pad)[None, :] + jnp.cumsum(nw, axis=0) - nw
    run_start = sorted_start.T.reshape(-1)
    run_len = nw.T.reshape(-1)
    run_src = (jnp.arange(n_tt, dtype=jnp.int32)[:, None] * lw_tile + local_start).T.reshape(-1)
    g = jnp.arange(n_tiles * WINS_PER_TILE, dtype=jnp.int32)[:, None]
    off = g - run_start[None, :]
    src_win = jnp.sum(jnp.where((off >= 0) & (off < run_len[None, :]), run_src[None, :] + off, 0), axis=1)
    off_l = jnp.arange(lw_tile, dtype=jnp.int32)[None, :, None] - local_start[:, None, :]
    back_win = jnp.sum(jnp.where((off_l >= 0) & (off_l < nw[:, None, :]), sorted_start[:, None, :] + off_l, 0),
                       axis=2)
    tile_first = (jnp.arange(n_tiles, dtype=jnp.int32) * WINS_PER_TILE)[:, None]
    tile_valid = tile_first[:, 0] < expert_end[-1]
    used = (per_expert > 0)[None, :]
    expert_start = (expert_end - per_expert_pad)[None, :]
    ids = jnp.arange(n_exp, dtype=jnp.int32)[None, :]
    tile_expert = jnp.minimum(jnp.sum(expert_end[None, :] <= tile_first, axis=1), n_exp - 1)
    opens = jnp.sum(used & (expert_start == tile_first), axis=1) > 0
    buffer = (jnp.sum(used & (expert_start <= tile_first), axis=1) - 1) % 2
    next_expert = jnp.min(jnp.where(used & (expert_start > tile_first), ids, n_exp), axis=1)
    next_expert = jnp.where(next_expert == n_exp, -1, next_expert)
    tile_plan = tuple(a.astype(jnp.int32) for a in (tile_expert, tile_valid, opens, buffer, next_expert))
    return src_win.astype(jnp.int32), back_win.astype(jnp.int32), tile_plan


def _experts_kernel(te_ref, tv_ref, first_ref, par_ref, ne_ref, src_ref, nxt_ref, xl_ref, wgu_ref, bgu_ref, wd_ref,
                    bd_ref, y_ref, xbuf, wgu_f, wd_f, wgu_s, wd_s, sem, wsem, *, d_exp):
    j = pl.program_id(0)
    n_tiles = pl.num_programs(0)
    slot = j % 2

    def window_copy(win, dst_slot, w):
        return pltpu.make_async_copy(xl_ref.at[win], xbuf.at[dst_slot, w], sem.at[dst_slot])

    def start_gather(idx_ref, dst_slot):
        for w in range(WINS_PER_TILE):
            window_copy(idx_ref[0, 0, w], dst_slot, w).start()

    def weight_copies(expert, buf):
        return (pltpu.make_async_copy(wgu_ref.at[expert], wgu_f.at[buf], wsem.at[buf, 0]),
                pltpu.make_async_copy(wd_ref.at[expert], wd_f.at[buf], wsem.at[buf, 1]))

    @pl.when((j == 0) & (tv_ref[0] > 0))
    def _():
        for cp in weight_copies(te_ref[0], par_ref[0]):
            cp.start()
        start_gather(src_ref, 0)

    @pl.when((j + 1 < n_tiles) & (tv_ref[jnp.minimum(j + 1, n_tiles - 1)] > 0))
    def _():
        start_gather(nxt_ref, 1 - slot)

    @pl.when((first_ref[j] > 0) & (tv_ref[j] > 0))
    def _():
        buf = par_ref[j]

        @pl.when(ne_ref[j] >= 0)
        def _():
            for cp in weight_copies(ne_ref[j], 1 - buf):
                cp.start()

        for cp in weight_copies(te_ref[j], buf):
            cp.wait()
        wgu_s[...] = wgu_f[buf].astype(BF16)
        wd_s[...] = wd_f[buf].astype(BF16)

    @pl.when(tv_ref[j] > 0)
    def _():
        for w in range(WINS_PER_TILE):
            window_copy(0, slot, w).wait()
        x = xbuf[slot].reshape(WINS_PER_TILE * WIN_ROWS, xbuf.shape[-1])
        hgu = jnp.dot(x, wgu_s[...], preferred_element_type=F32) + bgu_ref[0]
        gate = jnp.minimum(hgu[:, :d_exp], SWIGLU_LIMIT)
        up = jnp.clip(hgu[:, d_exp:], -SWIGLU_LIMIT, SWIGLU_LIMIT)
        act = (up + 1.0) * (gate * (1.0 / (1.0 + jnp.exp(-SWIGLU_ALPHA * gate))))
        y = jnp.dot(act.astype(BF16), wd_s[...], preferred_element_type=F32) + bd_ref[0]
        y_ref[...] = y.astype(y_ref.dtype)

    @pl.when(tv_ref[j] == 0)
    def _():
        y_ref[...] = jnp.zeros_like(y_ref)


def _experts(xl, src_win, tile_plan, wgu, bgu, wd, bd):
    D = xl.shape[1]
    E, _, d2 = wgu.shape
    d_exp = d2 // 2
    n_tiles = tile_plan[0].shape[0]
    tmm = WINS_PER_TILE * WIN_ROWS
    src3 = src_win.reshape(n_tiles, 1, WINS_PER_TILE)
    grid_spec = pltpu.PrefetchScalarGridSpec(
        num_scalar_prefetch=len(tile_plan),
        grid=(n_tiles,),
        in_specs=[pl.BlockSpec((1, 1, WINS_PER_TILE), lambda j, *_: (j, 0, 0), memory_space=pltpu.SMEM),
                  pl.BlockSpec((1, 1, WINS_PER_TILE), lambda j, *_: (jnp.minimum(j + 1, n_tiles - 1), 0, 0),
                               memory_space=pltpu.SMEM),
                  pl.BlockSpec(memory_space=pl.ANY),
                  pl.BlockSpec(memory_space=pl.ANY),
                  pl.BlockSpec((1, 1, d2), lambda j, te, *_: (te[j], 0, 0)),
                  pl.BlockSpec(memory_space=pl.ANY),
                  pl.BlockSpec((1, 1, D), lambda j, te, *_: (te[j], 0, 0))],
        out_specs=pl.BlockSpec((tmm, D), lambda j, *_: (j, 0)),
        scratch_shapes=[pltpu.VMEM((2, WINS_PER_TILE, WIN_ROWS, D), BF16),
                        pltpu.VMEM((2, D, d2), F32), pltpu.VMEM((2, d_exp, D), F32),
                        pltpu.VMEM((D, d2), BF16), pltpu.VMEM((d_exp, D), BF16),
                        pltpu.SemaphoreType.DMA((2,)), pltpu.SemaphoreType.DMA((2, 2))],
    )
    return pl.pallas_call(
        functools.partial(_experts_kernel, d_exp=d_exp),
        grid_spec=grid_spec,
        out_shape=jax.ShapeDtypeStruct((n_tiles * tmm, D), BF16),
        compiler_params=pltpu.CompilerParams(dimension_semantics=("arbitrary",),
                                             vmem_limit_bytes=VMEM_LIMIT),
        name="experts",
    )(*tile_plan, src3, src3, xl.reshape(-1, WIN_ROWS, D), wgu, bgu.reshape(E, 1, d2), wd, bd.reshape(E, 1, D))


def _combine_kernel(back_ref, nxt_ref, y_ref, h_ref, route_ref, g_ref, o_ref, ybuf, sem, *, tm, chunk):
    i = pl.program_id(0)
    slot = i % 2
    n_win = ybuf.shape[1]

    def window_copy(win, dst_slot, w):
        return pltpu.make_async_copy(y_ref.at[win], ybuf.at[dst_slot, w], sem.at[dst_slot])

    def start_fetch(idx_ref, dst_slot):
        for w in range(n_win):
            window_copy(idx_ref[0, 0, w], dst_slot, w).start()

    @pl.when(i == 0)
    def _():
        start_fetch(back_ref, 0)

    @pl.when(i + 1 < pl.num_programs(0))
    def _():
        start_fetch(nxt_ref, 1 - slot)

    route = route_ref[...]
    weights = [route[:, k:k + 1] for k in range(TOP_K)]
    slots = [route[:, TOP_K + k:TOP_K + k + 1] for k in range(TOP_K)]
    for w in range(n_win):
        window_copy(0, slot, w).wait()
    acc = h_ref[...]
    for ci in range(n_win * WIN_ROWS // chunk):
        sid = (lax.broadcasted_iota(jnp.int32, (1, chunk), 1) + ci * chunk).astype(F32)
        wmat = jnp.zeros((tm, chunk), F32)
        for k in range(TOP_K):
            wmat = jnp.where(sid == slots[k], weights[k], wmat)
        hi, lo = _split_bf16(wmat)
        rows = ybuf[slot, ci * chunk // WIN_ROWS:(ci + 1) * chunk // WIN_ROWS].reshape(chunk, ybuf.shape[-1])
        acc = acc + jnp.dot(hi, rows, preferred_element_type=F32) + jnp.dot(lo, rows, preferred_element_type=F32)
    var = jnp.mean(acc * acc, axis=-1, keepdims=True)
    o_ref[...] = (acc * lax.rsqrt(var + RMS_EPS)) * g_ref[...]


def _combine(y, back_win, h, route, g, *, tm):
    T, D = h.shape
    n_tt, n_win = back_win.shape
    back3 = back_win.reshape(n_tt, 1, n_win)
    return pl.pallas_call(
        functools.partial(_combine_kernel, tm=tm, chunk=256),
        grid=(n_tt,),
        in_specs=[pl.BlockSpec((1, 1, n_win), lambda i: (i, 0, 0), memory_space=pltpu.SMEM),
                  pl.BlockSpec((1, 1, n_win), lambda i: (jnp.minimum(i + 1, n_tt - 1), 0, 0),
                               memory_space=pltpu.SMEM),
                  pl.BlockSpec(memory_space=pl.ANY),
                  pl.BlockSpec((tm, D), lambda i: (i, 0)),
                  pl.BlockSpec((tm, LANES), lambda i: (i, 0)),
                  pl.BlockSpec((1, D), lambda i: (0, 0))],
        out_specs=pl.BlockSpec((tm, D), lambda i: (i, 0)),
        out_shape=jax.ShapeDtypeStruct((T, D), F32),
        scratch_shapes=[pltpu.VMEM((2, n_win, WIN_ROWS, D), BF16), pltpu.SemaphoreType.DMA((2,))],
        compiler_params=pltpu.CompilerParams(dimension_semantics=("arbitrary",),
                                             vmem_limit_bytes=VMEM_LIMIT),
        name="combine",
    )(back3, back3, y.reshape(-1, WIN_ROWS, D), h, route, g.reshape(1, D))


def kernel(x, norm_mix_g, w_in, w_proj_sb, w_proj_moba, w_out, rel_bias, norm_ffn_g, w_router, b_router,
           w_gate_up, b_gate_up, w_down, b_down, norm_final_g):
    batch, seq, D = x.shape
    assert norm_mix_g.shape[0] == 1, "single-layer trunk"
    tm = 256
    xt = x.reshape(batch * seq, D)
    proj = _in_proj(xt, norm_mix_g[0], w_in[0].astype(BF16))
    ysb = _sb_attention(proj, batch=batch, seq=seq)
    ymb = _moba_attention(proj, _moba_bias(rel_bias), rel_bias, batch=batch, seq=seq)
    wr_pad = jnp.pad(w_router[0], ((0, 0), (0, LANES - N_EXPERTS)))
    br_pad = jnp.pad(b_router[0], (0, LANES - N_EXPERTS)).reshape(1, LANES)
    h, xl, route, cnt = _post_attn(ysb, ymb, proj, xt, w_proj_sb[0].astype(BF16), w_proj_moba[0].astype(BF16),
                                   w_out[0].astype(BF16), norm_ffn_g[0], wr_pad, br_pad, tm=tm)
    src_win, back_win, tile_plan = _window_plan(cnt[:, 0, :N_EXPERTS].astype(jnp.int32), tm)
    y = _experts(xl, src_win, tile_plan, w_gate_up[0], b_gate_up[0], w_down[0], b_down[0])
    out = _combine(y, back_win, h, route, norm_final_g, tm=tm)
    return out.reshape(batch, seq, D)
```

```python
import functools
import math

import numpy as np
import jax
import jax.numpy as jnp
from jax import lax
from jax.experimental import pallas as pl
from jax.experimental.pallas import tpu as pltpu

HEAD_DIM = 64
N_HEADS = 8
D_MIX = N_HEADS * HEAD_DIM
MOBA_BLOCK = 256
MOBA_TOPK = 3
REL_BUCKETS = 32
REL_MAX_DIST = 128
N_EXPERTS = 32
TOP_K = 4
SWIGLU_LIMIT = 7.0
SWIGLU_ALPHA = 1.702
RMS_EPS = 1e-6

LANES = 128
NEG_BIG = -1e30
LOG2_E = math.log2(math.e)
QUERY_SCALE = LOG2_E / math.sqrt(HEAD_DIM)
VMEM_LIMIT = 56 * 1024 * 1024

F32 = jnp.float32
BF16 = jnp.bfloat16


def _split_bf16(a):
    hi = a.astype(BF16)
    lo = (a - hi.astype(F32)).astype(BF16)
    return hi, lo


def _dot_nt(a, b):
    return lax.dot_general(a, b, (((1,), (1,)), ((), ())), preferred_element_type=F32)


def _in_proj_kernel(x_ref, g_ref, w_ref, o_ref, *, n_slab, slab):
    x = x_ref[...]
    var = jnp.mean(x * x, axis=-1, keepdims=True)
    xn = ((x * lax.rsqrt(var + RMS_EPS)) * g_ref[...]).astype(BF16)
    for j in range(n_slab):
        o_ref[j] = jnp.dot(xn, w_ref[:, j * slab:(j + 1) * slab],
                           preferred_element_type=F32).astype(o_ref.dtype)


def _in_proj(xt, g, w_bf16, *, tm=512, slab=D_MIX):
    T, D = xt.shape
    n_slab = w_bf16.shape[1] // slab
    return pl.pallas_call(
        functools.partial(_in_proj_kernel, n_slab=n_slab, slab=slab),
        grid=(T // tm,),
        in_specs=[pl.BlockSpec((tm, D), lambda i: (i, 0)),
                  pl.BlockSpec((1, D), lambda i: (0, 0)),
                  pl.BlockSpec(w_bf16.shape, lambda i: (0, 0))],
        out_specs=pl.BlockSpec((n_slab, tm, slab), lambda i: (0, i, 0)),
        out_shape=jax.ShapeDtypeStruct((n_slab, T, slab), BF16),
        compiler_params=pltpu.CompilerParams(dimension_semantics=("arbitrary",),
                                             vmem_limit_bytes=VMEM_LIMIT),
        name="in_proj",
    )(xt, g.reshape(1, D), w_bf16)


def _sb_kernel(q_ref, k_ref, v_ref, o_ref, *, tq, tk):
    i = pl.program_id(2)
    q = q_ref[0]
    lane = lax.broadcasted_iota(jnp.int32, (1, LANES), 1)
    jj = lax.broadcasted_iota(jnp.int32, (tk, tk), 0)
    ss = lax.broadcasted_iota(jnp.int32, (tk, tk), 1)
    later = (jj >= ss).astype(BF16)
    zero = jnp.zeros((), q.dtype)
    qhs = [jnp.where(lane // HEAD_DIM == h, q, zero) for h in range(2)]
    n_sub = tq // tk

    def q_tile_keys(base, carry, diagonal):
        tiles = [(h, sub) for sub in reversed(range(n_sub)) for h in range(2)]
        starts = {sub: pl.multiple_of(base + sub * tk, tk) for sub in range(n_sub)}
        c = [list(carry[h][0]) for h in range(2)]
        acc = [carry[h][1] for h in range(2)]
        z2, drop, inner, pv = {}, {}, {}, {}

        def first_row(t):
            return t[1] * tk if diagonal else 0

        def causal(t):
            shape = (tq - first_row(t), tk)
            return lax.broadcasted_iota(jnp.int32, shape, 1) < lax.broadcasted_iota(jnp.int32, shape, 0)

        def scores(t):
            z2[t] = _dot_nt(qhs[t[0]][first_row(t):], k_ref[0, pl.ds(starts[t[1]], tk), :])

        def drops(t):
            d = jnp.maximum(z2[t], 0.0) + jnp.log2(1.0 + jnp.exp2(-jnp.abs(z2[t])))
            if diagonal:
                d = jnp.where(causal(t), d, 0.0)
            drop[t] = d
            inner[t] = jnp.dot(d.astype(BF16), later, preferred_element_type=F32)

        def weights(t):
            h, b0 = t[0], first_row(t) // tk
            blocks = [slice((b - b0) * tk, (b - b0 + 1) * tk) for b in range(b0, n_sub)]
            w = [jnp.exp2((z2[t][rows] - c[h][b0 + n]) - inner[t][rows]) for n, rows in enumerate(blocks)]
            w = w[0] if len(w) == 1 else jnp.concatenate(w, axis=0)
            if diagonal:
                w = jnp.where(causal(t), w, 0.0)
            for n, rows in enumerate(blocks):
                c[h][b0 + n] = c[h][b0 + n] + jnp.sum(drop[t][rows], axis=-1, keepdims=True)
            pv[t] = jnp.dot(w.astype(BF16), v_ref[0, pl.ds(starts[t[1]], tk), :], preferred_element_type=F32)

        def accumulate(t):
            h, r0 = t[0], first_row(t)
            acc[h] = acc[h] + pv[t] if r0 == 0 else jnp.concatenate([acc[h][:r0], acc[h][r0:] + pv[t]], axis=0)

        stages = (scores, drops, weights, accumulate)
        for step in range(len(tiles) + len(stages) - 1):
            for k, stage in enumerate(stages):
                if 0 <= step - k < len(tiles):
                    stage(tiles[step - k])
        return tuple((tuple(c[h]), acc[h]) for h in range(2))

    init = tuple((tuple(jnp.zeros((tk, 1), F32) for _ in range(n_sub)), jnp.zeros((tq, LANES), F32))
                 for _ in range(2))
    carry = q_tile_keys(i * tq, init, True)
    carry = lax.fori_loop(0, i, lambda it, cr: q_tile_keys((i - 1 - it) * tq, cr, False), carry)
    o_ref[...] = jnp.where(lane < HEAD_DIM, carry[0][1], carry[1][1]).astype(o_ref.dtype)


def _sb_attention(proj, *, batch, seq, tq=512, tk=256):
    T = batch * seq
    nq = seq // tq
    n_hp = D_MIX // LANES
    return pl.pallas_call(
        functools.partial(_sb_kernel, tq=tq, tk=tk),
        grid=(batch, n_hp, nq),
        in_specs=[pl.BlockSpec((1, tq, LANES), lambda b, p, i: (0, b * nq + i, p)),
                  pl.BlockSpec((1, seq, LANES), lambda b, p, i: (1, b, p)),
                  pl.BlockSpec((1, seq, LANES), lambda b, p, i: (2, b, p))],
        out_specs=pl.BlockSpec((tq, LANES), lambda b, p, i: (b * nq + i, p)),
        out_shape=jax.ShapeDtypeStruct((T, D_MIX), BF16),
        compiler_params=pltpu.CompilerParams(
            dimension_semantics=("arbitrary", "arbitrary", "arbitrary"), vmem_limit_bytes=VMEM_LIMIT),
        name="sb_attn",
    )(proj, proj, proj)


def _rel_bucket_np(dist):
    n = np.maximum(dist, 0)
    max_exact = REL_BUCKETS // 2
    nf = np.maximum(n, 1).astype(np.float64)
    large = max_exact + (np.log(nf / max_exact) / math.log(REL_MAX_DIST / max_exact)
                         * (REL_BUCKETS - max_exact)).astype(np.int32)
    large = np.minimum(large, REL_BUCKETS - 1)
    return np.where(n < max_exact, n, large).astype(np.int32)


def _bias_kernel(rel_ref, bkt_ref, o_ref):
    h = pl.program_id(0)
    for d in range(2):
        bkt = bkt_ref[d]
        acc = jnp.zeros(bkt.shape, F32)
        for b in range(REL_BUCKETS):
            acc = jnp.where(bkt == b, rel_ref[b, h], acc)
        o_ref[0, d] = acc * LOG2_E


def _moba_bias(rel_bias):
    L = MOBA_BLOCK
    r = np.arange(L)[:, None]
    c = np.arange(L)[None, :]
    bkt = np.stack([_rel_bucket_np(r - c), _rel_bucket_np(L + r - c)])
    return pl.pallas_call(
        _bias_kernel,
        grid=(N_HEADS,),
        in_specs=[pl.BlockSpec(memory_space=pltpu.SMEM),
                  pl.BlockSpec((2, L, L), lambda h: (0, 0, 0))],
        out_specs=pl.BlockSpec((1, 2, L, L), lambda h: (h, 0, 0, 0)),
        out_shape=jax.ShapeDtypeStruct((N_HEADS, 2, L, L), F32),
        name="moba_bias",
    )(rel_bias, jnp.asarray(bkt))


def _moba_kernel(rel_ref, q_ref, k_ref, v_ref, bias_ref, o_ref, km_ref, *, L, nblk):
    p = pl.program_id(1)
    i = pl.program_id(2)
    lane = lax.broadcasted_iota(jnp.int32, (1, LANES), 1)
    row = lax.broadcasted_iota(jnp.int32, (L, L), 0)
    col = lax.broadcasted_iota(jnp.int32, (L, L), 1)

    @pl.when(i == 0)
    def _():
        km_ref[...] = jnp.mean(k_ref[0].astype(F32).reshape(nblk, L, LANES), axis=1)

    def attend(n_off):
        rr = lax.broadcasted_iota(jnp.int32, (n_off, 1), 0)
        km = jnp.concatenate([km_ref[pl.ds(jnp.maximum(i - r, 0), 1), :] for r in range(n_off)], axis=0)
        km = jnp.concatenate([km, jnp.zeros((16 - n_off, LANES), F32)], axis=0)
        km_hi, km_lo = _split_bf16(km)
        km_lo2 = (km - km_hi.astype(F32) - km_lo.astype(F32)).astype(BF16)
        q = q_ref[0]
        past = (rr >= 1) & (rr <= i)
        zero = jnp.zeros((), q.dtype)
        outs, all_scores = [], []
        for h in range(2):
            own = lane // HEAD_DIM == h
            aux0 = (1 - h) * HEAD_DIM
            qh = jnp.where(own, q, zero)
            gs = (_dot_nt(km_hi, qh) + _dot_nt(km_lo, qh) + _dot_nt(km_lo2, qh))[:n_off]
            pen = jnp.zeros((n_off, L), F32)
            for r in range(1, n_off):
                g_r = gs[r:r + 1, :]
                beats = past & ((gs > g_r) | ((gs == g_r) & (rr > r)))
                cnt = jnp.sum(beats.astype(F32), axis=0, keepdims=True)
                keep = (cnt < MOBA_TOPK) & (r <= i)
                pen = jnp.where((rr == r) & jnp.logical_not(keep), NEG_BIG, pen)
            far = jnp.full((n_off, L), rel_ref[REL_BUCKETS - 1, 2 * p + h] * LOG2_E, F32)
            f_hi, f_lo = _split_bf16(far)
            f_lo2 = far - f_hi.astype(F32) - f_lo.astype(F32)
            far3 = jnp.where(rr == 0, f_hi.astype(F32), jnp.where(rr == 1, f_lo.astype(F32),
                                                                  jnp.where(rr == 2, f_lo2, 0.0)))
            aux = [pen, far3, jnp.zeros((LANES - aux0 - 2 * n_off, L), F32)]
            aux = jnp.concatenate(([jnp.zeros((aux0, L), F32)] if aux0 else []) + aux, axis=0)
            q_aug = jnp.where(own, q, aux.T.astype(q.dtype))
            scores, values = [], []
            for r in range(n_off):
                ks = pl.multiple_of(jnp.maximum(i - r, 0) * L, L)
                kblk = k_ref[0, pl.ds(ks, L), :]
                values.append(v_ref[0, pl.ds(ks, L), :])
                if r == 0:
                    s = _dot_nt(qh, kblk) + bias_ref[h, 0]
                    s = jnp.where(col <= row, s, NEG_BIG)
                else:
                    hot = (lane == aux0 + r)
                    if r >= 2:
                        hot = hot | ((lane >= aux0 + n_off) & (lane < aux0 + n_off + 3))
                    k_aug = jnp.where(own, kblk, jnp.where(hot, 1.0, 0.0).astype(kblk.dtype))
                    s = _dot_nt(q_aug, k_aug)
                    if r == 1:
                        s = s + bias_ref[h, 1]
                scores.append(s)
            all_scores.append(scores)
        for scores in all_scores:
            m = scores[0]
            for s in scores[1:]:
                m = jnp.maximum(m, s)
            m = jnp.max(m, axis=-1, keepdims=True)
            probs = [jnp.exp2(s - m) for s in scores]
            l = probs[0]
            for pr in probs[1:]:
                l = l + pr
            l = jnp.sum(l, axis=-1, keepdims=True)
            acc = jnp.dot(probs[0].astype(BF16), values[0], preferred_element_type=F32)
            for pr, vblk in zip(probs[1:], values[1:]):
                acc = acc + jnp.dot(pr.astype(BF16), vblk, preferred_element_type=F32)
            outs.append(acc / l)
        o_ref[...] = jnp.where(lane < HEAD_DIM, outs[0], outs[1]).astype(o_ref.dtype)

    for n_off in range(1, nblk + 1):
        pl.when(i == n_off - 1)(functools.partial(attend, n_off))


def _moba_attention(proj, bias, rel_bias, *, batch, seq):
    T = batch * seq
    L = MOBA_BLOCK
    nblk = seq // L
    assert seq % (2 * L) == 0, "query blocks are handled in pairs"
    n_hp = D_MIX // LANES
    return pl.pallas_call(
        functools.partial(_moba_kernel, L=L, nblk=nblk),
        grid=(batch, n_hp, nblk),
        in_specs=[pl.BlockSpec(memory_space=pltpu.SMEM),
                  pl.BlockSpec((1, L, LANES), lambda b, p, i: (3, b * nblk + i, p)),
                  pl.BlockSpec((1, seq, LANES), lambda b, p, i: (4, b, p)),
                  pl.BlockSpec((1, seq, LANES), lambda b, p, i: (5, b, p)),
                  pl.BlockSpec((2, 2, L, L), lambda b, p, i: (p, 0, 0, 0))],
        out_specs=pl.BlockSpec((L, LANES), lambda b, p, i: (b * nblk + i, p)),
        out_shape=jax.ShapeDtypeStruct((T, D_MIX), BF16),
        scratch_shapes=[pltpu.VMEM((nblk, LANES), F32)],
        compiler_params=pltpu.CompilerParams(
            dimension_semantics=("arbitrary", "arbitrary", "arbitrary"), vmem_limit_bytes=VMEM_LIMIT),
        name="moba_attn",
    )(rel_bias, proj, proj, proj, bias)


WIN_ROWS = 16
WINS_PER_TILE = 32


def _local_slots(tm):
    return tm * TOP_K + N_EXPERTS * WIN_ROWS


def _post_attn_kernel(ysb_ref, ymb_ref, gsb_ref, gmb_ref, x_ref, wsb_ref, wmb_ref, wo_ref, nrm_ref,
                      wr_ref, br_ref, h_ref, xl_ref, route_ref, cnt_ref, *, tm, chunk):
    def gate(g_ref):
        g = jnp.concatenate([g_ref[0], g_ref[1]], axis=1).astype(F32)
        return 1.0 / (1.0 + jnp.exp(-g))

    merged = (gate(gsb_ref) * jnp.dot(ysb_ref[...], wsb_ref[...], preferred_element_type=F32)
              + gate(gmb_ref) * jnp.dot(ymb_ref[...], wmb_ref[...], preferred_element_type=F32))
    hres = x_ref[...] + jnp.dot(merged.astype(BF16), wo_ref[...], preferred_element_type=F32)
    h_ref[...] = hres
    var = jnp.mean(hres * hres, axis=-1, keepdims=True)
    hn = (hres * lax.rsqrt(var + RMS_EPS)) * nrm_ref[...]
    a_hi, a_lo = _split_bf16(hn)
    w_hi, w_lo = _split_bf16(wr_ref[...])
    both = jnp.dot(a_hi, jnp.concatenate([w_hi, w_lo], axis=1), preferred_element_type=F32)
    logits = (both[:, :LANES] + both[:, LANES:] + jnp.dot(a_lo, w_hi, preferred_element_type=F32)) + br_ref[...]
    lane = lax.broadcasted_iota(jnp.int32, (1, LANES), 1).astype(F32)
    work = jnp.where(lane < N_EXPERTS, logits, NEG_BIG)
    vals, hots = [], []
    for _ in range(TOP_K):
        mx = jnp.max(work, axis=-1, keepdims=True)
        idx = jnp.min(jnp.where(work == mx, lane, float(LANES)), axis=-1, keepdims=True)
        hot = lane == idx
        work = jnp.where(hot, NEG_BIG, work)
        vals.append(mx)
        hots.append(hot)
    exps = [jnp.exp(v - vals[0]) for v in vals]
    denom = exps[0] + exps[1] + exps[2] + exps[3]
    sel = jnp.zeros((tm, LANES), F32)
    for hot in hots:
        sel = jnp.where(hot, 1.0, sel)
    r = lax.broadcasted_iota(jnp.int32, (tm, tm), 0)
    c = lax.broadcasted_iota(jnp.int32, (tm, tm), 1)
    earlier = jnp.dot((c < r).astype(BF16), sel.astype(BF16), preferred_element_type=F32)
    cnt = jnp.sum(sel, axis=0, keepdims=True)
    padded = jnp.floor((cnt + (WIN_ROWS - 1)) * (1.0 / WIN_ROWS)) * WIN_ROWS
    er = lax.broadcasted_iota(jnp.int32, (LANES, LANES), 0)
    ec = lax.broadcasted_iota(jnp.int32, (LANES, LANES), 1)
    lower = jnp.dot(jnp.broadcast_to(padded, (8, LANES)).astype(BF16), (er < ec).astype(BF16),
                    preferred_element_type=F32)[0:1]
    slot = lower + earlier
    slab = jnp.zeros((tm, LANES), F32)
    for k, hot in enumerate(hots):
        slab = jnp.where(lane == k, exps[k] / denom, slab)
        slab = jnp.where(lane == TOP_K + k, jnp.sum(jnp.where(hot, slot, 0.0), axis=-1, keepdims=True), slab)
    route_ref[...] = slab
    cnt_ref[0] = jnp.broadcast_to(cnt, (8, LANES))
    slots_t = slab.T
    hn_bf = hn.astype(BF16)
    for ci in range(_local_slots(tm) // chunk):
        rid = (lax.broadcasted_iota(jnp.int32, (chunk, 1), 0) + ci * chunk).astype(F32)
        onehot = rid == slots_t[TOP_K:TOP_K + 1, :]
        for k in range(1, TOP_K):
            onehot = onehot | (rid == slots_t[TOP_K + k:TOP_K + k + 1, :])
        xl_ref[ci * chunk:(ci + 1) * chunk, :] = jnp.dot(
            jnp.where(onehot, 1.0, 0.0).astype(BF16), hn_bf, preferred_element_type=F32).astype(xl_ref.dtype)


def _post_attn(ysb, ymb, proj, xt, wsb, wmb, wo, nrm_g, wr_pad, br_pad, *, tm):
    T, D = xt.shape
    half = D // 2
    n_tt = T // tm
    ls = _local_slots(tm)
    const = lambda i: (0, 0)
    return pl.pallas_call(
        functools.partial(_post_attn_kernel, tm=tm, chunk=256),
        grid=(n_tt,),
        in_specs=[pl.BlockSpec((tm, D_MIX), lambda i: (i, 0)),
                  pl.BlockSpec((tm, D_MIX), lambda i: (i, 0)),
                  pl.BlockSpec((2, tm, half), lambda i: (3, i, 0)),
                  pl.BlockSpec((2, tm, half), lambda i: (4, i, 0)),
                  pl.BlockSpec((tm, D), lambda i: (i, 0)),
                  pl.BlockSpec(wsb.shape, const),
                  pl.BlockSpec(wmb.shape, const),
                  pl.BlockSpec(wo.shape, const),
                  pl.BlockSpec((1, D), const),
                  pl.BlockSpec(wr_pad.shape, const),
                  pl.BlockSpec((1, LANES), const)],
        out_specs=[pl.BlockSpec((tm, D), lambda i: (i, 0)),
                   pl.BlockSpec((ls, D), lambda i: (i, 0)),
                   pl.BlockSpec((tm, LANES), lambda i: (i, 0)),
                   pl.BlockSpec((1, 8, LANES), lambda i: (i, 0, 0))],
        out_shape=[jax.ShapeDtypeStruct((T, D), F32),
                   jax.ShapeDtypeStruct((n_tt * ls, D), BF16),
                   jax.ShapeDtypeStruct((T, LANES), F32),
                   jax.ShapeDtypeStruct((n_tt, 8, LANES), F32)],
        compiler_params=pltpu.CompilerParams(dimension_semantics=("arbitrary",),
                                             vmem_limit_bytes=VMEM_LIMIT),
        name="post_attn",
    )(ysb, ymb, proj, proj, xt, wsb, wmb, wo, nrm_g.reshape(1, D), wr_pad, br_pad)


def _window_plan(counts, tm):
    n_tt, n_exp = counts.shape
    lw_tile = _local_slots(tm) // WIN_ROWS
    n_tiles = -(-(n_tt * (tm * TOP_K // WIN_ROWS + n_exp)) // WINS_PER_TILE) + n_exp
    nw = (counts + WIN_ROWS - 1) // WIN_ROWS
    local_start = jnp.cumsum(nw, axis=1) - nw
    per_expert = jnp.sum(nw, axis=0)
    per_expert_pad = -(-per_expert // WINS_PER_TILE) * WINS_PER_TILE
    expert_end = jnp.cumsum(per_expert_pad)
    sorted_start = (expert_end - per_expert_pad)[None, :] + jnp.cumsum(nw, axis=0) - nw
    run_start = sorted_start.T.reshape(-1)
    run_len = nw.T.reshape(-1)
    run_src = (jnp.arange(n_tt, dtype=jnp.int32)[:, None] * lw_tile + local_start).T.reshape(-1)
    g = jnp.arange(n_tiles * WINS_PER_TILE, dtype=jnp.int32)[:, None]
    off = g - run_start[None, :]
    src_win = jnp.sum(jnp.where((off >= 0) & (off < run_len[None, :]), run_src[None, :] + off, 0), axis=1)
    off_l = jnp.arange(lw_tile, dtype=jnp.int32)[None, :, None] - local_start[:, None, :]
    back_win = jnp.sum(jnp.where((off_l >= 0) & (off_l < nw[:, None, :]), sorted_start[:, None, :] + off_l, 0),
                       axis=2)
    tile_first = (jnp.arange(n_tiles, dtype=jnp.int32) * WINS_PER_TILE)[:, None]
    tile_valid = tile_first[:, 0] < expert_end[-1]
    used = (per_expert > 0)[None, :]
    expert_start = (expert_end - per_expert_pad)[None, :]
    ids = jnp.arange(n_exp, dtype=jnp.int32)[None, :]
    tile_expert = jnp.minimum(jnp.sum(expert_end[None, :] <= tile_first, axis=1), n_exp - 1)
    opens = jnp.sum(used & (expert_start == tile_first), axis=1) > 0
    buffer = (jnp.sum(used & (expert_start <= tile_first), axis=1) - 1) % 2
    next_expert = jnp.min(jnp.where(used & (expert_start > tile_first), ids, n_exp), axis=1)
    next_expert = jnp.where(next_expert == n_exp, -1, next_expert)
    tile_plan = tuple(a.astype(jnp.int32) for a in (tile_expert, tile_valid, opens, buffer, next_expert))
    return src_win.astype(jnp.int32), back_win.astype(jnp.int32), tile_plan


def _experts_kernel(te_ref, tv_ref, first_ref, par_ref, ne_ref, src_ref, nxt_ref, xl_ref, wgu_ref, bgu_ref, wd_ref,
                    bd_ref, y_ref, xbuf, wgu_f, wd_f, wgu_s, wd_s, sem, wsem, *, d_exp):
    j = pl.program_id(0)
    n_tiles = pl.num_programs(0)
    slot = j % 2

    def window_copy(win, dst_slot, w):
        return pltpu.make_async_copy(xl_ref.at[win], xbuf.at[dst_slot, w], sem.at[dst_slot])

    def start_gather(idx_ref, dst_slot):
        for w in range(WINS_PER_TILE):
            window_copy(idx_ref[0, 0, w], dst_slot, w).start()

    def weight_copies(expert, buf):
        return (pltpu.make_async_copy(wgu_ref.at[expert], wgu_f.at[buf], wsem.at[buf, 0]),
                pltpu.make_async_copy(wd_ref.at[expert], wd_f.at[buf], wsem.at[buf, 1]))

    @pl.when((j == 0) & (tv_ref[0] > 0))
    def _():
        for cp in weight_copies(te_ref[0], par_ref[0]):
            cp.start()
        start_gather(src_ref, 0)

    @pl.when((j + 1 < n_tiles) & (tv_ref[jnp.minimum(j + 1, n_tiles - 1)] > 0))
    def _():
        start_gather(nxt_ref, 1 - slot)

    @pl.when((first_ref[j] > 0) & (tv_ref[j] > 0))
    def _():
        buf = par_ref[j]

        @pl.when(ne_ref[j] >= 0)
        def _():
            for cp in weight_copies(ne_ref[j], 1 - buf):
                cp.start()

        for cp in weight_copies(te_ref[j], buf):
            cp.wait()
        wgu_s[...] = wgu_f[buf].astype(BF16)
        wd_s[...] = wd_f[buf].astype(BF16)

    @pl.when(tv_ref[j] > 0)
    def _():
        for w in range(WINS_PER_TILE):
            window_copy(0, slot, w).wait()
        x = xbuf[slot].reshape(WINS_PER_TILE * WIN_ROWS, xbuf.shape[-1])
        hgu = jnp.dot(x, wgu_s[...], preferred_element_type=F32) + bgu_ref[0]
        gate = jnp.minimum(hgu[:, :d_exp], SWIGLU_LIMIT)
        up = jnp.clip(hgu[:, d_exp:], -SWIGLU_LIMIT, SWIGLU_LIMIT)
        act = (up + 1.0) * (gate * (1.0 / (1.0 + jnp.exp(-SWIGLU_ALPHA * gate))))
        y = jnp.dot(act.astype(BF16), wd_s[...], preferred_element_type=F32) + bd_ref[0]
        y_ref[...] = y.astype(y_ref.dtype)

    @pl.when(tv_ref[j] == 0)
    def _():
        y_ref[...] = jnp.zeros_like(y_ref)


def _experts(xl, src_win, tile_plan, wgu, bgu, wd, bd):
    D = xl.shape[1]
    E, _, d2 = wgu.shape
    d_exp = d2 // 2
    n_tiles = tile_plan[0].shape[0]
    tmm = WINS_PER_TILE * WIN_ROWS
    src3 = src_win.reshape(n_tiles, 1, WINS_PER_TILE)
    grid_spec = pltpu.PrefetchScalarGridSpec(
        num_scalar_prefetch=len(tile_plan),
        grid=(n_tiles,),
        in_specs=[pl.BlockSpec((1, 1, WINS_PER_TILE), lambda j, *_: (j, 0, 0), memory_space=pltpu.SMEM),
                  pl.BlockSpec((1, 1, WINS_PER_TILE), lambda j, *_: (jnp.minimum(j + 1, n_tiles - 1), 0, 0),
                               memory_space=pltpu.SMEM),
                  pl.BlockSpec(memory_space=pl.ANY),
                  pl.BlockSpec(memory_space=pl.ANY),
                  pl.BlockSpec((1, 1, d2), lambda j, te, *_: (te[j], 0, 0)),
                  pl.BlockSpec(memory_space=pl.ANY),
                  pl.BlockSpec((1, 1, D), lambda j, te, *_: (te[j], 0, 0))],
        out_specs=pl.BlockSpec((tmm, D), lambda j, *_: (j, 0)),
        scratch_shapes=[pltpu.VMEM((2, WINS_PER_TILE, WIN_ROWS, D), BF16),
                        pltpu.VMEM((2, D, d2), F32), pltpu.VMEM((2, d_exp, D), F32),
                        pltpu.VMEM((D, d2), BF16), pltpu.VMEM((d_exp, D), BF16),
                        pltpu.SemaphoreType.DMA((2,)), pltpu.SemaphoreType.DMA((2, 2))],
    )
    return pl.pallas_call(
        functools.partial(_experts_kernel, d_exp=d_exp),
        grid_spec=grid_spec,
        out_shape=jax.ShapeDtypeStruct((n_tiles * tmm, D), BF16),
        compiler_params=pltpu.CompilerParams(dimension_semantics=("arbitrary",),
                                             vmem_limit_bytes=VMEM_LIMIT),
        name="experts",
    )(*tile_plan, src3, src3, xl.reshape(-1, WIN_ROWS, D), wgu, bgu.reshape(E, 1, d2), wd, bd.reshape(E, 1, D))


def _combine_kernel(back_ref, nxt_ref, y_ref, h_ref, route_ref, g_ref, o_ref, ybuf, sem, *, tm, chunk):
    i = pl.program_id(0)
    slot = i % 2
    n_win = ybuf.shape[1]

    def window_copy(win, dst_slot, w):
        return pltpu.make_async_copy(y_ref.at[win], ybuf.at[dst_slot, w], sem.at[dst_slot])

    def start_fetch(idx_ref, dst_slot):
        for w in range(n_win):
            window_copy(idx_ref[0, 0, w], dst_slot, w).start()

    @pl.when(i == 0)
    def _():
        start_fetch(back_ref, 0)

    @pl.when(i + 1 < pl.num_programs(0))
    def _():
        start_fetch(nxt_ref, 1 - slot)

    route = route_ref[...]
    weights = [route[:, k:k + 1] for k in range(TOP_K)]
    slots = [route[:, TOP_K + k:TOP_K + k + 1] for k in range(TOP_K)]
    for w in range(n_win):
        window_copy(0, slot, w).wait()
    acc = h_ref[...]
    for ci in range(n_win * WIN_ROWS // chunk):
        sid = (lax.broadcasted_iota(jnp.int32, (1, chunk), 1) + ci * chunk).astype(F32)
        wmat = jnp.zeros((tm, chunk), F32)
        for k in range(TOP_K):
            wmat = jnp.where(sid == slots[k], weights[k], wmat)
        hi, lo = _split_bf16(wmat)
        rows = ybuf[slot, ci * chunk // WIN_ROWS:(ci + 1) * chunk // WIN_ROWS].reshape(chunk, ybuf.shape[-1])
        acc = acc + jnp.dot(hi, rows, preferred_element_type=F32) + jnp.dot(lo, rows, preferred_element_type=F32)
    var = jnp.mean(acc * acc, axis=-1, keepdims=True)
    o_ref[...] = (acc * lax.rsqrt(var + RMS_EPS)) * g_ref[...]


def _combine(y, back_win, h, route, g, *, tm):
    T, D = h.shape
    n_tt, n_win = back_win.shape
    back3 = back_win.reshape(n_tt, 1, n_win)
    return pl.pallas_call(
        functools.partial(_combine_kernel, tm=tm, chunk=256),
        grid=(n_tt,),
        in_specs=[pl.BlockSpec((1, 1, n_win), lambda i: (i, 0, 0), memory_space=pltpu.SMEM),
                  pl.BlockSpec((1, 1, n_win), lambda i: (jnp.minimum(i + 1, n_tt - 1), 0, 0),
                               memory_space=pltpu.SMEM),
                  pl.BlockSpec(memory_space=pl.ANY),
                  pl.BlockSpec((tm, D), lambda i: (i, 0)),
                  pl.BlockSpec((tm, LANES), lambda i: (i, 0)),
                  pl.BlockSpec((1, D), lambda i: (0, 0))],
        out_specs=pl.BlockSpec((tm, D), lambda i: (i, 0)),
        out_shape=jax.ShapeDtypeStruct((T, D), F32),
        scratch_shapes=[pltpu.VMEM((2, n_win, WIN_ROWS, D), BF16), pltpu.SemaphoreType.DMA((2,))],
        compiler_params=pltpu.CompilerParams(dimension_semantics=("arbitrary",),
                                             vmem_limit_bytes=VMEM_LIMIT),
        name="combine",
    )(back3, back3, y.reshape(-1, WIN_ROWS, D), h, route, g.reshape(1, D))


def kernel(x, norm_mix_g, w_in, w_proj_sb, w_proj_moba, w_out, rel_bias, norm_ffn_g, w_router, b_router,
           w_gate_up, b_gate_up, w_down, b_down, norm_final_g):
    batch, seq, D = x.shape
    assert norm_mix_g.shape[0] == 1, "single-layer trunk"
    tm = 256
    xt = x.reshape(batch * seq, D)
    col_scale = np.ones((w_in.shape[-1],), np.float32)
    col_scale[0:D_MIX] = QUERY_SCALE
    col_scale[3 * D_MIX:4 * D_MIX] = QUERY_SCALE
    proj = _in_proj(xt, norm_mix_g[0], (w_in[0] * col_scale).astype(BF16))
    ysb = _sb_attention(proj, batch=batch, seq=seq)
    ymb = _moba_attention(proj, _moba_bias(rel_bias), rel_bias, batch=batch, seq=seq)
    wr_pad = jnp.pad(w_router[0], ((0, 0), (0, LANES - N_EXPERTS)))
    br_pad = jnp.pad(b_router[0], (0, LANES - N_EXPERTS)).reshape(1, LANES)
    h, xl, route, cnt = _post_attn(ysb, ymb, proj, xt, w_proj_sb[0].astype(BF16), w_proj_moba[0].astype(BF16),
                                   w_out[0].astype(BF16), norm_ffn_g[0], wr_pad, br_pad, tm=tm)
    src_win, back_win, tile_plan = _window_plan(cnt[:, 0, :N_EXPERTS].astype(jnp.int32), tm)
    y = _experts(xl, src_win, tile_plan, w_gate_up[0], b_gate_up[0], w_down[0], b_down[0])
    out = _combine(y, back_win, h, route, norm_final_g, tm=tm)
    return out.reshape(batch, seq, D)
```

```python
import functools
import math

import numpy as np
import jax
import jax.numpy as jnp
from jax import lax
from jax.experimental import pallas as pl
from jax.experimental.pallas import tpu as pltpu

HEAD_DIM = 64
N_HEADS = 8
D_MIX = N_HEADS * HEAD_DIM
MOBA_BLOCK = 256
MOBA_TOPK = 3
REL_BUCKETS = 32
REL_MAX_DIST = 128
N_EXPERTS = 32
TOP_K = 4
SWIGLU_LIMIT = 7.0
SWIGLU_ALPHA = 1.702
RMS_EPS = 1e-6

LANES = 128
NEG_BIG = -1e30
LOG2_E = math.log2(math.e)
QUERY_SCALE = LOG2_E / math.sqrt(HEAD_DIM)
VMEM_LIMIT = 56 * 1024 * 1024

F32 = jnp.float32
BF16 = jnp.bfloat16


def _split_bf16(a):
    hi = a.astype(BF16)
    lo = (a - hi.astype(F32)).astype(BF16)
    return hi, lo


def _dot_nt(a, b):
    return lax.dot_general(a, b, (((1,), (1,)), ((), ())), preferred_element_type=F32)


def _in_proj_kernel(x_ref, g_ref, w_ref, o_ref, *, n_slab, slab):
    x = x_ref[...]
    var = jnp.mean(x * x, axis=-1, keepdims=True)
    xn = ((x * lax.rsqrt(var + RMS_EPS)) * g_ref[...]).astype(BF16)
    for j in range(n_slab):
        o_ref[j] = jnp.dot(xn, w_ref[:, j * slab:(j + 1) * slab],
                           preferred_element_type=F32).astype(o_ref.dtype)


def _in_proj(xt, g, w_bf16, *, tm=1024, slab=D_MIX):
    T, D = xt.shape
    n_slab = w_bf16.shape[1] // slab
    return pl.pallas_call(
        functools.partial(_in_proj_kernel, n_slab=n_slab, slab=slab),
        grid=(T // tm,),
        in_specs=[pl.BlockSpec((tm, D), lambda i: (i, 0)),
                  pl.BlockSpec((1, D), lambda i: (0, 0)),
                  pl.BlockSpec(w_bf16.shape, lambda i: (0, 0))],
        out_specs=pl.BlockSpec((n_slab, tm, slab), lambda i: (0, i, 0)),
        out_shape=jax.ShapeDtypeStruct((n_slab, T, slab), BF16),
        compiler_params=pltpu.CompilerParams(dimension_semantics=("arbitrary",),
                                             vmem_limit_bytes=VMEM_LIMIT),
        name="in_proj",
    )(xt, g.reshape(1, D), w_bf16)


def _sb_kernel(q_ref, k_ref, v_ref, o_ref, *, tq, tk):
    i = pl.program_id(2)
    lane = lax.broadcasted_iota(jnp.int32, (1, LANES), 1)
    jj = lax.broadcasted_iota(jnp.int32, (tk, tk), 0)
    ss = lax.broadcasted_iota(jnp.int32, (tk, tk), 1)
    later = (jj >= ss).astype(BF16)
    n_sub = tq // tk
    streams = range(SB_HEADS_PER_STEP)

    def lanes_of(head):
        return slice((head // 2) * LANES, (head // 2 + 1) * LANES)

    qhs = []
    for head in streams:
        q = q_ref[0, :, lanes_of(head)]
        qhs.append(jnp.where(lane // HEAD_DIM == head % 2, q, jnp.zeros((), q.dtype)))

    def q_tile_keys(base, carry, diagonal):
        tiles = [(h, sub) for sub in reversed(range(n_sub)) for h in streams]
        starts = {sub: pl.multiple_of(base + sub * tk, tk) for sub in range(n_sub)}
        c = [list(carry[h][0]) for h in streams]
        acc = [carry[h][1] for h in streams]
        z2, drop, inner, pv = {}, {}, {}, {}

        def first_row(t):
            return t[1] * tk if diagonal else 0

        def causal(t):
            shape = (tq - first_row(t), tk)
            return lax.broadcasted_iota(jnp.int32, shape, 1) < lax.broadcasted_iota(jnp.int32, shape, 0)

        def scores(t):
            z2[t] = _dot_nt(qhs[t[0]][first_row(t):], k_ref[0, pl.ds(starts[t[1]], tk), lanes_of(t[0])])

        def drops(t):
            d = jnp.maximum(z2[t], 0.0) + jnp.log2(1.0 + jnp.exp2(-jnp.abs(z2[t])))
            if diagonal:
                d = jnp.where(causal(t), d, 0.0)
            drop[t] = d
            inner[t] = jnp.dot(d.astype(BF16), later, preferred_element_type=F32)

        def weights(t):
            h, b0 = t[0], first_row(t) // tk
            blocks = [slice((b - b0) * tk, (b - b0 + 1) * tk) for b in range(b0, n_sub)]
            w = [jnp.exp2((z2[t][rows] - c[h][b0 + n]) - inner[t][rows]) for n, rows in enumerate(blocks)]
            w = w[0] if len(w) == 1 else jnp.concatenate(w, axis=0)
            if diagonal:
                w = jnp.where(causal(t), w, 0.0)
            for n, rows in enumerate(blocks):
                c[h][b0 + n] = c[h][b0 + n] + jnp.sum(drop[t][rows], axis=-1, keepdims=True)
            pv[t] = jnp.dot(w.astype(BF16), v_ref[0, pl.ds(starts[t[1]], tk), lanes_of(h)],
                            preferred_element_type=F32)

        def accumulate(t):
            h, r0 = t[0], first_row(t)
            acc[h] = acc[h] + pv[t] if r0 == 0 else jnp.concatenate([acc[h][:r0], acc[h][r0:] + pv[t]], axis=0)

        stages = (scores, drops, weights, accumulate)
        for step in range(len(tiles) + len(stages) - 1):
            for k, stage in enumerate(stages):
                if 0 <= step - k < len(tiles):
                    stage(tiles[step - k])
        return tuple((tuple(c[h]), acc[h]) for h in streams)

    init = tuple((tuple(jnp.zeros((tk, 1), F32) for _ in range(n_sub)), jnp.zeros((tq, LANES), F32))
                 for _ in streams)
    carry = q_tile_keys(i * tq, init, True)
    carry = lax.fori_loop(0, i, lambda it, cr: q_tile_keys((i - 1 - it) * tq, cr, False), carry)
    for g in range(SB_HEADS_PER_STEP // 2):
        o_ref[:, g * LANES:(g + 1) * LANES] = jnp.where(lane < HEAD_DIM, carry[2 * g][1],
                                                        carry[2 * g + 1][1]).astype(o_ref.dtype)


SB_HEADS_PER_STEP = 4


def _sb_attention(proj, *, batch, seq, tq=512, tk=256):
    T = batch * seq
    nq = seq // tq
    width = SB_HEADS_PER_STEP * HEAD_DIM
    return pl.pallas_call(
        functools.partial(_sb_kernel, tq=tq, tk=tk),
        grid=(batch, D_MIX // width, nq),
        in_specs=[pl.BlockSpec((1, tq, width), lambda b, p, i: (0, b * nq + i, p)),
                  pl.BlockSpec((1, seq, width), lambda b, p, i: (1, b, p)),
                  pl.BlockSpec((1, seq, width), lambda b, p, i: (2, b, p))],
        out_specs=pl.BlockSpec((tq, width), lambda b, p, i: (b * nq + i, p)),
        out_shape=jax.ShapeDtypeStruct((T, D_MIX), BF16),
        compiler_params=pltpu.CompilerParams(
            dimension_semantics=("arbitrary", "arbitrary", "arbitrary"), vmem_limit_bytes=VMEM_LIMIT),
        name="sb_attn",
    )(proj, proj, proj)


def _rel_bucket_np(dist):
    n = np.maximum(dist, 0)
    max_exact = REL_BUCKETS // 2
    nf = np.maximum(n, 1).astype(np.float64)
    large = max_exact + (np.log(nf / max_exact) / math.log(REL_MAX_DIST / max_exact)
                         * (REL_BUCKETS - max_exact)).astype(np.int32)
    large = np.minimum(large, REL_BUCKETS - 1)
    return np.where(n < max_exact, n, large).astype(np.int32)


def _bias_kernel(rel_ref, bkt_ref, o_ref):
    h = pl.program_id(0)
    for d in range(2):
        bkt = bkt_ref[d]
        acc = jnp.zeros(bkt.shape, F32)
        for b in range(REL_BUCKETS):
            acc = jnp.where(bkt == b, rel_ref[b, h], acc)
        o_ref[0, d] = acc * LOG2_E


def _moba_bias(rel_bias):
    L = MOBA_BLOCK
    r = np.arange(L)[:, None]
    c = np.arange(L)[None, :]
    bkt = np.stack([_rel_bucket_np(r - c), _rel_bucket_np(L + r - c)])
    return pl.pallas_call(
        _bias_kernel,
        grid=(N_HEADS,),
        in_specs=[pl.BlockSpec(memory_space=pltpu.SMEM),
                  pl.BlockSpec((2, L, L), lambda h: (0, 0, 0))],
        out_specs=pl.BlockSpec((1, 2, L, L), lambda h: (h, 0, 0, 0)),
        out_shape=jax.ShapeDtypeStruct((N_HEADS, 2, L, L), F32),
        name="moba_bias",
    )(rel_bias, jnp.asarray(bkt))


def _moba_kernel(rel_ref, q_ref, k_ref, v_ref, bias_ref, o_ref, km_ref, *, L, nblk):
    p = pl.program_id(1)
    i = pl.program_id(2)
    lane = lax.broadcasted_iota(jnp.int32, (1, LANES), 1)
    row = lax.broadcasted_iota(jnp.int32, (L, L), 0)
    col = lax.broadcasted_iota(jnp.int32, (L, L), 1)

    @pl.when(i == 0)
    def _():
        km_ref[...] = jnp.mean(k_ref[0].astype(F32).reshape(nblk, L, k_ref.shape[-1]), axis=1)

    def attend(n_off):
        rr = lax.broadcasted_iota(jnp.int32, (n_off, 1), 0)
        km = jnp.concatenate([km_ref[pl.ds(jnp.maximum(i - r, 0), 1), :] for r in range(n_off)], axis=0)
        km = jnp.concatenate([km, jnp.zeros((16 - n_off, km.shape[-1]), F32)], axis=0)
        km_hi, km_lo = _split_bf16(km)
        km_lo2 = (km - km_hi.astype(F32) - km_lo.astype(F32)).astype(BF16)
        past = (rr >= 1) & (rr <= i)
        outs, all_scores, all_values = [], [], []
        for head in range(MOBA_HEADS_PER_STEP):
            h = head % 2
            group = slice((head // 2) * LANES, (head // 2 + 1) * LANES)
            q = q_ref[0, :, group]
            own = lane // HEAD_DIM == h
            aux0 = (1 - h) * HEAD_DIM
            qh = jnp.where(own, q, jnp.zeros((), q.dtype))
            gs = (_dot_nt(km_hi[:, group], qh) + _dot_nt(km_lo[:, group], qh)
                  + _dot_nt(km_lo2[:, group], qh))[:n_off]
            pen = jnp.zeros((n_off, L), F32)
            for r in range(1, n_off):
                g_r = gs[r:r + 1, :]
                beats = past & ((gs > g_r) | ((gs == g_r) & (rr > r)))
                cnt = jnp.sum(beats.astype(F32), axis=0, keepdims=True)
                keep = (cnt < MOBA_TOPK) & (r <= i)
                pen = jnp.where((rr == r) & jnp.logical_not(keep), NEG_BIG, pen)
            far = jnp.full((n_off, L), rel_ref[REL_BUCKETS - 1, MOBA_HEADS_PER_STEP * p + head] * LOG2_E, F32)
            f_hi, f_lo = _split_bf16(far)
            f_lo2 = far - f_hi.astype(F32) - f_lo.astype(F32)
            far3 = jnp.where(rr == 0, f_hi.astype(F32), jnp.where(rr == 1, f_lo.astype(F32),
                                                                  jnp.where(rr == 2, f_lo2, 0.0)))
            aux = [pen, far3, jnp.zeros((LANES - aux0 - 2 * n_off, L), F32)]
            aux = jnp.concatenate(([jnp.zeros((aux0, L), F32)] if aux0 else []) + aux, axis=0)
            q_aug = jnp.where(own, q, aux.T.astype(q.dtype))
            scores, values = [], []
            for r in range(n_off):
                ks = pl.multiple_of(jnp.maximum(i - r, 0) * L, L)
                kblk = k_ref[0, pl.ds(ks, L), group]
                values.append(v_ref[0, pl.ds(ks, L), group])
                if r == 0:
                    s = _dot_nt(qh, kblk) + bias_ref[head, 0]
                    s = jnp.where(col <= row, s, NEG_BIG)
                else:
                    hot = (lane == aux0 + r)
                    if r >= 2:
                        hot = hot | ((lane >= aux0 + n_off) & (lane < aux0 + n_off + 3))
                    k_aug = jnp.where(own, kblk, jnp.where(hot, 1.0, 0.0).astype(kblk.dtype))
                    s = _dot_nt(q_aug, k_aug)
                    if r == 1:
                        s = s + bias_ref[head, 1]
                scores.append(s)
            all_scores.append(scores)
            all_values.append(values)
        for scores, values in zip(all_scores, all_values):
            m = scores[0]
            for s in scores[1:]:
                m = jnp.maximum(m, s)
            m = jnp.max(m, axis=-1, keepdims=True)
            probs = [jnp.exp2(s - m) for s in scores]
            l = probs[0]
            for pr in probs[1:]:
                l = l + pr
            l = jnp.sum(l, axis=-1, keepdims=True)
            acc = jnp.dot(probs[0].astype(BF16), values[0], preferred_element_type=F32)
            for pr, vblk in zip(probs[1:], values[1:]):
                acc = acc + jnp.dot(pr.astype(BF16), vblk, preferred_element_type=F32)
            outs.append(acc / l)
        for g in range(MOBA_HEADS_PER_STEP // 2):
            o_ref[:, g * LANES:(g + 1) * LANES] = jnp.where(lane < HEAD_DIM, outs[2 * g],
                                                            outs[2 * g + 1]).astype(o_ref.dtype)

    for n_off in range(1, nblk + 1):
        pl.when(i == n_off - 1)(functools.partial(attend, n_off))


MOBA_HEADS_PER_STEP = 4


def _moba_attention(proj, bias, rel_bias, *, batch, seq):
    T = batch * seq
    L = MOBA_BLOCK
    nblk = seq // L
    assert seq % (2 * L) == 0, "query blocks are handled in pairs"
    width = MOBA_HEADS_PER_STEP * HEAD_DIM
    return pl.pallas_call(
        functools.partial(_moba_kernel, L=L, nblk=nblk),
        grid=(batch, D_MIX // width, nblk),
        in_specs=[pl.BlockSpec(memory_space=pltpu.SMEM),
                  pl.BlockSpec((1, L, width), lambda b, p, i: (3, b * nblk + i, p)),
                  pl.BlockSpec((1, seq, width), lambda b, p, i: (4, b, p)),
                  pl.BlockSpec((1, seq, width), lambda b, p, i: (5, b, p)),
                  pl.BlockSpec((MOBA_HEADS_PER_STEP, 2, L, L), lambda b, p, i: (p, 0, 0, 0))],
        out_specs=pl.BlockSpec((L, width), lambda b, p, i: (b * nblk + i, p)),
        out_shape=jax.ShapeDtypeStruct((T, D_MIX), BF16),
        scratch_shapes=[pltpu.VMEM((nblk, width), F32)],
        compiler_params=pltpu.CompilerParams(
            dimension_semantics=("arbitrary", "arbitrary", "arbitrary"), vmem_limit_bytes=VMEM_LIMIT),
        name="moba_attn",
    )(rel_bias, proj, proj, proj, bias)


WIN_ROWS = 16
WINS_PER_TILE = 32


def _local_slots(tm):
    return tm * TOP_K + N_EXPERTS * WIN_ROWS


def _post_attn_kernel(ysb_ref, ymb_ref, gsb_ref, gmb_ref, x_ref, wsb_ref, wmb_ref, wo_ref, nrm_ref,
                      wr_ref, br_ref, h_ref, xl_ref, route_ref, cnt_ref, *, tm, chunk):
    def gate(g_ref):
        g = jnp.concatenate([g_ref[0], g_ref[1]], axis=1).astype(F32)
        return 1.0 / (1.0 + jnp.exp(-g))

    merged = (gate(gsb_ref) * jnp.dot(ysb_ref[...], wsb_ref[...], preferred_element_type=F32)
              + gate(gmb_ref) * jnp.dot(ymb_ref[...], wmb_ref[...], preferred_element_type=F32))
    hres = x_ref[...] + jnp.dot(merged.astype(BF16), wo_ref[...], preferred_element_type=F32)
    h_ref[...] = hres
    var = jnp.mean(hres * hres, axis=-1, keepdims=True)
    hn = (hres * lax.rsqrt(var + RMS_EPS)) * nrm_ref[...]
    a_hi, a_lo = _split_bf16(hn)
    w_hi, w_lo = _split_bf16(wr_ref[...])
    both = jnp.dot(a_hi, jnp.concatenate([w_hi, w_lo], axis=1), preferred_element_type=F32)
    logits = (both[:, :LANES] + both[:, LANES:] + jnp.dot(a_lo, w_hi, preferred_element_type=F32)) + br_ref[...]
    lane = lax.broadcasted_iota(jnp.int32, (1, LANES), 1).astype(F32)
    work = jnp.where(lane < N_EXPERTS, logits, NEG_BIG)
    vals, hots = [], []
    for _ in range(TOP_K):
        mx = jnp.max(work, axis=-1, keepdims=True)
        idx = jnp.min(jnp.where(work == mx, lane, float(LANES)), axis=-1, keepdims=True)
        hot = lane == idx
        work = jnp.where(hot, NEG_BIG, work)
        vals.append(mx)
        hots.append(hot)
    exps = [jnp.exp(v - vals[0]) for v in vals]
    denom = exps[0] + exps[1] + exps[2] + exps[3]
    sel = jnp.zeros((tm, LANES), F32)
    for hot in hots:
        sel = jnp.where(hot, 1.0, sel)
    r = lax.broadcasted_iota(jnp.int32, (tm, tm), 0)
    c = lax.broadcasted_iota(jnp.int32, (tm, tm), 1)
    earlier = jnp.dot((c < r).astype(BF16), sel.astype(BF16), preferred_element_type=F32)
    cnt = jnp.sum(sel, axis=0, keepdims=True)
    padded = jnp.floor((cnt + (WIN_ROWS - 1)) * (1.0 / WIN_ROWS)) * WIN_ROWS
    er = lax.broadcasted_iota(jnp.int32, (LANES, LANES), 0)
    ec = lax.broadcasted_iota(jnp.int32, (LANES, LANES), 1)
    lower = jnp.dot(jnp.broadcast_to(padded, (8, LANES)).astype(BF16), (er < ec).astype(BF16),
                    preferred_element_type=F32)[0:1]
    slot = lower + earlier
    slab = jnp.zeros((tm, LANES), F32)
    for k, hot in enumerate(hots):
        slab = jnp.where(lane == k, exps[k] / denom, slab)
        slab = jnp.where(lane == TOP_K + k, jnp.sum(jnp.where(hot, slot, 0.0), axis=-1, keepdims=True), slab)
    route_ref[...] = slab
    cnt_ref[0] = jnp.broadcast_to(cnt, (8, LANES))
    slots_t = slab.T
    hn_bf = hn.astype(BF16)
    for ci in range(_local_slots(tm) // chunk):
        rid = (lax.broadcasted_iota(jnp.int32, (chunk, 1), 0) + ci * chunk).astype(F32)
        onehot = rid == slots_t[TOP_K:TOP_K + 1, :]
        for k in range(1, TOP_K):
            onehot = onehot | (rid == slots_t[TOP_K + k:TOP_K + k + 1, :])
        xl_ref[ci * chunk:(ci + 1) * chunk, :] = jnp.dot(
            jnp.where(onehot, 1.0, 0.0).astype(BF16), hn_bf, preferred_element_type=F32).astype(xl_ref.dtype)


def _post_attn(ysb, ymb, proj, xt, wsb, wmb, wo, nrm_g, wr_pad, br_pad, *, tm):
    T, D = xt.shape
    half = D // 2
    n_tt = T // tm
    ls = _local_slots(tm)
    const = lambda i: (0, 0)
    return pl.pallas_call(
        functools.partial(_post_attn_kernel, tm=tm, chunk=256),
        grid=(n_tt,),
        in_specs=[pl.BlockSpec((tm, D_MIX), lambda i: (i, 0)),
                  pl.BlockSpec((tm, D_MIX), lambda i: (i, 0)),
                  pl.BlockSpec((2, tm, half), lambda i: (3, i, 0)),
                  pl.BlockSpec((2, tm, half), lambda i: (4, i, 0)),
                  pl.BlockSpec((tm, D), lambda i: (i, 0)),
                  pl.BlockSpec(wsb.shape, const),
                  pl.BlockSpec(wmb.shape, const),
                  pl.BlockSpec(wo.shape, const),
                  pl.BlockSpec((1, D), const),
                  pl.BlockSpec(wr_pad.shape, const),
                  pl.BlockSpec((1, LANES), const)],
        out_specs=[pl.BlockSpec((tm, D), lambda i: (i, 0)),
                   pl.BlockSpec((ls, D), lambda i: (i, 0)),
                   pl.BlockSpec((tm, LANES), lambda i: (i, 0)),
                   pl.BlockSpec((1, 8, LANES), lambda i: (i, 0, 0))],
        out_shape=[jax.ShapeDtypeStruct((T, D), F32),
                   jax.ShapeDtypeStruct((n_tt * ls, D), BF16),
                   jax.ShapeDtypeStruct((T, LANES), F32),
                   jax.ShapeDtypeStruct((n_tt, 8, LANES), F32)],
        compiler_params=pltpu.CompilerParams(dimension_semantics=("arbitrary",),
                                             vmem_limit_bytes=VMEM_LIMIT),
        name="post_attn",
    )(ysb, ymb, proj, proj, xt, wsb, wmb, wo, nrm_g.reshape(1, D), wr_pad, br_pad)


def _window_plan(counts, tm):
    n_tt, n_exp = counts.shape
    lw_tile = _local_slots(tm) // WIN_ROWS
    n_tiles = -(-(n_tt * (tm * TOP_K // WIN_ROWS + n_exp)) // WINS_PER_TILE) + n_exp
    nw = (counts + WIN_ROWS - 1) // WIN_ROWS
    local_start = jnp.cumsum(nw, axis=1) - nw
    per_expert = jnp.sum(nw, axis=0)
    per_expert_pad = -(-per_expert // WINS_PER_TILE) * WINS_PER_TILE
    expert_end = jnp.cumsum(per_expert_pad)
    sorted_start = (expert_end - per_expert_pad)[None, :] + jnp.cumsum(nw, axis=0) - nw
    run_start = sorted_start.T.reshape(-1)
    run_len = nw.T.reshape(-1)
    run_src = (jnp.arange(n_tt, dtype=jnp.int32)[:, None] * lw_tile + local_start).T.reshape(-1)
    g = jnp.arange(n_tiles * WINS_PER_TILE, dtype=jnp.int32)[:, None]
    off = g - run_start[None, :]
    src_win = jnp.sum(jnp.where((off >= 0) & (off < run_len[None, :]), run_src[None, :] + off, 0), axis=1)
    off_l = jnp.arange(lw_tile, dtype=jnp.int32)[None, :, None] - local_start[:, None, :]
    back_win = jnp.sum(jnp.where((off_l >= 0) & (off_l < nw[:, None, :]), sorted_start[:, None, :] + off_l, 0),
                       axis=2)
    tile_first = (jnp.arange(n_tiles, dtype=jnp.int32) * WINS_PER_TILE)[:, None]
    tile_valid = tile_first[:, 0] < expert_end[-1]
    used = (per_expert > 0)[None, :]
    expert_start = (expert_end - per_expert_pad)[None, :]
    ids = jnp.arange(n_exp, dtype=jnp.int32)[None, :]
    tile_expert = jnp.minimum(jnp.sum(expert_end[None, :] <= tile_first, axis=1), n_exp - 1)
    opens = jnp.sum(used & (expert_start == tile_first), axis=1) > 0
    buffer = (jnp.sum(used & (expert_start <= tile_first), axis=1) - 1) % 2
    next_expert = jnp.min(jnp.where(used & (expert_start > tile_first), ids, n_exp), axis=1)
    next_expert = jnp.where(next_expert == n_exp, -1, next_expert)
    tile_plan = tuple(a.astype(jnp.int32) for a in (tile_expert, tile_valid, opens, buffer, next_expert))
    return src_win.astype(jnp.int32), back_win.astype(jnp.int32), tile_plan


def _experts_kernel(te_ref, tv_ref, first_ref, par_ref, ne_ref, src_ref, nxt_ref, xl_ref, wgu_ref, bgu_ref, wd_ref,
                    bd_ref, y_ref, xbuf, wgu_f, wd_f, wgu_s, wd_s, sem, wsem, *, d_exp):
    j = pl.program_id(0)
    n_tiles = pl.num_programs(0)
    slot = j % 2

    def window_copy(win, dst_slot, w):
        return pltpu.make_async_copy(xl_ref.at[win], xbuf.at[dst_slot, w], sem.at[dst_slot])

    def start_gather(idx_ref, dst_slot):
        for w in range(WINS_PER_TILE):
            window_copy(idx_ref[0, 0, w], dst_slot, w).start()

    def weight_copies(expert, buf):
        return (pltpu.make_async_copy(wgu_ref.at[expert], wgu_f.at[buf], wsem.at[buf, 0]),
                pltpu.make_async_copy(wd_ref.at[expert], wd_f.at[buf], wsem.at[buf, 1]))

    @pl.when((j == 0) & (tv_ref[0] > 0))
    def _():
        for cp in weight_copies(te_ref[0], par_ref[0]):
            cp.start()
        start_gather(src_ref, 0)

    @pl.when((j + 1 < n_tiles) & (tv_ref[jnp.minimum(j + 1, n_tiles - 1)] > 0))
    def _():
        start_gather(nxt_ref, 1 - slot)

    @pl.when((first_ref[j] > 0) & (tv_ref[j] > 0))
    def _():
        buf = par_ref[j]

        @pl.when(ne_ref[j] >= 0)
        def _():
            for cp in weight_copies(ne_ref[j], 1 - buf):
                cp.start()

        for cp in weight_copies(te_ref[j], buf):
            cp.wait()
        wgu_s[...] = wgu_f[buf].astype(BF16)
        wd_s[...] = wd_f[buf].astype(BF16)

    @pl.when(tv_ref[j] > 0)
    def _():
        pltpu.make_async_copy(xl_ref.at[pl.ds(0, WINS_PER_TILE)], xbuf.at[slot], sem.at[slot]).wait()
        x = xbuf[slot].reshape(WINS_PER_TILE * WIN_ROWS, xbuf.shape[-1])
        hgu = jnp.dot(x, wgu_s[...], preferred_element_type=F32) + bgu_ref[0]
        gate = jnp.minimum(hgu[:, :d_exp], SWIGLU_LIMIT)
        up = jnp.clip(hgu[:, d_exp:], -SWIGLU_LIMIT, SWIGLU_LIMIT)
        act = (up + 1.0) * (gate * (1.0 / (1.0 + jnp.exp(-SWIGLU_ALPHA * gate))))
        y = jnp.dot(act.astype(BF16), wd_s[...], preferred_element_type=F32) + bd_ref[0]
        y_ref[...] = y.astype(y_ref.dtype)

    @pl.when(tv_ref[j] == 0)
    def _():
        y_ref[...] = jnp.zeros_like(y_ref)


def _experts(xl, src_win, tile_plan, wgu, bgu, wd, bd):
    D = xl.shape[1]
    E, _, d2 = wgu.shape
    d_exp = d2 // 2
    n_tiles = tile_plan[0].shape[0]
    tmm = WINS_PER_TILE * WIN_ROWS
    src3 = src_win.reshape(n_tiles, 1, WINS_PER_TILE)
    grid_spec = pltpu.PrefetchScalarGridSpec(
        num_scalar_prefetch=len(tile_plan),
        grid=(n_tiles,),
        in_specs=[pl.BlockSpec((1, 1, WINS_PER_TILE), lambda j, *_: (j, 0, 0), memory_space=pltpu.SMEM),
                  pl.BlockSpec((1, 1, WINS_PER_TILE), lambda j, *_: (jnp.minimum(j + 1, n_tiles - 1), 0, 0),
                               memory_space=pltpu.SMEM),
                  pl.BlockSpec(memory_space=pl.ANY),
                  pl.BlockSpec(memory_space=pl.ANY),
                  pl.BlockSpec((1, 1, d2), lambda j, te, *_: (te[j], 0, 0)),
                  pl.BlockSpec(memory_space=pl.ANY),
                  pl.BlockSpec((1, 1, D), lambda j, te, *_: (te[j], 0, 0))],
        out_specs=pl.BlockSpec((tmm, D), lambda j, *_: (j, 0)),
        scratch_shapes=[pltpu.VMEM((2, WINS_PER_TILE, WIN_ROWS, D), BF16),
                        pltpu.VMEM((2, D, d2), F32), pltpu.VMEM((2, d_exp, D), F32),
                        pltpu.VMEM((D, d2), BF16), pltpu.VMEM((d_exp, D), BF16),
                        pltpu.SemaphoreType.DMA((2,)), pltpu.SemaphoreType.DMA((2, 2))],
    )
    return pl.pallas_call(
        functools.partial(_experts_kernel, d_exp=d_exp),
        grid_spec=grid_spec,
        out_shape=jax.ShapeDtypeStruct((n_tiles * tmm, D), BF16),
        compiler_params=pltpu.CompilerParams(dimension_semantics=("arbitrary",),
                                             vmem_limit_bytes=VMEM_LIMIT),
        name="experts",
    )(*tile_plan, src3, src3, xl.reshape(-1, WIN_ROWS, D), wgu, bgu.reshape(E, 1, d2), wd, bd.reshape(E, 1, D))


def _combine_kernel(back_ref, nxt_ref, y_ref, h_ref, route_ref, g_ref, o_ref, ybuf, sem, *, tm, chunk):
    i = pl.program_id(0)
    slot = i % 2
    n_win = ybuf.shape[1]

    def window_copy(win, dst_slot, w):
        return pltpu.make_async_copy(y_ref.at[win], ybuf.at[dst_slot, w], sem.at[dst_slot])

    def start_fetch(idx_ref, dst_slot):
        for w in range(n_win):
            window_copy(idx_ref[0, 0, w], dst_slot, w).start()

    @pl.when(i == 0)
    def _():
        start_fetch(back_ref, 0)

    @pl.when(i + 1 < pl.num_programs(0))
    def _():
        start_fetch(nxt_ref, 1 - slot)

    route = route_ref[...]
    weights = [route[:, k:k + 1] for k in range(TOP_K)]
    slots = [route[:, TOP_K + k:TOP_K + k + 1] for k in range(TOP_K)]
    pltpu.make_async_copy(y_ref.at[pl.ds(0, n_win)], ybuf.at[slot], sem.at[slot]).wait()
    acc = h_ref[...]
    for ci in range(n_win * WIN_ROWS // chunk):
        sid = (lax.broadcasted_iota(jnp.int32, (1, chunk), 1) + ci * chunk).astype(F32)
        wmat = jnp.zeros((tm, chunk), F32)
        for k in range(TOP_K):
            wmat = jnp.where(sid == slots[k], weights[k], wmat)
        hi, lo = _split_bf16(wmat)
        rows = ybuf[slot, ci * chunk // WIN_ROWS:(ci + 1) * chunk // WIN_ROWS].reshape(chunk, ybuf.shape[-1])
        acc = acc + jnp.dot(hi, rows, preferred_element_type=F32) + jnp.dot(lo, rows, preferred_element_type=F32)
    var = jnp.mean(acc * acc, axis=-1, keepdims=True)
    o_ref[...] = (acc * lax.rsqrt(var + RMS_EPS)) * g_ref[...]


def _combine(y, back_win, h, route, g, *, tm):
    T, D = h.shape
    n_tt, n_win = back_win.shape
    back3 = back_win.reshape(n_tt, 1, n_win)
    return pl.pallas_call(
        functools.partial(_combine_kernel, tm=tm, chunk=256),
        grid=(n_tt,),
        in_specs=[pl.BlockSpec((1, 1, n_win), lambda i: (i, 0, 0), memory_space=pltpu.SMEM),
                  pl.BlockSpec((1, 1, n_win), lambda i: (jnp.minimum(i + 1, n_tt - 1), 0, 0),
                               memory_space=pltpu.SMEM),
                  pl.BlockSpec(memory_space=pl.ANY),
                  pl.BlockSpec((tm, D), lambda i: (i, 0)),
                  pl.BlockSpec((tm, LANES), lambda i: (i, 0)),
                  pl.BlockSpec((1, D), lambda i: (0, 0))],
        out_specs=pl.BlockSpec((tm, D), lambda i: (i, 0)),
        out_shape=jax.ShapeDtypeStruct((T, D), F32),
        scratch_shapes=[pltpu.VMEM((2, n_win, WIN_ROWS, D), BF16), pltpu.SemaphoreType.DMA((2,))],
        compiler_params=pltpu.CompilerParams(dimension_semantics=("arbitrary",),
                                             vmem_limit_bytes=VMEM_LIMIT),
        name="combine",
    )(back3, back3, y.reshape(-1, WIN_ROWS, D), h, route, g.reshape(1, D))


def kernel(x, norm_mix_g, w_in, w_proj_sb, w_proj_moba, w_out, rel_bias, norm_ffn_g, w_router, b_router,
           w_gate_up, b_gate_up, w_down, b_down, norm_final_g):
    batch, seq, D = x.shape
    assert norm_mix_g.shape[0] == 1, "single-layer trunk"
    tm = 256
    xt = x.reshape(batch * seq, D)
    col_scale = np.ones((w_in.shape[-1],), np.float32)
    col_scale[0:D_MIX] = QUERY_SCALE
    col_scale[3 * D_MIX:4 * D_MIX] = QUERY_SCALE
    proj = _in_proj(xt, norm_mix_g[0], (w_in[0] * col_scale).astype(BF16))
    ysb = _sb_attention(proj, batch=batch, seq=seq)
    ymb = _moba_attention(proj, _moba_bias(rel_bias), rel_bias, batch=batch, seq=seq)
    wr_pad = jnp.pad(w_router[0], ((0, 0), (0, LANES - N_EXPERTS)))
    br_pad = jnp.pad(b_router[0], (0, LANES - N_EXPERTS)).reshape(1, LANES)
    h, xl, route, cnt = _post_attn(ysb, ymb, proj, xt, w_proj_sb[0].astype(BF16), w_proj_moba[0].astype(BF16),
                                   w_out[0].astype(BF16), norm_ffn_g[0], wr_pad, br_pad, tm=tm)
    src_win, back_win, tile_plan = _window_plan(cnt[:, 0, :N_EXPERTS].astype(jnp.int32), tm)
    y = _experts(xl, src_win, tile_plan, w_gate_up[0], b_gate_up[0], w_down[0], b_down[0])
    out = _combine(y, back_win, h, route, norm_final_g, tm=tm)
    return out.reshape(batch, seq, D)
```

```python
import functools
import math

import numpy as np
import jax
import jax.numpy as jnp
from jax import lax
from jax.experimental import pallas as pl
from jax.experimental.pallas import tpu as pltpu

HEAD_DIM = 64
N_HEADS = 8
D_MIX = N_HEADS * HEAD_DIM
MOBA_BLOCK = 256
MOBA_TOPK = 3
REL_BUCKETS = 32
REL_MAX_DIST = 128
N_EXPERTS = 32
TOP_K = 4
SWIGLU_LIMIT = 7.0
SWIGLU_ALPHA = 1.702
RMS_EPS = 1e-6

LANES = 128
NEG_BIG = -1e30
LOG2_E = math.log2(math.e)
QUERY_SCALE = LOG2_E / math.sqrt(HEAD_DIM)
VMEM_LIMIT = 56 * 1024 * 1024

F32 = jnp.float32
BF16 = jnp.bfloat16


def _split_bf16(a):
    hi = a.astype(BF16)
    lo = (a - hi.astype(F32)).astype(BF16)
    return hi, lo


def _dot_nt(a, b):
    return lax.dot_general(a, b, (((1,), (1,)), ((), ())), preferred_element_type=F32)


def _in_proj_kernel(x_ref, g_ref, w_ref, o_ref, *, n_slab, slab):
    x = x_ref[...]
    var = jnp.mean(x * x, axis=-1, keepdims=True)
    xn = ((x * lax.rsqrt(var + RMS_EPS)) * g_ref[...]).astype(BF16)
    for j in range(n_slab):
        o_ref[j] = jnp.dot(xn, w_ref[:, j * slab:(j + 1) * slab],
                           preferred_element_type=F32).astype(o_ref.dtype)


def _in_proj(xt, g, w_bf16, *, tm=1024, slab=D_MIX):
    T, D = xt.shape
    n_slab = w_bf16.shape[1] // slab
    return pl.pallas_call(
        functools.partial(_in_proj_kernel, n_slab=n_slab, slab=slab),
        grid=(T // tm,),
        in_specs=[pl.BlockSpec((tm, D), lambda i: (i, 0)),
                  pl.BlockSpec((1, D), lambda i: (0, 0)),
                  pl.BlockSpec(w_bf16.shape, lambda i: (0, 0))],
        out_specs=pl.BlockSpec((n_slab, tm, slab), lambda i: (0, i, 0)),
        out_shape=jax.ShapeDtypeStruct((n_slab, T, slab), BF16),
        compiler_params=pltpu.CompilerParams(dimension_semantics=("arbitrary",),
                                             vmem_limit_bytes=VMEM_LIMIT),
        name="in_proj",
    )(xt, g.reshape(1, D), w_bf16)


def _sb_kernel(q_ref, k_ref, v_ref, o_ref, *, tq, tk):
    i = pl.program_id(2)
    lane = lax.broadcasted_iota(jnp.int32, (1, LANES), 1)
    jj = lax.broadcasted_iota(jnp.int32, (tk, tk), 0)
    ss = lax.broadcasted_iota(jnp.int32, (tk, tk), 1)
    later = (jj >= ss).astype(BF16)
    n_sub = tq // tk
    streams = range(SB_HEADS_PER_STEP)

    def lanes_of(head):
        return slice((head // 2) * LANES, (head // 2 + 1) * LANES)

    qhs = []
    for head in streams:
        q = q_ref[0, :, lanes_of(head)]
        qhs.append(jnp.where(lane // HEAD_DIM == head % 2, q, jnp.zeros((), q.dtype)))

    def q_tile_keys(base, carry, diagonal):
        tiles = [(h, sub) for sub in reversed(range(n_sub)) for h in streams]
        starts = {sub: pl.multiple_of(base + sub * tk, tk) for sub in range(n_sub)}
        c = [list(carry[h][0]) for h in streams]
        acc = [carry[h][1] for h in streams]
        z2, drop, inner, pv = {}, {}, {}, {}

        def first_row(t):
            return t[1] * tk if diagonal else 0

        def causal(t):
            shape = (tq - first_row(t), tk)
            return lax.broadcasted_iota(jnp.int32, shape, 1) < lax.broadcasted_iota(jnp.int32, shape, 0)

        def scores(t):
            z2[t] = _dot_nt(qhs[t[0]][first_row(t):], k_ref[0, pl.ds(starts[t[1]], tk), lanes_of(t[0])])

        def drops(t):
            d = jnp.maximum(z2[t], 0.0) + jnp.log2(1.0 + jnp.exp2(-jnp.abs(z2[t])))
            if diagonal:
                d = jnp.where(causal(t), d, 0.0)
            drop[t] = d
            inner[t] = jnp.dot(d.astype(BF16), later, preferred_element_type=F32)

        def weights(t):
            h, b0 = t[0], first_row(t) // tk
            blocks = [slice((b - b0) * tk, (b - b0 + 1) * tk) for b in range(b0, n_sub)]
            w = [jnp.exp2((z2[t][rows] - c[h][b0 + n]) - inner[t][rows]) for n, rows in enumerate(blocks)]
            w = w[0] if len(w) == 1 else jnp.concatenate(w, axis=0)
            if diagonal:
                w = jnp.where(causal(t), w, 0.0)
            for n, rows in enumerate(blocks):
                c[h][b0 + n] = c[h][b0 + n] + jnp.sum(drop[t][rows], axis=-1, keepdims=True)
            pv[t] = jnp.dot(w.astype(BF16), v_ref[0, pl.ds(starts[t[1]], tk), lanes_of(h)],
                            preferred_element_type=F32)

        def accumulate(t):
            h, r0 = t[0], first_row(t)
            acc[h] = acc[h] + pv[t] if r0 == 0 else jnp.concatenate([acc[h][:r0], acc[h][r0:] + pv[t]], axis=0)

        stages = (scores, drops, weights, accumulate)
        for step in range(len(tiles) + len(stages) - 1):
            for k, stage in enumerate(stages):
                if 0 <= step - k < len(tiles):
                    stage(tiles[step - k])
        return tuple((tuple(c[h]), acc[h]) for h in streams)

    init = tuple((tuple(jnp.zeros((tk, 1), F32) for _ in range(n_sub)), jnp.zeros((tq, LANES), F32))
                 for _ in streams)
    carry = q_tile_keys(i * tq, init, True)
    carry = lax.fori_loop(0, i, lambda it, cr: q_tile_keys((i - 1 - it) * tq, cr, False), carry)
    for g in range(SB_HEADS_PER_STEP // 2):
        o_ref[:, g * LANES:(g + 1) * LANES] = jnp.where(lane < HEAD_DIM, carry[2 * g][1],
                                                        carry[2 * g + 1][1]).astype(o_ref.dtype)


SB_HEADS_PER_STEP = 4


def _sb_attention(proj, *, batch, seq, tq=512, tk=256):
    T = batch * seq
    nq = seq // tq
    width = SB_HEADS_PER_STEP * HEAD_DIM
    return pl.pallas_call(
        functools.partial(_sb_kernel, tq=tq, tk=tk),
        grid=(batch, D_MIX // width, nq),
        in_specs=[pl.BlockSpec((1, tq, width), lambda b, p, i: (0, b * nq + i, p)),
                  pl.BlockSpec((1, seq, width), lambda b, p, i: (1, b, p)),
                  pl.BlockSpec((1, seq, width), lambda b, p, i: (2, b, p))],
        out_specs=pl.BlockSpec((tq, width), lambda b, p, i: (b * nq + i, p)),
        out_shape=jax.ShapeDtypeStruct((T, D_MIX), BF16),
        compiler_params=pltpu.CompilerParams(
            dimension_semantics=("arbitrary", "arbitrary", "arbitrary"), vmem_limit_bytes=VMEM_LIMIT),
        name="sb_attn",
    )(proj, proj, proj)


def _rel_bucket_np(dist):
    n = np.maximum(dist, 0)
    max_exact = REL_BUCKETS // 2
    nf = np.maximum(n, 1).astype(np.float64)
    large = max_exact + (np.log(nf / max_exact) / math.log(REL_MAX_DIST / max_exact)
                         * (REL_BUCKETS - max_exact)).astype(np.int32)
    large = np.minimum(large, REL_BUCKETS - 1)
    return np.where(n < max_exact, n, large).astype(np.int32)


def _bias_kernel(rel_ref, bkt_ref, o_ref):
    h = pl.program_id(0)
    for d in range(2):
        bkt = bkt_ref[d]
        acc = jnp.zeros(bkt.shape, F32)
        for b in range(REL_BUCKETS):
            acc = jnp.where(bkt == b, rel_ref[b, h], acc)
        o_ref[0, d] = acc * LOG2_E


def _moba_bias(rel_bias):
    L = MOBA_BLOCK
    r = np.arange(L)[:, None]
    c = np.arange(L)[None, :]
    bkt = np.stack([_rel_bucket_np(r - c), _rel_bucket_np(L + r - c)])
    return pl.pallas_call(
        _bias_kernel,
        grid=(N_HEADS,),
        in_specs=[pl.BlockSpec(memory_space=pltpu.SMEM),
                  pl.BlockSpec((2, L, L), lambda h: (0, 0, 0))],
        out_specs=pl.BlockSpec((1, 2, L, L), lambda h: (h, 0, 0, 0)),
        out_shape=jax.ShapeDtypeStruct((N_HEADS, 2, L, L), F32),
        name="moba_bias",
    )(rel_bias, jnp.asarray(bkt))


def _moba_kernel(rel_ref, q_ref, k_ref, v_ref, bias_ref, o_ref, km_ref, *, L, nblk):
    p = pl.program_id(1)
    i = pl.program_id(2)
    lane = lax.broadcasted_iota(jnp.int32, (1, LANES), 1)
    row = lax.broadcasted_iota(jnp.int32, (L, L), 0)
    col = lax.broadcasted_iota(jnp.int32, (L, L), 1)

    @pl.when(i == 0)
    def _():
        km_ref[...] = jnp.mean(k_ref[0].astype(F32).reshape(nblk, L, k_ref.shape[-1]), axis=1)

    def attend(n_off):
        rr = lax.broadcasted_iota(jnp.int32, (n_off, 1), 0)
        km = jnp.concatenate([km_ref[pl.ds(jnp.maximum(i - r, 0), 1), :] for r in range(n_off)], axis=0)
        km = jnp.concatenate([km, jnp.zeros((16 - n_off, km.shape[-1]), F32)], axis=0)
        km_hi, km_lo = _split_bf16(km)
        km_lo2 = (km - km_hi.astype(F32) - km_lo.astype(F32)).astype(BF16)
        past = (rr >= 1) & (rr <= i)
        outs, all_scores, all_values = [], [], []
        for head in range(MOBA_HEADS_PER_STEP):
            h = head % 2
            group = slice((head // 2) * LANES, (head // 2 + 1) * LANES)
            q = q_ref[0, :, group]
            own = lane // HEAD_DIM == h
            aux0 = (1 - h) * HEAD_DIM
            qh = jnp.where(own, q, jnp.zeros((), q.dtype))
            gs = (_dot_nt(km_hi[:, group], qh) + _dot_nt(km_lo[:, group], qh)
                  + _dot_nt(km_lo2[:, group], qh))[:n_off]
            pen = jnp.zeros((n_off, L), F32)
            for r in range(1, n_off):
                g_r = gs[r:r + 1, :]
                beats = past & ((gs > g_r) | ((gs == g_r) & (rr > r)))
                cnt = jnp.sum(beats.astype(F32), axis=0, keepdims=True)
                keep = (cnt < MOBA_TOPK) & (r <= i)
                pen = jnp.where((rr == r) & jnp.logical_not(keep), NEG_BIG, pen)
            far = jnp.full((n_off, L), rel_ref[REL_BUCKETS - 1, MOBA_HEADS_PER_STEP * p + head] * LOG2_E, F32)
            f_hi, f_lo = _split_bf16(far)
            f_lo2 = far - f_hi.astype(F32) - f_lo.astype(F32)
            far3 = jnp.where(rr == 0, f_hi.astype(F32), jnp.where(rr == 1, f_lo.astype(F32),
                                                                  jnp.where(rr == 2, f_lo2, 0.0)))
            aux = [pen, far3, jnp.zeros((LANES - aux0 - 2 * n_off, L), F32)]
            aux = jnp.concatenate(([jnp.zeros((aux0, L), F32)] if aux0 else []) + aux, axis=0)
            q_aug = jnp.where(own, q, aux.T.astype(q.dtype))
            scores, values = [], []
            for r in range(n_off):
                ks = pl.multiple_of(jnp.maximum(i - r, 0) * L, L)
                kblk = k_ref[0, pl.ds(ks, L), group]
                values.append(v_ref[0, pl.ds(ks, L), group])
                if r == 0:
                    s = _dot_nt(qh, kblk) + bias_ref[head, 0]
                    s = jnp.where(col <= row, s, NEG_BIG)
                else:
                    hot = (lane == aux0 + r)
                    if r >= 2:
                        hot = hot | ((lane >= aux0 + n_off) & (lane < aux0 + n_off + 3))
                    k_aug = jnp.where(own, kblk, jnp.where(hot, 1.0, 0.0).astype(kblk.dtype))
                    s = _dot_nt(q_aug, k_aug)
                    if r == 1:
                        s = s + bias_ref[head, 1]
                scores.append(s)
            all_scores.append(scores)
            all_values.append(values)
        for scores, values in zip(all_scores, all_values):
            m = scores[0]
            for s in scores[1:]:
                m = jnp.maximum(m, s)
            m = jnp.max(m, axis=-1, keepdims=True)
            probs = [jnp.exp2(s - m) for s in scores]
            l = probs[0]
            for pr in probs[1:]:
                l = l + pr
            l = jnp.sum(l, axis=-1, keepdims=True)
            acc = jnp.dot(probs[0].astype(BF16), values[0], preferred_element_type=F32)
            for pr, vblk in zip(probs[1:], values[1:]):
                acc = acc + jnp.dot(pr.astype(BF16), vblk, preferred_element_type=F32)
            outs.append(acc / l)
        for g in range(MOBA_HEADS_PER_STEP // 2):
            o_ref[:, g * LANES:(g + 1) * LANES] = jnp.where(lane < HEAD_DIM, outs[2 * g],
                                                            outs[2 * g + 1]).astype(o_ref.dtype)

    for n_off in range(1, nblk + 1):
        pl.when(i == n_off - 1)(functools.partial(attend, n_off))


MOBA_HEADS_PER_STEP = 4


def _moba_attention(proj, bias, rel_bias, *, batch, seq):
    T = batch * seq
    L = MOBA_BLOCK
    nblk = seq // L
    assert seq % (2 * L) == 0, "query blocks are handled in pairs"
    width = MOBA_HEADS_PER_STEP * HEAD_DIM
    return pl.pallas_call(
        functools.partial(_moba_kernel, L=L, nblk=nblk),
        grid=(batch, D_MIX // width, nblk),
        in_specs=[pl.BlockSpec(memory_space=pltpu.SMEM),
                  pl.BlockSpec((1, L, width), lambda b, p, i: (3, b * nblk + i, p)),
                  pl.BlockSpec((1, seq, width), lambda b, p, i: (4, b, p)),
                  pl.BlockSpec((1, seq, width), lambda b, p, i: (5, b, p)),
                  pl.BlockSpec((MOBA_HEADS_PER_STEP, 2, L, L), lambda b, p, i: (p, 0, 0, 0))],
        out_specs=pl.BlockSpec((L, width), lambda b, p, i: (b * nblk + i, p)),
        out_shape=jax.ShapeDtypeStruct((T, D_MIX), BF16),
        scratch_shapes=[pltpu.VMEM((nblk, width), F32)],
        compiler_params=pltpu.CompilerParams(
            dimension_semantics=("arbitrary", "arbitrary", "arbitrary"), vmem_limit_bytes=VMEM_LIMIT),
        name="moba_attn",
    )(rel_bias, proj, proj, proj, bias)


WIN_ROWS = 16
WINS_PER_TILE = 32


def _local_slots(tm):
    return tm * TOP_K + N_EXPERTS * WIN_ROWS


def _post_attn_kernel(ysb_ref, ymb_ref, gsb_ref, gmb_ref, x_ref, wsb_ref, wmb_ref, wo_ref, nrm_ref,
                      wr_ref, br_ref, h_ref, xl_ref, route_ref, cnt_ref, *, tm, chunk):
    lane = lax.broadcasted_iota(jnp.int32, (1, LANES), 1).astype(F32)
    ls = _local_slots(tm)
    w_hi, w_lo = _split_bf16(wr_ref[...])
    w_both = jnp.concatenate([w_hi, w_lo], axis=1)
    hn_of, slots_of = {}, {}

    def dense(sub):
        rows = slice(sub * tm, (sub + 1) * tm)

        def gate(g_ref):
            g = jnp.concatenate([g_ref[0, rows], g_ref[1, rows]], axis=1).astype(F32)
            return 1.0 / (1.0 + jnp.exp(-g))

        merged = (gate(gsb_ref) * jnp.dot(ysb_ref[rows], wsb_ref[...], preferred_element_type=F32)
                  + gate(gmb_ref) * jnp.dot(ymb_ref[rows], wmb_ref[...], preferred_element_type=F32))
        hres = x_ref[rows] + jnp.dot(merged.astype(BF16), wo_ref[...], preferred_element_type=F32)
        h_ref[rows] = hres
        var = jnp.mean(hres * hres, axis=-1, keepdims=True)
        hn_of[sub] = (hres * lax.rsqrt(var + RMS_EPS)) * nrm_ref[...]

    def route(sub):
        hn = hn_of[sub]
        a_hi, a_lo = _split_bf16(hn)
        both = jnp.dot(a_hi, w_both, preferred_element_type=F32)
        logits = (both[:, :LANES] + both[:, LANES:] + jnp.dot(a_lo, w_hi, preferred_element_type=F32)) + br_ref[...]
        work = jnp.where(lane < N_EXPERTS, logits, NEG_BIG)
        vals, hots = [], []
        for _ in range(TOP_K):
            mx = jnp.max(work, axis=-1, keepdims=True)
            idx = jnp.min(jnp.where(work == mx, lane, float(LANES)), axis=-1, keepdims=True)
            hot = lane == idx
            work = jnp.where(hot, NEG_BIG, work)
            vals.append(mx)
            hots.append(hot)
        exps = [jnp.exp(v - vals[0]) for v in vals]
        denom = exps[0] + exps[1] + exps[2] + exps[3]
        sel = jnp.zeros((tm, LANES), F32)
        for hot in hots:
            sel = jnp.where(hot, 1.0, sel)
        r = lax.broadcasted_iota(jnp.int32, (tm, tm), 0)
        c = lax.broadcasted_iota(jnp.int32, (tm, tm), 1)
        earlier = jnp.dot((c < r).astype(BF16), sel.astype(BF16), preferred_element_type=F32)
        cnt = jnp.sum(sel, axis=0, keepdims=True)
        padded = jnp.floor((cnt + (WIN_ROWS - 1)) * (1.0 / WIN_ROWS)) * WIN_ROWS
        er = lax.broadcasted_iota(jnp.int32, (LANES, LANES), 0)
        ec = lax.broadcasted_iota(jnp.int32, (LANES, LANES), 1)
        lower = jnp.dot(jnp.broadcast_to(padded, (8, LANES)).astype(BF16), (er < ec).astype(BF16),
                        preferred_element_type=F32)[0:1]
        slot = lower + earlier
        slab = jnp.zeros((tm, LANES), F32)
        for k, hot in enumerate(hots):
            slab = jnp.where(lane == k, exps[k] / denom, slab)
            slab = jnp.where(lane == TOP_K + k, jnp.sum(jnp.where(hot, slot, 0.0), axis=-1, keepdims=True), slab)
        route_ref[sub * tm:(sub + 1) * tm] = slab
        cnt_ref[sub] = jnp.broadcast_to(cnt, (8, LANES))
        slots_of[sub] = slab.T

    def sort_rows(sub):
        slots_t = slots_of[sub]
        hn_bf = hn_of[sub].astype(BF16)
        for ci in range(ls // chunk):
            rid = (lax.broadcasted_iota(jnp.int32, (chunk, 1), 0) + ci * chunk).astype(F32)
            onehot = rid == slots_t[TOP_K:TOP_K + 1, :]
            for k in range(1, TOP_K):
                onehot = onehot | (rid == slots_t[TOP_K + k:TOP_K + k + 1, :])
            xl_ref[sub * ls + ci * chunk:sub * ls + (ci + 1) * chunk, :] = jnp.dot(
                jnp.where(onehot, 1.0, 0.0).astype(BF16), hn_bf, preferred_element_type=F32).astype(xl_ref.dtype)

    stages = (dense, route, sort_rows)
    for step in range(POST_TILES_PER_STEP + len(stages) - 1):
        for k, stage in enumerate(stages):
            if 0 <= step - k < POST_TILES_PER_STEP:
                stage(step - k)


POST_TILES_PER_STEP = 2


def _post_attn(ysb, ymb, proj, xt, wsb, wmb, wo, nrm_g, wr_pad, br_pad, *, tm):
    T, D = xt.shape
    half = D // 2
    n_tt = T // tm
    ls = _local_slots(tm)
    sub = POST_TILES_PER_STEP
    const = lambda i: (0, 0)
    return pl.pallas_call(
        functools.partial(_post_attn_kernel, tm=tm, chunk=256),
        grid=(n_tt // sub,),
        in_specs=[pl.BlockSpec((sub * tm, D_MIX), lambda i: (i, 0)),
                  pl.BlockSpec((sub * tm, D_MIX), lambda i: (i, 0)),
                  pl.BlockSpec((2, sub * tm, half), lambda i: (3, i, 0)),
                  pl.BlockSpec((2, sub * tm, half), lambda i: (4, i, 0)),
                  pl.BlockSpec((sub * tm, D), lambda i: (i, 0)),
                  pl.BlockSpec(wsb.shape, const),
                  pl.BlockSpec(wmb.shape, const),
                  pl.BlockSpec(wo.shape, const),
                  pl.BlockSpec((1, D), const),
                  pl.BlockSpec(wr_pad.shape, const),
                  pl.BlockSpec((1, LANES), const)],
        out_specs=[pl.BlockSpec((sub * tm, D), lambda i: (i, 0)),
                   pl.BlockSpec((sub * ls, D), lambda i: (i, 0)),
                   pl.BlockSpec((sub * tm, LANES), lambda i: (i, 0)),
                   pl.BlockSpec((sub, 8, LANES), lambda i: (i, 0, 0))],
        out_shape=[jax.ShapeDtypeStruct((T, D), F32),
                   jax.ShapeDtypeStruct((n_tt * ls, D), BF16),
                   jax.ShapeDtypeStruct((T, LANES), F32),
                   jax.ShapeDtypeStruct((n_tt, 8, LANES), F32)],
        compiler_params=pltpu.CompilerParams(dimension_semantics=("arbitrary",),
                                             vmem_limit_bytes=VMEM_LIMIT),
        name="post_attn",
    )(ysb, ymb, proj, proj, xt, wsb, wmb, wo, nrm_g.reshape(1, D), wr_pad, br_pad)


def _window_plan(counts, tm):
    n_tt, n_exp = counts.shape
    lw_tile = _local_slots(tm) // WIN_ROWS
    n_tiles = -(-(n_tt * (tm * TOP_K // WIN_ROWS + n_exp)) // WINS_PER_TILE) + n_exp
    nw = (counts + WIN_ROWS - 1) // WIN_ROWS
    local_start = jnp.cumsum(nw, axis=1) - nw
    per_expert = jnp.sum(nw, axis=0)
    per_expert_pad = -(-per_expert // WINS_PER_TILE) * WINS_PER_TILE
    expert_end = jnp.cumsum(per_expert_pad)
    sorted_start = (expert_end - per_expert_pad)[None, :] + jnp.cumsum(nw, axis=0) - nw
    run_start = sorted_start.T.reshape(-1)
    run_len = nw.T.reshape(-1)
    run_src = (jnp.arange(n_tt, dtype=jnp.int32)[:, None] * lw_tile + local_start).T.reshape(-1)
    g = jnp.arange(n_tiles * WINS_PER_TILE, dtype=jnp.int32)[:, None]
    off = g - run_start[None, :]
    src_win = jnp.sum(jnp.where((off >= 0) & (off < run_len[None, :]), run_src[None, :] + off, 0), axis=1)
    off_l = jnp.arange(lw_tile, dtype=jnp.int32)[None, :, None] - local_start[:, None, :]
    back_win = jnp.sum(jnp.where((off_l >= 0) & (off_l < nw[:, None, :]), sorted_start[:, None, :] + off_l, 0),
                       axis=2)
    tile_first = (jnp.arange(n_tiles, dtype=jnp.int32) * WINS_PER_TILE)[:, None]
    tile_valid = tile_first[:, 0] < expert_end[-1]
    used = (per_expert > 0)[None, :]
    expert_start = (expert_end - per_expert_pad)[None, :]
    ids = jnp.arange(n_exp, dtype=jnp.int32)[None, :]
    tile_expert = jnp.minimum(jnp.sum(expert_end[None, :] <= tile_first, axis=1), n_exp - 1)
    opens = jnp.sum(used & (expert_start == tile_first), axis=1) > 0
    buffer = (jnp.sum(used & (expert_start <= tile_first), axis=1) - 1) % 2
    next_expert = jnp.min(jnp.where(used & (expert_start > tile_first), ids, n_exp), axis=1)
    next_expert = jnp.where(next_expert == n_exp, -1, next_expert)
    tile_plan = tuple(a.astype(jnp.int32) for a in (tile_expert, tile_valid, opens, buffer, next_expert))
    return src_win.astype(jnp.int32), back_win.astype(jnp.int32), tile_plan


def _experts_kernel(te_ref, tv_ref, first_ref, par_ref, ne_ref, src_ref, nxt_ref, xl_ref, wgu_ref, bgu_ref, wd_ref,
                    bd_ref, y_ref, xbuf, wgu_f, wd_f, wgu_s, wd_s, sem, wsem, *, d_exp):
    j = pl.program_id(0)
    n_tiles = pl.num_programs(0)
    slot = j % 2

    def window_copy(win, dst_slot, w):
        return pltpu.make_async_copy(xl_ref.at[win], xbuf.at[dst_slot, w], sem.at[dst_slot])

    def start_gather(idx_ref, dst_slot):
        for w in range(WINS_PER_TILE):
            window_copy(idx_ref[0, 0, w], dst_slot, w).start()

    def weight_copies(expert, buf):
        return (pltpu.make_async_copy(wgu_ref.at[expert], wgu_f.at[buf], wsem.at[buf, 0]),
                pltpu.make_async_copy(wd_ref.at[expert], wd_f.at[buf], wsem.at[buf, 1]))

    @pl.when((j == 0) & (tv_ref[0] > 0))
    def _():
        for cp in weight_copies(te_ref[0], par_ref[0]):
            cp.start()
        start_gather(src_ref, 0)

    @pl.when((j + 1 < n_tiles) & (tv_ref[jnp.minimum(j + 1, n_tiles - 1)] > 0))
    def _():
        start_gather(nxt_ref, 1 - slot)

    @pl.when((first_ref[j] > 0) & (tv_ref[j] > 0))
    def _():
        buf = par_ref[j]

        @pl.when(ne_ref[j] >= 0)
        def _():
            for cp in weight_copies(ne_ref[j], 1 - buf):
                cp.start()

        for cp in weight_copies(te_ref[j], buf):
            cp.wait()
        wgu_s[...] = wgu_f[buf].astype(BF16)
        wd_s[...] = wd_f[buf].astype(BF16)

    @pl.when(tv_ref[j] > 0)
    def _():
        pltpu.make_async_copy(xl_ref.at[pl.ds(0, WINS_PER_TILE)], xbuf.at[slot], sem.at[slot]).wait()
        x = xbuf[slot].reshape(WINS_PER_TILE * WIN_ROWS, xbuf.shape[-1])
        hgu = jnp.dot(x, wgu_s[...], preferred_element_type=F32) + bgu_ref[0]
        gate = jnp.minimum(hgu[:, :d_exp], SWIGLU_LIMIT)
        up = jnp.clip(hgu[:, d_exp:], -SWIGLU_LIMIT, SWIGLU_LIMIT)
        act = (up + 1.0) * (gate * (1.0 / (1.0 + jnp.exp(-SWIGLU_ALPHA * gate))))
        y = jnp.dot(act.astype(BF16), wd_s[...], preferred_element_type=F32) + bd_ref[0]
        y_ref[...] = y.astype(y_ref.dtype)

    @pl.when(tv_ref[j] == 0)
    def _():
        y_ref[...] = jnp.zeros_like(y_ref)


def _experts(xl, src_win, tile_plan, wgu, bgu, wd, bd):
    D = xl.shape[1]
    E, _, d2 = wgu.shape
    d_exp = d2 // 2
    n_tiles = tile_plan[0].shape[0]
    tmm = WINS_PER_TILE * WIN_ROWS
    src3 = src_win.reshape(n_tiles, 1, WINS_PER_TILE)
    grid_spec = pltpu.PrefetchScalarGridSpec(
        num_scalar_prefetch=len(tile_plan),
        grid=(n_tiles,),
        in_specs=[pl.BlockSpec((1, 1, WINS_PER_TILE), lambda j, *_: (j, 0, 0), memory_space=pltpu.SMEM),
                  pl.BlockSpec((1, 1, WINS_PER_TILE), lambda j, *_: (jnp.minimum(j + 1, n_tiles - 1), 0, 0),
                               memory_space=pltpu.SMEM),
                  pl.BlockSpec(memory_space=pl.ANY),
                  pl.BlockSpec(memory_space=pl.ANY),
                  pl.BlockSpec((1, 1, d2), lambda j, te, *_: (te[j], 0, 0)),
                  pl.BlockSpec(memory_space=pl.ANY),
                  pl.BlockSpec((1, 1, D), lambda j, te, *_: (te[j], 0, 0))],
        out_specs=pl.BlockSpec((tmm, D), lambda j, *_: (j, 0)),
        scratch_shapes=[pltpu.VMEM((2, WINS_PER_TILE, WIN_ROWS, D), BF16),
                        pltpu.VMEM((2, D, d2), F32), pltpu.VMEM((2, d_exp, D), F32),
                        pltpu.VMEM((D, d2), BF16), pltpu.VMEM((d_exp, D), BF16),
                        pltpu.SemaphoreType.DMA((2,)), pltpu.SemaphoreType.DMA((2, 2))],
    )
    return pl.pallas_call(
        functools.partial(_experts_kernel, d_exp=d_exp),
        grid_spec=grid_spec,
        out_shape=jax.ShapeDtypeStruct((n_tiles * tmm, D), BF16),
        compiler_params=pltpu.CompilerParams(dimension_semantics=("arbitrary",),
                                             vmem_limit_bytes=VMEM_LIMIT),
        name="experts",
    )(*tile_plan, src3, src3, xl.reshape(-1, WIN_ROWS, D), wgu, bgu.reshape(E, 1, d2), wd, bd.reshape(E, 1, D))


def _combine_kernel(back_ref, nxt_ref, y_ref, h_ref, route_ref, g_ref, o_ref, ybuf, sem, *, tm, chunk):
    i = pl.program_id(0)
    slot = i % 2
    n_win = ybuf.shape[1]
    tile_win = n_win // COMBINE_TILES_PER_STEP

    def window_copy(win, dst_slot, w):
        return pltpu.make_async_copy(y_ref.at[win], ybuf.at[dst_slot, w], sem.at[dst_slot])

    def start_fetch(idx_ref, dst_slot):
        for w in range(n_win):
            window_copy(idx_ref[0, 0, w], dst_slot, w).start()

    @pl.when(i == 0)
    def _():
        start_fetch(back_ref, 0)

    @pl.when(i + 1 < pl.num_programs(0))
    def _():
        start_fetch(nxt_ref, 1 - slot)

    tiles = [slice(t * tm, (t + 1) * tm) for t in range(COMBINE_TILES_PER_STEP)]
    routes = [route_ref[rows] for rows in tiles]
    weights = [[route[:, k:k + 1] for k in range(TOP_K)] for route in routes]
    slots = [[route[:, TOP_K + k:TOP_K + k + 1] for k in range(TOP_K)] for route in routes]
    pltpu.make_async_copy(y_ref.at[pl.ds(0, n_win)], ybuf.at[slot], sem.at[slot]).wait()
    acc = [h_ref[rows] for rows in tiles]
    for ci in range(tile_win * WIN_ROWS // chunk):
        sid = (lax.broadcasted_iota(jnp.int32, (1, chunk), 1) + ci * chunk).astype(F32)
        for t in range(COMBINE_TILES_PER_STEP):
            wmat = jnp.zeros((tm, chunk), F32)
            for k in range(TOP_K):
                wmat = jnp.where(sid == slots[t][k], weights[t][k], wmat)
            hi, lo = _split_bf16(wmat)
            first = t * tile_win + ci * chunk // WIN_ROWS
            y_rows = ybuf[slot, first:first + chunk // WIN_ROWS].reshape(chunk, ybuf.shape[-1])
            acc[t] = (acc[t] + jnp.dot(hi, y_rows, preferred_element_type=F32)
                      + jnp.dot(lo, y_rows, preferred_element_type=F32))
    for t, rows in enumerate(tiles):
        var = jnp.mean(acc[t] * acc[t], axis=-1, keepdims=True)
        o_ref[rows] = (acc[t] * lax.rsqrt(var + RMS_EPS)) * g_ref[...]


COMBINE_TILES_PER_STEP = 2


def _combine(y, back_win, h, route, g, *, tm):
    T, D = h.shape
    sub = COMBINE_TILES_PER_STEP
    n_steps, n_win = back_win.shape[0] // sub, back_win.shape[1] * sub
    back3 = back_win.reshape(n_steps, 1, n_win)
    return pl.pallas_call(
        functools.partial(_combine_kernel, tm=tm, chunk=256),
        grid=(n_steps,),
        in_specs=[pl.BlockSpec((1, 1, n_win), lambda i: (i, 0, 0), memory_space=pltpu.SMEM),
                  pl.BlockSpec((1, 1, n_win), lambda i: (jnp.minimum(i + 1, n_steps - 1), 0, 0),
                               memory_space=pltpu.SMEM),
                  pl.BlockSpec(memory_space=pl.ANY),
                  pl.BlockSpec((sub * tm, D), lambda i: (i, 0)),
                  pl.BlockSpec((sub * tm, LANES), lambda i: (i, 0)),
                  pl.BlockSpec((1, D), lambda i: (0, 0))],
        out_specs=pl.BlockSpec((sub * tm, D), lambda i: (i, 0)),
        out_shape=jax.ShapeDtypeStruct((T, D), F32),
        scratch_shapes=[pltpu.VMEM((2, n_win, WIN_ROWS, D), BF16), pltpu.SemaphoreType.DMA((2,))],
        compiler_params=pltpu.CompilerParams(dimension_semantics=("arbitrary",),
                                             vmem_limit_bytes=VMEM_LIMIT),
        name="combine",
    )(back3, back3, y.reshape(-1, WIN_ROWS, D), h, route, g.reshape(1, D))


def kernel(x, norm_mix_g, w_in, w_proj_sb, w_proj_moba, w_out, rel_bias, norm_ffn_g, w_router, b_router,
           w_gate_up, b_gate_up, w_down, b_down, norm_final_g):
    batch, seq, D = x.shape
    assert norm_mix_g.shape[0] == 1, "single-layer trunk"
    tm = 256
    xt = x.reshape(batch * seq, D)
    col_scale = np.ones((w_in.shape[-1],), np.float32)
    col_scale[0:D_MIX] = QUERY_SCALE
    col_scale[3 * D_MIX:4 * D_MIX] = QUERY_SCALE
    proj = _in_proj(xt, norm_mix_g[0], (w_in[0] * col_scale).astype(BF16))
    ysb = _sb_attention(proj, batch=batch, seq=seq)
    ymb = _moba_attention(proj, _moba_bias(rel_bias), rel_bias, batch=batch, seq=seq)
    wr_pad = jnp.pad(w_router[0], ((0, 0), (0, LANES - N_EXPERTS)))
    br_pad = jnp.pad(b_router[0], (0, LANES - N_EXPERTS)).reshape(1, LANES)
    h, xl, route, cnt = _post_attn(ysb, ymb, proj, xt, w_proj_sb[0].astype(BF16), w_proj_moba[0].astype(BF16),
                                   w_out[0].astype(BF16), norm_ffn_g[0], wr_pad, br_pad, tm=tm)
    src_win, back_win, tile_plan = _window_plan(cnt[:, 0, :N_EXPERTS].astype(jnp.int32), tm)
    y = _experts(xl, src_win, tile_plan, w_gate_up[0], b_gate_up[0], w_down[0], b_down[0])
    out = _combine(y, back_win, h, route, norm_final_g, tm=tm)
    return out.reshape(batch, seq, D)
```

```python
import functools
import math

import numpy as np
import jax
import jax.numpy as jnp
from jax import lax
from jax.experimental import pallas as pl
from jax.experimental.pallas import tpu as pltpu

HEAD_DIM = 64
N_HEADS = 8
D_MIX = N_HEADS * HEAD_DIM
MOBA_BLOCK = 256
MOBA_TOPK = 3
REL_BUCKETS = 32
REL_MAX_DIST = 128
N_EXPERTS = 32
TOP_K = 4
SWIGLU_LIMIT = 7.0
SWIGLU_ALPHA = 1.702
RMS_EPS = 1e-6

LANES = 128
NEG_BIG = -1e30
LOG2_E = math.log2(math.e)
QUERY_SCALE = LOG2_E / math.sqrt(HEAD_DIM)
VMEM_LIMIT = 56 * 1024 * 1024

F32 = jnp.float32
BF16 = jnp.bfloat16


def _split_bf16(a):
    hi = a.astype(BF16)
    lo = (a - hi.astype(F32)).astype(BF16)
    return hi, lo


def _dot_nt(a, b):
    return lax.dot_general(a, b, (((1,), (1,)), ((), ())), preferred_element_type=F32)


def _in_proj_kernel(x_ref, g_ref, w_ref, o_ref, *, n_slab, slab):
    x = x_ref[...]
    var = jnp.mean(x * x, axis=-1, keepdims=True)
    xn = ((x * lax.rsqrt(var + RMS_EPS)) * g_ref[...]).astype(BF16)
    for j in range(n_slab):
        o_ref[j] = jnp.dot(xn, w_ref[:, j * slab:(j + 1) * slab],
                           preferred_element_type=F32).astype(o_ref.dtype)


def _in_proj(xt, g, w_bf16, *, tm=1024, slab=D_MIX):
    T, D = xt.shape
    n_slab = w_bf16.shape[1] // slab
    return pl.pallas_call(
        functools.partial(_in_proj_kernel, n_slab=n_slab, slab=slab),
        grid=(T // tm,),
        in_specs=[pl.BlockSpec((tm, D), lambda i: (i, 0)),
                  pl.BlockSpec((1, D), lambda i: (0, 0)),
                  pl.BlockSpec(w_bf16.shape, lambda i: (0, 0))],
        out_specs=pl.BlockSpec((n_slab, tm, slab), lambda i: (0, i, 0)),
        out_shape=jax.ShapeDtypeStruct((n_slab, T, slab), BF16),
        compiler_params=pltpu.CompilerParams(dimension_semantics=("arbitrary",),
                                             vmem_limit_bytes=VMEM_LIMIT),
        name="in_proj",
    )(xt, g.reshape(1, D), w_bf16)


def _sb_kernel(q_ref, k_ref, v_ref, o_ref, *, tq, tk):
    i = pl.program_id(2)
    lane = lax.broadcasted_iota(jnp.int32, (1, LANES), 1)
    jj = lax.broadcasted_iota(jnp.int32, (tk, tk), 0)
    ss = lax.broadcasted_iota(jnp.int32, (tk, tk), 1)
    later = (jj >= ss).astype(BF16)
    n_sub = tq // tk
    streams = range(SB_HEADS_PER_STEP)

    def lanes_of(head):
        return slice((head // 2) * LANES, (head // 2 + 1) * LANES)

    qhs = []
    for head in streams:
        q = q_ref[0, :, lanes_of(head)]
        qhs.append(jnp.where(lane // HEAD_DIM == head % 2, q, jnp.zeros((), q.dtype)))

    def q_tile_keys(base, carry, diagonal):
        tiles = [(h, sub) for sub in reversed(range(n_sub)) for h in streams]
        starts = {sub: pl.multiple_of(base + sub * tk, tk) for sub in range(n_sub)}
        c = [list(carry[h][0]) for h in streams]
        acc = [carry[h][1] for h in streams]
        z2, drop, inner, pv = {}, {}, {}, {}

        def first_row(t):
            return t[1] * tk if diagonal else 0

        def causal(t):
            shape = (tq - first_row(t), tk)
            return lax.broadcasted_iota(jnp.int32, shape, 1) < lax.broadcasted_iota(jnp.int32, shape, 0)

        def scores(t):
            z2[t] = _dot_nt(qhs[t[0]][first_row(t):], k_ref[0, pl.ds(starts[t[1]], tk), lanes_of(t[0])])

        def drops(t):
            d = jnp.maximum(z2[t], 0.0) + jnp.log2(1.0 + jnp.exp2(-jnp.abs(z2[t])))
            if diagonal:
                d = jnp.where(causal(t), d, 0.0)
            drop[t] = d
            inner[t] = jnp.dot(d.astype(BF16), later, preferred_element_type=F32)

        def weights(t):
            h, b0 = t[0], first_row(t) // tk
            blocks = [slice((b - b0) * tk, (b - b0 + 1) * tk) for b in range(b0, n_sub)]
            w = [jnp.exp2((z2[t][rows] - c[h][b0 + n]) - inner[t][rows]) for n, rows in enumerate(blocks)]
            w = w[0] if len(w) == 1 else jnp.concatenate(w, axis=0)
            if diagonal:
                w = jnp.where(causal(t), w, 0.0)
            for n, rows in enumerate(blocks):
                c[h][b0 + n] = c[h][b0 + n] + jnp.sum(drop[t][rows], axis=-1, keepdims=True)
            pv[t] = jnp.dot(w.astype(BF16), v_ref[0, pl.ds(starts[t[1]], tk), lanes_of(h)],
                            preferred_element_type=F32)

        def accumulate(t):
            h, r0 = t[0], first_row(t)
            acc[h] = acc[h] + pv[t] if r0 == 0 else jnp.concatenate([acc[h][:r0], acc[h][r0:] + pv[t]], axis=0)

        stages = (scores, drops, weights, accumulate)
        for step in range(len(tiles) + len(stages) - 1):
            for k, stage in enumerate(stages):
                if 0 <= step - k < len(tiles):
                    stage(tiles[step - k])
        return tuple((tuple(c[h]), acc[h]) for h in streams)

    init = tuple((tuple(jnp.zeros((tk, 1), F32) for _ in range(n_sub)), jnp.zeros((tq, LANES), F32))
                 for _ in streams)
    carry = q_tile_keys(i * tq, init, True)
    carry = lax.fori_loop(0, i, lambda it, cr: q_tile_keys((i - 1 - it) * tq, cr, False), carry)
    for g in range(SB_HEADS_PER_STEP // 2):
        o_ref[:, g * LANES:(g + 1) * LANES] = jnp.where(lane < HEAD_DIM, carry[2 * g][1],
                                                        carry[2 * g + 1][1]).astype(o_ref.dtype)


SB_HEADS_PER_STEP = 8


def _sb_attention(proj, *, batch, seq, tq=512, tk=256):
    T = batch * seq
    nq = seq // tq
    width = SB_HEADS_PER_STEP * HEAD_DIM
    return pl.pallas_call(
        functools.partial(_sb_kernel, tq=tq, tk=tk),
        grid=(batch, D_MIX // width, nq),
        in_specs=[pl.BlockSpec((1, tq, width), lambda b, p, i: (0, b * nq + i, p)),
                  pl.BlockSpec((1, seq, width), lambda b, p, i: (1, b, p)),
                  pl.BlockSpec((1, seq, width), lambda b, p, i: (2, b, p))],
        out_specs=pl.BlockSpec((tq, width), lambda b, p, i: (b * nq + i, p)),
        out_shape=jax.ShapeDtypeStruct((T, D_MIX), BF16),
        compiler_params=pltpu.CompilerParams(
            dimension_semantics=("arbitrary", "arbitrary", "arbitrary"), vmem_limit_bytes=VMEM_LIMIT),
        name="sb_attn",
    )(proj, proj, proj)


def _rel_bucket_np(dist):
    n = np.maximum(dist, 0)
    max_exact = REL_BUCKETS // 2
    nf = np.maximum(n, 1).astype(np.float64)
    large = max_exact + (np.log(nf / max_exact) / math.log(REL_MAX_DIST / max_exact)
                         * (REL_BUCKETS - max_exact)).astype(np.int32)
    large = np.minimum(large, REL_BUCKETS - 1)
    return np.where(n < max_exact, n, large).astype(np.int32)


def _bias_kernel(rel_ref, bkt_ref, o_ref):
    h = pl.program_id(0)
    for d in range(2):
        bkt = bkt_ref[d]
        acc = jnp.zeros(bkt.shape, F32)
        for b in range(REL_BUCKETS):
            acc = jnp.where(bkt == b, rel_ref[b, h], acc)
        o_ref[0, d] = acc * LOG2_E


def _moba_bias(rel_bias):
    L = MOBA_BLOCK
    r = np.arange(L)[:, None]
    c = np.arange(L)[None, :]
    bkt = np.stack([_rel_bucket_np(r - c), _rel_bucket_np(L + r - c)])
    return pl.pallas_call(
        _bias_kernel,
        grid=(N_HEADS,),
        in_specs=[pl.BlockSpec(memory_space=pltpu.SMEM),
                  pl.BlockSpec((2, L, L), lambda h: (0, 0, 0))],
        out_specs=pl.BlockSpec((1, 2, L, L), lambda h: (h, 0, 0, 0)),
        out_shape=jax.ShapeDtypeStruct((N_HEADS, 2, L, L), F32),
        name="moba_bias",
    )(rel_bias, jnp.asarray(bkt))


def _moba_kernel(rel_ref, q_ref, k_ref, v_ref, bias_ref, o_ref, km_ref, *, L, nblk):
    p = pl.program_id(1)
    i = pl.program_id(2)
    lane = lax.broadcasted_iota(jnp.int32, (1, LANES), 1)
    row = lax.broadcasted_iota(jnp.int32, (L, L), 0)
    col = lax.broadcasted_iota(jnp.int32, (L, L), 1)

    @pl.when(i == 0)
    def _():
        km_ref[...] = jnp.mean(k_ref[0].astype(F32).reshape(nblk, L, k_ref.shape[-1]), axis=1)

    def attend(n_off):
        rr = lax.broadcasted_iota(jnp.int32, (n_off, 1), 0)
        km = jnp.concatenate([km_ref[pl.ds(jnp.maximum(i - r, 0), 1), :] for r in range(n_off)], axis=0)
        km = jnp.concatenate([km, jnp.zeros((16 - n_off, km.shape[-1]), F32)], axis=0)
        km_hi, km_lo = _split_bf16(km)
        km_lo2 = (km - km_hi.astype(F32) - km_lo.astype(F32)).astype(BF16)
        past = (rr >= 1) & (rr <= i)
        outs, all_scores, all_values = [], [], []
        for head in range(MOBA_HEADS_PER_STEP):
            h = head % 2
            group = slice((head // 2) * LANES, (head // 2 + 1) * LANES)
            q = q_ref[0, :, group]
            own = lane // HEAD_DIM == h
            aux0 = (1 - h) * HEAD_DIM
            qh = jnp.where(own, q, jnp.zeros((), q.dtype))
            gs = (_dot_nt(km_hi[:, group], qh) + _dot_nt(km_lo[:, group], qh)
                  + _dot_nt(km_lo2[:, group], qh))[:n_off]
            pen = jnp.zeros((n_off, L), F32)
            for r in range(1, n_off):
                g_r = gs[r:r + 1, :]
                beats = past & ((gs > g_r) | ((gs == g_r) & (rr > r)))
                cnt = jnp.sum(beats.astype(F32), axis=0, keepdims=True)
                keep = (cnt < MOBA_TOPK) & (r <= i)
                pen = jnp.where((rr == r) & jnp.logical_not(keep), NEG_BIG, pen)
            far = jnp.full((n_off, L), rel_ref[REL_BUCKETS - 1, MOBA_HEADS_PER_STEP * p + head] * LOG2_E, F32)
            f_hi, f_lo = _split_bf16(far)
            f_lo2 = far - f_hi.astype(F32) - f_lo.astype(F32)
            far3 = jnp.where(rr == 0, f_hi.astype(F32), jnp.where(rr == 1, f_lo.astype(F32),
                                                                  jnp.where(rr == 2, f_lo2, 0.0)))
            aux = [pen, far3, jnp.zeros((LANES - aux0 - 2 * n_off, L), F32)]
            aux = jnp.concatenate(([jnp.zeros((aux0, L), F32)] if aux0 else []) + aux, axis=0)
            q_aug = jnp.where(own, q, aux.T.astype(q.dtype))
            scores, values = [], []
            for r in range(n_off):
                ks = pl.multiple_of(jnp.maximum(i - r, 0) * L, L)
                kblk = k_ref[0, pl.ds(ks, L), group]
                values.append(v_ref[0, pl.ds(ks, L), group])
                if r == 0:
                    s = _dot_nt(qh, kblk) + bias_ref[head, 0]
                    s = jnp.where(col <= row, s, NEG_BIG)
                else:
                    hot = (lane == aux0 + r)
                    if r >= 2:
                        hot = hot | ((lane >= aux0 + n_off) & (lane < aux0 + n_off + 3))
                    k_aug = jnp.where(own, kblk, jnp.where(hot, 1.0, 0.0).astype(kblk.dtype))
                    s = _dot_nt(q_aug, k_aug)
                    if r == 1:
                        s = s + bias_ref[head, 1]
                scores.append(s)
            all_scores.append(scores)
            all_values.append(values)
        for scores, values in zip(all_scores, all_values):
            m = scores[0]
            for s in scores[1:]:
                m = jnp.maximum(m, s)
            m = jnp.max(m, axis=-1, keepdims=True)
            probs = [jnp.exp2(s - m) for s in scores]
            l = probs[0]
            for pr in probs[1:]:
                l = l + pr
            l = jnp.sum(l, axis=-1, keepdims=True)
            acc = jnp.dot(probs[0].astype(BF16), values[0], preferred_element_type=F32)
            for pr, vblk in zip(probs[1:], values[1:]):
                acc = acc + jnp.dot(pr.astype(BF16), vblk, preferred_element_type=F32)
            outs.append(acc / l)
        for g in range(MOBA_HEADS_PER_STEP // 2):
            o_ref[:, g * LANES:(g + 1) * LANES] = jnp.where(lane < HEAD_DIM, outs[2 * g],
                                                            outs[2 * g + 1]).astype(o_ref.dtype)

    for n_off in range(1, nblk + 1):
        pl.when(i == n_off - 1)(functools.partial(attend, n_off))


MOBA_HEADS_PER_STEP = 4


def _moba_attention(proj, bias, rel_bias, *, batch, seq):
    T = batch * seq
    L = MOBA_BLOCK
    nblk = seq // L
    assert seq % (2 * L) == 0, "query blocks are handled in pairs"
    width = MOBA_HEADS_PER_STEP * HEAD_DIM
    return pl.pallas_call(
        functools.partial(_moba_kernel, L=L, nblk=nblk),
        grid=(batch, D_MIX // width, nblk),
        in_specs=[pl.BlockSpec(memory_space=pltpu.SMEM),
                  pl.BlockSpec((1, L, width), lambda b, p, i: (3, b * nblk + i, p)),
                  pl.BlockSpec((1, seq, width), lambda b, p, i: (4, b, p)),
                  pl.BlockSpec((1, seq, width), lambda b, p, i: (5, b, p)),
                  pl.BlockSpec((MOBA_HEADS_PER_STEP, 2, L, L), lambda b, p, i: (p, 0, 0, 0))],
        out_specs=pl.BlockSpec((L, width), lambda b, p, i: (b * nblk + i, p)),
        out_shape=jax.ShapeDtypeStruct((T, D_MIX), BF16),
        scratch_shapes=[pltpu.VMEM((nblk, width), F32)],
        compiler_params=pltpu.CompilerParams(
            dimension_semantics=("arbitrary", "arbitrary", "arbitrary"), vmem_limit_bytes=VMEM_LIMIT),
        name="moba_attn",
    )(rel_bias, proj, proj, proj, bias)


WIN_ROWS = 16
WINS_PER_TILE = 32


def _local_slots(tm):
    return tm * TOP_K + N_EXPERTS * WIN_ROWS


def _post_attn_kernel(ysb_ref, ymb_ref, gsb_ref, gmb_ref, x_ref, wsb_ref, wmb_ref, wo_ref, nrm_ref,
                      wr_ref, br_ref, h_ref, xl_ref, route_ref, cnt_ref, *, tm, chunk):
    lane = lax.broadcasted_iota(jnp.int32, (1, LANES), 1).astype(F32)
    ls = _local_slots(tm)
    w_hi, w_lo = _split_bf16(wr_ref[...])
    w_both = jnp.concatenate([w_hi, w_lo], axis=1)
    hn_of, slots_of, tag_of = {}, {}, {}

    def dense(sub):
        rows = slice(sub * tm, (sub + 1) * tm)

        def gate(g_ref):
            g = jnp.concatenate([g_ref[0, rows], g_ref[1, rows]], axis=1).astype(F32)
            return 1.0 / (1.0 + jnp.exp(-g))

        merged = (gate(gsb_ref) * jnp.dot(ysb_ref[rows], wsb_ref[...], preferred_element_type=F32)
                  + gate(gmb_ref) * jnp.dot(ymb_ref[rows], wmb_ref[...], preferred_element_type=F32))
        hres = x_ref[rows] + jnp.dot(merged.astype(BF16), wo_ref[...], preferred_element_type=F32)
        h_ref[rows] = hres
        var = jnp.mean(hres * hres, axis=-1, keepdims=True)
        hn_of[sub] = (hres * lax.rsqrt(var + RMS_EPS)) * nrm_ref[...]

    def route(sub):
        hn = hn_of[sub]
        a_hi, a_lo = _split_bf16(hn)
        both = jnp.dot(a_hi, w_both, preferred_element_type=F32)
        logits = (both[:, :LANES] + both[:, LANES:] + jnp.dot(a_lo, w_hi, preferred_element_type=F32)) + br_ref[...]
        work = jnp.where(lane < N_EXPERTS, logits, NEG_BIG)
        vals, hots, ids = [], [], []
        for _ in range(TOP_K):
            mx = jnp.max(work, axis=-1, keepdims=True)
            idx = jnp.min(jnp.where(work == mx, lane, float(LANES)), axis=-1, keepdims=True)
            hot = lane == idx
            work = jnp.where(hot, NEG_BIG, work)
            vals.append(mx)
            hots.append(hot)
            ids.append(idx)
        exps = [jnp.exp(v - vals[0]) for v in vals]
        denom = exps[0] + exps[1] + exps[2] + exps[3]
        sel = jnp.zeros((tm, LANES), F32)
        for hot in hots:
            sel = jnp.where(hot, 1.0, sel)
        r = lax.broadcasted_iota(jnp.int32, (tm, tm), 0)
        c = lax.broadcasted_iota(jnp.int32, (tm, tm), 1)
        earlier = jnp.dot((c < r).astype(BF16), sel.astype(BF16), preferred_element_type=F32)
        cnt = jnp.sum(sel, axis=0, keepdims=True)
        padded = jnp.floor((cnt + (WIN_ROWS - 1)) * (1.0 / WIN_ROWS)) * WIN_ROWS
        er = lax.broadcasted_iota(jnp.int32, (LANES, LANES), 0)
        ec = lax.broadcasted_iota(jnp.int32, (LANES, LANES), 1)
        lower = jnp.dot(jnp.broadcast_to(padded, (8, LANES)).astype(BF16), (er < ec).astype(BF16),
                        preferred_element_type=F32)[0:1]
        slot = lower + earlier
        slab = jnp.zeros((tm, LANES), F32)
        tag = jnp.zeros((tm, LANES), F32)
        for k, hot in enumerate(hots):
            slab = jnp.where(lane == k, jnp.sum(jnp.where(hot, slot, 0.0), axis=-1, keepdims=True), slab)
            gate_w = exps[k] / denom
            g_hi = gate_w.astype(BF16).astype(F32)
            g_lo = (gate_w - g_hi).astype(BF16).astype(F32)
            for part, value in enumerate((ids[k], g_hi, g_lo, (gate_w - g_hi) - g_lo)):
                tag = jnp.where(lane == part * TOP_K + k, value, tag)
        route_ref[sub * tm:(sub + 1) * tm] = slab
        cnt_ref[sub] = jnp.broadcast_to(cnt, (8, LANES))
        slots_of[sub] = slab.T
        tag_of[sub] = tag.astype(BF16)

    def sort_rows(sub):
        slots_t = slots_of[sub]
        hn_bf = jnp.concatenate([hn_of[sub].astype(BF16), tag_of[sub]], axis=1)
        for ci in range(ls // chunk):
            rid = (lax.broadcasted_iota(jnp.int32, (chunk, 1), 0) + ci * chunk).astype(F32)
            onehot = rid == slots_t[0:1, :]
            for k in range(1, TOP_K):
                onehot = onehot | (rid == slots_t[k:k + 1, :])
            xl_ref[sub * ls + ci * chunk:sub * ls + (ci + 1) * chunk, :] = jnp.dot(
                jnp.where(onehot, 1.0, 0.0).astype(BF16), hn_bf, preferred_element_type=F32).astype(xl_ref.dtype)

    stages = (dense, route, sort_rows)
    for step in range(POST_TILES_PER_STEP + len(stages) - 1):
        for k, stage in enumerate(stages):
            if 0 <= step - k < POST_TILES_PER_STEP:
                stage(step - k)


POST_TILES_PER_STEP = 2


def _post_attn(ysb, ymb, proj, xt, wsb, wmb, wo, nrm_g, wr_pad, br_pad, *, tm):
    T, D = xt.shape
    half = D // 2
    n_tt = T // tm
    ls = _local_slots(tm)
    sub = POST_TILES_PER_STEP
    const = lambda i: (0, 0)
    return pl.pallas_call(
        functools.partial(_post_attn_kernel, tm=tm, chunk=256),
        grid=(n_tt // sub,),
        in_specs=[pl.BlockSpec((sub * tm, D_MIX), lambda i: (i, 0)),
                  pl.BlockSpec((sub * tm, D_MIX), lambda i: (i, 0)),
                  pl.BlockSpec((2, sub * tm, half), lambda i: (3, i, 0)),
                  pl.BlockSpec((2, sub * tm, half), lambda i: (4, i, 0)),
                  pl.BlockSpec((sub * tm, D), lambda i: (i, 0)),
                  pl.BlockSpec(wsb.shape, const),
                  pl.BlockSpec(wmb.shape, const),
                  pl.BlockSpec(wo.shape, const),
                  pl.BlockSpec((1, D), const),
                  pl.BlockSpec(wr_pad.shape, const),
                  pl.BlockSpec((1, LANES), const)],
        out_specs=[pl.BlockSpec((sub * tm, D), lambda i: (i, 0)),
                   pl.BlockSpec((sub * ls, D + LANES), lambda i: (i, 0)),
                   pl.BlockSpec((sub * tm, LANES), lambda i: (i, 0)),
                   pl.BlockSpec((sub, 8, LANES), lambda i: (i, 0, 0))],
        out_shape=[jax.ShapeDtypeStruct((T, D), F32),
                   jax.ShapeDtypeStruct((n_tt * ls, D + LANES), BF16),
                   jax.ShapeDtypeStruct((T, LANES), F32),
                   jax.ShapeDtypeStruct((n_tt, 8, LANES), F32)],
        compiler_params=pltpu.CompilerParams(dimension_semantics=("arbitrary",),
                                             vmem_limit_bytes=VMEM_LIMIT),
        name="post_attn",
    )(ysb, ymb, proj, proj, xt, wsb, wmb, wo, nrm_g.reshape(1, D), wr_pad, br_pad)


def _window_plan(counts, tm):
    n_tt, n_exp = counts.shape
    lw_tile = _local_slots(tm) // WIN_ROWS
    n_tiles = -(-(n_tt * (tm * TOP_K // WIN_ROWS + n_exp)) // WINS_PER_TILE) + n_exp
    nw = (counts + WIN_ROWS - 1) // WIN_ROWS
    local_start = jnp.cumsum(nw, axis=1) - nw
    per_expert = jnp.sum(nw, axis=0)
    per_expert_pad = -(-per_expert // WINS_PER_TILE) * WINS_PER_TILE
    expert_end = jnp.cumsum(per_expert_pad)
    sorted_start = (expert_end - per_expert_pad)[None, :] + jnp.cumsum(nw, axis=0) - nw
    run_start = sorted_start.T.reshape(-1)
    run_len = nw.T.reshape(-1)
    run_src = (jnp.arange(n_tt, dtype=jnp.int32)[:, None] * lw_tile + local_start).T.reshape(-1)
    g = jnp.arange(n_tiles * WINS_PER_TILE, dtype=jnp.int32)[:, None]
    off = g - run_start[None, :]
    src_win = jnp.sum(jnp.where((off >= 0) & (off < run_len[None, :]), run_src[None, :] + off, 0), axis=1)
    off_l = jnp.arange(lw_tile, dtype=jnp.int32)[None, :, None] - local_start[:, None, :]
    back_win = jnp.sum(jnp.where((off_l >= 0) & (off_l < nw[:, None, :]), sorted_start[:, None, :] + off_l, 0),
                       axis=2)
    tile_first = (jnp.arange(n_tiles, dtype=jnp.int32) * WINS_PER_TILE)[:, None]
    tile_valid = tile_first[:, 0] < expert_end[-1]
    used = (per_expert > 0)[None, :]
    expert_start = (expert_end - per_expert_pad)[None, :]
    ids = jnp.arange(n_exp, dtype=jnp.int32)[None, :]
    tile_expert = jnp.minimum(jnp.sum(expert_end[None, :] <= tile_first, axis=1), n_exp - 1)
    opens = jnp.sum(used & (expert_start == tile_first), axis=1) > 0
    buffer = (jnp.sum(used & (expert_start <= tile_first), axis=1) - 1) % 2
    next_expert = jnp.min(jnp.where(used & (expert_start > tile_first), ids, n_exp), axis=1)
    next_expert = jnp.where(next_expert == n_exp, -1, next_expert)
    tile_plan = tuple(a.astype(jnp.int32) for a in (tile_expert, tile_valid, opens, buffer, next_expert))
    return src_win.astype(jnp.int32), back_win.astype(jnp.int32), tile_plan


def _experts_kernel(te_ref, tv_ref, first_ref, par_ref, ne_ref, src_ref, nxt_ref, xl_ref, wgu_ref, bgu_ref, wd_ref,
                    bd_ref, y_ref, xbuf, wgu_f, wd_f, wgu_s, wd_s, sem, wsem, *, d_exp, d_model):
    j = pl.program_id(0)
    n_tiles = pl.num_programs(0)
    slot = j % 2

    def window_copy(win, dst_slot, w):
        return pltpu.make_async_copy(xl_ref.at[win], xbuf.at[dst_slot, w], sem.at[dst_slot])

    def start_gather(idx_ref, dst_slot):
        for w in range(WINS_PER_TILE):
            window_copy(idx_ref[0, 0, w], dst_slot, w).start()

    def weight_copies(expert, buf):
        return (pltpu.make_async_copy(wgu_ref.at[expert], wgu_f.at[buf], wsem.at[buf, 0]),
                pltpu.make_async_copy(wd_ref.at[expert], wd_f.at[buf], wsem.at[buf, 1]))

    @pl.when((j == 0) & (tv_ref[0] > 0))
    def _():
        for cp in weight_copies(te_ref[0], par_ref[0]):
            cp.start()
        start_gather(src_ref, 0)

    @pl.when((j + 1 < n_tiles) & (tv_ref[jnp.minimum(j + 1, n_tiles - 1)] > 0))
    def _():
        start_gather(nxt_ref, 1 - slot)

    @pl.when((first_ref[j] > 0) & (tv_ref[j] > 0))
    def _():
        buf = par_ref[j]

        @pl.when(ne_ref[j] >= 0)
        def _():
            for cp in weight_copies(ne_ref[j], 1 - buf):
                cp.start()

        for cp in weight_copies(te_ref[j], buf):
            cp.wait()
        wgu_s[...] = wgu_f[buf].astype(BF16)
        wd_s[...] = wd_f[buf].astype(BF16)

    @pl.when(tv_ref[j] > 0)
    def _():
        pltpu.make_async_copy(xl_ref.at[pl.ds(0, WINS_PER_TILE)], xbuf.at[slot], sem.at[slot]).wait()
        rows = xbuf[slot].reshape(WINS_PER_TILE * WIN_ROWS, xbuf.shape[-1])
        x, tag = rows[:, :d_model], rows[:, d_model:].astype(F32)
        scale = jnp.zeros((x.shape[0], 1), F32)
        for k in range(TOP_K):
            col = lambda part: tag[:, part * TOP_K + k:part * TOP_K + k + 1]
            scale = scale + jnp.where(col(0) == te_ref[j].astype(F32), col(1) + col(2) + col(3), 0.0)
        hgu = jnp.dot(x, wgu_s[...], preferred_element_type=F32) + bgu_ref[0]
        gate = jnp.minimum(hgu[:, :d_exp], SWIGLU_LIMIT)
        up = jnp.clip(hgu[:, d_exp:], -SWIGLU_LIMIT, SWIGLU_LIMIT)
        act = (up + 1.0) * (gate * (1.0 / (1.0 + jnp.exp(-SWIGLU_ALPHA * gate))))
        y = jnp.dot(act.astype(BF16), wd_s[...], preferred_element_type=F32) + bd_ref[0]
        y_ref[...] = (scale * y).astype(y_ref.dtype)

    @pl.when(tv_ref[j] == 0)
    def _():
        y_ref[...] = jnp.zeros_like(y_ref)


def _experts(xl, src_win, tile_plan, wgu, bgu, wd, bd):
    D = wd.shape[-1]
    d_row = xl.shape[1]
    E, _, d2 = wgu.shape
    d_exp = d2 // 2
    n_tiles = tile_plan[0].shape[0]
    tmm = WINS_PER_TILE * WIN_ROWS
    src3 = src_win.reshape(n_tiles, 1, WINS_PER_TILE)
    grid_spec = pltpu.PrefetchScalarGridSpec(
        num_scalar_prefetch=len(tile_plan),
        grid=(n_tiles,),
        in_specs=[pl.BlockSpec((1, 1, WINS_PER_TILE), lambda j, *_: (j, 0, 0), memory_space=pltpu.SMEM),
                  pl.BlockSpec((1, 1, WINS_PER_TILE), lambda j, *_: (jnp.minimum(j + 1, n_tiles - 1), 0, 0),
                               memory_space=pltpu.SMEM),
                  pl.BlockSpec(memory_space=pl.ANY),
                  pl.BlockSpec(memory_space=pl.ANY),
                  pl.BlockSpec((1, 1, d2), lambda j, te, *_: (te[j], 0, 0)),
                  pl.BlockSpec(memory_space=pl.ANY),
                  pl.BlockSpec((1, 1, D), lambda j, te, *_: (te[j], 0, 0))],
        out_specs=pl.BlockSpec((tmm, D), lambda j, *_: (j, 0)),
        scratch_shapes=[pltpu.VMEM((2, WINS_PER_TILE, WIN_ROWS, d_row), BF16),
                        pltpu.VMEM((2, D, d2), F32), pltpu.VMEM((2, d_exp, D), F32),
                        pltpu.VMEM((D, d2), BF16), pltpu.VMEM((d_exp, D), BF16),
                        pltpu.SemaphoreType.DMA((2,)), pltpu.SemaphoreType.DMA((2, 2))],
    )
    return pl.pallas_call(
        functools.partial(_experts_kernel, d_exp=d_exp, d_model=D),
        grid_spec=grid_spec,
        out_shape=jax.ShapeDtypeStruct((n_tiles * tmm, D), BF16),
        compiler_params=pltpu.CompilerParams(dimension_semantics=("arbitrary",),
                                             vmem_limit_bytes=VMEM_LIMIT),
        name="experts",
    )(*tile_plan, src3, src3, xl.reshape(-1, WIN_ROWS, d_row), wgu, bgu.reshape(E, 1, d2), wd, bd.reshape(E, 1, D))


def _combine_kernel(back_ref, nxt_ref, y_ref, h_ref, route_ref, g_ref, o_ref, ybuf, sem, *, tm, chunk):
    i = pl.program_id(0)
    slot = i % 2
    n_win = ybuf.shape[1]
    tile_win = n_win // COMBINE_TILES_PER_STEP

    def window_copy(win, dst_slot, w):
        return pltpu.make_async_copy(y_ref.at[win], ybuf.at[dst_slot, w], sem.at[dst_slot])

    def start_fetch(idx_ref, dst_slot):
        for w in range(n_win):
            window_copy(idx_ref[0, 0, w], dst_slot, w).start()

    @pl.when(i == 0)
    def _():
        start_fetch(back_ref, 0)

    @pl.when(i + 1 < pl.num_programs(0))
    def _():
        start_fetch(nxt_ref, 1 - slot)

    tiles = [slice(t * tm, (t + 1) * tm) for t in range(COMBINE_TILES_PER_STEP)]
    slots = [[route_ref[rows][:, k:k + 1] for k in range(TOP_K)] for rows in tiles]
    pltpu.make_async_copy(y_ref.at[pl.ds(0, n_win)], ybuf.at[slot], sem.at[slot]).wait()
    acc = [h_ref[rows] for rows in tiles]
    for ci in range(tile_win * WIN_ROWS // chunk):
        sid = (lax.broadcasted_iota(jnp.int32, (1, chunk), 1) + ci * chunk).astype(F32)
        for t in range(COMBINE_TILES_PER_STEP):
            picks = sid == slots[t][0]
            for k in range(1, TOP_K):
                picks = picks | (sid == slots[t][k])
            first = t * tile_win + ci * chunk // WIN_ROWS
            y_rows = ybuf[slot, first:first + chunk // WIN_ROWS].reshape(chunk, ybuf.shape[-1])
            acc[t] = acc[t] + jnp.dot(jnp.where(picks, 1.0, 0.0).astype(BF16), y_rows, preferred_element_type=F32)
    for t, rows in enumerate(tiles):
        var = jnp.mean(acc[t] * acc[t], axis=-1, keepdims=True)
        o_ref[rows] = (acc[t] * lax.rsqrt(var + RMS_EPS)) * g_ref[...]


COMBINE_TILES_PER_STEP = 2


def _combine(y, back_win, h, route, g, *, tm):
    T, D = h.shape
    sub = COMBINE_TILES_PER_STEP
    n_steps, n_win = back_win.shape[0] // sub, back_win.shape[1] * sub
    back3 = back_win.reshape(n_steps, 1, n_win)
    return pl.pallas_call(
        functools.partial(_combine_kernel, tm=tm, chunk=256),
        grid=(n_steps,),
        in_specs=[pl.BlockSpec((1, 1, n_win), lambda i: (i, 0, 0), memory_space=pltpu.SMEM),
                  pl.BlockSpec((1, 1, n_win), lambda i: (jnp.minimum(i + 1, n_steps - 1), 0, 0),
                               memory_space=pltpu.SMEM),
                  pl.BlockSpec(memory_space=pl.ANY),
                  pl.BlockSpec((sub * tm, D), lambda i: (i, 0)),
                  pl.BlockSpec((sub * tm, LANES), lambda i: (i, 0)),
                  pl.BlockSpec((1, D), lambda i: (0, 0))],
        out_specs=pl.BlockSpec((sub * tm, D), lambda i: (i, 0)),
        out_shape=jax.ShapeDtypeStruct((T, D), F32),
        scratch_shapes=[pltpu.VMEM((2, n_win, WIN_ROWS, D), BF16), pltpu.SemaphoreType.DMA((2,))],
        compiler_params=pltpu.CompilerParams(dimension_semantics=("arbitrary",),
                                             vmem_limit_bytes=VMEM_LIMIT),
        name="combine",
    )(back3, back3, y.reshape(-1, WIN_ROWS, D), h, route, g.reshape(1, D))


def kernel(x, norm_mix_g, w_in, w_proj_sb, w_proj_moba, w_out, rel_bias, norm_ffn_g, w_router, b_router,
           w_gate_up, b_gate_up, w_down, b_down, norm_final_g):
    batch, seq, D = x.shape
    assert norm_mix_g.shape[0] == 1, "single-layer trunk"
    tm = 256
    xt = x.reshape(batch * seq, D)
    col_scale = np.ones((w_in.shape[-1],), np.float32)
    col_scale[0:D_MIX] = QUERY_SCALE
    col_scale[3 * D_MIX:4 * D_MIX] = QUERY_SCALE
    proj = _in_proj(xt, norm_mix_g[0], (w_in[0] * col_scale).astype(BF16))
    ysb = _sb_attention(proj, batch=batch, seq=seq)
    ymb = _moba_attention(proj, _moba_bias(rel_bias), rel_bias, batch=batch, seq=seq)
    wr_pad = jnp.pad(w_router[0], ((0, 0), (0, LANES - N_EXPERTS)))
    br_pad = jnp.pad(b_router[0], (0, LANES - N_EXPERTS)).reshape(1, LANES)
    h, xl, route, cnt = _post_attn(ysb, ymb, proj, xt, w_proj_sb[0].astype(BF16), w_proj_moba[0].astype(BF16),
                                   w_out[0].astype(BF16), norm_ffn_g[0], wr_pad, br_pad, tm=tm)
    src_win, back_win, tile_plan = _window_plan(cnt[:, 0, :N_EXPERTS].astype(jnp.int32), tm)
    y = _experts(xl, src_win, tile_plan, w_gate_up[0], b_gate_up[0], w_down[0], b_down[0])
    out = _combine(y, back_win, h, route, norm_final_g, tm=tm)
    return out.reshape(batch, seq, D)
```

```python
import functools
import math

import numpy as np
import jax
import jax.numpy as jnp
from jax import lax
from jax.experimental import pallas as pl
from jax.experimental.pallas import tpu as pltpu

HEAD_DIM = 64
N_HEADS = 8
D_MIX = N_HEADS * HEAD_DIM
MOBA_BLOCK = 256
MOBA_TOPK = 3
REL_BUCKETS = 32
REL_MAX_DIST = 128
N_EXPERTS = 32
TOP_K = 4
SWIGLU_LIMIT = 7.0
SWIGLU_ALPHA = 1.702
RMS_EPS = 1e-6

LANES = 128
NEG_BIG = -1e30
LOG2_E = math.log2(math.e)
QUERY_SCALE = LOG2_E / math.sqrt(HEAD_DIM)
VMEM_LIMIT = 56 * 1024 * 1024

F32 = jnp.float32
BF16 = jnp.bfloat16


def _split_bf16(a):
    hi = a.astype(BF16)
    lo = (a - hi.astype(F32)).astype(BF16)
    return hi, lo


def _dot_nt(a, b):
    return lax.dot_general(a, b, (((1,), (1,)), ((), ())), preferred_element_type=F32)


def _in_proj_kernel(x_ref, g_ref, w_ref, o_ref, *, n_slab, slab):
    x = x_ref[...]
    var = jnp.mean(x * x, axis=-1, keepdims=True)
    xn = ((x * lax.rsqrt(var + RMS_EPS)) * g_ref[...]).astype(BF16)
    for j in range(n_slab):
        o_ref[j] = jnp.dot(xn, w_ref[:, j * slab:(j + 1) * slab],
                           preferred_element_type=F32).astype(o_ref.dtype)


def _in_proj(xt, g, w_bf16, *, tm=1024, slab=D_MIX):
    T, D = xt.shape
    n_slab = w_bf16.shape[1] // slab
    return pl.pallas_call(
        functools.partial(_in_proj_kernel, n_slab=n_slab, slab=slab),
        grid=(T // tm,),
        in_specs=[pl.BlockSpec((tm, D), lambda i: (i, 0)),
                  pl.BlockSpec((1, D), lambda i: (0, 0)),
                  pl.BlockSpec(w_bf16.shape, lambda i: (0, 0))],
        out_specs=pl.BlockSpec((n_slab, tm, slab), lambda i: (0, i, 0)),
        out_shape=jax.ShapeDtypeStruct((n_slab, T, slab), BF16),
        compiler_params=pltpu.CompilerParams(dimension_semantics=("arbitrary",),
                                             vmem_limit_bytes=VMEM_LIMIT),
        name="in_proj",
    )(xt, g.reshape(1, D), w_bf16)


def _sb_kernel(q_ref, k_ref, v_ref, o_ref, *, tq, tk):
    i = pl.program_id(2)
    lane = lax.broadcasted_iota(jnp.int32, (1, LANES), 1)
    jj = lax.broadcasted_iota(jnp.int32, (tk, tk), 0)
    ss = lax.broadcasted_iota(jnp.int32, (tk, tk), 1)
    later = (jj >= ss).astype(BF16)
    n_sub = tq // tk
    streams = range(SB_HEADS_PER_STEP)

    def lanes_of(head):
        return slice((head // 2) * LANES, (head // 2 + 1) * LANES)

    qhs = []
    for head in streams:
        q = q_ref[0, :, lanes_of(head)]
        qhs.append(jnp.where(lane // HEAD_DIM == head % 2, q, jnp.zeros((), q.dtype)))

    def q_tile_keys(base, carry, diagonal):
        tiles = [(h, sub) for sub in reversed(range(n_sub)) for h in streams]
        starts = {sub: pl.multiple_of(base + sub * tk, tk) for sub in range(n_sub)}
        c = [list(carry[h][0]) for h in streams]
        acc = [carry[h][1] for h in streams]
        z2, drop, inner, pv = {}, {}, {}, {}

        def first_row(t):
            return t[1] * tk if diagonal else 0

        def causal(t):
            shape = (tq - first_row(t), tk)
            return lax.broadcasted_iota(jnp.int32, shape, 1) < lax.broadcasted_iota(jnp.int32, shape, 0)

        def scores(t):
            z2[t] = _dot_nt(qhs[t[0]][first_row(t):], k_ref[0, pl.ds(starts[t[1]], tk), lanes_of(t[0])])

        def drops(t):
            d = jnp.maximum(z2[t], 0.0) + jnp.log2(1.0 + jnp.exp2(-jnp.abs(z2[t])))
            if diagonal:
                d = jnp.where(causal(t), d, 0.0)
            drop[t] = d
            inner[t] = jnp.dot(d.astype(BF16), later, preferred_element_type=F32)

        def weights(t):
            h, b0 = t[0], first_row(t) // tk
            blocks = [slice((b - b0) * tk, (b - b0 + 1) * tk) for b in range(b0, n_sub)]
            w = [jnp.exp2((z2[t][rows] - c[h][b0 + n]) - inner[t][rows]) for n, rows in enumerate(blocks)]
            w = w[0] if len(w) == 1 else jnp.concatenate(w, axis=0)
            if diagonal:
                w = jnp.where(causal(t), w, 0.0)
            for n, rows in enumerate(blocks):
                c[h][b0 + n] = c[h][b0 + n] + jnp.sum(drop[t][rows], axis=-1, keepdims=True)
            pv[t] = jnp.dot(w.astype(BF16), v_ref[0, pl.ds(starts[t[1]], tk), lanes_of(h)],
                            preferred_element_type=F32)

        def accumulate(t):
            h, r0 = t[0], first_row(t)
            acc[h] = acc[h] + pv[t] if r0 == 0 else jnp.concatenate([acc[h][:r0], acc[h][r0:] + pv[t]], axis=0)

        stages = (scores, drops, weights, accumulate)
        for step in range(len(tiles) + len(stages) - 1):
            for k, stage in enumerate(stages):
                if 0 <= step - k < len(tiles):
                    stage(tiles[step - k])
        return tuple((tuple(c[h]), acc[h]) for h in streams)

    init = tuple((tuple(jnp.zeros((tk, 1), F32) for _ in range(n_sub)), jnp.zeros((tq, LANES), F32))
                 for _ in streams)
    carry = q_tile_keys(i * tq, init, True)
    carry = lax.fori_loop(0, i, lambda it, cr: q_tile_keys((i - 1 - it) * tq, cr, False), carry)
    for g in range(SB_HEADS_PER_STEP // 2):
        o_ref[:, g * LANES:(g + 1) * LANES] = jnp.where(lane < HEAD_DIM, carry[2 * g][1],
                                                        carry[2 * g + 1][1]).astype(o_ref.dtype)


SB_HEADS_PER_STEP = 8


def _sb_attention(proj, *, batch, seq, tq=512, tk=256):
    T = batch * seq
    nq = seq // tq
    width = SB_HEADS_PER_STEP * HEAD_DIM
    return pl.pallas_call(
        functools.partial(_sb_kernel, tq=tq, tk=tk),
        grid=(batch, D_MIX // width, nq),
        in_specs=[pl.BlockSpec((1, tq, width), lambda b, p, i: (0, b * nq + i, p)),
                  pl.BlockSpec((1, seq, width), lambda b, p, i: (1, b, p)),
                  pl.BlockSpec((1, seq, width), lambda b, p, i: (2, b, p))],
        out_specs=pl.BlockSpec((tq, width), lambda b, p, i: (b * nq + i, p)),
        out_shape=jax.ShapeDtypeStruct((T, D_MIX), BF16),
        compiler_params=pltpu.CompilerParams(
            dimension_semantics=("arbitrary", "arbitrary", "arbitrary"), vmem_limit_bytes=VMEM_LIMIT),
        name="sb_attn",
    )(proj, proj, proj)


def _rel_bucket_np(dist):
    n = np.maximum(dist, 0)
    max_exact = REL_BUCKETS // 2
    nf = np.maximum(n, 1).astype(np.float64)
    large = max_exact + (np.log(nf / max_exact) / math.log(REL_MAX_DIST / max_exact)
                         * (REL_BUCKETS - max_exact)).astype(np.int32)
    large = np.minimum(large, REL_BUCKETS - 1)
    return np.where(n < max_exact, n, large).astype(np.int32)


def _bias_kernel(rel_ref, bkt_ref, o_ref):
    h = pl.program_id(0)
    for d in range(2):
        bkt = bkt_ref[d]
        acc = jnp.zeros(bkt.shape, F32)
        for b in range(REL_BUCKETS):
            acc = jnp.where(bkt == b, rel_ref[b, h], acc)
        o_ref[0, d] = acc * LOG2_E


def _moba_bias(rel_bias):
    L = MOBA_BLOCK
    r = np.arange(L)[:, None]
    c = np.arange(L)[None, :]
    bkt = np.stack([_rel_bucket_np(r - c), _rel_bucket_np(L + r - c)])
    return pl.pallas_call(
        _bias_kernel,
        grid=(N_HEADS,),
        in_specs=[pl.BlockSpec(memory_space=pltpu.SMEM),
                  pl.BlockSpec((2, L, L), lambda h: (0, 0, 0))],
        out_specs=pl.BlockSpec((1, 2, L, L), lambda h: (h, 0, 0, 0)),
        out_shape=jax.ShapeDtypeStruct((N_HEADS, 2, L, L), F32),
        name="moba_bias",
    )(rel_bias, jnp.asarray(bkt))


def _moba_kernel(rel_ref, q_ref, k_ref, v_ref, bias_ref, o_ref, km_ref, *, L, nblk):
    p = pl.program_id(1)
    i = pl.program_id(2)
    lane = lax.broadcasted_iota(jnp.int32, (1, LANES), 1)
    row = lax.broadcasted_iota(jnp.int32, (L, L), 0)
    col = lax.broadcasted_iota(jnp.int32, (L, L), 1)

    @pl.when(i == 0)
    def _():
        km_ref[...] = jnp.mean(k_ref[0].astype(F32).reshape(nblk, L, k_ref.shape[-1]), axis=1)

    def attend(n_off):
        rr = lax.broadcasted_iota(jnp.int32, (n_off, 1), 0)
        km = jnp.concatenate([km_ref[pl.ds(jnp.maximum(i - r, 0), 1), :] for r in range(n_off)], axis=0)
        km = jnp.concatenate([km, jnp.zeros((16 - n_off, km.shape[-1]), F32)], axis=0)
        km_hi, km_lo = _split_bf16(km)
        km_lo2 = (km - km_hi.astype(F32) - km_lo.astype(F32)).astype(BF16)
        past = (rr >= 1) & (rr <= i)
        outs, all_scores, all_values = [], [], []
        for head in range(MOBA_HEADS_PER_STEP):
            h = head % 2
            group = slice((head // 2) * LANES, (head // 2 + 1) * LANES)
            q = q_ref[0, :, group]
            own = lane // HEAD_DIM == h
            aux0 = (1 - h) * HEAD_DIM
            qh = jnp.where(own, q, jnp.zeros((), q.dtype))
            gs = (_dot_nt(km_hi[:, group], qh) + _dot_nt(km_lo[:, group], qh)
                  + _dot_nt(km_lo2[:, group], qh))[:n_off]
            pen = jnp.zeros((n_off, L), F32)
            for r in range(1, n_off):
                g_r = gs[r:r + 1, :]
                beats = past & ((gs > g_r) | ((gs == g_r) & (rr > r)))
                cnt = jnp.sum(beats.astype(F32), axis=0, keepdims=True)
                keep = (cnt < MOBA_TOPK) & (r <= i)
                pen = jnp.where((rr == r) & jnp.logical_not(keep), NEG_BIG, pen)
            far = jnp.full((n_off, L), rel_ref[REL_BUCKETS - 1, MOBA_HEADS_PER_STEP * p + head] * LOG2_E, F32)
            f_hi, f_lo = _split_bf16(far)
            f_lo2 = far - f_hi.astype(F32) - f_lo.astype(F32)
            far3 = jnp.where(rr == 0, f_hi.astype(F32), jnp.where(rr == 1, f_lo.astype(F32),
                                                                  jnp.where(rr == 2, f_lo2, 0.0)))
            aux = [pen, far3, jnp.zeros((LANES - aux0 - 2 * n_off, L), F32)]
            aux = jnp.concatenate(([jnp.zeros((aux0, L), F32)] if aux0 else []) + aux, axis=0)
            q_aug = jnp.where(own, q, aux.T.astype(q.dtype))
            scores, values = [], []
            for r in range(n_off):
                ks = pl.multiple_of(jnp.maximum(i - r, 0) * L, L)
                kblk = k_ref[0, pl.ds(ks, L), group]
                values.append(v_ref[0, pl.ds(ks, L), group])
                if r == 0:
                    s = _dot_nt(qh, kblk) + bias_ref[head, 0]
                    s = jnp.where(col <= row, s, NEG_BIG)
                else:
                    hot = (lane == aux0 + r)
                    if r >= 2:
                        hot = hot | ((lane >= aux0 + n_off) & (lane < aux0 + n_off + 3))
                    k_aug = jnp.where(own, kblk, jnp.where(hot, 1.0, 0.0).astype(kblk.dtype))
                    s = _dot_nt(q_aug, k_aug)
                    if r == 1:
                        s = s + bias_ref[head, 1]
                scores.append(s)
            all_scores.append(scores)
            all_values.append(values)
        for scores, values in zip(all_scores, all_values):
            m = scores[0]
            for s in scores[1:]:
                m = jnp.maximum(m, s)
            m = jnp.max(m, axis=-1, keepdims=True)
            probs = [jnp.exp2(s - m) for s in scores]
            l = probs[0]
            for pr in probs[1:]:
                l = l + pr
            l = jnp.sum(l, axis=-1, keepdims=True)
            acc = jnp.dot(probs[0].astype(BF16), values[0], preferred_element_type=F32)
            for pr, vblk in zip(probs[1:], values[1:]):
                acc = acc + jnp.dot(pr.astype(BF16), vblk, preferred_element_type=F32)
            outs.append(acc / l)
        for g in range(MOBA_HEADS_PER_STEP // 2):
            o_ref[:, g * LANES:(g + 1) * LANES] = jnp.where(lane < HEAD_DIM, outs[2 * g],
                                                            outs[2 * g + 1]).astype(o_ref.dtype)

    for n_off in range(1, nblk + 1):
        pl.when(i == n_off - 1)(functools.partial(attend, n_off))


MOBA_HEADS_PER_STEP = 4


def _moba_attention(proj, bias, rel_bias, *, batch, seq):
    T = batch * seq
    L = MOBA_BLOCK
    nblk = seq // L
    assert seq % (2 * L) == 0, "query blocks are handled in pairs"
    width = MOBA_HEADS_PER_STEP * HEAD_DIM
    return pl.pallas_call(
        functools.partial(_moba_kernel, L=L, nblk=nblk),
        grid=(batch, D_MIX // width, nblk),
        in_specs=[pl.BlockSpec(memory_space=pltpu.SMEM),
                  pl.BlockSpec((1, L, width), lambda b, p, i: (3, b * nblk + i, p)),
                  pl.BlockSpec((1, seq, width), lambda b, p, i: (4, b, p)),
                  pl.BlockSpec((1, seq, width), lambda b, p, i: (5, b, p)),
                  pl.BlockSpec((MOBA_HEADS_PER_STEP, 2, L, L), lambda b, p, i: (p, 0, 0, 0))],
        out_specs=pl.BlockSpec((L, width), lambda b, p, i: (b * nblk + i, p)),
        out_shape=jax.ShapeDtypeStruct((T, D_MIX), BF16),
        scratch_shapes=[pltpu.VMEM((nblk, width), F32)],
        compiler_params=pltpu.CompilerParams(
            dimension_semantics=("arbitrary", "arbitrary", "arbitrary"), vmem_limit_bytes=VMEM_LIMIT),
        name="moba_attn",
    )(rel_bias, proj, proj, proj, bias)


WIN_ROWS = 16
WINS_PER_TILE = 32


def _local_slots(tm):
    return tm * TOP_K + N_EXPERTS * WIN_ROWS


def _post_attn_kernel(ysb_ref, ymb_ref, gsb_ref, gmb_ref, x_ref, wsb_ref, wmb_ref, wo_ref, nrm_ref,
                      wr_ref, br_ref, h_ref, xl_ref, route_ref, cnt_ref, *, tm, chunk):
    lane = lax.broadcasted_iota(jnp.int32, (1, LANES), 1).astype(F32)
    ls = _local_slots(tm)
    w_hi, w_lo = _split_bf16(wr_ref[...])
    w_both = jnp.concatenate([w_hi, w_lo], axis=1)
    hn_of, slots_of = {}, {}

    def dense(sub):
        rows = slice(sub * tm, (sub + 1) * tm)

        def gate(g_ref):
            g = jnp.concatenate([g_ref[0, rows], g_ref[1, rows]], axis=1).astype(F32)
            return 1.0 / (1.0 + jnp.exp(-g))

        merged = (gate(gsb_ref) * jnp.dot(ysb_ref[rows], wsb_ref[...], preferred_element_type=F32)
                  + gate(gmb_ref) * jnp.dot(ymb_ref[rows], wmb_ref[...], preferred_element_type=F32))
        hres = x_ref[rows] + jnp.dot(merged.astype(BF16), wo_ref[...], preferred_element_type=F32)
        h_ref[rows] = hres
        var = jnp.mean(hres * hres, axis=-1, keepdims=True)
        hn_of[sub] = (hres * lax.rsqrt(var + RMS_EPS)) * nrm_ref[...]

    def route(sub):
        hn = hn_of[sub]
        a_hi, a_lo = _split_bf16(hn)
        both = jnp.dot(a_hi, w_both, preferred_element_type=F32)
        logits = (both[:, :LANES] + both[:, LANES:] + jnp.dot(a_lo, w_hi, preferred_element_type=F32)) + br_ref[...]
        work = jnp.where(lane < N_EXPERTS, logits, NEG_BIG)
        vals, hots = [], []
        for _ in range(TOP_K):
            mx = jnp.max(work, axis=-1, keepdims=True)
            idx = jnp.min(jnp.where(work == mx, lane, float(LANES)), axis=-1, keepdims=True)
            hot = lane == idx
            work = jnp.where(hot, NEG_BIG, work)
            vals.append(mx)
            hots.append(hot)
        exps = [jnp.exp(v - vals[0]) for v in vals]
        denom = exps[0] + exps[1] + exps[2] + exps[3]
        sel = jnp.zeros((tm, LANES), F32)
        for hot in hots:
            sel = jnp.where(hot, 1.0, sel)
        r = lax.broadcasted_iota(jnp.int32, (tm, tm), 0)
        c = lax.broadcasted_iota(jnp.int32, (tm, tm), 1)
        earlier = jnp.dot((c < r).astype(BF16), sel.astype(BF16), preferred_element_type=F32)
        cnt = jnp.sum(sel, axis=0, keepdims=True)
        padded = jnp.floor((cnt + (WIN_ROWS - 1)) * (1.0 / WIN_ROWS)) * WIN_ROWS
        er = lax.broadcasted_iota(jnp.int32, (LANES, LANES), 0)
        ec = lax.broadcasted_iota(jnp.int32, (LANES, LANES), 1)
        lower = jnp.dot(jnp.broadcast_to(padded, (8, LANES)).astype(BF16), (er < ec).astype(BF16),
                        preferred_element_type=F32)[0:1]
        slot = lower + earlier
        slab = jnp.zeros((tm, LANES), F32)
        for k, hot in enumerate(hots):
            slab = jnp.where(lane == k, exps[k] / denom, slab)
            slab = jnp.where(lane == TOP_K + k, jnp.sum(jnp.where(hot, slot, 0.0), axis=-1, keepdims=True), slab)
        route_ref[sub * tm:(sub + 1) * tm] = slab
        cnt_ref[sub] = jnp.broadcast_to(cnt, (8, LANES))
        slots_of[sub] = slab.T

    def sort_rows(sub):
        slots_t = slots_of[sub]
        hn_bf = hn_of[sub].astype(BF16)
        for ci in range(ls // chunk):
            rid = (lax.broadcasted_iota(jnp.int32, (chunk, 1), 0) + ci * chunk).astype(F32)
            onehot = rid == slots_t[TOP_K:TOP_K + 1, :]
            for k in range(1, TOP_K):
                onehot = onehot | (rid == slots_t[TOP_K + k:TOP_K + k + 1, :])
            xl_ref[sub * ls + ci * chunk:sub * ls + (ci + 1) * chunk, :] = jnp.dot(
                jnp.where(onehot, 1.0, 0.0).astype(BF16), hn_bf, preferred_element_type=F32).astype(xl_ref.dtype)

    stages = (dense, route, sort_rows)
    for step in range(POST_TILES_PER_STEP + len(stages) - 1):
        for k, stage in enumerate(stages):
            if 0 <= step - k < POST_TILES_PER_STEP:
                stage(step - k)


POST_TILES_PER_STEP = 2


def _post_attn(ysb, ymb, proj, xt, wsb, wmb, wo, nrm_g, wr_pad, br_pad, *, tm):
    T, D = xt.shape
    half = D // 2
    n_tt = T // tm
    ls = _local_slots(tm)
    sub = POST_TILES_PER_STEP
    const = lambda i: (0, 0)
    return pl.pallas_call(
        functools.partial(_post_attn_kernel, tm=tm, chunk=256),
        grid=(n_tt // sub,),
        in_specs=[pl.BlockSpec((sub * tm, D_MIX), lambda i: (i, 0)),
                  pl.BlockSpec((sub * tm, D_MIX), lambda i: (i, 0)),
                  pl.BlockSpec((2, sub * tm, half), lambda i: (3, i, 0)),
                  pl.BlockSpec((2, sub * tm, half), lambda i: (4, i, 0)),
                  pl.BlockSpec((sub * tm, D), lambda i: (i, 0)),
                  pl.BlockSpec(wsb.shape, const),
                  pl.BlockSpec(wmb.shape, const),
                  pl.BlockSpec(wo.shape, const),
                  pl.BlockSpec((1, D), const),
                  pl.BlockSpec(wr_pad.shape, const),
                  pl.BlockSpec((1, LANES), const)],
        out_specs=[pl.BlockSpec((sub * tm, D), lambda i: (i, 0)),
                   pl.BlockSpec((sub * ls, D), lambda i: (i, 0)),
                   pl.BlockSpec((sub * tm, LANES), lambda i: (i, 0)),
                   pl.BlockSpec((sub, 8, LANES), lambda i: (i, 0, 0))],
        out_shape=[jax.ShapeDtypeStruct((T, D), F32),
                   jax.ShapeDtypeStruct((n_tt * ls, D), BF16),
                   jax.ShapeDtypeStruct((T, LANES), F32),
                   jax.ShapeDtypeStruct((n_tt, 8, LANES), F32)],
        compiler_params=pltpu.CompilerParams(dimension_semantics=("arbitrary",),
                                             vmem_limit_bytes=VMEM_LIMIT),
        name="post_attn",
    )(ysb, ymb, proj, proj, xt, wsb, wmb, wo, nrm_g.reshape(1, D), wr_pad, br_pad)


def _window_plan(counts, tm):
    n_tt, n_exp = counts.shape
    lw_tile = _local_slots(tm) // WIN_ROWS
    n_tiles = -(-(n_tt * (tm * TOP_K // WIN_ROWS + n_exp)) // WINS_PER_TILE) + n_exp
    nw = (counts + WIN_ROWS - 1) // WIN_ROWS
    local_start = jnp.cumsum(nw, axis=1) - nw
    per_expert = jnp.sum(nw, axis=0)
    per_expert_pad = -(-per_expert // WINS_PER_TILE) * WINS_PER_TILE
    expert_end = jnp.cumsum(per_expert_pad)
    sorted_start = (expert_end - per_expert_pad)[None, :] + jnp.cumsum(nw, axis=0) - nw
    run_start = sorted_start.T.reshape(-1)
    run_len = nw.T.reshape(-1)
    run_src = (jnp.arange(n_tt, dtype=jnp.int32)[:, None] * lw_tile + local_start).T.reshape(-1)
    g = jnp.arange(n_tiles * WINS_PER_TILE, dtype=jnp.int32)[:, None]
    off = g - run_start[None, :]
    src_win = jnp.sum(jnp.where((off >= 0) & (off < run_len[None, :]), run_src[None, :] + off, 0), axis=1)
    off_l = jnp.arange(lw_tile, dtype=jnp.int32)[None, :, None] - local_start[:, None, :]
    back_win = jnp.sum(jnp.where((off_l >= 0) & (off_l < nw[:, None, :]), sorted_start[:, None, :] + off_l, 0),
                       axis=2)
    tile_first = (jnp.arange(n_tiles, dtype=jnp.int32) * WINS_PER_TILE)[:, None]
    tile_valid = tile_first[:, 0] < expert_end[-1]
    used = (per_expert > 0)[None, :]
    expert_start = (expert_end - per_expert_pad)[None, :]
    ids = jnp.arange(n_exp, dtype=jnp.int32)[None, :]
    tile_expert = jnp.minimum(jnp.sum(expert_end[None, :] <= tile_first, axis=1), n_exp - 1)
    opens = jnp.sum(used & (expert_start == tile_first), axis=1) > 0
    buffer = (jnp.sum(used & (expert_start <= tile_first), axis=1) - 1) % 2
    next_expert = jnp.min(jnp.where(used & (expert_start > tile_first), ids, n_exp), axis=1)
    next_expert = jnp.where(next_expert == n_exp, -1, next_expert)
    tile_plan = tuple(a.astype(jnp.int32) for a in (tile_expert, tile_valid, opens, buffer, next_expert))
    return src_win.astype(jnp.int32), back_win.astype(jnp.int32), tile_plan


def _experts_kernel(te_ref, tv_ref, first_ref, par_ref, ne_ref, src_ref, nxt_ref, xl_ref, wgu_ref, bgu_ref, wd_ref,
                    bd_ref, y_ref, xbuf, wgu_f, wd_f, wgu_s, wd_s, sem, wsem, *, d_exp):
    j = pl.program_id(0)
    n_tiles = pl.num_programs(0)
    slot = j % 2

    def window_copy(win, dst_slot, w):
        return pltpu.make_async_copy(xl_ref.at[win], xbuf.at[dst_slot, w], sem.at[dst_slot])

    def start_gather(idx_ref, dst_slot):
        for w in range(WINS_PER_TILE):
            window_copy(idx_ref[0, 0, w], dst_slot, w).start()

    def weight_copies(expert, buf):
        return (pltpu.make_async_copy(wgu_ref.at[expert], wgu_f.at[buf], wsem.at[buf, 0]),
                pltpu.make_async_copy(wd_ref.at[expert], wd_f.at[buf], wsem.at[buf, 1]))

    @pl.when((j == 0) & (tv_ref[0] > 0))
    def _():
        for cp in weight_copies(te_ref[0], par_ref[0]):
            cp.start()
        start_gather(src_ref, 0)

    @pl.when((j + 1 < n_tiles) & (tv_ref[jnp.minimum(j + 1, n_tiles - 1)] > 0))
    def _():
        start_gather(nxt_ref, 1 - slot)

    @pl.when((first_ref[j] > 0) & (tv_ref[j] > 0))
    def _():
        buf = par_ref[j]

        @pl.when(ne_ref[j] >= 0)
        def _():
            for cp in weight_copies(ne_ref[j], 1 - buf):
                cp.start()

        for cp in weight_copies(te_ref[j], buf):
            cp.wait()
        wgu_s[...] = wgu_f[buf].astype(BF16)
        wd_s[...] = wd_f[buf].astype(BF16)

    @pl.when(tv_ref[j] > 0)
    def _():
        pltpu.make_async_copy(xl_ref.at[pl.ds(0, WINS_PER_TILE)], xbuf.at[slot], sem.at[slot]).wait()
        x = xbuf[slot].reshape(WINS_PER_TILE * WIN_ROWS, xbuf.shape[-1])
        hgu = jnp.dot(x, wgu_s[...], preferred_element_type=F32) + bgu_ref[0]
        gate = jnp.minimum(hgu[:, :d_exp], SWIGLU_LIMIT)
        up = jnp.clip(hgu[:, d_exp:], -SWIGLU_LIMIT, SWIGLU_LIMIT)
        act = (up + 1.0) * (gate * (1.0 / (1.0 + jnp.exp(-SWIGLU_ALPHA * gate))))
        y = jnp.dot(act.astype(BF16), wd_s[...], preferred_element_type=F32) + bd_ref[0]
        y_ref[...] = y.astype(y_ref.dtype)

    @pl.when(tv_ref[j] == 0)
    def _():
        y_ref[...] = jnp.zeros_like(y_ref)


def _experts(xl, src_win, tile_plan, wgu, bgu, wd, bd):
    D = xl.shape[1]
    E, _, d2 = wgu.shape
    d_exp = d2 // 2
    n_tiles = tile_plan[0].shape[0]
    tmm = WINS_PER_TILE * WIN_ROWS
    src3 = src_win.reshape(n_tiles, 1, WINS_PER_TILE)
    grid_spec = pltpu.PrefetchScalarGridSpec(
        num_scalar_prefetch=len(tile_plan),
        grid=(n_tiles,),
        in_specs=[pl.BlockSpec((1, 1, WINS_PER_TILE), lambda j, *_: (j, 0, 0), memory_space=pltpu.SMEM),
                  pl.BlockSpec((1, 1, WINS_PER_TILE), lambda j, *_: (jnp.minimum(j + 1, n_tiles - 1), 0, 0),
                               memory_space=pltpu.SMEM),
                  pl.BlockSpec(memory_space=pl.ANY),
                  pl.BlockSpec(memory_space=pl.ANY),
                  pl.BlockSpec((1, 1, d2), lambda j, te, *_: (te[j], 0, 0)),
                  pl.BlockSpec(memory_space=pl.ANY),
                  pl.BlockSpec((1, 1, D), lambda j, te, *_: (te[j], 0, 0))],
        out_specs=pl.BlockSpec((tmm, D), lambda j, *_: (j, 0)),
        scratch_shapes=[pltpu.VMEM((2, WINS_PER_TILE, WIN_ROWS, D), BF16),
                        pltpu.VMEM((2, D, d2), F32), pltpu.VMEM((2, d_exp, D), F32),
                        pltpu.VMEM((D, d2), BF16), pltpu.VMEM((d_exp, D), BF16),
                        pltpu.SemaphoreType.DMA((2,)), pltpu.SemaphoreType.DMA((2, 2))],
    )
    return pl.pallas_call(
        functools.partial(_experts_kernel, d_exp=d_exp),
        grid_spec=grid_spec,
        out_shape=jax.ShapeDtypeStruct((n_tiles * tmm, D), BF16),
        compiler_params=pltpu.CompilerParams(dimension_semantics=("arbitrary",),
                                             vmem_limit_bytes=VMEM_LIMIT),
        name="experts",
    )(*tile_plan, src3, src3, xl.reshape(-1, WIN_ROWS, D), wgu, bgu.reshape(E, 1, d2), wd, bd.reshape(E, 1, D))


def _combine_kernel(back_ref, nxt_ref, y_ref, h_ref, route_ref, g_ref, o_ref, ybuf, sem, *, tm, chunk):
    i = pl.program_id(0)
    slot = i % 2
    n_win = ybuf.shape[1]
    tile_win = n_win // COMBINE_TILES_PER_STEP

    def window_copy(win, dst_slot, w):
        return pltpu.make_async_copy(y_ref.at[win], ybuf.at[dst_slot, w], sem.at[dst_slot])

    def start_fetch(idx_ref, dst_slot):
        for w in range(n_win):
            window_copy(idx_ref[0, 0, w], dst_slot, w).start()

    @pl.when(i == 0)
    def _():
        start_fetch(back_ref, 0)

    @pl.when(i + 1 < pl.num_programs(0))
    def _():
        start_fetch(nxt_ref, 1 - slot)

    tiles = [slice(t * tm, (t + 1) * tm) for t in range(COMBINE_TILES_PER_STEP)]
    routes = [route_ref[rows] for rows in tiles]
    weights = [[route[:, k:k + 1] for k in range(TOP_K)] for route in routes]
    slots = [[route[:, TOP_K + k:TOP_K + k + 1] for k in range(TOP_K)] for route in routes]
    pltpu.make_async_copy(y_ref.at[pl.ds(0, n_win)], ybuf.at[slot], sem.at[slot]).wait()
    acc = [h_ref[rows] for rows in tiles]
    for ci in range(tile_win * WIN_ROWS // chunk):
        sid = (lax.broadcasted_iota(jnp.int32, (1, chunk), 1) + ci * chunk).astype(F32)
        for t in range(COMBINE_TILES_PER_STEP):
            wmat = jnp.zeros((tm, chunk), F32)
            for k in range(TOP_K):
                wmat = jnp.where(sid == slots[t][k], weights[t][k], wmat)
            hi, lo = _split_bf16(wmat)
            first = t * tile_win + ci * chunk // WIN_ROWS
            y_rows = ybuf[slot, first:first + chunk // WIN_ROWS].reshape(chunk, ybuf.shape[-1])
            acc[t] = (acc[t] + jnp.dot(hi, y_rows, preferred_element_type=F32)
                      + jnp.dot(lo, y_rows, preferred_element_type=F32))
    for t, rows in enumerate(tiles):
        var = jnp.mean(acc[t] * acc[t], axis=-1, keepdims=True)
        o_ref[rows] = (acc[t] * lax.rsqrt(var + RMS_EPS)) * g_ref[...]


COMBINE_TILES_PER_STEP = 2


def _combine(y, back_win, h, route, g, *, tm):
    T, D = h.shape
    sub = COMBINE_TILES_PER_STEP
    n_steps, n_win = back_win.shape[0] // sub, back_win.shape[1] * sub
    back3 = back_win.reshape(n_steps, 1, n_win)
    return pl.pallas_call(
        functools.partial(_combine_kernel, tm=tm, chunk=256),
        grid=(n_steps,),
        in_specs=[pl.BlockSpec((1, 1, n_win), lambda i: (i, 0, 0), memory_space=pltpu.SMEM),
                  pl.BlockSpec((1, 1, n_win), lambda i: (jnp.minimum(i + 1, n_steps - 1), 0, 0),
                               memory_space=pltpu.SMEM),
                  pl.BlockSpec(memory_space=pl.ANY),
                  pl.BlockSpec((sub * tm, D), lambda i: (i, 0)),
                  pl.BlockSpec((sub * tm, LANES), lambda i: (i, 0)),
                  pl.BlockSpec((1, D), lambda i: (0, 0))],
        out_specs=pl.BlockSpec((sub * tm, D), lambda i: (i, 0)),
        out_shape=jax.ShapeDtypeStruct((T, D), F32),
        scratch_shapes=[pltpu.VMEM((2, n_win, WIN_ROWS, D), BF16), pltpu.SemaphoreType.DMA((2,))],
        compiler_params=pltpu.CompilerParams(dimension_semantics=("arbitrary",),
                                             vmem_limit_bytes=VMEM_LIMIT),
        name="combine",
    )(back3, back3, y.reshape(-1, WIN_ROWS, D), h, route, g.reshape(1, D))


def kernel(x, norm_mix_g, w_in, w_proj_sb, w_proj_moba, w_out, rel_bias, norm_ffn_g, w_router, b_router,
           w_gate_up, b_gate_up, w_down, b_down, norm_final_g):
    batch, seq, D = x.shape
    assert norm_mix_g.shape[0] == 1, "single-layer trunk"
    tm = 256
    xt = x.reshape(batch * seq, D)
    col_scale = np.ones((w_in.shape[-1],), np.float32)
    col_scale[0:D_MIX] = QUERY_SCALE
    col_scale[3 * D_MIX:4 * D_MIX] = QUERY_SCALE
    proj = _in_proj(xt, norm_mix_g[0], (w_in[0] * col_scale).astype(BF16))
    ysb = _sb_attention(proj, batch=batch, seq=seq)
    ymb = _moba_attention(proj, _moba_bias(rel_bias), rel_bias, batch=batch, seq=seq)
    wr_pad = jnp.pad(w_router[0], ((0, 0), (0, LANES - N_EXPERTS)))
    br_pad = jnp.pad(b_router[0], (0, LANES - N_EXPERTS)).reshape(1, LANES)
    h, xl, route, cnt = _post_attn(ysb, ymb, proj, xt, w_proj_sb[0].astype(BF16), w_proj_moba[0].astype(BF16),
                                   w_out[0].astype(BF16), norm_ffn_g[0], wr_pad, br_pad, tm=tm)
    src_win, back_win, tile_plan = _window_plan(cnt[:, 0, :N_EXPERTS].astype(jnp.int32), tm)
    y = _experts(xl, src_win, tile_plan, w_gate_up[0], b_gate_up[0], w_down[0], b_down[0])
    out = _combine(y, back_win, h, route, norm_final_g, tm=tm)
    return out.reshape(batch, seq, D)
```

```python
import functools
import math

import numpy as np
import jax
import jax.numpy as jnp
from jax import lax
from jax.experimental import pallas as pl
from jax.experimental.pallas import tpu as pltpu

HEAD_DIM = 64
N_HEADS = 8
D_MIX = N_HEADS * HEAD_DIM
MOBA_BLOCK = 256
MOBA_TOPK = 3
REL_BUCKETS = 32
REL_MAX_DIST = 128
N_EXPERTS = 32
TOP_K = 4
SWIGLU_LIMIT = 7.0
SWIGLU_ALPHA = 1.702
RMS_EPS = 1e-6

LANES = 128
NEG_BIG = -1e30
LOG2_E = math.log2(math.e)
QUERY_SCALE = LOG2_E / math.sqrt(HEAD_DIM)
VMEM_LIMIT = 56 * 1024 * 1024

F32 = jnp.float32
BF16 = jnp.bfloat16


def _split_bf16(a):
    hi = a.astype(BF16)
    lo = (a - hi.astype(F32)).astype(BF16)
    return hi, lo


def _dot_nt(a, b):
    return lax.dot_general(a, b, (((1,), (1,)), ((), ())), preferred_element_type=F32)


def _in_proj_kernel(x_ref, g_ref, w_ref, o_ref, *, n_slab, slab):
    x = x_ref[...]
    var = jnp.mean(x * x, axis=-1, keepdims=True)
    xn = ((x * lax.rsqrt(var + RMS_EPS)) * g_ref[...]).astype(BF16)
    for j in range(n_slab):
        o_ref[j] = jnp.dot(xn, w_ref[:, j * slab:(j + 1) * slab],
                           preferred_element_type=F32).astype(o_ref.dtype)


def _in_proj(xt, g, w_bf16, *, tm=1024, slab=D_MIX):
    T, D = xt.shape
    n_slab = w_bf16.shape[1] // slab
    return pl.pallas_call(
        functools.partial(_in_proj_kernel, n_slab=n_slab, slab=slab),
        grid=(T // tm,),
        in_specs=[pl.BlockSpec((tm, D), lambda i: (i, 0)),
                  pl.BlockSpec((1, D), lambda i: (0, 0)),
                  pl.BlockSpec(w_bf16.shape, lambda i: (0, 0))],
        out_specs=pl.BlockSpec((n_slab, tm, slab), lambda i: (0, i, 0)),
        out_shape=jax.ShapeDtypeStruct((n_slab, T, slab), BF16),
        compiler_params=pltpu.CompilerParams(dimension_semantics=("arbitrary",),
                                             vmem_limit_bytes=VMEM_LIMIT),
        name="in_proj",
    )(xt, g.reshape(1, D), w_bf16)


def _sb_kernel(q_ref, k_ref, v_ref, o_ref, *, tq, tk):
    i = pl.program_id(2)
    lane = lax.broadcasted_iota(jnp.int32, (1, LANES), 1)
    jj = lax.broadcasted_iota(jnp.int32, (tk, tk), 0)
    ss = lax.broadcasted_iota(jnp.int32, (tk, tk), 1)
    later = (jj >= ss).astype(BF16)
    n_sub = tq // tk
    streams = range(SB_HEADS_PER_STEP)

    def lanes_of(head):
        return slice((head // 2) * LANES, (head // 2 + 1) * LANES)

    qhs = []
    for head in streams:
        q = q_ref[0, :, lanes_of(head)]
        qhs.append(jnp.where(lane // HEAD_DIM == head % 2, q, jnp.zeros((), q.dtype)))

    def q_tile_keys(base, carry, diagonal):
        tiles = [(h, sub) for sub in reversed(range(n_sub)) for h in streams]
        starts = {sub: pl.multiple_of(base + sub * tk, tk) for sub in range(n_sub)}
        c = [list(carry[h][0]) for h in streams]
        acc = [carry[h][1] for h in streams]
        z2, drop, inner, pv = {}, {}, {}, {}

        def first_row(t):
            return t[1] * tk if diagonal else 0

        def causal(t):
            shape = (tq - first_row(t), tk)
            return lax.broadcasted_iota(jnp.int32, shape, 1) < lax.broadcasted_iota(jnp.int32, shape, 0)

        def scores(t):
            z2[t] = _dot_nt(qhs[t[0]][first_row(t):], k_ref[0, pl.ds(starts[t[1]], tk), lanes_of(t[0])])

        def drops(t):
            d = jnp.maximum(z2[t], 0.0) + jnp.log2(1.0 + jnp.exp2(-jnp.abs(z2[t])))
            if diagonal:
                d = jnp.where(causal(t), d, 0.0)
            drop[t] = d
            inner[t] = jnp.dot(d.astype(BF16), later, preferred_element_type=F32)

        def weights(t):
            h, b0 = t[0], first_row(t) // tk
            blocks = [slice((b - b0) * tk, (b - b0 + 1) * tk) for b in range(b0, n_sub)]
            w = [jnp.exp2((z2[t][rows] - c[h][b0 + n]) - inner[t][rows]) for n, rows in enumerate(blocks)]
            w = w[0] if len(w) == 1 else jnp.concatenate(w, axis=0)
            if diagonal:
                w = jnp.where(causal(t), w, 0.0)
            for n, rows in enumerate(blocks):
                c[h][b0 + n] = c[h][b0 + n] + jnp.sum(drop[t][rows], axis=-1, keepdims=True)
            pv[t] = jnp.dot(w.astype(BF16), v_ref[0, pl.ds(starts[t[1]], tk), lanes_of(h)],
                            preferred_element_type=F32)

        def accumulate(t):
            h, r0 = t[0], first_row(t)
            acc[h] = acc[h] + pv[t] if r0 == 0 else jnp.concatenate([acc[h][:r0], acc[h][r0:] + pv[t]], axis=0)

        stages = (scores, drops, weights, accumulate)
        for step in range(len(tiles) + len(stages) - 1):
            for k, stage in enumerate(stages):
                if 0 <= step - k < len(tiles):
                    stage(tiles[step - k])
        return tuple((tuple(c[h]), acc[h]) for h in streams)

    init = tuple((tuple(jnp.zeros((tk, 1), F32) for _ in range(n_sub)), jnp.zeros((tq, LANES), F32))
                 for _ in streams)
    carry = q_tile_keys(i * tq, init, True)
    carry = lax.fori_loop(0, i, lambda it, cr: q_tile_keys((i - 1 - it) * tq, cr, False), carry)
    for g in range(SB_HEADS_PER_STEP // 2):
        o_ref[:, g * LANES:(g + 1) * LANES] = jnp.where(lane < HEAD_DIM, carry[2 * g][1],
                                                        carry[2 * g + 1][1]).astype(o_ref.dtype)


SB_HEADS_PER_STEP = 8


def _sb_attention(proj, *, batch, seq, tq=512, tk=256):
    T = batch * seq
    nq = seq // tq
    width = SB_HEADS_PER_STEP * HEAD_DIM
    return pl.pallas_call(
        functools.partial(_sb_kernel, tq=tq, tk=tk),
        grid=(batch, D_MIX // width, nq),
        in_specs=[pl.BlockSpec((1, tq, width), lambda b, p, i: (0, b * nq + i, p)),
                  pl.BlockSpec((1, seq, width), lambda b, p, i: (1, b, p)),
                  pl.BlockSpec((1, seq, width), lambda b, p, i: (2, b, p))],
        out_specs=pl.BlockSpec((tq, width), lambda b, p, i: (b * nq + i, p)),
        out_shape=jax.ShapeDtypeStruct((T, D_MIX), BF16),
        compiler_params=pltpu.CompilerParams(
            dimension_semantics=("arbitrary", "arbitrary", "arbitrary"), vmem_limit_bytes=VMEM_LIMIT),
        name="sb_attn",
    )(proj, proj, proj)


def _rel_bucket_np(dist):
    n = np.maximum(dist, 0)
    max_exact = REL_BUCKETS // 2
    nf = np.maximum(n, 1).astype(np.float64)
    large = max_exact + (np.log(nf / max_exact) / math.log(REL_MAX_DIST / max_exact)
                         * (REL_BUCKETS - max_exact)).astype(np.int32)
    large = np.minimum(large, REL_BUCKETS - 1)
    return np.where(n < max_exact, n, large).astype(np.int32)


def _bias_kernel(rel_ref, bkt_ref, o_ref):
    h = pl.program_id(0)
    for d in range(2):
        bkt = bkt_ref[d]
        acc = jnp.zeros(bkt.shape, F32)
        for b in range(REL_BUCKETS):
            acc = jnp.where(bkt == b, rel_ref[b, h], acc)
        o_ref[0, d] = acc * LOG2_E


def _moba_bias(rel_bias):
    L = MOBA_BLOCK
    r = np.arange(L)[:, None]
    c = np.arange(L)[None, :]
    bkt = np.stack([_rel_bucket_np(r - c), _rel_bucket_np(L + r - c)])
    return pl.pallas_call(
        _bias_kernel,
        grid=(N_HEADS,),
        in_specs=[pl.BlockSpec(memory_space=pltpu.SMEM),
                  pl.BlockSpec((2, L, L), lambda h: (0, 0, 0))],
        out_specs=pl.BlockSpec((1, 2, L, L), lambda h: (h, 0, 0, 0)),
        out_shape=jax.ShapeDtypeStruct((N_HEADS, 2, L, L), F32),
        name="moba_bias",
    )(rel_bias, jnp.asarray(bkt))


def _moba_kernel(rel_ref, q_ref, k_ref, v_ref, bias_ref, o_ref, km_ref, *, L, nblk):
    p = pl.program_id(1)
    i = pl.program_id(2)
    lane = lax.broadcasted_iota(jnp.int32, (1, LANES), 1)
    row = lax.broadcasted_iota(jnp.int32, (L, L), 0)
    col = lax.broadcasted_iota(jnp.int32, (L, L), 1)

    @pl.when(i == 0)
    def _():
        km_ref[...] = jnp.mean(k_ref[0].astype(F32).reshape(nblk, L, k_ref.shape[-1]), axis=1)

    def attend(n_off):
        rr = lax.broadcasted_iota(jnp.int32, (n_off, 1), 0)
        km = jnp.concatenate([km_ref[pl.ds(jnp.maximum(i - r, 0), 1), :] for r in range(n_off)], axis=0)
        km = jnp.concatenate([km, jnp.zeros((16 - n_off, km.shape[-1]), F32)], axis=0)
        km_hi, km_lo = _split_bf16(km)
        km_lo2 = (km - km_hi.astype(F32) - km_lo.astype(F32)).astype(BF16)
        past = (rr >= 1) & (rr <= i)
        outs, all_scores, all_values = [], [], []
        for head in range(MOBA_HEADS_PER_STEP):
            h = head % 2
            group = slice((head // 2) * LANES, (head // 2 + 1) * LANES)
            q = q_ref[0, :, group]
            own = lane // HEAD_DIM == h
            aux0 = (1 - h) * HEAD_DIM
            qh = jnp.where(own, q, jnp.zeros((), q.dtype))
            gs = (_dot_nt(km_hi[:, group], qh) + _dot_nt(km_lo[:, group], qh)
                  + _dot_nt(km_lo2[:, group], qh))[:n_off]
            pen = jnp.zeros((n_off, L), F32)
            for r in range(1, n_off):
                g_r = gs[r:r + 1, :]
                beats = past & ((gs > g_r) | ((gs == g_r) & (rr > r)))
                cnt = jnp.sum(beats.astype(F32), axis=0, keepdims=True)
                keep = (cnt < MOBA_TOPK) & (r <= i)
                pen = jnp.where((rr == r) & jnp.logical_not(keep), NEG_BIG, pen)
            far = jnp.full((n_off, L), rel_ref[REL_BUCKETS - 1, MOBA_HEADS_PER_STEP * p + head] * LOG2_E, F32)
            f_hi, f_lo = _split_bf16(far)
            f_lo2 = far - f_hi.astype(F32) - f_lo.astype(F32)
            far3 = jnp.where(rr == 0, f_hi.astype(F32), jnp.where(rr == 1, f_lo.astype(F32),
                                                                  jnp.where(rr == 2, f_lo2, 0.0)))
            aux = [pen, far3, jnp.zeros((LANES - aux0 - 2 * n_off, L), F32)]
            aux = jnp.concatenate(([jnp.zeros((aux0, L), F32)] if aux0 else []) + aux, axis=0)
            q_aug = jnp.where(own, q, aux.T.astype(q.dtype))
            scores, values = [], []
            for r in range(n_off):
                ks = pl.multiple_of(jnp.maximum(i - r, 0) * L, L)
                kblk = k_ref[0, pl.ds(ks, L), group]
                values.append(v_ref[0, pl.ds(ks, L), group])
                if r == 0:
                    s = _dot_nt(qh, kblk) + bias_ref[head, 0]
                    s = jnp.where(col <= row, s, NEG_BIG)
                else:
                    hot = (lane == aux0 + r)
                    if r >= 2:
                        hot = hot | ((lane >= aux0 + n_off) & (lane < aux0 + n_off + 3))
                    k_aug = jnp.where(own, kblk, jnp.where(hot, 1.0, 0.0).astype(kblk.dtype))
                    s = _dot_nt(q_aug, k_aug)
                    if r == 1:
                        s = s + bias_ref[head, 1]
                scores.append(s)
            all_scores.append(scores)
            all_values.append(values)
        for scores, values in zip(all_scores, all_values):
            m = scores[0]
            for s in scores[1:]:
                m = jnp.maximum(m, s)
            m = jnp.max(m, axis=-1, keepdims=True)
            probs = [jnp.exp2(s - m) for s in scores]
            l = probs[0]
            for pr in probs[1:]:
                l = l + pr
            l = jnp.sum(l, axis=-1, keepdims=True)
            acc = jnp.dot(probs[0].astype(BF16), values[0], preferred_element_type=F32)
            for pr, vblk in zip(probs[1:], values[1:]):
                acc = acc + jnp.dot(pr.astype(BF16), vblk, preferred_element_type=F32)
            outs.append(acc / l)
        for g in range(MOBA_HEADS_PER_STEP // 2):
            o_ref[:, g * LANES:(g + 1) * LANES] = jnp.where(lane < HEAD_DIM, outs[2 * g],
                                                            outs[2 * g + 1]).astype(o_ref.dtype)

    for n_off in range(1, nblk + 1):
        pl.when(i == n_off - 1)(functools.partial(attend, n_off))


MOBA_HEADS_PER_STEP = 4


def _moba_attention(proj, bias, rel_bias, *, batch, seq):
    T = batch * seq
    L = MOBA_BLOCK
    nblk = seq // L
    assert seq % (2 * L) == 0, "query blocks are handled in pairs"
    width = MOBA_HEADS_PER_STEP * HEAD_DIM
    return pl.pallas_call(
        functools.partial(_moba_kernel, L=L, nblk=nblk),
        grid=(batch, D_MIX // width, nblk),
        in_specs=[pl.BlockSpec(memory_space=pltpu.SMEM),
                  pl.BlockSpec((1, L, width), lambda b, p, i: (3, b * nblk + i, p)),
                  pl.BlockSpec((1, seq, width), lambda b, p, i: (4, b, p)),
                  pl.BlockSpec((1, seq, width), lambda b, p, i: (5, b, p)),
                  pl.BlockSpec((MOBA_HEADS_PER_STEP, 2, L, L), lambda b, p, i: (p, 0, 0, 0))],
        out_specs=pl.BlockSpec((L, width), lambda b, p, i: (b * nblk + i, p)),
        out_shape=jax.ShapeDtypeStruct((T, D_MIX), BF16),
        scratch_shapes=[pltpu.VMEM((nblk, width), F32)],
        compiler_params=pltpu.CompilerParams(
            dimension_semantics=("arbitrary", "arbitrary", "arbitrary"), vmem_limit_bytes=VMEM_LIMIT),
        name="moba_attn",
    )(rel_bias, proj, proj, proj, bias)


WIN_ROWS = 16
WINS_PER_TILE = 32


def _local_slots(tm):
    return tm * TOP_K + N_EXPERTS * WIN_ROWS


def _post_attn_kernel(ysb_ref, ymb_ref, gsb_ref, gmb_ref, x_ref, wsb_ref, wmb_ref, wo_ref, nrm_ref,
                      wr_ref, br_ref, h_ref, xl_ref, route_ref, cnt_ref, *, tm, chunk):
    lane = lax.broadcasted_iota(jnp.int32, (1, LANES), 1).astype(F32)
    ls = _local_slots(tm)
    w_hi, w_lo = _split_bf16(wr_ref[...])
    w_both = jnp.concatenate([w_hi, w_lo], axis=1)
    hn_of, slots_of = {}, {}

    def dense(sub):
        rows = slice(sub * tm, (sub + 1) * tm)

        def gate(g_ref):
            g = jnp.concatenate([g_ref[0, rows], g_ref[1, rows]], axis=1).astype(F32)
            return 1.0 / (1.0 + jnp.exp(-g))

        merged = (gate(gsb_ref) * jnp.dot(ysb_ref[rows], wsb_ref[...], preferred_element_type=F32)
                  + gate(gmb_ref) * jnp.dot(ymb_ref[rows], wmb_ref[...], preferred_element_type=F32))
        hres = x_ref[rows] + jnp.dot(merged.astype(BF16), wo_ref[...], preferred_element_type=F32)
        h_ref[rows] = hres
        var = jnp.mean(hres * hres, axis=-1, keepdims=True)
        hn_of[sub] = (hres * lax.rsqrt(var + RMS_EPS)) * nrm_ref[...]

    def route(sub):
        hn = hn_of[sub]
        a_hi, a_lo = _split_bf16(hn)
        both = jnp.dot(a_hi, w_both, preferred_element_type=F32)
        logits = (both[:, :LANES] + both[:, LANES:] + jnp.dot(a_lo, w_hi, preferred_element_type=F32)) + br_ref[...]
        work = jnp.where(lane < N_EXPERTS, logits, NEG_BIG)
        vals, hots = [], []
        for _ in range(TOP_K):
            mx = jnp.max(work, axis=-1, keepdims=True)
            idx = jnp.min(jnp.where(work == mx, lane, float(LANES)), axis=-1, keepdims=True)
            hot = lane == idx
            work = jnp.where(hot, NEG_BIG, work)
            vals.append(mx)
            hots.append(hot)
        exps = [jnp.exp(v - vals[0]) for v in vals]
        denom = exps[0] + exps[1] + exps[2] + exps[3]
        sel = jnp.zeros((tm, LANES), F32)
        for hot in hots:
            sel = jnp.where(hot, 1.0, sel)
        r = lax.broadcasted_iota(jnp.int32, (tm, tm), 0)
        c = lax.broadcasted_iota(jnp.int32, (tm, tm), 1)
        earlier = jnp.dot((c < r).astype(BF16), sel.astype(BF16), preferred_element_type=F32)
        cnt = jnp.sum(sel, axis=0, keepdims=True)
        padded = jnp.floor((cnt + (WIN_ROWS - 1)) * (1.0 / WIN_ROWS)) * WIN_ROWS
        er = lax.broadcasted_iota(jnp.int32, (LANES, LANES), 0)
        ec = lax.broadcasted_iota(jnp.int32, (LANES, LANES), 1)
        lower = jnp.dot(jnp.broadcast_to(padded, (8, LANES)).astype(BF16), (er < ec).astype(BF16),
                        preferred_element_type=F32)[0:1]
        slot = lower + earlier
        slab = jnp.zeros((tm, LANES), F32)
        for k, hot in enumerate(hots):
            slab = jnp.where(lane == k, exps[k] / denom, slab)
            slab = jnp.where(lane == TOP_K + k, jnp.sum(jnp.where(hot, slot, 0.0), axis=-1, keepdims=True), slab)
        route_ref[sub * tm:(sub + 1) * tm] = slab
        cnt_ref[sub] = jnp.broadcast_to(cnt, (8, LANES))
        slots_of[sub] = slab.T

    def sort_rows(sub):
        slots_t = slots_of[sub]
        hn_bf = hn_of[sub].astype(BF16)
        for ci in range(ls // chunk):
            rid = (lax.broadcasted_iota(jnp.int32, (chunk, 1), 0) + ci * chunk).astype(F32)
            onehot = rid == slots_t[TOP_K:TOP_K + 1, :]
            for k in range(1, TOP_K):
                onehot = onehot | (rid == slots_t[TOP_K + k:TOP_K + k + 1, :])
            xl_ref[sub * ls + ci * chunk:sub * ls + (ci + 1) * chunk, :] = jnp.dot(
                jnp.where(onehot, 1.0, 0.0).astype(BF16), hn_bf, preferred_element_type=F32).astype(xl_ref.dtype)

    stages = (dense, route, sort_rows)
    for step in range(POST_TILES_PER_STEP + len(stages) - 1):
        for k, stage in enumerate(stages):
            if 0 <= step - k < POST_TILES_PER_STEP:
                stage(step - k)


POST_TILES_PER_STEP = 2


def _post_attn(ysb, ymb, proj, xt, wsb, wmb, wo, nrm_g, wr_pad, br_pad, *, tm):
    T, D = xt.shape
    half = D // 2
    n_tt = T // tm
    ls = _local_slots(tm)
    sub = POST_TILES_PER_STEP
    const = lambda i: (0, 0)
    return pl.pallas_call(
        functools.partial(_post_attn_kernel, tm=tm, chunk=256),
        grid=(n_tt // sub,),
        in_specs=[pl.BlockSpec((sub * tm, D_MIX), lambda i: (i, 0)),
                  pl.BlockSpec((sub * tm, D_MIX), lambda i: (i, 0)),
                  pl.BlockSpec((2, sub * tm, half), lambda i: (3, i, 0)),
                  pl.BlockSpec((2, sub * tm, half), lambda i: (4, i, 0)),
                  pl.BlockSpec((sub * tm, D), lambda i: (i, 0)),
                  pl.BlockSpec(wsb.shape, const),
                  pl.BlockSpec(wmb.shape, const),
                  pl.BlockSpec(wo.shape, const),
                  pl.BlockSpec((1, D), const),
                  pl.BlockSpec(wr_pad.shape, const),
                  pl.BlockSpec((1, LANES), const)],
        out_specs=[pl.BlockSpec((sub * tm, D), lambda i: (i, 0)),
                   pl.BlockSpec((sub * ls, D), lambda i: (i, 0)),
                   pl.BlockSpec((sub * tm, LANES), lambda i: (i, 0)),
                   pl.BlockSpec((sub, 8, LANES), lambda i: (i, 0, 0))],
        out_shape=[jax.ShapeDtypeStruct((T, D), F32),
                   jax.ShapeDtypeStruct((n_tt * ls, D), BF16),
                   jax.ShapeDtypeStruct((T, LANES), F32),
                   jax.ShapeDtypeStruct((n_tt, 8, LANES), F32)],
        compiler_params=pltpu.CompilerParams(dimension_semantics=("arbitrary",),
                                             vmem_limit_bytes=VMEM_LIMIT),
        name="post_attn",
    )(ysb, ymb, proj, proj, xt, wsb, wmb, wo, nrm_g.reshape(1, D), wr_pad, br_pad)


def _window_plan(counts, tm):
    n_tt, n_exp = counts.shape
    lw_tile = _local_slots(tm) // WIN_ROWS
    n_tiles = -(-(n_tt * (tm * TOP_K // WIN_ROWS + n_exp)) // WINS_PER_TILE) + n_exp
    n_tiles += n_tiles % 2
    nw = (counts + WIN_ROWS - 1) // WIN_ROWS
    local_start = jnp.cumsum(nw, axis=1) - nw
    per_expert = jnp.sum(nw, axis=0)
    per_expert_pad = -(-per_expert // WINS_PER_TILE) * WINS_PER_TILE
    expert_end = jnp.cumsum(per_expert_pad)
    sorted_start = (expert_end - per_expert_pad)[None, :] + jnp.cumsum(nw, axis=0) - nw
    run_start = sorted_start.T.reshape(-1)
    run_len = nw.T.reshape(-1)
    run_src = (jnp.arange(n_tt, dtype=jnp.int32)[:, None] * lw_tile + local_start).T.reshape(-1)
    g = jnp.arange(n_tiles * WINS_PER_TILE, dtype=jnp.int32)[:, None]
    off = g - run_start[None, :]
    src_win = jnp.sum(jnp.where((off >= 0) & (off < run_len[None, :]), run_src[None, :] + off, 0), axis=1)
    off_l = jnp.arange(lw_tile, dtype=jnp.int32)[None, :, None] - local_start[:, None, :]
    back_win = jnp.sum(jnp.where((off_l >= 0) & (off_l < nw[:, None, :]), sorted_start[:, None, :] + off_l, 0),
                       axis=2)
    tile_first = (jnp.arange(n_tiles, dtype=jnp.int32) * WINS_PER_TILE)[:, None]
    tile_valid = tile_first[:, 0] < expert_end[-1]
    used = (per_expert > 0)[None, :]
    expert_start = (expert_end - per_expert_pad)[None, :]
    ids = jnp.arange(n_exp, dtype=jnp.int32)[None, :]
    tile_expert = jnp.minimum(jnp.sum(expert_end[None, :] <= tile_first, axis=1), n_exp - 1)
    opens = jnp.sum(used & (expert_start == tile_first), axis=1) > 0
    buffer = (jnp.sum(used & (expert_start <= tile_first), axis=1) - 1) % 2
    next_expert = jnp.min(jnp.where(used & (expert_start > tile_first), ids, n_exp), axis=1)
    next_expert = jnp.where(next_expert == n_exp, -1, next_expert)
    tile_plan = tuple(a.astype(jnp.int32) for a in (tile_expert, tile_valid, opens, buffer, next_expert))
    return src_win.astype(jnp.int32), back_win.astype(jnp.int32), tile_plan


EXPERT_TILES_PER_STEP = 2


def _experts_kernel(te_ref, tv_ref, first_ref, par_ref, ne_ref, src_ref, nxt_ref, xl_ref, wgu_ref, bgu_a, bgu_b,
                    wd_ref, bd_a, bd_b, y_ref, xbuf, wgu_f, wd_f, wgu_s, wd_s, sem, wsem, *, d_exp):
    step = pl.program_id(0)
    n_steps = pl.num_programs(0)
    slot = step % 2
    ta, tb = 2 * step, 2 * step + 1
    tmm = WINS_PER_TILE * WIN_ROWS
    n_win = EXPERT_TILES_PER_STEP * WINS_PER_TILE

    def window_copy(win, dst_slot, w):
        return pltpu.make_async_copy(xl_ref.at[win], xbuf.at[dst_slot, w], sem.at[dst_slot])

    def start_gather(idx_ref, dst_slot):
        for w in range(n_win):
            window_copy(idx_ref[0, 0, w], dst_slot, w).start()

    def weight_copies(expert, buf):
        return (pltpu.make_async_copy(wgu_ref.at[expert], wgu_f.at[buf], wsem.at[buf, 0]),
                pltpu.make_async_copy(wd_ref.at[expert], wd_f.at[buf], wsem.at[buf, 1]))

    @pl.when((step == 0) & (tv_ref[0] > 0))
    def _():
        for cp in weight_copies(te_ref[0], par_ref[0]):
            cp.start()
        start_gather(src_ref, 0)

    @pl.when((step + 1 < n_steps) & (tv_ref[jnp.minimum(ta + 2, 2 * n_steps - 1)] > 0))
    def _():
        start_gather(nxt_ref, 1 - slot)

    def switch_weights(t):
        @pl.when((first_ref[t] > 0) & (tv_ref[t] > 0))
        def _():
            buf = par_ref[t]

            @pl.when(ne_ref[t] >= 0)
            def _():
                for cp in weight_copies(ne_ref[t], 1 - buf):
                    cp.start()

            for cp in weight_copies(te_ref[t], buf):
                cp.wait()
            wgu_s[buf] = wgu_f[buf].astype(BF16)
            wd_s[buf] = wd_f[buf].astype(BF16)

    switch_weights(ta)
    switch_weights(tb)

    def ffn(tiles):
        xs = [xbuf[slot, w0:w0 + WINS_PER_TILE].reshape(tmm, xbuf.shape[-1]) for w0, _, _, _ in tiles]
        hgu = [jnp.dot(x, wgu_s[buf], preferred_element_type=F32) + bias[0][0]
               for x, (_, buf, bias, _) in zip(xs, tiles)]
        acts = []
        for h in hgu:
            gate = jnp.minimum(h[:, :d_exp], SWIGLU_LIMIT)
            up = jnp.clip(h[:, d_exp:], -SWIGLU_LIMIT, SWIGLU_LIMIT)
            acts.append(((up + 1.0) * (gate * (1.0 / (1.0 + jnp.exp(-SWIGLU_ALPHA * gate))))).astype(BF16))
        for act, (_, buf, bias, row0) in zip(acts, tiles):
            y = jnp.dot(act, wd_s[buf], preferred_element_type=F32) + bias[1][0]
            y_ref[row0:row0 + tmm] = y.astype(y_ref.dtype)

    def wait_windows():
        pltpu.make_async_copy(xl_ref.at[pl.ds(0, n_win)], xbuf.at[slot], sem.at[slot]).wait()

    tile_a = (0, par_ref[ta], (bgu_a, bd_a), 0)
    tile_b = (WINS_PER_TILE, par_ref[tb], (bgu_b, bd_b), tmm)

    @pl.when(tv_ref[tb] > 0)
    def _():
        wait_windows()
        ffn([tile_a, tile_b])

    @pl.when((tv_ref[ta] > 0) & (tv_ref[tb] == 0))
    def _():
        wait_windows()
        ffn([tile_a])
        y_ref[tmm:] = jnp.zeros((tmm, y_ref.shape[-1]), y_ref.dtype)

    @pl.when(tv_ref[ta] == 0)
    def _():
        y_ref[...] = jnp.zeros_like(y_ref)


def _experts(xl, src_win, tile_plan, wgu, bgu, wd, bd):
    D = xl.shape[1]
    E, _, d2 = wgu.shape
    d_exp = d2 // 2
    sub = EXPERT_TILES_PER_STEP
    n_steps = tile_plan[0].shape[0] // sub
    tmm = WINS_PER_TILE * WIN_ROWS
    src3 = src_win.reshape(n_steps, 1, sub * WINS_PER_TILE)
    grid_spec = pltpu.PrefetchScalarGridSpec(
        num_scalar_prefetch=len(tile_plan),
        grid=(n_steps,),
        in_specs=[pl.BlockSpec((1, 1, sub * WINS_PER_TILE), lambda s, *_: (s, 0, 0), memory_space=pltpu.SMEM),
                  pl.BlockSpec((1, 1, sub * WINS_PER_TILE), lambda s, *_: (jnp.minimum(s + 1, n_steps - 1), 0, 0),
                               memory_space=pltpu.SMEM),
                  pl.BlockSpec(memory_space=pl.ANY),
                  pl.BlockSpec(memory_space=pl.ANY),
                  pl.BlockSpec((1, 1, d2), lambda s, te, *_: (te[2 * s], 0, 0)),
                  pl.BlockSpec((1, 1, d2), lambda s, te, *_: (te[2 * s + 1], 0, 0)),
                  pl.BlockSpec(memory_space=pl.ANY),
                  pl.BlockSpec((1, 1, D), lambda s, te, *_: (te[2 * s], 0, 0)),
                  pl.BlockSpec((1, 1, D), lambda s, te, *_: (te[2 * s + 1], 0, 0))],
        out_specs=pl.BlockSpec((sub * tmm, D), lambda s, *_: (s, 0)),
        scratch_shapes=[pltpu.VMEM((2, sub * WINS_PER_TILE, WIN_ROWS, D), BF16),
                        pltpu.VMEM((2, D, d2), F32), pltpu.VMEM((2, d_exp, D), F32),
                        pltpu.VMEM((2, D, d2), BF16), pltpu.VMEM((2, d_exp, D), BF16),
                        pltpu.SemaphoreType.DMA((2,)), pltpu.SemaphoreType.DMA((2, 2))],
    )
    bgu3, bd3 = bgu.reshape(E, 1, d2), bd.reshape(E, 1, D)
    return pl.pallas_call(
        functools.partial(_experts_kernel, d_exp=d_exp),
        grid_spec=grid_spec,
        out_shape=jax.ShapeDtypeStruct((n_steps * sub * tmm, D), BF16),
        compiler_params=pltpu.CompilerParams(dimension_semantics=("arbitrary",),
                                             vmem_limit_bytes=VMEM_LIMIT),
        name="experts",
    )(*tile_plan, src3, src3, xl.reshape(-1, WIN_ROWS, D), wgu, bgu3, bgu3, wd, bd3, bd3)


def _combine_kernel(back_ref, nxt_ref, y_ref, h_ref, route_ref, g_ref, o_ref, ybuf, sem, *, tm, chunk):
    i = pl.program_id(0)
    slot = i % 2
    n_win = ybuf.shape[1]
    tile_win = n_win // COMBINE_TILES_PER_STEP

    def window_copy(win, dst_slot, w):
        return pltpu.make_async_copy(y_ref.at[win], ybuf.at[dst_slot, w], sem.at[dst_slot])

    def start_fetch(idx_ref, dst_slot):
        for w in range(n_win):
            window_copy(idx_ref[0, 0, w], dst_slot, w).start()

    @pl.when(i == 0)
    def _():
        start_fetch(back_ref, 0)

    @pl.when(i + 1 < pl.num_programs(0))
    def _():
        start_fetch(nxt_ref, 1 - slot)

    tiles = [slice(t * tm, (t + 1) * tm) for t in range(COMBINE_TILES_PER_STEP)]
    routes = [route_ref[rows] for rows in tiles]
    weights = [[route[:, k:k + 1] for k in range(TOP_K)] for route in routes]
    slots = [[route[:, TOP_K + k:TOP_K + k + 1] for k in range(TOP_K)] for route in routes]
    pltpu.make_async_copy(y_ref.at[pl.ds(0, n_win)], ybuf.at[slot], sem.at[slot]).wait()
    acc = [h_ref[rows] for rows in tiles]
    for ci in range(tile_win * WIN_ROWS // chunk):
        sid = (lax.broadcasted_iota(jnp.int32, (1, chunk), 1) + ci * chunk).astype(F32)
        for t in range(COMBINE_TILES_PER_STEP):
            wmat = jnp.zeros((tm, chunk), F32)
            for k in range(TOP_K):
                wmat = jnp.where(sid == slots[t][k], weights[t][k], wmat)
            hi, lo = _split_bf16(wmat)
            first = t * tile_win + ci * chunk // WIN_ROWS
            y_rows = ybuf[slot, first:first + chunk // WIN_ROWS].reshape(chunk, ybuf.shape[-1])
            acc[t] = (acc[t] + jnp.dot(hi, y_rows, preferred_element_type=F32)
                      + jnp.dot(lo, y_rows, preferred_element_type=F32))
    for t, rows in enumerate(tiles):
        var = jnp.mean(acc[t] * acc[t], axis=-1, keepdims=True)
        o_ref[rows] = (acc[t] * lax.rsqrt(var + RMS_EPS)) * g_ref[...]


COMBINE_TILES_PER_STEP = 2


def _combine(y, back_win, h, route, g, *, tm):
    T, D = h.shape
    sub = COMBINE_TILES_PER_STEP
    n_steps, n_win = back_win.shape[0] // sub, back_win.shape[1] * sub
    back3 = back_win.reshape(n_steps, 1, n_win)
    return pl.pallas_call(
        functools.partial(_combine_kernel, tm=tm, chunk=256),
        grid=(n_steps,),
        in_specs=[pl.BlockSpec((1, 1, n_win), lambda i: (i, 0, 0), memory_space=pltpu.SMEM),
                  pl.BlockSpec((1, 1, n_win), lambda i: (jnp.minimum(i + 1, n_steps - 1), 0, 0),
                               memory_space=pltpu.SMEM),
                  pl.BlockSpec(memory_space=pl.ANY),
                  pl.BlockSpec((sub * tm, D), lambda i: (i, 0)),
                  pl.BlockSpec((sub * tm, LANES), lambda i: (i, 0)),
                  pl.BlockSpec((1, D), lambda i: (0, 0))],
        out_specs=pl.BlockSpec((sub * tm, D), lambda i: (i, 0)),
        out_shape=jax.ShapeDtypeStruct((T, D), F32),
        scratch_shapes=[pltpu.VMEM((2, n_win, WIN_ROWS, D), BF16), pltpu.SemaphoreType.DMA((2,))],
        compiler_params=pltpu.CompilerParams(dimension_semantics=("arbitrary",),
                                             vmem_limit_bytes=VMEM_LIMIT),
        name="combine",
    )(back3, back3, y.reshape(-1, WIN_ROWS, D), h, route, g.reshape(1, D))


def kernel(x, norm_mix_g, w_in, w_proj_sb, w_proj_moba, w_out, rel_bias, norm_ffn_g, w_router, b_router,
           w_gate_up, b_gate_up, w_down, b_down, norm_final_g):
    batch, seq, D = x.shape
    assert norm_mix_g.shape[0] == 1, "single-layer trunk"
    tm = 256
    xt = x.reshape(batch * seq, D)
    col_scale = np.ones((w_in.shape[-1],), np.float32)
    col_scale[0:D_MIX] = QUERY_SCALE
    col_scale[3 * D_MIX:4 * D_MIX] = QUERY_SCALE
    proj = _in_proj(xt, norm_mix_g[0], (w_in[0] * col_scale).astype(BF16))
    ysb = _sb_attention(proj, batch=batch, seq=seq)
    ymb = _moba_attention(proj, _moba_bias(rel_bias), rel_bias, batch=batch, seq=seq)
    wr_pad = jnp.pad(w_router[0], ((0, 0), (0, LANES - N_EXPERTS)))
    br_pad = jnp.pad(b_router[0], (0, LANES - N_EXPERTS)).reshape(1, LANES)
    h, xl, route, cnt = _post_attn(ysb, ymb, proj, xt, w_proj_sb[0].astype(BF16), w_proj_moba[0].astype(BF16),
                                   w_out[0].astype(BF16), norm_ffn_g[0], wr_pad, br_pad, tm=tm)
    src_win, back_win, tile_plan = _window_plan(cnt[:, 0, :N_EXPERTS].astype(jnp.int32), tm)
    y = _experts(xl, src_win, tile_plan, w_gate_up[0], b_gate_up[0], w_down[0], b_down[0])
    out = _combine(y, back_win, h, route, norm_final_g, tm=tm)
    return out.reshape(batch, seq, D)
```

```python
import functools
import math

import numpy as np
import jax
import jax.numpy as jnp
from jax import lax
from jax.experimental import pallas as pl
from jax.experimental.pallas import tpu as pltpu

HEAD_DIM = 64
N_HEADS = 8
D_MIX = N_HEADS * HEAD_DIM
MOBA_BLOCK = 256
MOBA_TOPK = 3
REL_BUCKETS = 32
REL_MAX_DIST = 128
N_EXPERTS = 32
TOP_K = 4
SWIGLU_LIMIT = 7.0
SWIGLU_ALPHA = 1.702
RMS_EPS = 1e-6

LANES = 128
NEG_BIG = -1e30
LOG2_E = math.log2(math.e)
QUERY_SCALE = LOG2_E / math.sqrt(HEAD_DIM)
VMEM_LIMIT = 56 * 1024 * 1024

F32 = jnp.float32
BF16 = jnp.bfloat16


def _split_bf16(a):
    hi = a.astype(BF16)
    lo = (a - hi.astype(F32)).astype(BF16)
    return hi, lo


def _dot_nt(a, b):
    return lax.dot_general(a, b, (((1,), (1,)), ((), ())), preferred_element_type=F32)


def _in_proj_kernel(x_ref, g_ref, w_ref, o_ref, *, n_slab, slab):
    x = x_ref[...]
    var = jnp.mean(x * x, axis=-1, keepdims=True)
    xn = ((x * lax.rsqrt(var + RMS_EPS)) * g_ref[...]).astype(BF16)
    for j in range(n_slab):
        o_ref[j] = jnp.dot(xn, w_ref[:, j * slab:(j + 1) * slab],
                           preferred_element_type=F32).astype(o_ref.dtype)


def _in_proj(xt, g, w_bf16, *, tm=1024, slab=D_MIX):
    T, D = xt.shape
    n_slab = w_bf16.shape[1] // slab
    return pl.pallas_call(
        functools.partial(_in_proj_kernel, n_slab=n_slab, slab=slab),
        grid=(T // tm,),
        in_specs=[pl.BlockSpec((tm, D), lambda i: (i, 0)),
                  pl.BlockSpec((1, D), lambda i: (0, 0)),
                  pl.BlockSpec(w_bf16.shape, lambda i: (0, 0))],
        out_specs=pl.BlockSpec((n_slab, tm, slab), lambda i: (0, i, 0)),
        out_shape=jax.ShapeDtypeStruct((n_slab, T, slab), BF16),
        compiler_params=pltpu.CompilerParams(dimension_semantics=("arbitrary",),
                                             vmem_limit_bytes=VMEM_LIMIT),
        name="in_proj",
    )(xt, g.reshape(1, D), w_bf16)


def _sb_kernel(q_ref, k_ref, v_ref, o_ref, *, tq, tk):
    i = pl.program_id(2)
    lane = lax.broadcasted_iota(jnp.int32, (1, LANES), 1)
    jj = lax.broadcasted_iota(jnp.int32, (tk, tk), 0)
    ss = lax.broadcasted_iota(jnp.int32, (tk, tk), 1)
    later = (jj >= ss).astype(BF16)
    n_sub = tq // tk
    streams = range(SB_HEADS_PER_STEP)

    def lanes_of(head):
        return slice((head // 2) * LANES, (head // 2 + 1) * LANES)

    qhs = []
    for head in streams:
        q = q_ref[0, :, lanes_of(head)]
        qhs.append(jnp.where(lane // HEAD_DIM == head % 2, q, jnp.zeros((), q.dtype)))

    def q_tile_keys(base, carry, diagonal):
        tiles = [(h, sub) for sub in reversed(range(n_sub)) for h in streams]
        starts = {sub: pl.multiple_of(base + sub * tk, tk) for sub in range(n_sub)}
        c = [list(carry[h][0]) for h in streams]
        acc = [carry[h][1] for h in streams]
        z2, drop, inner, pv = {}, {}, {}, {}

        def first_row(t):
            return t[1] * tk if diagonal else 0

        def causal(t):
            shape = (tq - first_row(t), tk)
            return lax.broadcasted_iota(jnp.int32, shape, 1) < lax.broadcasted_iota(jnp.int32, shape, 0)

        def scores(t):
            z2[t] = _dot_nt(qhs[t[0]][first_row(t):], k_ref[0, pl.ds(starts[t[1]], tk), lanes_of(t[0])])

        def drops(t):
            d = jnp.maximum(z2[t], 0.0) + jnp.log2(1.0 + jnp.exp2(-jnp.abs(z2[t])))
            if diagonal:
                d = jnp.where(causal(t), d, 0.0)
            drop[t] = d
            inner[t] = jnp.dot(d.astype(BF16), later, preferred_element_type=F32)

        def weights(t):
            h, b0 = t[0], first_row(t) // tk
            blocks = [slice((b - b0) * tk, (b - b0 + 1) * tk) for b in range(b0, n_sub)]
            w = [jnp.exp2((z2[t][rows] - c[h][b0 + n]) - inner[t][rows]) for n, rows in enumerate(blocks)]
            w = w[0] if len(w) == 1 else jnp.concatenate(w, axis=0)
            if diagonal:
                w = jnp.where(causal(t), w, 0.0)
            for n, rows in enumerate(blocks):
                c[h][b0 + n] = c[h][b0 + n] + jnp.sum(drop[t][rows], axis=-1, keepdims=True)
            pv[t] = jnp.dot(w.astype(BF16), v_ref[0, pl.ds(starts[t[1]], tk), lanes_of(h)],
                            preferred_element_type=F32)

        def accumulate(t):
            h, r0 = t[0], first_row(t)
            acc[h] = acc[h] + pv[t] if r0 == 0 else jnp.concatenate([acc[h][:r0], acc[h][r0:] + pv[t]], axis=0)

        stages = (scores, drops, weights, accumulate)
        for step in range(len(tiles) + len(stages) - 1):
            for k, stage in enumerate(stages):
                if 0 <= step - k < len(tiles):
                    stage(tiles[step - k])
        return tuple((tuple(c[h]), acc[h]) for h in streams)

    init = tuple((tuple(jnp.zeros((tk, 1), F32) for _ in range(n_sub)), jnp.zeros((tq, LANES), F32))
                 for _ in streams)
    carry = q_tile_keys(i * tq, init, True)
    carry = lax.fori_loop(0, i, lambda it, cr: q_tile_keys((i - 1 - it) * tq, cr, False), carry)
    for g in range(SB_HEADS_PER_STEP // 2):
        o_ref[:, g * LANES:(g + 1) * LANES] = jnp.where(lane < HEAD_DIM, carry[2 * g][1],
                                                        carry[2 * g + 1][1]).astype(o_ref.dtype)


SB_HEADS_PER_STEP = 8


def _sb_attention(proj, *, batch, seq, tq=512, tk=256):
    T = batch * seq
    nq = seq // tq
    width = SB_HEADS_PER_STEP * HEAD_DIM
    return pl.pallas_call(
        functools.partial(_sb_kernel, tq=tq, tk=tk),
        grid=(batch, D_MIX // width, nq),
        in_specs=[pl.BlockSpec((1, tq, width), lambda b, p, i: (0, b * nq + i, p)),
                  pl.BlockSpec((1, seq, width), lambda b, p, i: (1, b, p)),
                  pl.BlockSpec((1, seq, width), lambda b, p, i: (2, b, p))],
        out_specs=pl.BlockSpec((tq, width), lambda b, p, i: (b * nq + i, p)),
        out_shape=jax.ShapeDtypeStruct((T, D_MIX), BF16),
        compiler_params=pltpu.CompilerParams(
            dimension_semantics=("arbitrary", "arbitrary", "arbitrary"), vmem_limit_bytes=VMEM_LIMIT),
        name="sb_attn",
    )(proj, proj, proj)


def _rel_bucket_np(dist):
    n = np.maximum(dist, 0)
    max_exact = REL_BUCKETS // 2
    nf = np.maximum(n, 1).astype(np.float64)
    large = max_exact + (np.log(nf / max_exact) / math.log(REL_MAX_DIST / max_exact)
                         * (REL_BUCKETS - max_exact)).astype(np.int32)
    large = np.minimum(large, REL_BUCKETS - 1)
    return np.where(n < max_exact, n, large).astype(np.int32)


def _bias_kernel(rel_ref, bkt_ref, o_ref):
    h = pl.program_id(0)
    for d in range(2):
        bkt = bkt_ref[d]
        acc = jnp.zeros(bkt.shape, F32)
        for b in range(REL_BUCKETS):
            acc = jnp.where(bkt == b, rel_ref[b, h], acc)
        o_ref[0, d] = acc * LOG2_E


def _moba_bias(rel_bias):
    L = MOBA_BLOCK
    r = np.arange(L)[:, None]
    c = np.arange(L)[None, :]
    bkt = np.stack([_rel_bucket_np(r - c), _rel_bucket_np(L + r - c)])
    return pl.pallas_call(
        _bias_kernel,
        grid=(N_HEADS,),
        in_specs=[pl.BlockSpec(memory_space=pltpu.SMEM),
                  pl.BlockSpec((2, L, L), lambda h: (0, 0, 0))],
        out_specs=pl.BlockSpec((1, 2, L, L), lambda h: (h, 0, 0, 0)),
        out_shape=jax.ShapeDtypeStruct((N_HEADS, 2, L, L), F32),
        name="moba_bias",
    )(rel_bias, jnp.asarray(bkt))


def _moba_kernel(rel_ref, q_ref, k_ref, v_ref, bias_ref, o_ref, km_ref, *, L, nblk):
    p = pl.program_id(1)
    i = pl.program_id(2)
    lane = lax.broadcasted_iota(jnp.int32, (1, LANES), 1)
    row = lax.broadcasted_iota(jnp.int32, (L, L), 0)
    col = lax.broadcasted_iota(jnp.int32, (L, L), 1)

    @pl.when(i == 0)
    def _():
        km_ref[...] = jnp.mean(k_ref[0].astype(F32).reshape(nblk, L, k_ref.shape[-1]), axis=1)

    def attend(n_off):
        rr = lax.broadcasted_iota(jnp.int32, (n_off, 1), 0)
        km = jnp.concatenate([km_ref[pl.ds(jnp.maximum(i - r, 0), 1), :] for r in range(n_off)], axis=0)
        km = jnp.concatenate([km, jnp.zeros((16 - n_off, km.shape[-1]), F32)], axis=0)
        km_hi, km_lo = _split_bf16(km)
        km_lo2 = (km - km_hi.astype(F32) - km_lo.astype(F32)).astype(BF16)
        past = (rr >= 1) & (rr <= i)
        outs, all_scores, all_values = [], [], []
        for head in range(MOBA_HEADS_PER_STEP):
            h = head % 2
            group = slice((head // 2) * LANES, (head // 2 + 1) * LANES)
            q = q_ref[0, :, group]
            own = lane // HEAD_DIM == h
            aux0 = (1 - h) * HEAD_DIM
            qh = jnp.where(own, q, jnp.zeros((), q.dtype))
            gs = (_dot_nt(km_hi[:, group], qh) + _dot_nt(km_lo[:, group], qh)
                  + _dot_nt(km_lo2[:, group], qh))[:n_off]
            pen = jnp.zeros((n_off, L), F32)
            for r in range(1, n_off):
                g_r = gs[r:r + 1, :]
                beats = past & ((gs > g_r) | ((gs == g_r) & (rr > r)))
                cnt = jnp.sum(beats.astype(F32), axis=0, keepdims=True)
                keep = (cnt < MOBA_TOPK) & (r <= i)
                pen = jnp.where((rr == r) & jnp.logical_not(keep), NEG_BIG, pen)
            far = jnp.full((n_off, L), rel_ref[REL_BUCKETS - 1, MOBA_HEADS_PER_STEP * p + head] * LOG2_E, F32)
            f_hi, f_lo = _split_bf16(far)
            f_lo2 = far - f_hi.astype(F32) - f_lo.astype(F32)
            far3 = jnp.where(rr == 0, f_hi.astype(F32), jnp.where(rr == 1, f_lo.astype(F32),
                                                                  jnp.where(rr == 2, f_lo2, 0.0)))
            aux = [pen, far3, jnp.zeros((LANES - aux0 - 2 * n_off, L), F32)]
            aux = jnp.concatenate(([jnp.zeros((aux0, L), F32)] if aux0 else []) + aux, axis=0)
            q_aug = jnp.where(own, q, aux.T.astype(q.dtype))
            scores, values = [], []
            for r in range(n_off):
                ks = pl.multiple_of(jnp.maximum(i - r, 0) * L, L)
                kblk = k_ref[0, pl.ds(ks, L), group]
                values.append(v_ref[0, pl.ds(ks, L), group])
                if r == 0:
                    s = _dot_nt(qh, kblk) + bias_ref[head, 0]
                    s = jnp.where(col <= row, s, NEG_BIG)
                else:
                    hot = (lane == aux0 + r)
                    if r >= 2:
                        hot = hot | ((lane >= aux0 + n_off) & (lane < aux0 + n_off + 3))
                    k_aug = jnp.where(own, kblk, jnp.where(hot, 1.0, 0.0).astype(kblk.dtype))
                    s = _dot_nt(q_aug, k_aug)
                    if r == 1:
                        s = s + bias_ref[head, 1]
                scores.append(s)
            all_scores.append(scores)
            all_values.append(values)
        for scores, values in zip(all_scores, all_values):
            m = scores[0]
            for s in scores[1:]:
                m = jnp.maximum(m, s)
            m = jnp.max(m, axis=-1, keepdims=True)
            probs = [jnp.exp2(s - m) for s in scores]
            l = probs[0]
            for pr in probs[1:]:
                l = l + pr
            l = jnp.sum(l, axis=-1, keepdims=True)
            acc = jnp.dot(probs[0].astype(BF16), values[0], preferred_element_type=F32)
            for pr, vblk in zip(probs[1:], values[1:]):
                acc = acc + jnp.dot(pr.astype(BF16), vblk, preferred_element_type=F32)
            outs.append(acc / l)
        for g in range(MOBA_HEADS_PER_STEP // 2):
            o_ref[:, g * LANES:(g + 1) * LANES] = jnp.where(lane < HEAD_DIM, outs[2 * g],
                                                            outs[2 * g + 1]).astype(o_ref.dtype)

    for n_off in range(1, nblk + 1):
        pl.when(i == n_off - 1)(functools.partial(attend, n_off))


MOBA_HEADS_PER_STEP = 4


def _moba_attention(proj, bias, rel_bias, *, batch, seq):
    T = batch * seq
    L = MOBA_BLOCK
    nblk = seq // L
    assert seq % (2 * L) == 0, "query blocks are handled in pairs"
    width = MOBA_HEADS_PER_STEP * HEAD_DIM
    return pl.pallas_call(
        functools.partial(_moba_kernel, L=L, nblk=nblk),
        grid=(batch, D_MIX // width, nblk),
        in_specs=[pl.BlockSpec(memory_space=pltpu.SMEM),
                  pl.BlockSpec((1, L, width), lambda b, p, i: (3, b * nblk + i, p)),
                  pl.BlockSpec((1, seq, width), lambda b, p, i: (4, b, p)),
                  pl.BlockSpec((1, seq, width), lambda b, p, i: (5, b, p)),
                  pl.BlockSpec((MOBA_HEADS_PER_STEP, 2, L, L), lambda b, p, i: (p, 0, 0, 0))],
        out_specs=pl.BlockSpec((L, width), lambda b, p, i: (b * nblk + i, p)),
        out_shape=jax.ShapeDtypeStruct((T, D_MIX), BF16),
        scratch_shapes=[pltpu.VMEM((nblk, width), F32)],
        compiler_params=pltpu.CompilerParams(
            dimension_semantics=("arbitrary", "arbitrary", "arbitrary"), vmem_limit_bytes=VMEM_LIMIT),
        name="moba_attn",
    )(rel_bias, proj, proj, proj, bias)


WIN_ROWS = 16
WINS_PER_TILE = 32


def _local_slots(tm):
    return tm * TOP_K + N_EXPERTS * WIN_ROWS


def _post_attn_kernel(ysb_ref, ymb_ref, gsb_ref, gmb_ref, x_ref, wsb_ref, wmb_ref, wo_ref, nrm_ref,
                      wt_ref, bt_ref, h_ref, xl_ref, route_ref, cnt_ref, *, tm, chunk):
    ls = _local_slots(tm)
    wt_both = jnp.concatenate(_split_bf16(wt_ref[...]), axis=0)
    hn_of, slots_of = {}, {}

    def dense(sub):
        rows = slice(sub * tm, (sub + 1) * tm)

        def gate(g_ref):
            g = jnp.concatenate([g_ref[0, rows], g_ref[1, rows]], axis=1).astype(F32)
            return 1.0 / (1.0 + jnp.exp(-g))

        merged = (gate(gsb_ref) * jnp.dot(ysb_ref[rows], wsb_ref[...], preferred_element_type=F32)
                  + gate(gmb_ref) * jnp.dot(ymb_ref[rows], wmb_ref[...], preferred_element_type=F32))
        hres = x_ref[rows] + jnp.dot(merged.astype(BF16), wo_ref[...], preferred_element_type=F32)
        h_ref[rows] = hres
        var = jnp.mean(hres * hres, axis=-1, keepdims=True)
        hn_of[sub] = (hres * lax.rsqrt(var + RMS_EPS)) * nrm_ref[...]

    def route(sub):
        hn = hn_of[sub]
        a_hi, a_lo = _split_bf16(hn)
        both = _dot_nt(wt_both, a_hi)
        logits = (both[:LANES] + both[LANES:] + _dot_nt(wt_both[:LANES], a_lo))[:N_EXPERTS] + bt_ref[...]
        eid = lax.broadcasted_iota(jnp.int32, (N_EXPERTS, 1), 0).astype(F32)
        work = logits
        vals, hots = [], []
        for _ in range(TOP_K):
            mx = jnp.max(work, axis=0, keepdims=True)
            idx = jnp.min(jnp.where(work == mx, eid, float(N_EXPERTS)), axis=0, keepdims=True)
            hot = eid == idx
            work = jnp.where(hot, NEG_BIG, work)
            vals.append(mx)
            hots.append(hot)
        exps = [jnp.exp(v - vals[0]) for v in vals]
        denom = exps[0] + exps[1] + exps[2] + exps[3]
        sel = jnp.zeros((N_EXPERTS, tm), F32)
        for hot in hots:
            sel = jnp.where(hot, 1.0, sel)
        sel_bf = sel.astype(BF16)
        r = lax.broadcasted_iota(jnp.int32, (tm, tm), 0)
        c = lax.broadcasted_iota(jnp.int32, (tm, tm), 1)
        earlier = jnp.dot(sel_bf, (r < c).astype(BF16), preferred_element_type=F32)
        cnt = jnp.sum(sel, axis=1, keepdims=True)
        padded = jnp.floor((cnt + (WIN_ROWS - 1)) * (1.0 / WIN_ROWS)) * WIN_ROWS
        er = lax.broadcasted_iota(jnp.int32, (N_EXPERTS, N_EXPERTS), 0)
        ec = lax.broadcasted_iota(jnp.int32, (N_EXPERTS, N_EXPERTS), 1)
        lower = jnp.dot((ec < er).astype(BF16), jnp.broadcast_to(padded, (N_EXPERTS, LANES)).astype(BF16),
                        preferred_element_type=F32)[:, 0:1]
        slot = lower + earlier
        slot_rows = [jnp.sum(jnp.where(hot, slot, 0.0), axis=0, keepdims=True) for hot in hots]
        rows = [e / denom for e in exps] + slot_rows
        route_t = jnp.concatenate(rows + [jnp.zeros((LANES - 2 * TOP_K, tm), F32)], axis=0)
        route_ref[sub * tm:(sub + 1) * tm] = route_t.T
        cnt_ref[sub] = _dot_nt(jnp.ones((8, tm), BF16),
                               jnp.concatenate([sel_bf, jnp.zeros((LANES - N_EXPERTS, tm), BF16)], axis=0))
        slots_of[sub] = slot_rows

    def sort_rows(sub):
        slots_t = slots_of[sub]
        hn_bf = hn_of[sub].astype(BF16)
        for ci in range(ls // chunk):
            rid = (lax.broadcasted_iota(jnp.int32, (chunk, 1), 0) + ci * chunk).astype(F32)
            onehot = rid == slots_t[0]
            for k in range(1, TOP_K):
                onehot = onehot | (rid == slots_t[k])
            xl_ref[sub * ls + ci * chunk:sub * ls + (ci + 1) * chunk, :] = jnp.dot(
                jnp.where(onehot, 1.0, 0.0).astype(BF16), hn_bf, preferred_element_type=F32).astype(xl_ref.dtype)

    stages = (dense, route, sort_rows)
    for step in range(POST_TILES_PER_STEP + len(stages) - 1):
        for k, stage in enumerate(stages):
            if 0 <= step - k < POST_TILES_PER_STEP:
                stage(step - k)


POST_TILES_PER_STEP = 2


def _post_attn(ysb, ymb, proj, xt, wsb, wmb, wo, nrm_g, wr_t, br_t, *, tm):
    T, D = xt.shape
    half = D // 2
    n_tt = T // tm
    ls = _local_slots(tm)
    sub = POST_TILES_PER_STEP
    const = lambda i: (0, 0)
    return pl.pallas_call(
        functools.partial(_post_attn_kernel, tm=tm, chunk=256),
        grid=(n_tt // sub,),
        in_specs=[pl.BlockSpec((sub * tm, D_MIX), lambda i: (i, 0)),
                  pl.BlockSpec((sub * tm, D_MIX), lambda i: (i, 0)),
                  pl.BlockSpec((2, sub * tm, half), lambda i: (3, i, 0)),
                  pl.BlockSpec((2, sub * tm, half), lambda i: (4, i, 0)),
                  pl.BlockSpec((sub * tm, D), lambda i: (i, 0)),
                  pl.BlockSpec(wsb.shape, const),
                  pl.BlockSpec(wmb.shape, const),
                  pl.BlockSpec(wo.shape, const),
                  pl.BlockSpec((1, D), const),
                  pl.BlockSpec(wr_t.shape, const),
                  pl.BlockSpec(br_t.shape, const)],
        out_specs=[pl.BlockSpec((sub * tm, D), lambda i: (i, 0)),
                   pl.BlockSpec((sub * ls, D), lambda i: (i, 0)),
                   pl.BlockSpec((sub * tm, LANES), lambda i: (i, 0)),
                   pl.BlockSpec((sub, 8, LANES), lambda i: (i, 0, 0))],
        out_shape=[jax.ShapeDtypeStruct((T, D), F32),
                   jax.ShapeDtypeStruct((n_tt * ls, D), BF16),
                   jax.ShapeDtypeStruct((T, LANES), F32),
                   jax.ShapeDtypeStruct((n_tt, 8, LANES), F32)],
        compiler_params=pltpu.CompilerParams(dimension_semantics=("arbitrary",),
                                             vmem_limit_bytes=VMEM_LIMIT),
        name="post_attn",
    )(ysb, ymb, proj, proj, xt, wsb, wmb, wo, nrm_g.reshape(1, D), wr_t, br_t)


def _window_plan(counts, tm):
    n_tt, n_exp = counts.shape
    lw_tile = _local_slots(tm) // WIN_ROWS
    n_tiles = -(-(n_tt * (tm * TOP_K // WIN_ROWS + n_exp)) // WINS_PER_TILE) + n_exp
    n_tiles += n_tiles % 2
    nw = (counts + WIN_ROWS - 1) // WIN_ROWS
    local_start = jnp.cumsum(nw, axis=1) - nw
    per_expert = jnp.sum(nw, axis=0)
    per_expert_pad = -(-per_expert // WINS_PER_TILE) * WINS_PER_TILE
    expert_end = jnp.cumsum(per_expert_pad)
    sorted_start = (expert_end - per_expert_pad)[None, :] + jnp.cumsum(nw, axis=0) - nw
    run_start = sorted_start.T.reshape(-1)
    run_len = nw.T.reshape(-1)
    run_src = (jnp.arange(n_tt, dtype=jnp.int32)[:, None] * lw_tile + local_start).T.reshape(-1)
    g = jnp.arange(n_tiles * WINS_PER_TILE, dtype=jnp.int32)[:, None]
    off = g - run_start[None, :]
    src_win = jnp.sum(jnp.where((off >= 0) & (off < run_len[None, :]), run_src[None, :] + off, 0), axis=1)
    off_l = jnp.arange(lw_tile, dtype=jnp.int32)[None, :, None] - local_start[:, None, :]
    back_win = jnp.sum(jnp.where((off_l >= 0) & (off_l < nw[:, None, :]), sorted_start[:, None, :] + off_l, 0),
                       axis=2)
    tile_first = (jnp.arange(n_tiles, dtype=jnp.int32) * WINS_PER_TILE)[:, None]
    tile_valid = tile_first[:, 0] < expert_end[-1]
    used = (per_expert > 0)[None, :]
    expert_start = (expert_end - per_expert_pad)[None, :]
    ids = jnp.arange(n_exp, dtype=jnp.int32)[None, :]
    tile_expert = jnp.minimum(jnp.sum(expert_end[None, :] <= tile_first, axis=1), n_exp - 1)
    opens = jnp.sum(used & (expert_start == tile_first), axis=1) > 0
    buffer = (jnp.sum(used & (expert_start <= tile_first), axis=1) - 1) % 2
    next_expert = jnp.min(jnp.where(used & (expert_start > tile_first), ids, n_exp), axis=1)
    next_expert = jnp.where(next_expert == n_exp, -1, next_expert)
    tile_plan = tuple(a.astype(jnp.int32) for a in (tile_expert, tile_valid, opens, buffer, next_expert))
    return src_win.astype(jnp.int32), back_win.astype(jnp.int32), tile_plan


EXPERT_TILES_PER_STEP = 2


def _experts_kernel(te_ref, tv_ref, first_ref, par_ref, ne_ref, src_ref, nxt_ref, xl_ref, wgu_ref, bgu_a, bgu_b,
                    wd_ref, bd_a, bd_b, y_ref, xbuf, wgu_f, wd_f, wgu_s, wd_s, sem, wsem, *, d_exp):
    step = pl.program_id(0)
    n_steps = pl.num_programs(0)
    slot = step % 2
    ta, tb = 2 * step, 2 * step + 1
    tmm = WINS_PER_TILE * WIN_ROWS
    n_win = EXPERT_TILES_PER_STEP * WINS_PER_TILE

    def window_copy(win, dst_slot, w):
        return pltpu.make_async_copy(xl_ref.at[win], xbuf.at[dst_slot, w], sem.at[dst_slot])

    def start_gather(idx_ref, dst_slot):
        for w in range(n_win):
            window_copy(idx_ref[0, 0, w], dst_slot, w).start()

    def weight_copies(expert, buf):
        return (pltpu.make_async_copy(wgu_ref.at[expert], wgu_f.at[buf], wsem.at[buf, 0]),
                pltpu.make_async_copy(wd_ref.at[expert], wd_f.at[buf], wsem.at[buf, 1]))

    @pl.when((step == 0) & (tv_ref[0] > 0))
    def _():
        for cp in weight_copies(te_ref[0], par_ref[0]):
            cp.start()
        start_gather(src_ref, 0)

    @pl.when((step + 1 < n_steps) & (tv_ref[jnp.minimum(ta + 2, 2 * n_steps - 1)] > 0))
    def _():
        start_gather(nxt_ref, 1 - slot)

    def switch_weights(t):
        @pl.when((first_ref[t] > 0) & (tv_ref[t] > 0))
        def _():
            buf = par_ref[t]

            @pl.when(ne_ref[t] >= 0)
            def _():
                for cp in weight_copies(ne_ref[t], 1 - buf):
                    cp.start()

            for cp in weight_copies(te_ref[t], buf):
                cp.wait()
            wgu_s[buf] = wgu_f[buf].astype(BF16)
            wd_s[buf] = wd_f[buf].astype(BF16)

    switch_weights(ta)
    switch_weights(tb)

    def ffn(tiles):
        xs = [xbuf[slot, w0:w0 + WINS_PER_TILE].reshape(tmm, xbuf.shape[-1]) for w0, _, _, _ in tiles]
        hgu = [jnp.dot(x, wgu_s[buf], preferred_element_type=F32) + bias[0][0]
               for x, (_, buf, bias, _) in zip(xs, tiles)]
        acts = []
        for h in hgu:
            gate = jnp.minimum(h[:, :d_exp], SWIGLU_LIMIT)
            up = jnp.clip(h[:, d_exp:], -SWIGLU_LIMIT, SWIGLU_LIMIT)
            acts.append(((up + 1.0) * (gate * (1.0 / (1.0 + jnp.exp(-SWIGLU_ALPHA * gate))))).astype(BF16))
        for act, (_, buf, bias, row0) in zip(acts, tiles):
            y = jnp.dot(act, wd_s[buf], preferred_element_type=F32) + bias[1][0]
            y_ref[row0:row0 + tmm] = y.astype(y_ref.dtype)

    def wait_windows():
        pltpu.make_async_copy(xl_ref.at[pl.ds(0, n_win)], xbuf.at[slot], sem.at[slot]).wait()

    tile_a = (0, par_ref[ta], (bgu_a, bd_a), 0)
    tile_b = (WINS_PER_TILE, par_ref[tb], (bgu_b, bd_b), tmm)

    @pl.when(tv_ref[tb] > 0)
    def _():
        wait_windows()
        ffn([tile_a, tile_b])

    @pl.when((tv_ref[ta] > 0) & (tv_ref[tb] == 0))
    def _():
        wait_windows()
        ffn([tile_a])
        y_ref[tmm:] = jnp.zeros((tmm, y_ref.shape[-1]), y_ref.dtype)

    @pl.when(tv_ref[ta] == 0)
    def _():
        y_ref[...] = jnp.zeros_like(y_ref)


def _experts(xl, src_win, tile_plan, wgu, bgu, wd, bd):
    D = xl.shape[1]
    E, _, d2 = wgu.shape
    d_exp = d2 // 2
    sub = EXPERT_TILES_PER_STEP
    n_steps = tile_plan[0].shape[0] // sub
    tmm = WINS_PER_TILE * WIN_ROWS
    src3 = src_win.reshape(n_steps, 1, sub * WINS_PER_TILE)
    grid_spec = pltpu.PrefetchScalarGridSpec(
        num_scalar_prefetch=len(tile_plan),
        grid=(n_steps,),
        in_specs=[pl.BlockSpec((1, 1, sub * WINS_PER_TILE), lambda s, *_: (s, 0, 0), memory_space=pltpu.SMEM),
                  pl.BlockSpec((1, 1, sub * WINS_PER_TILE), lambda s, *_: (jnp.minimum(s + 1, n_steps - 1), 0, 0),
                               memory_space=pltpu.SMEM),
                  pl.BlockSpec(memory_space=pl.ANY),
                  pl.BlockSpec(memory_space=pl.ANY),
                  pl.BlockSpec((1, 1, d2), lambda s, te, *_: (te[2 * s], 0, 0)),
                  pl.BlockSpec((1, 1, d2), lambda s, te, *_: (te[2 * s + 1], 0, 0)),
                  pl.BlockSpec(memory_space=pl.ANY),
                  pl.BlockSpec((1, 1, D), lambda s, te, *_: (te[2 * s], 0, 0)),
                  pl.BlockSpec((1, 1, D), lambda s, te, *_: (te[2 * s + 1], 0, 0))],
        out_specs=pl.BlockSpec((sub * tmm, D), lambda s, *_: (s, 0)),
        scratch_shapes=[pltpu.VMEM((2, sub * WINS_PER_TILE, WIN_ROWS, D), BF16),
                        pltpu.VMEM((2, D, d2), F32), pltpu.VMEM((2, d_exp, D), F32),
                        pltpu.VMEM((2, D, d2), BF16), pltpu.VMEM((2, d_exp, D), BF16),
                        pltpu.SemaphoreType.DMA((2,)), pltpu.SemaphoreType.DMA((2, 2))],
    )
    bgu3, bd3 = bgu.reshape(E, 1, d2), bd.reshape(E, 1, D)
    return pl.pallas_call(
        functools.partial(_experts_kernel, d_exp=d_exp),
        grid_spec=grid_spec,
        out_shape=jax.ShapeDtypeStruct((n_steps * sub * tmm, D), BF16),
        compiler_params=pltpu.CompilerParams(dimension_semantics=("arbitrary",),
                                             vmem_limit_bytes=VMEM_LIMIT),
        name="experts",
    )(*tile_plan, src3, src3, xl.reshape(-1, WIN_ROWS, D), wgu, bgu3, bgu3, wd, bd3, bd3)


def _combine_kernel(back_ref, nxt_ref, y_ref, h_ref, route_ref, g_ref, o_ref, ybuf, sem, *, tm, chunk):
    i = pl.program_id(0)
    slot = i % 2
    n_win = ybuf.shape[1]
    tile_win = n_win // COMBINE_TILES_PER_STEP

    def window_copy(win, dst_slot, w):
        return pltpu.make_async_copy(y_ref.at[win], ybuf.at[dst_slot, w], sem.at[dst_slot])

    def start_fetch(idx_ref, dst_slot):
        for w in range(n_win):
            window_copy(idx_ref[0, 0, w], dst_slot, w).start()

    @pl.when(i == 0)
    def _():
        start_fetch(back_ref, 0)

    @pl.when(i + 1 < pl.num_programs(0))
    def _():
        start_fetch(nxt_ref, 1 - slot)

    tiles = [slice(t * tm, (t + 1) * tm) for t in range(COMBINE_TILES_PER_STEP)]
    routes = [route_ref[rows] for rows in tiles]
    weights = [[route[:, k:k + 1] for k in range(TOP_K)] for route in routes]
    slots = [[route[:, TOP_K + k:TOP_K + k + 1] for k in range(TOP_K)] for route in routes]
    pltpu.make_async_copy(y_ref.at[pl.ds(0, n_win)], ybuf.at[slot], sem.at[slot]).wait()
    acc = [h_ref[rows] for rows in tiles]
    for ci in range(tile_win * WIN_ROWS // chunk):
        sid = (lax.broadcasted_iota(jnp.int32, (1, chunk), 1) + ci * chunk).astype(F32)
        for t in range(COMBINE_TILES_PER_STEP):
            wmat = jnp.zeros((tm, chunk), F32)
            for k in range(TOP_K):
                wmat = jnp.where(sid == slots[t][k], weights[t][k], wmat)
            hi, lo = _split_bf16(wmat)
            first = t * tile_win + ci * chunk // WIN_ROWS
            y_rows = ybuf[slot, first:first + chunk // WIN_ROWS].reshape(chunk, ybuf.shape[-1])
            acc[t] = (acc[t] + jnp.dot(hi, y_rows, preferred_element_type=F32)
                      + jnp.dot(lo, y_rows, preferred_element_type=F32))
    for t, rows in enumerate(tiles):
        var = jnp.mean(acc[t] * acc[t], axis=-1, keepdims=True)
        o_ref[rows] = (acc[t] * lax.rsqrt(var + RMS_EPS)) * g_ref[...]


COMBINE_TILES_PER_STEP = 2


def _combine(y, back_win, h, route, g, *, tm):
    T, D = h.shape
    sub = COMBINE_TILES_PER_STEP
    n_steps, n_win = back_win.shape[0] // sub, back_win.shape[1] * sub
    back3 = back_win.reshape(n_steps, 1, n_win)
    return pl.pallas_call(
        functools.partial(_combine_kernel, tm=tm, chunk=256),
        grid=(n_steps,),
        in_specs=[pl.BlockSpec((1, 1, n_win), lambda i: (i, 0, 0), memory_space=pltpu.SMEM),
                  pl.BlockSpec((1, 1, n_win), lambda i: (jnp.minimum(i + 1, n_steps - 1), 0, 0),
                               memory_space=pltpu.SMEM),
                  pl.BlockSpec(memory_space=pl.ANY),
                  pl.BlockSpec((sub * tm, D), lambda i: (i, 0)),
                  pl.BlockSpec((sub * tm, LANES), lambda i: (i, 0)),
                  pl.BlockSpec((1, D), lambda i: (0, 0))],
        out_specs=pl.BlockSpec((sub * tm, D), lambda i: (i, 0)),
        out_shape=jax.ShapeDtypeStruct((T, D), F32),
        scratch_shapes=[pltpu.VMEM((2, n_win, WIN_ROWS, D), BF16), pltpu.SemaphoreType.DMA((2,))],
        compiler_params=pltpu.CompilerParams(dimension_semantics=("arbitrary",),
                                             vmem_limit_bytes=VMEM_LIMIT),
        name="combine",
    )(back3, back3, y.reshape(-1, WIN_ROWS, D), h, route, g.reshape(1, D))


def kernel(x, norm_mix_g, w_in, w_proj_sb, w_proj_moba, w_out, rel_bias, norm_ffn_g, w_router, b_router,
           w_gate_up, b_gate_up, w_down, b_down, norm_final_g):
    batch, seq, D = x.shape
    assert norm_mix_g.shape[0] == 1, "single-layer trunk"
    tm = 256
    xt = x.reshape(batch * seq, D)
    col_scale = np.ones((w_in.shape[-1],), np.float32)
    col_scale[0:D_MIX] = QUERY_SCALE
    col_scale[3 * D_MIX:4 * D_MIX] = QUERY_SCALE
    proj = _in_proj(xt, norm_mix_g[0], (w_in[0] * col_scale).astype(BF16))
    ysb = _sb_attention(proj, batch=batch, seq=seq)
    ymb = _moba_attention(proj, _moba_bias(rel_bias), rel_bias, batch=batch, seq=seq)
    wr_t = jnp.pad(w_router[0].T, ((0, LANES - N_EXPERTS), (0, 0)))
    br_t = b_router[0].reshape(N_EXPERTS, 1)
    h, xl, route, cnt = _post_attn(ysb, ymb, proj, xt, w_proj_sb[0].astype(BF16), w_proj_moba[0].astype(BF16),
                                   w_out[0].astype(BF16), norm_ffn_g[0], wr_t, br_t, tm=tm)
    src_win, back_win, tile_plan = _window_plan(cnt[:, 0, :N_EXPERTS].astype(jnp.int32), tm)
    y = _experts(xl, src_win, tile_plan, w_gate_up[0], b_gate_up[0], w_down[0], b_down[0])
    out = _combine(y, back_win, h, route, norm_final_g, tm=tm)
    return out.reshape(batch, seq, D)
```

```python
import functools
import math

import numpy as np
import jax
import jax.numpy as jnp
from jax import lax
from jax.experimental import pallas as pl
from jax.experimental.pallas import tpu as pltpu

HEAD_DIM = 64
N_HEADS = 8
D_MIX = N_HEADS * HEAD_DIM
MOBA_BLOCK = 256
MOBA_TOPK = 3
REL_BUCKETS = 32
REL_MAX_DIST = 128
N_EXPERTS = 32
TOP_K = 4
SWIGLU_LIMIT = 7.0
SWIGLU_ALPHA = 1.702
RMS_EPS = 1e-6

LANES = 128
SUBLANES = 8
BF16_TILE_ROWS = 16
NEG_BIG = -1e30
LOG2_E = math.log2(math.e)
QUERY_SCALE = LOG2_E / math.sqrt(HEAD_DIM)
VMEM_LIMIT = 56 * 1024 * 1024

F32 = jnp.float32
BF16 = jnp.bfloat16


def _split_bf16(a):
    hi = a.astype(BF16)
    lo = (a - hi.astype(F32)).astype(BF16)
    return hi, lo


def _dot_nt(a, b):
    return lax.dot_general(a, b, (((1,), (1,)), ((), ())), preferred_element_type=F32)


def _in_proj_kernel(x_ref, g_ref, w_ref, o_ref, *, n_slab, slab):
    x = x_ref[...]
    var = jnp.mean(x * x, axis=-1, keepdims=True)
    xn = ((x * lax.rsqrt(var + RMS_EPS)) * g_ref[...]).astype(BF16)
    for j in range(n_slab):
        o_ref[j] = jnp.dot(xn, w_ref[:, j * slab:(j + 1) * slab],
                           preferred_element_type=F32).astype(o_ref.dtype)


def _in_proj(xt, g, w_bf16, *, tm=1024, slab=D_MIX):
    T, D = xt.shape
    n_slab = w_bf16.shape[1] // slab
    return pl.pallas_call(
        functools.partial(_in_proj_kernel, n_slab=n_slab, slab=slab),
        grid=(T // tm,),
        in_specs=[pl.BlockSpec((tm, D), lambda i: (i, 0)),
                  pl.BlockSpec((1, D), lambda i: (0, 0)),
                  pl.BlockSpec(w_bf16.shape, lambda i: (0, 0))],
        out_specs=pl.BlockSpec((n_slab, tm, slab), lambda i: (0, i, 0)),
        out_shape=jax.ShapeDtypeStruct((n_slab, T, slab), BF16),
        compiler_params=pltpu.CompilerParams(dimension_semantics=("arbitrary",),
                                             vmem_limit_bytes=VMEM_LIMIT),
        name="in_proj",
    )(xt, g.reshape(1, D), w_bf16)


def _sb_kernel(q_ref, k_ref, v_ref, o_ref, *, tq, tk):
    i = pl.program_id(2)
    lane = lax.broadcasted_iota(jnp.int32, (1, LANES), 1)
    jj = lax.broadcasted_iota(jnp.int32, (tk, tk), 0)
    ss = lax.broadcasted_iota(jnp.int32, (tk, tk), 1)
    later = (jj >= ss).astype(BF16)
    n_sub = tq // tk
    streams = range(SB_HEADS_PER_STEP)

    def lanes_of(head):
        return slice((head // 2) * LANES, (head // 2 + 1) * LANES)

    qhs = []
    for head in streams:
        q = q_ref[0, :, lanes_of(head)]
        qhs.append(jnp.where(lane // HEAD_DIM == head % 2, q, jnp.zeros((), q.dtype)))

    def q_tile_keys(base, carry, diagonal):
        tiles = [(h, sub) for sub in reversed(range(n_sub)) for h in streams]
        starts = {sub: pl.multiple_of(base + sub * tk, tk) for sub in range(n_sub)}
        c = [list(carry[h][0]) for h in streams]
        acc = [carry[h][1] for h in streams]
        z2, drop, inner, pv = {}, {}, {}, {}

        def first_row(t):
            return t[1] * tk if diagonal else 0

        def causal(t):
            shape = (tq - first_row(t), tk)
            return lax.broadcasted_iota(jnp.int32, shape, 1) < lax.broadcasted_iota(jnp.int32, shape, 0)

        def scores(t):
            z2[t] = _dot_nt(qhs[t[0]][first_row(t):], k_ref[0, pl.ds(starts[t[1]], tk), lanes_of(t[0])])

        def drops(t):
            d = jnp.maximum(z2[t], 0.0) + jnp.log2(1.0 + jnp.exp2(-jnp.abs(z2[t])))
            if diagonal:
                d = jnp.where(causal(t), d, 0.0)
            drop[t] = d
            inner[t] = jnp.dot(d.astype(BF16), later, preferred_element_type=F32)

        def weights(t):
            h, b0 = t[0], first_row(t) // tk
            blocks = [slice((b - b0) * tk, (b - b0 + 1) * tk) for b in range(b0, n_sub)]
            w = [jnp.exp2((z2[t][rows] - c[h][b0 + n]) - inner[t][rows]) for n, rows in enumerate(blocks)]
            w = w[0] if len(w) == 1 else jnp.concatenate(w, axis=0)
            if diagonal:
                w = jnp.where(causal(t), w, 0.0)
            for n, rows in enumerate(blocks):
                c[h][b0 + n] = c[h][b0 + n] + jnp.sum(drop[t][rows], axis=-1, keepdims=True)
            pv[t] = jnp.dot(w.astype(BF16), v_ref[0, pl.ds(starts[t[1]], tk), lanes_of(h)],
                            preferred_element_type=F32)

        def accumulate(t):
            h, r0 = t[0], first_row(t)
            acc[h] = acc[h] + pv[t] if r0 == 0 else jnp.concatenate([acc[h][:r0], acc[h][r0:] + pv[t]], axis=0)

        stages = (scores, drops, weights, accumulate)
        for step in range(len(tiles) + len(stages) - 1):
            for k, stage in enumerate(stages):
                if 0 <= step - k < len(tiles):
                    stage(tiles[step - k])
        return tuple((tuple(c[h]), acc[h]) for h in streams)

    init = tuple((tuple(jnp.zeros((tk, 1), F32) for _ in range(n_sub)), jnp.zeros((tq, LANES), F32))
                 for _ in streams)
    carry = q_tile_keys(i * tq, init, True)
    carry = lax.fori_loop(0, i, lambda it, cr: q_tile_keys((i - 1 - it) * tq, cr, False), carry)
    for g in range(SB_HEADS_PER_STEP // 2):
        o_ref[:, g * LANES:(g + 1) * LANES] = jnp.where(lane < HEAD_DIM, carry[2 * g][1],
                                                        carry[2 * g + 1][1]).astype(o_ref.dtype)


SB_HEADS_PER_STEP = 8


def _sb_attention(proj, *, batch, seq, tq=512, tk=256):
    T = batch * seq
    nq = seq // tq
    width = SB_HEADS_PER_STEP * HEAD_DIM
    return pl.pallas_call(
        functools.partial(_sb_kernel, tq=tq, tk=tk),
        grid=(batch, D_MIX // width, nq),
        in_specs=[pl.BlockSpec((1, tq, width), lambda b, p, i: (0, b * nq + i, p)),
                  pl.BlockSpec((1, seq, width), lambda b, p, i: (1, b, p)),
                  pl.BlockSpec((1, seq, width), lambda b, p, i: (2, b, p))],
        out_specs=pl.BlockSpec((tq, width), lambda b, p, i: (b * nq + i, p)),
        out_shape=jax.ShapeDtypeStruct((T, D_MIX), BF16),
        compiler_params=pltpu.CompilerParams(
            dimension_semantics=("arbitrary", "arbitrary", "arbitrary"), vmem_limit_bytes=VMEM_LIMIT),
        name="sb_attn",
    )(proj, proj, proj)


def _rel_bucket_np(dist):
    n = np.maximum(dist, 0)
    max_exact = REL_BUCKETS // 2
    nf = np.maximum(n, 1).astype(np.float64)
    large = max_exact + (np.log(nf / max_exact) / math.log(REL_MAX_DIST / max_exact)
                         * (REL_BUCKETS - max_exact)).astype(np.int32)
    large = np.minimum(large, REL_BUCKETS - 1)
    return np.where(n < max_exact, n, large).astype(np.int32)


def _bias_kernel(rel_ref, bkt_ref, o_ref):
    h = pl.program_id(0)
    for d in range(2):
        bkt = bkt_ref[d]
        acc = jnp.zeros(bkt.shape, F32)
        for b in range(REL_BUCKETS):
            acc = jnp.where(bkt == b, rel_ref[b, h], acc)
        o_ref[0, d] = acc * LOG2_E


def _moba_bias(rel_bias):
    L = MOBA_BLOCK
    r = np.arange(L)[:, None]
    c = np.arange(L)[None, :]
    bkt = np.stack([_rel_bucket_np(r - c), _rel_bucket_np(L + r - c)])
    return pl.pallas_call(
        _bias_kernel,
        grid=(N_HEADS,),
        in_specs=[pl.BlockSpec(memory_space=pltpu.SMEM),
                  pl.BlockSpec((2, L, L), lambda h: (0, 0, 0))],
        out_specs=pl.BlockSpec((1, 2, L, L), lambda h: (h, 0, 0, 0)),
        out_shape=jax.ShapeDtypeStruct((N_HEADS, 2, L, L), F32),
        name="moba_bias",
    )(rel_bias, jnp.asarray(bkt))


def _moba_kernel(rel_ref, q_ref, k_ref, v_ref, bias_ref, o_ref, km_ref, *, L, nblk):
    p = pl.program_id(1)
    i = pl.program_id(2)
    lane = lax.broadcasted_iota(jnp.int32, (1, LANES), 1)
    row = lax.broadcasted_iota(jnp.int32, (L, L), 0)
    col = lax.broadcasted_iota(jnp.int32, (L, L), 1)

    @pl.when(i == 0)
    def _():
        km_ref[...] = jnp.mean(k_ref[0].astype(F32).reshape(nblk, L, k_ref.shape[-1]), axis=1)

    def attend(n_off):
        rr = lax.broadcasted_iota(jnp.int32, (n_off, 1), 0)
        km = jnp.concatenate([km_ref[pl.ds(jnp.maximum(i - r, 0), 1), :] for r in range(n_off)], axis=0)
        km = jnp.concatenate([km, jnp.zeros((BF16_TILE_ROWS - n_off, km.shape[-1]), F32)], axis=0)
        km_hi, km_lo = _split_bf16(km)
        km_lo2 = (km - km_hi.astype(F32) - km_lo.astype(F32)).astype(BF16)
        past = (rr >= 1) & (rr <= i)
        outs, all_scores, all_values = [], [], []
        for head in range(MOBA_HEADS_PER_STEP):
            h = head % 2
            group = slice((head // 2) * LANES, (head // 2 + 1) * LANES)
            q = q_ref[0, :, group]
            own = lane // HEAD_DIM == h
            aux0 = (1 - h) * HEAD_DIM
            qh = jnp.where(own, q, jnp.zeros((), q.dtype))
            gs = (_dot_nt(km_hi[:, group], qh) + _dot_nt(km_lo[:, group], qh)
                  + _dot_nt(km_lo2[:, group], qh))[:n_off]
            pen = jnp.zeros((n_off, L), F32)
            for r in range(1, n_off):
                g_r = gs[r:r + 1, :]
                beats = past & ((gs > g_r) | ((gs == g_r) & (rr > r)))
                cnt = jnp.sum(beats.astype(F32), axis=0, keepdims=True)
                keep = (cnt < MOBA_TOPK) & (r <= i)
                pen = jnp.where((rr == r) & jnp.logical_not(keep), NEG_BIG, pen)
            far = jnp.full((n_off, L), rel_ref[REL_BUCKETS - 1, MOBA_HEADS_PER_STEP * p + head] * LOG2_E, F32)
            f_hi, f_lo = _split_bf16(far)
            f_lo2 = far - f_hi.astype(F32) - f_lo.astype(F32)
            far3 = jnp.where(rr == 0, f_hi.astype(F32), jnp.where(rr == 1, f_lo.astype(F32),
                                                                  jnp.where(rr == 2, f_lo2, 0.0)))
            aux = [pen, far3, jnp.zeros((LANES - aux0 - 2 * n_off, L), F32)]
            aux = jnp.concatenate(([jnp.zeros((aux0, L), F32)] if aux0 else []) + aux, axis=0)
            q_aug = jnp.where(own, q, aux.T.astype(q.dtype))
            scores, values = [], []
            for r in range(n_off):
                ks = pl.multiple_of(jnp.maximum(i - r, 0) * L, L)
                kblk = k_ref[0, pl.ds(ks, L), group]
                values.append(v_ref[0, pl.ds(ks, L), group])
                if r == 0:
                    s = _dot_nt(qh, kblk) + bias_ref[head, 0]
                    s = jnp.where(col <= row, s, NEG_BIG)
                else:
                    hot = (lane == aux0 + r)
                    if r >= 2:
                        hot = hot | ((lane >= aux0 + n_off) & (lane < aux0 + n_off + 3))
                    k_aug = jnp.where(own, kblk, jnp.where(hot, 1.0, 0.0).astype(kblk.dtype))
                    s = _dot_nt(q_aug, k_aug)
                    if r == 1:
                        s = s + bias_ref[head, 1]
                scores.append(s)
            all_scores.append(scores)
            all_values.append(values)
        for scores, values in zip(all_scores, all_values):
            m = scores[0]
            for s in scores[1:]:
                m = jnp.maximum(m, s)
            m = jnp.max(m, axis=-1, keepdims=True)
            probs = [jnp.exp2(s - m) for s in scores]
            l = probs[0]
            for pr in probs[1:]:
                l = l + pr
            l = jnp.sum(l, axis=-1, keepdims=True)
            acc = jnp.dot(probs[0].astype(BF16), values[0], preferred_element_type=F32)
            for pr, vblk in zip(probs[1:], values[1:]):
                acc = acc + jnp.dot(pr.astype(BF16), vblk, preferred_element_type=F32)
            outs.append(acc / l)
        for g in range(MOBA_HEADS_PER_STEP // 2):
            o_ref[:, g * LANES:(g + 1) * LANES] = jnp.where(lane < HEAD_DIM, outs[2 * g],
                                                            outs[2 * g + 1]).astype(o_ref.dtype)

    for n_off in range(1, nblk + 1):
        pl.when(i == n_off - 1)(functools.partial(attend, n_off))


MOBA_HEADS_PER_STEP = 4


def _moba_attention(proj, bias, rel_bias, *, batch, seq):
    T = batch * seq
    L = MOBA_BLOCK
    nblk = seq // L
    assert seq % (2 * L) == 0, "query blocks are handled in pairs"
    width = MOBA_HEADS_PER_STEP * HEAD_DIM
    return pl.pallas_call(
        functools.partial(_moba_kernel, L=L, nblk=nblk),
        grid=(batch, D_MIX // width, nblk),
        in_specs=[pl.BlockSpec(memory_space=pltpu.SMEM),
                  pl.BlockSpec((1, L, width), lambda b, p, i: (3, b * nblk + i, p)),
                  pl.BlockSpec((1, seq, width), lambda b, p, i: (4, b, p)),
                  pl.BlockSpec((1, seq, width), lambda b, p, i: (5, b, p)),
                  pl.BlockSpec((MOBA_HEADS_PER_STEP, 2, L, L), lambda b, p, i: (p, 0, 0, 0))],
        out_specs=pl.BlockSpec((L, width), lambda b, p, i: (b * nblk + i, p)),
        out_shape=jax.ShapeDtypeStruct((T, D_MIX), BF16),
        scratch_shapes=[pltpu.VMEM((nblk, width), F32)],
        compiler_params=pltpu.CompilerParams(
            dimension_semantics=("arbitrary", "arbitrary", "arbitrary"), vmem_limit_bytes=VMEM_LIMIT),
        name="moba_attn",
    )(rel_bias, proj, proj, proj, bias)


WIN_ROWS = BF16_TILE_ROWS
WINS_PER_TILE = 32


def _local_slots(tm):
    return tm * TOP_K + N_EXPERTS * WIN_ROWS


def _post_attn_kernel(ysb_ref, ymb_ref, gsb_ref, gmb_ref, x_ref, wsb_ref, wmb_ref, wo_ref, nrm_ref,
                      wt_ref, bt_ref, h_ref, xl_ref, route_ref, cnt_ref, *, tm, chunk):
    ls = _local_slots(tm)
    wt_both = jnp.concatenate(_split_bf16(wt_ref[...]), axis=0)
    hn_of, slots_of = {}, {}

    def dense(sub):
        rows = slice(sub * tm, (sub + 1) * tm)

        def gate(g_ref):
            g = jnp.concatenate([g_ref[0, rows], g_ref[1, rows]], axis=1).astype(F32)
            return 1.0 / (1.0 + jnp.exp(-g))

        merged = (gate(gsb_ref) * jnp.dot(ysb_ref[rows], wsb_ref[...], preferred_element_type=F32)
                  + gate(gmb_ref) * jnp.dot(ymb_ref[rows], wmb_ref[...], preferred_element_type=F32))
        hres = x_ref[rows] + jnp.dot(merged.astype(BF16), wo_ref[...], preferred_element_type=F32)
        h_ref[rows] = hres
        var = jnp.mean(hres * hres, axis=-1, keepdims=True)
        hn_of[sub] = (hres * lax.rsqrt(var + RMS_EPS)) * nrm_ref[...]

    def route(sub):
        hn = hn_of[sub]
        a_hi, a_lo = _split_bf16(hn)
        both = _dot_nt(wt_both, a_hi)
        logits = (both[:LANES] + both[LANES:] + _dot_nt(wt_both[:LANES], a_lo))[:N_EXPERTS] + bt_ref[...]
        eid = lax.broadcasted_iota(jnp.int32, (N_EXPERTS, 1), 0).astype(F32)
        work = logits
        vals, hots = [], []
        for _ in range(TOP_K):
            mx = jnp.max(work, axis=0, keepdims=True)
            idx = jnp.min(jnp.where(work == mx, eid, float(N_EXPERTS)), axis=0, keepdims=True)
            hot = eid == idx
            work = jnp.where(hot, NEG_BIG, work)
            vals.append(mx)
            hots.append(hot)
        exps = [jnp.exp(v - vals[0]) for v in vals]
        denom = exps[0] + exps[1] + exps[2] + exps[3]
        sel = jnp.zeros((N_EXPERTS, tm), F32)
        for hot in hots:
            sel = jnp.where(hot, 1.0, sel)
        sel_bf = sel.astype(BF16)
        r = lax.broadcasted_iota(jnp.int32, (tm, tm), 0)
        c = lax.broadcasted_iota(jnp.int32, (tm, tm), 1)
        earlier = jnp.dot(sel_bf, (r < c).astype(BF16), preferred_element_type=F32)
        cnt = jnp.sum(sel, axis=1, keepdims=True)
        padded = jnp.floor((cnt + (WIN_ROWS - 1)) * (1.0 / WIN_ROWS)) * WIN_ROWS
        er = lax.broadcasted_iota(jnp.int32, (N_EXPERTS, N_EXPERTS), 0)
        ec = lax.broadcasted_iota(jnp.int32, (N_EXPERTS, N_EXPERTS), 1)
        lower = jnp.dot((ec < er).astype(BF16), jnp.broadcast_to(padded, (N_EXPERTS, LANES)).astype(BF16),
                        preferred_element_type=F32)[:, 0:1]
        slot = lower + earlier
        slot_rows = [jnp.sum(jnp.where(hot, slot, 0.0), axis=0, keepdims=True) for hot in hots]
        rows = [e / denom for e in exps] + slot_rows
        route_t = jnp.concatenate(rows + [jnp.zeros((LANES - 2 * TOP_K, tm), F32)], axis=0)
        route_ref[sub * tm:(sub + 1) * tm] = route_t.T
        cnt_ref[sub] = _dot_nt(jnp.ones((SUBLANES, tm), BF16),
                               jnp.concatenate([sel_bf, jnp.zeros((LANES - N_EXPERTS, tm), BF16)], axis=0))
        slots_of[sub] = slot_rows

    def sort_rows(sub):
        slots_t = slots_of[sub]
        hn_bf = hn_of[sub].astype(BF16)
        for ci in range(ls // chunk):
            rid = (lax.broadcasted_iota(jnp.int32, (chunk, 1), 0) + ci * chunk).astype(F32)
            onehot = rid == slots_t[0]
            for k in range(1, TOP_K):
                onehot = onehot | (rid == slots_t[k])
            xl_ref[sub * ls + ci * chunk:sub * ls + (ci + 1) * chunk, :] = jnp.dot(
                jnp.where(onehot, 1.0, 0.0).astype(BF16), hn_bf, preferred_element_type=F32).astype(xl_ref.dtype)

    stages = (dense, route, sort_rows)
    for step in range(POST_TILES_PER_STEP + len(stages) - 1):
        for k, stage in enumerate(stages):
            if 0 <= step - k < POST_TILES_PER_STEP:
                stage(step - k)


POST_TILES_PER_STEP = 2


def _post_attn(ysb, ymb, proj, xt, wsb, wmb, wo, nrm_g, wr_t, br_t, *, tm):
    T, D = xt.shape
    half = D // 2
    n_tt = T // tm
    ls = _local_slots(tm)
    sub = POST_TILES_PER_STEP
    const = lambda i: (0, 0)
    return pl.pallas_call(
        functools.partial(_post_attn_kernel, tm=tm, chunk=256),
        grid=(n_tt // sub,),
        in_specs=[pl.BlockSpec((sub * tm, D_MIX), lambda i: (i, 0)),
                  pl.BlockSpec((sub * tm, D_MIX), lambda i: (i, 0)),
                  pl.BlockSpec((2, sub * tm, half), lambda i: (3, i, 0)),
                  pl.BlockSpec((2, sub * tm, half), lambda i: (4, i, 0)),
                  pl.BlockSpec((sub * tm, D), lambda i: (i, 0)),
                  pl.BlockSpec(wsb.shape, const),
                  pl.BlockSpec(wmb.shape, const),
                  pl.BlockSpec(wo.shape, const),
                  pl.BlockSpec((1, D), const),
                  pl.BlockSpec(wr_t.shape, const),
                  pl.BlockSpec(br_t.shape, const)],
        out_specs=[pl.BlockSpec((sub * tm, D), lambda i: (i, 0)),
                   pl.BlockSpec((sub * ls, D), lambda i: (i, 0)),
                   pl.BlockSpec((sub * tm, LANES), lambda i: (i, 0)),
                   pl.BlockSpec((sub, SUBLANES, LANES), lambda i: (i, 0, 0))],
        out_shape=[jax.ShapeDtypeStruct((T, D), F32),
                   jax.ShapeDtypeStruct((n_tt * ls, D), BF16),
                   jax.ShapeDtypeStruct((T, LANES), F32),
                   jax.ShapeDtypeStruct((n_tt, SUBLANES, LANES), F32)],
        compiler_params=pltpu.CompilerParams(dimension_semantics=("arbitrary",),
                                             vmem_limit_bytes=VMEM_LIMIT),
        name="post_attn",
    )(ysb, ymb, proj, proj, xt, wsb, wmb, wo, nrm_g.reshape(1, D), wr_t, br_t)


def _window_plan(counts, tm):
    n_tt, n_exp = counts.shape
    lw_tile = _local_slots(tm) // WIN_ROWS
    n_tiles = -(-(n_tt * (tm * TOP_K // WIN_ROWS + n_exp)) // WINS_PER_TILE) + n_exp
    n_tiles += n_tiles % 2
    nw = (counts + WIN_ROWS - 1) // WIN_ROWS
    local_start = jnp.cumsum(nw, axis=1) - nw
    per_expert = jnp.sum(nw, axis=0)
    per_expert_pad = -(-per_expert // WINS_PER_TILE) * WINS_PER_TILE
    expert_end = jnp.cumsum(per_expert_pad)
    sorted_start = (expert_end - per_expert_pad)[None, :] + jnp.cumsum(nw, axis=0) - nw
    run_start = sorted_start.T.reshape(-1)
    run_len = nw.T.reshape(-1)
    run_src = (jnp.arange(n_tt, dtype=jnp.int32)[:, None] * lw_tile + local_start).T.reshape(-1)
    g = jnp.arange(n_tiles * WINS_PER_TILE, dtype=jnp.int32)[:, None]
    off = g - run_start[None, :]
    src_win = jnp.sum(jnp.where((off >= 0) & (off < run_len[None, :]), run_src[None, :] + off, 0), axis=1)
    off_l = jnp.arange(lw_tile, dtype=jnp.int32)[None, :, None] - local_start[:, None, :]
    back_win = jnp.sum(jnp.where((off_l >= 0) & (off_l < nw[:, None, :]), sorted_start[:, None, :] + off_l, 0),
                       axis=2)
    tile_first = (jnp.arange(n_tiles, dtype=jnp.int32) * WINS_PER_TILE)[:, None]
    tile_valid = tile_first[:, 0] < expert_end[-1]
    used = (per_expert > 0)[None, :]
    expert_start = (expert_end - per_expert_pad)[None, :]
    ids = jnp.arange(n_exp, dtype=jnp.int32)[None, :]
    tile_expert = jnp.minimum(jnp.sum(expert_end[None, :] <= tile_first, axis=1), n_exp - 1)
    opens = jnp.sum(used & (expert_start == tile_first), axis=1) > 0
    buffer = (jnp.sum(used & (expert_start <= tile_first), axis=1) - 1) % 2
    next_expert = jnp.min(jnp.where(used & (expert_start > tile_first), ids, n_exp), axis=1)
    next_expert = jnp.where(next_expert == n_exp, -1, next_expert)
    tile_plan = tuple(a.astype(jnp.int32) for a in (tile_expert, tile_valid, opens, buffer, next_expert))
    return src_win.astype(jnp.int32), back_win.astype(jnp.int32), tile_plan


EXPERT_TILES_PER_STEP = 2


def _experts_kernel(te_ref, tv_ref, first_ref, par_ref, ne_ref, src_ref, nxt_ref, xl_ref, wgu_ref, bgu_a, bgu_b,
                    wd_ref, bd_a, bd_b, y_ref, xbuf, wgu_f, wd_f, wgu_s, wd_s, sem, wsem, *, d_exp):
    step = pl.program_id(0)
    n_steps = pl.num_programs(0)
    slot = step % 2
    ta, tb = 2 * step, 2 * step + 1
    tmm = WINS_PER_TILE * WIN_ROWS
    n_win = EXPERT_TILES_PER_STEP * WINS_PER_TILE

    def window_copy(win, dst_slot, w):
        return pltpu.make_async_copy(xl_ref.at[win], xbuf.at[dst_slot, w], sem.at[dst_slot])

    def start_gather(idx_ref, dst_slot):
        for w in range(n_win):
            window_copy(idx_ref[0, 0, w], dst_slot, w).start()

    def weight_copies(expert, buf):
        return (pltpu.make_async_copy(wgu_ref.at[expert], wgu_f.at[buf], wsem.at[buf, 0]),
                pltpu.make_async_copy(wd_ref.at[expert], wd_f.at[buf], wsem.at[buf, 1]))

    @pl.when((step == 0) & (tv_ref[0] > 0))
    def _():
        for cp in weight_copies(te_ref[0], par_ref[0]):
            cp.start()
        start_gather(src_ref, 0)

    @pl.when((step + 1 < n_steps) & (tv_ref[jnp.minimum(ta + 2, 2 * n_steps - 1)] > 0))
    def _():
        start_gather(nxt_ref, 1 - slot)

    def switch_weights(t):
        @pl.when((first_ref[t] > 0) & (tv_ref[t] > 0))
        def _():
            buf = par_ref[t]

            @pl.when(ne_ref[t] >= 0)
            def _():
                for cp in weight_copies(ne_ref[t], 1 - buf):
                    cp.start()

            for cp in weight_copies(te_ref[t], buf):
                cp.wait()
            wgu_s[buf] = wgu_f[buf].astype(BF16)
            wd_s[buf] = wd_f[buf].astype(BF16)

    switch_weights(ta)
    switch_weights(tb)

    def ffn(tiles):
        xs = [xbuf[slot, w0:w0 + WINS_PER_TILE].reshape(tmm, xbuf.shape[-1]) for w0, _, _, _ in tiles]
        hgu = [jnp.dot(x, wgu_s[buf], preferred_element_type=F32) + bias[0][0]
               for x, (_, buf, bias, _) in zip(xs, tiles)]
        acts = []
        for h in hgu:
            gate = jnp.minimum(h[:, :d_exp], SWIGLU_LIMIT)
            up = jnp.clip(h[:, d_exp:], -SWIGLU_LIMIT, SWIGLU_LIMIT)
            acts.append(((up + 1.0) * (gate * (1.0 / (1.0 + jnp.exp(-SWIGLU_ALPHA * gate))))).astype(BF16))
        for act, (_, buf, bias, row0) in zip(acts, tiles):
            y = jnp.dot(act, wd_s[buf], preferred_element_type=F32) + bias[1][0]
            y_ref[row0:row0 + tmm] = y.astype(y_ref.dtype)

    def wait_windows():
        pltpu.make_async_copy(xl_ref.at[pl.ds(0, n_win)], xbuf.at[slot], sem.at[slot]).wait()

    tile_a = (0, par_ref[ta], (bgu_a, bd_a), 0)
    tile_b = (WINS_PER_TILE, par_ref[tb], (bgu_b, bd_b), tmm)

    @pl.when(tv_ref[tb] > 0)
    def _():
        wait_windows()
        ffn([tile_a, tile_b])

    @pl.when((tv_ref[ta] > 0) & (tv_ref[tb] == 0))
    def _():
        wait_windows()
        ffn([tile_a])
        y_ref[tmm:] = jnp.zeros((tmm, y_ref.shape[-1]), y_ref.dtype)

    @pl.when(tv_ref[ta] == 0)
    def _():
        y_ref[...] = jnp.zeros_like(y_ref)


def _experts(xl, src_win, tile_plan, wgu, bgu, wd, bd):
    D = xl.shape[1]
    E, _, d2 = wgu.shape
    d_exp = d2 // 2
    sub = EXPERT_TILES_PER_STEP
    n_steps = tile_plan[0].shape[0] // sub
    tmm = WINS_PER_TILE * WIN_ROWS
    src3 = src_win.reshape(n_steps, 1, sub * WINS_PER_TILE)
    grid_spec = pltpu.PrefetchScalarGridSpec(
        num_scalar_prefetch=len(tile_plan),
        grid=(n_steps,),
        in_specs=[pl.BlockSpec((1, 1, sub * WINS_PER_TILE), lambda s, *_: (s, 0, 0), memory_space=pltpu.SMEM),
                  pl.BlockSpec((1, 1, sub * WINS_PER_TILE), lambda s, *_: (jnp.minimum(s + 1, n_steps - 1), 0, 0),
                               memory_space=pltpu.SMEM),
                  pl.BlockSpec(memory_space=pl.ANY),
                  pl.BlockSpec(memory_space=pl.ANY),
                  pl.BlockSpec((1, 1, d2), lambda s, te, *_: (te[2 * s], 0, 0)),
                  pl.BlockSpec((1, 1, d2), lambda s, te, *_: (te[2 * s + 1], 0, 0)),
                  pl.BlockSpec(memory_space=pl.ANY),
                  pl.BlockSpec((1, 1, D), lambda s, te, *_: (te[2 * s], 0, 0)),
                  pl.BlockSpec((1, 1, D), lambda s, te, *_: (te[2 * s + 1], 0, 0))],
        out_specs=pl.BlockSpec((sub * tmm, D), lambda s, *_: (s, 0)),
        scratch_shapes=[pltpu.VMEM((2, sub * WINS_PER_TILE, WIN_ROWS, D), BF16),
                        pltpu.VMEM((2, D, d2), F32), pltpu.VMEM((2, d_exp, D), F32),
                        pltpu.VMEM((2, D, d2), BF16), pltpu.VMEM((2, d_exp, D), BF16),
                        pltpu.SemaphoreType.DMA((2,)), pltpu.SemaphoreType.DMA((2, 2))],
    )
    bgu3, bd3 = bgu.reshape(E, 1, d2), bd.reshape(E, 1, D)
    return pl.pallas_call(
        functools.partial(_experts_kernel, d_exp=d_exp),
        grid_spec=grid_spec,
        out_shape=jax.ShapeDtypeStruct((n_steps * sub * tmm, D), BF16),
        compiler_params=pltpu.CompilerParams(dimension_semantics=("arbitrary",),
                                             vmem_limit_bytes=VMEM_LIMIT),
        name="experts",
    )(*tile_plan, src3, src3, xl.reshape(-1, WIN_ROWS, D), wgu, bgu3, bgu3, wd, bd3, bd3)


def _combine_kernel(back_ref, nxt_ref, y_ref, h_ref, route_ref, g_ref, o_ref, ybuf, sem, *, tm, chunk):
    i = pl.program_id(0)
    slot = i % 2
    n_win = ybuf.shape[1]
    tile_win = n_win // COMBINE_TILES_PER_STEP

    def window_copy(win, dst_slot, w):
        return pltpu.make_async_copy(y_ref.at[win], ybuf.at[dst_slot, w], sem.at[dst_slot])

    def start_fetch(idx_ref, dst_slot):
        for w in range(n_win):
            window_copy(idx_ref[0, 0, w], dst_slot, w).start()

    @pl.when(i == 0)
    def _():
        start_fetch(back_ref, 0)

    @pl.when(i + 1 < pl.num_programs(0))
    def _():
        start_fetch(nxt_ref, 1 - slot)

    tiles = [slice(t * tm, (t + 1) * tm) for t in range(COMBINE_TILES_PER_STEP)]
    routes = [route_ref[rows] for rows in tiles]
    weights = [[route[:, k:k + 1] for k in range(TOP_K)] for route in routes]
    slots = [[route[:, TOP_K + k:TOP_K + k + 1] for k in range(TOP_K)] for route in routes]
    pltpu.make_async_copy(y_ref.at[pl.ds(0, n_win)], ybuf.at[slot], sem.at[slot]).wait()
    acc = [h_ref[rows] for rows in tiles]
    for ci in range(tile_win * WIN_ROWS // chunk):
        sid = (lax.broadcasted_iota(jnp.int32, (1, chunk), 1) + ci * chunk).astype(F32)
        for t in range(COMBINE_TILES_PER_STEP):
            wmat = jnp.zeros((tm, chunk), F32)
            for k in range(TOP_K):
                wmat = jnp.where(sid == slots[t][k], weights[t][k], wmat)
            hi, lo = _split_bf16(wmat)
            first = t * tile_win + ci * chunk // WIN_ROWS
            y_rows = ybuf[slot, first:first + chunk // WIN_ROWS].reshape(chunk, ybuf.shape[-1])
            acc[t] = (acc[t] + jnp.dot(hi, y_rows, preferred_element_type=F32)
                      + jnp.dot(lo, y_rows, preferred_element_type=F32))
    for t, rows in enumerate(tiles):
        var = jnp.mean(acc[t] * acc[t], axis=-1, keepdims=True)
        o_ref[rows] = (acc[t] * lax.rsqrt(var + RMS_EPS)) * g_ref[...]


COMBINE_TILES_PER_STEP = 2


def _combine(y, back_win, h, route, g, *, tm):
    T, D = h.shape
    sub = COMBINE_TILES_PER_STEP
    n_steps, n_win = back_win.shape[0] // sub, back_win.shape[1] * sub
    back3 = back_win.reshape(n_steps, 1, n_win)
    return pl.pallas_call(
        functools.partial(_combine_kernel, tm=tm, chunk=256),
        grid=(n_steps,),
        in_specs=[pl.BlockSpec((1, 1, n_win), lambda i: (i, 0, 0), memory_space=pltpu.SMEM),
                  pl.BlockSpec((1, 1, n_win), lambda i: (jnp.minimum(i + 1, n_steps - 1), 0, 0),
                               memory_space=pltpu.SMEM),
                  pl.BlockSpec(memory_space=pl.ANY),
                  pl.BlockSpec((sub * tm, D), lambda i: (i, 0)),
                  pl.BlockSpec((sub * tm, LANES), lambda i: (i, 0)),
                  pl.BlockSpec((1, D), lambda i: (0, 0))],
        out_specs=pl.BlockSpec((sub * tm, D), lambda i: (i, 0)),
        out_shape=jax.ShapeDtypeStruct((T, D), F32),
        scratch_shapes=[pltpu.VMEM((2, n_win, WIN_ROWS, D), BF16), pltpu.SemaphoreType.DMA((2,))],
        compiler_params=pltpu.CompilerParams(dimension_semantics=("arbitrary",),
                                             vmem_limit_bytes=VMEM_LIMIT),
        name="combine",
    )(back3, back3, y.reshape(-1, WIN_ROWS, D), h, route, g.reshape(1, D))


def kernel(x, norm_mix_g, w_in, w_proj_sb, w_proj_moba, w_out, rel_bias, norm_ffn_g, w_router, b_router,
           w_gate_up, b_gate_up, w_down, b_down, norm_final_g):
    batch, seq, D = x.shape
    assert norm_mix_g.shape[0] == 1, "single-layer trunk"
    tm = 256
    xt = x.reshape(batch * seq, D)
    col_scale = np.ones((w_in.shape[-1],), np.float32)
    col_scale[0:D_MIX] = QUERY_SCALE
    col_scale[3 * D_MIX:4 * D_MIX] = QUERY_SCALE
    proj = _in_proj(xt, norm_mix_g[0], (w_in[0] * col_scale).astype(BF16))
    ysb = _sb_attention(proj, batch=batch, seq=seq)
    ymb = _moba_attention(proj, _moba_bias(rel_bias), rel_bias, batch=batch, seq=seq)
    wr_t = jnp.pad(w_router[0].T, ((0, LANES - N_EXPERTS), (0, 0)))
    br_t = b_router[0].reshape(N_EXPERTS, 1)
    h, xl, route, cnt = _post_attn(ysb, ymb, proj, xt, w_proj_sb[0].astype(BF16), w_proj_moba[0].astype(BF16),
                                   w_out[0].astype(BF16), norm_ffn_g[0], wr_t, br_t, tm=tm)
    src_win, back_win, tile_plan = _window_plan(cnt[:, 0, :N_EXPERTS].astype(jnp.int32), tm)
    y = _experts(xl, src_win, tile_plan, w_gate_up[0], b_gate_up[0], w_down[0], b_down[0])
    out = _combine(y, back_win, h, route, norm_final_g, tm=tm)
    return out.reshape(batch, seq, D)
```

```python
import functools
import math

import numpy as np
import jax
import jax.numpy as jnp
from jax import lax
from jax.experimental import pallas as pl
from jax.experimental.pallas import tpu as pltpu

HEAD_DIM = 64
N_HEADS = 8
D_MIX = N_HEADS * HEAD_DIM
MOBA_BLOCK = 256
MOBA_TOPK = 3
REL_BUCKETS = 32
REL_MAX_DIST = 128
N_EXPERTS = 32
TOP_K = 4
SWIGLU_LIMIT = 7.0
SWIGLU_ALPHA = 1.702
RMS_EPS = 1e-6

LANES = 128
SUBLANES = 8
BF16_TILE_ROWS = 16
NEG_BIG = -1e30
LOG2_E = math.log2(math.e)
QUERY_SCALE = LOG2_E / math.sqrt(HEAD_DIM)
VMEM_LIMIT = 56 * 1024 * 1024

F32 = jnp.float32
BF16 = jnp.bfloat16


def _split_bf16(a):
    hi = a.astype(BF16)
    lo = (a - hi.astype(F32)).astype(BF16)
    return hi, lo


def _dot_nt(a, b):
    return lax.dot_general(a, b, (((1,), (1,)), ((), ())), preferred_element_type=F32)


def _in_proj_kernel(x_ref, g_ref, w_ref, o_ref, *, n_slab, slab):
    x = x_ref[...]
    var = jnp.mean(x * x, axis=-1, keepdims=True)
    xn = ((x * lax.rsqrt(var + RMS_EPS)) * g_ref[...]).astype(BF16)
    for j in range(n_slab):
        o_ref[j] = jnp.dot(xn, w_ref[:, j * slab:(j + 1) * slab],
                           preferred_element_type=F32).astype(o_ref.dtype)


def _in_proj(xt, g, w_bf16, *, tm=1024, slab=D_MIX):
    T, D = xt.shape
    n_slab = w_bf16.shape[1] // slab
    return pl.pallas_call(
        functools.partial(_in_proj_kernel, n_slab=n_slab, slab=slab),
        grid=(T // tm,),
        in_specs=[pl.BlockSpec((tm, D), lambda i: (i, 0)),
                  pl.BlockSpec((1, D), lambda i: (0, 0)),
                  pl.BlockSpec(w_bf16.shape, lambda i: (0, 0))],
        out_specs=pl.BlockSpec((n_slab, tm, slab), lambda i: (0, i, 0)),
        out_shape=jax.ShapeDtypeStruct((n_slab, T, slab), BF16),
        compiler_params=pltpu.CompilerParams(dimension_semantics=("arbitrary",),
                                             vmem_limit_bytes=VMEM_LIMIT),
        name="in_proj",
    )(xt, g.reshape(1, D), w_bf16)


def _sb_kernel(q_ref, k_ref, v_ref, o_ref, *, tq, tk):
    i = pl.program_id(2)
    lane = lax.broadcasted_iota(jnp.int32, (1, LANES), 1)
    jj = lax.broadcasted_iota(jnp.int32, (tk, tk), 0)
    ss = lax.broadcasted_iota(jnp.int32, (tk, tk), 1)
    later = (jj >= ss).astype(BF16)
    n_sub = tq // tk
    streams = range(SB_HEADS_PER_STEP)

    def lanes_of(head):
        return slice((head // 2) * LANES, (head // 2 + 1) * LANES)

    qhs = []
    for head in streams:
        q = q_ref[0, :, lanes_of(head)]
        qhs.append(jnp.where(lane // HEAD_DIM == head % 2, q, jnp.zeros((), q.dtype)))

    def q_tile_keys(base, carry, diagonal):
        tiles = [(h, sub) for sub in reversed(range(n_sub)) for h in streams]
        starts = {sub: pl.multiple_of(base + sub * tk, tk) for sub in range(n_sub)}
        c = [list(carry[h][0]) for h in streams]
        acc = [carry[h][1] for h in streams]
        z2, drop, inner, pv = {}, {}, {}, {}

        def first_row(t):
            return t[1] * tk if diagonal else 0

        def causal(t):
            shape = (tq - first_row(t), tk)
            return lax.broadcasted_iota(jnp.int32, shape, 1) < lax.broadcasted_iota(jnp.int32, shape, 0)

        def scores(t):
            z2[t] = _dot_nt(qhs[t[0]][first_row(t):], k_ref[0, pl.ds(starts[t[1]], tk), lanes_of(t[0])])

        def drops(t):
            d = jnp.maximum(z2[t], 0.0) + jnp.log2(1.0 + jnp.exp2(-jnp.abs(z2[t])))
            if diagonal:
                d = jnp.where(causal(t), d, 0.0)
            drop[t] = d
            inner[t] = jnp.dot(d.astype(BF16), later, preferred_element_type=F32)

        def weights(t):
            h, b0 = t[0], first_row(t) // tk
            blocks = [slice((b - b0) * tk, (b - b0 + 1) * tk) for b in range(b0, n_sub)]
            w = [jnp.exp2((z2[t][rows] - c[h][b0 + n]) - inner[t][rows]) for n, rows in enumerate(blocks)]
            w = w[0] if len(w) == 1 else jnp.concatenate(w, axis=0)
            if diagonal:
                w = jnp.where(causal(t), w, 0.0)
            for n, rows in enumerate(blocks):
                c[h][b0 + n] = c[h][b0 + n] + jnp.sum(drop[t][rows], axis=-1, keepdims=True)
            pv[t] = jnp.dot(w.astype(BF16), v_ref[0, pl.ds(starts[t[1]], tk), lanes_of(h)],
                            preferred_element_type=F32)

        def accumulate(t):
            h, r0 = t[0], first_row(t)
            acc[h] = acc[h] + pv[t] if r0 == 0 else jnp.concatenate([acc[h][:r0], acc[h][r0:] + pv[t]], axis=0)

        stages = (scores, drops, weights, accumulate)
        for step in range(len(tiles) + len(stages) - 1):
            for k, stage in enumerate(stages):
                if 0 <= step - k < len(tiles):
                    stage(tiles[step - k])
        return tuple((tuple(c[h]), acc[h]) for h in streams)

    init = tuple((tuple(jnp.zeros((tk, 1), F32) for _ in range(n_sub)), jnp.zeros((tq, LANES), F32))
                 for _ in streams)
    carry = q_tile_keys(i * tq, init, True)
    carry = lax.fori_loop(0, i, lambda it, cr: q_tile_keys((i - 1 - it) * tq, cr, False), carry)
    for g in range(SB_HEADS_PER_STEP // 2):
        o_ref[:, g * LANES:(g + 1) * LANES] = jnp.where(lane < HEAD_DIM, carry[2 * g][1],
                                                        carry[2 * g + 1][1]).astype(o_ref.dtype)


SB_HEADS_PER_STEP = 4


def _sb_attention(proj, *, batch, seq, tq=1024, tk=256):
    T = batch * seq
    nq = seq // tq
    width = SB_HEADS_PER_STEP * HEAD_DIM
    return pl.pallas_call(
        functools.partial(_sb_kernel, tq=tq, tk=tk),
        grid=(batch, D_MIX // width, nq),
        in_specs=[pl.BlockSpec((1, tq, width), lambda b, p, i: (0, b * nq + i, p)),
                  pl.BlockSpec((1, seq, width), lambda b, p, i: (1, b, p)),
                  pl.BlockSpec((1, seq, width), lambda b, p, i: (2, b, p))],
        out_specs=pl.BlockSpec((tq, width), lambda b, p, i: (b * nq + i, p)),
        out_shape=jax.ShapeDtypeStruct((T, D_MIX), BF16),
        compiler_params=pltpu.CompilerParams(
            dimension_semantics=("arbitrary", "arbitrary", "arbitrary"), vmem_limit_bytes=VMEM_LIMIT),
        name="sb_attn",
    )(proj, proj, proj)


def _rel_bucket_np(dist):
    n = np.maximum(dist, 0)
    max_exact = REL_BUCKETS // 2
    nf = np.maximum(n, 1).astype(np.float64)
    large = max_exact + (np.log(nf / max_exact) / math.log(REL_MAX_DIST / max_exact)
                         * (REL_BUCKETS - max_exact)).astype(np.int32)
    large = np.minimum(large, REL_BUCKETS - 1)
    return np.where(n < max_exact, n, large).astype(np.int32)


def _bias_kernel(rel_ref, bkt_ref, o_ref):
    h = pl.program_id(0)
    for d in range(2):
        bkt = bkt_ref[d]
        acc = jnp.zeros(bkt.shape, F32)
        for b in range(REL_BUCKETS):
            acc = jnp.where(bkt == b, rel_ref[b, h], acc)
        o_ref[0, d] = acc * LOG2_E


def _moba_bias(rel_bias):
    L = MOBA_BLOCK
    r = np.arange(L)[:, None]
    c = np.arange(L)[None, :]
    bkt = np.stack([_rel_bucket_np(r - c), _rel_bucket_np(L + r - c)])
    return pl.pallas_call(
        _bias_kernel,
        grid=(N_HEADS,),
        in_specs=[pl.BlockSpec(memory_space=pltpu.SMEM),
                  pl.BlockSpec((2, L, L), lambda h: (0, 0, 0))],
        out_specs=pl.BlockSpec((1, 2, L, L), lambda h: (h, 0, 0, 0)),
        out_shape=jax.ShapeDtypeStruct((N_HEADS, 2, L, L), F32),
        name="moba_bias",
    )(rel_bias, jnp.asarray(bkt))


def _moba_kernel(rel_ref, q_ref, k_ref, v_ref, bias_ref, o_ref, km_ref, *, L, nblk):
    p = pl.program_id(1)
    i = pl.program_id(2)
    lane = lax.broadcasted_iota(jnp.int32, (1, LANES), 1)
    row = lax.broadcasted_iota(jnp.int32, (L, L), 0)
    col = lax.broadcasted_iota(jnp.int32, (L, L), 1)

    @pl.when(i == 0)
    def _():
        km_ref[...] = jnp.mean(k_ref[0].astype(F32).reshape(nblk, L, k_ref.shape[-1]), axis=1)

    def attend(n_off):
        rr = lax.broadcasted_iota(jnp.int32, (n_off, 1), 0)
        km = jnp.concatenate([km_ref[pl.ds(jnp.maximum(i - r, 0), 1), :] for r in range(n_off)], axis=0)
        km = jnp.concatenate([km, jnp.zeros((BF16_TILE_ROWS - n_off, km.shape[-1]), F32)], axis=0)
        km_hi, km_lo = _split_bf16(km)
        km_lo2 = (km - km_hi.astype(F32) - km_lo.astype(F32)).astype(BF16)
        past = (rr >= 1) & (rr <= i)
        outs, all_scores, all_values = [], [], []
        for head in range(MOBA_HEADS_PER_STEP):
            h = head % 2
            group = slice((head // 2) * LANES, (head // 2 + 1) * LANES)
            q = q_ref[0, :, group]
            own = lane // HEAD_DIM == h
            aux0 = (1 - h) * HEAD_DIM
            qh = jnp.where(own, q, jnp.zeros((), q.dtype))
            gs = (_dot_nt(km_hi[:, group], qh) + _dot_nt(km_lo[:, group], qh)
                  + _dot_nt(km_lo2[:, group], qh))[:n_off]
            pen = jnp.zeros((n_off, L), F32)
            for r in range(1, n_off):
                g_r = gs[r:r + 1, :]
                beats = past & ((gs > g_r) | ((gs == g_r) & (rr > r)))
                cnt = jnp.sum(beats.astype(F32), axis=0, keepdims=True)
                keep = (cnt < MOBA_TOPK) & (r <= i)
                pen = jnp.where((rr == r) & jnp.logical_not(keep), NEG_BIG, pen)
            far = jnp.full((n_off, L), rel_ref[REL_BUCKETS - 1, MOBA_HEADS_PER_STEP * p + head] * LOG2_E, F32)
            f_hi, f_lo = _split_bf16(far)
            f_lo2 = far - f_hi.astype(F32) - f_lo.astype(F32)
            far3 = jnp.where(rr == 0, f_hi.astype(F32), jnp.where(rr == 1, f_lo.astype(F32),
                                                                  jnp.where(rr == 2, f_lo2, 0.0)))
            aux = [pen, far3, jnp.zeros((LANES - aux0 - 2 * n_off, L), F32)]
            aux = jnp.concatenate(([jnp.zeros((aux0, L), F32)] if aux0 else []) + aux, axis=0)
            q_aug = jnp.where(own, q, aux.T.astype(q.dtype))
            scores, values = [], []
            for r in range(n_off):
                ks = pl.multiple_of(jnp.maximum(i - r, 0) * L, L)
                kblk = k_ref[0, pl.ds(ks, L), group]
                values.append(v_ref[0, pl.ds(ks, L), group])
                if r == 0:
                    s = _dot_nt(qh, kblk) + bias_ref[head, 0]
                    s = jnp.where(col <= row, s, NEG_BIG)
                else:
                    hot = (lane == aux0 + r)
                    if r >= 2:
                        hot = hot | ((lane >= aux0 + n_off) & (lane < aux0 + n_off + 3))
                    k_aug = jnp.where(own, kblk, jnp.where(hot, 1.0, 0.0).astype(kblk.dtype))
                    s = _dot_nt(q_aug, k_aug)
                    if r == 1:
                        s = s + bias_ref[head, 1]
                scores.append(s)
            all_scores.append(scores)
            all_values.append(values)
        for scores, values in zip(all_scores, all_values):
            m = scores[0]
            for s in scores[1:]:
                m = jnp.maximum(m, s)
            m = jnp.max(m, axis=-1, keepdims=True)
            probs = [jnp.exp2(s - m) for s in scores]
            l = probs[0]
            for pr in probs[1:]:
                l = l + pr
            l = jnp.sum(l, axis=-1, keepdims=True)
            acc = jnp.dot(probs[0].astype(BF16), values[0], preferred_element_type=F32)
            for pr, vblk in zip(probs[1:], values[1:]):
                acc = acc + jnp.dot(pr.astype(BF16), vblk, preferred_element_type=F32)
            outs.append(acc / l)
        for g in range(MOBA_HEADS_PER_STEP // 2):
            o_ref[:, g * LANES:(g + 1) * LANES] = jnp.where(lane < HEAD_DIM, outs[2 * g],
                                                            outs[2 * g + 1]).astype(o_ref.dtype)

    for n_off in range(1, nblk + 1):
        pl.when(i == n_off - 1)(functools.partial(attend, n_off))


MOBA_HEADS_PER_STEP = 4


def _moba_attention(proj, bias, rel_bias, *, batch, seq):
    T = batch * seq
    L = MOBA_BLOCK
    nblk = seq // L
    assert seq % (2 * L) == 0, "query blocks are handled in pairs"
    width = MOBA_HEADS_PER_STEP * HEAD_DIM
    return pl.pallas_call(
        functools.partial(_moba_kernel, L=L, nblk=nblk),
        grid=(batch, D_MIX // width, nblk),
        in_specs=[pl.BlockSpec(memory_space=pltpu.SMEM),
                  pl.BlockSpec((1, L, width), lambda b, p, i: (3, b * nblk + i, p)),
                  pl.BlockSpec((1, seq, width), lambda b, p, i: (4, b, p)),
                  pl.BlockSpec((1, seq, width), lambda b, p, i: (5, b, p)),
                  pl.BlockSpec((MOBA_HEADS_PER_STEP, 2, L, L), lambda b, p, i: (p, 0, 0, 0))],
        out_specs=pl.BlockSpec((L, width), lambda b, p, i: (b * nblk + i, p)),
        out_shape=jax.ShapeDtypeStruct((T, D_MIX), BF16),
        scratch_shapes=[pltpu.VMEM((nblk, width), F32)],
        compiler_params=pltpu.CompilerParams(
            dimension_semantics=("arbitrary", "arbitrary", "arbitrary"), vmem_limit_bytes=VMEM_LIMIT),
        name="moba_attn",
    )(rel_bias, proj, proj, proj, bias)


WIN_ROWS = BF16_TILE_ROWS
WINS_PER_TILE = 32


def _local_slots(tm):
    return tm * TOP_K + N_EXPERTS * WIN_ROWS


def _post_attn_kernel(ysb_ref, ymb_ref, gsb_ref, gmb_ref, x_ref, wsb_ref, wmb_ref, wo_ref, nrm_ref,
                      wt_ref, bt_ref, h_ref, xl_ref, route_ref, cnt_ref, *, tm, chunk):
    ls = _local_slots(tm)
    wt_both = jnp.concatenate(_split_bf16(wt_ref[...]), axis=0)
    hn_of, slots_of = {}, {}

    def dense(sub):
        rows = slice(sub * tm, (sub + 1) * tm)

        def gate(g_ref):
            g = jnp.concatenate([g_ref[0, rows], g_ref[1, rows]], axis=1).astype(F32)
            return 1.0 / (1.0 + jnp.exp(-g))

        merged = (gate(gsb_ref) * jnp.dot(ysb_ref[rows], wsb_ref[...], preferred_element_type=F32)
                  + gate(gmb_ref) * jnp.dot(ymb_ref[rows], wmb_ref[...], preferred_element_type=F32))
        hres = x_ref[rows] + jnp.dot(merged.astype(BF16), wo_ref[...], preferred_element_type=F32)
        h_ref[rows] = hres
        var = jnp.mean(hres * hres, axis=-1, keepdims=True)
        hn_of[sub] = (hres * lax.rsqrt(var + RMS_EPS)) * nrm_ref[...]

    def route(sub):
        hn = hn_of[sub]
        a_hi, a_lo = _split_bf16(hn)
        both = _dot_nt(wt_both, a_hi)
        logits = (both[:LANES] + both[LANES:] + _dot_nt(wt_both[:LANES], a_lo))[:N_EXPERTS] + bt_ref[...]
        eid = lax.broadcasted_iota(jnp.int32, (N_EXPERTS, 1), 0).astype(F32)
        work = logits
        vals, hots = [], []
        for _ in range(TOP_K):
            mx = jnp.max(work, axis=0, keepdims=True)
            idx = jnp.min(jnp.where(work == mx, eid, float(N_EXPERTS)), axis=0, keepdims=True)
            hot = eid == idx
            work = jnp.where(hot, NEG_BIG, work)
            vals.append(mx)
            hots.append(hot)
        exps = [jnp.exp(v - vals[0]) for v in vals]
        denom = exps[0] + exps[1] + exps[2] + exps[3]
        sel = jnp.zeros((N_EXPERTS, tm), F32)
        for hot in hots:
            sel = jnp.where(hot, 1.0, sel)
        sel_bf = sel.astype(BF16)
        r = lax.broadcasted_iota(jnp.int32, (tm, tm), 0)
        c = lax.broadcasted_iota(jnp.int32, (tm, tm), 1)
        earlier = jnp.dot(sel_bf, (r < c).astype(BF16), preferred_element_type=F32)
        cnt = jnp.sum(sel, axis=1, keepdims=True)
        padded = jnp.floor((cnt + (WIN_ROWS - 1)) * (1.0 / WIN_ROWS)) * WIN_ROWS
        er = lax.broadcasted_iota(jnp.int32, (N_EXPERTS, N_EXPERTS), 0)
        ec = lax.broadcasted_iota(jnp.int32, (N_EXPERTS, N_EXPERTS), 1)
        lower = jnp.dot((ec < er).astype(BF16), jnp.broadcast_to(padded, (N_EXPERTS, LANES)).astype(BF16),
                        preferred_element_type=F32)[:, 0:1]
        slot = lower + earlier
        slot_rows = [jnp.sum(jnp.where(hot, slot, 0.0), axis=0, keepdims=True) for hot in hots]
        rows = [e / denom for e in exps] + slot_rows
        route_t = jnp.concatenate(rows + [jnp.zeros((LANES - 2 * TOP_K, tm), F32)], axis=0)
        route_ref[sub * tm:(sub + 1) * tm] = route_t.T
        cnt_ref[sub] = _dot_nt(jnp.ones((SUBLANES, tm), BF16),
                               jnp.concatenate([sel_bf, jnp.zeros((LANES - N_EXPERTS, tm), BF16)], axis=0))
        slots_of[sub] = slot_rows

    def sort_rows(sub):
        slots_t = slots_of[sub]
        hn_bf = hn_of[sub].astype(BF16)
        for ci in range(ls // chunk):
            rid = (lax.broadcasted_iota(jnp.int32, (chunk, 1), 0) + ci * chunk).astype(F32)
            onehot = rid == slots_t[0]
            for k in range(1, TOP_K):
                onehot = onehot | (rid == slots_t[k])
            xl_ref[sub * ls + ci * chunk:sub * ls + (ci + 1) * chunk, :] = jnp.dot(
                jnp.where(onehot, 1.0, 0.0).astype(BF16), hn_bf, preferred_element_type=F32).astype(xl_ref.dtype)

    stages = (dense, route, sort_rows)
    for step in range(POST_TILES_PER_STEP + len(stages) - 1):
        for k, stage in enumerate(stages):
            if 0 <= step - k < POST_TILES_PER_STEP:
                stage(step - k)


POST_TILES_PER_STEP = 2


def _post_attn(ysb, ymb, proj, xt, wsb, wmb, wo, nrm_g, wr_t, br_t, *, tm):
    T, D = xt.shape
    half = D // 2
    n_tt = T // tm
    ls = _local_slots(tm)
    sub = POST_TILES_PER_STEP
    const = lambda i: (0, 0)
    return pl.pallas_call(
        functools.partial(_post_attn_kernel, tm=tm, chunk=128),
        grid=(n_tt // sub,),
        in_specs=[pl.BlockSpec((sub * tm, D_MIX), lambda i: (i, 0)),
                  pl.BlockSpec((sub * tm, D_MIX), lambda i: (i, 0)),
                  pl.BlockSpec((2, sub * tm, half), lambda i: (3, i, 0)),
                  pl.BlockSpec((2, sub * tm, half), lambda i: (4, i, 0)),
                  pl.BlockSpec((sub * tm, D), lambda i: (i, 0)),
                  pl.BlockSpec(wsb.shape, const),
                  pl.BlockSpec(wmb.shape, const),
                  pl.BlockSpec(wo.shape, const),
                  pl.BlockSpec((1, D), const),
                  pl.BlockSpec(wr_t.shape, const),
                  pl.BlockSpec(br_t.shape, const)],
        out_specs=[pl.BlockSpec((sub * tm, D), lambda i: (i, 0)),
                   pl.BlockSpec((sub * ls, D), lambda i: (i, 0)),
                   pl.BlockSpec((sub * tm, LANES), lambda i: (i, 0)),
                   pl.BlockSpec((sub, SUBLANES, LANES), lambda i: (i, 0, 0))],
        out_shape=[jax.ShapeDtypeStruct((T, D), F32),
                   jax.ShapeDtypeStruct((n_tt * ls, D), BF16),
                   jax.ShapeDtypeStruct((T, LANES), F32),
                   jax.ShapeDtypeStruct((n_tt, SUBLANES, LANES), F32)],
        compiler_params=pltpu.CompilerParams(dimension_semantics=("arbitrary",),
                                             vmem_limit_bytes=VMEM_LIMIT),
        name="post_attn",
    )(ysb, ymb, proj, proj, xt, wsb, wmb, wo, nrm_g.reshape(1, D), wr_t, br_t)


def _window_plan(counts, tm):
    n_tt, n_exp = counts.shape
    lw_tile = _local_slots(tm) // WIN_ROWS
    n_tiles = -(-(n_tt * (tm * TOP_K // WIN_ROWS + n_exp)) // WINS_PER_TILE) + n_exp
    n_tiles += n_tiles % 2
    nw = (counts + WIN_ROWS - 1) // WIN_ROWS
    local_start = jnp.cumsum(nw, axis=1) - nw
    per_expert = jnp.sum(nw, axis=0)
    per_expert_pad = -(-per_expert // WINS_PER_TILE) * WINS_PER_TILE
    expert_end = jnp.cumsum(per_expert_pad)
    sorted_start = (expert_end - per_expert_pad)[None, :] + jnp.cumsum(nw, axis=0) - nw
    run_start = sorted_start.T.reshape(-1)
    run_len = nw.T.reshape(-1)
    run_src = (jnp.arange(n_tt, dtype=jnp.int32)[:, None] * lw_tile + local_start).T.reshape(-1)
    g = jnp.arange(n_tiles * WINS_PER_TILE, dtype=jnp.int32)[:, None]
    off = g - run_start[None, :]
    src_win = jnp.sum(jnp.where((off >= 0) & (off < run_len[None, :]), run_src[None, :] + off, 0), axis=1)
    off_l = jnp.arange(lw_tile, dtype=jnp.int32)[None, :, None] - local_start[:, None, :]
    back_win = jnp.sum(jnp.where((off_l >= 0) & (off_l < nw[:, None, :]), sorted_start[:, None, :] + off_l, 0),
                       axis=2)
    tile_first = (jnp.arange(n_tiles, dtype=jnp.int32) * WINS_PER_TILE)[:, None]
    tile_valid = tile_first[:, 0] < expert_end[-1]
    used = (per_expert > 0)[None, :]
    expert_start = (expert_end - per_expert_pad)[None, :]
    ids = jnp.arange(n_exp, dtype=jnp.int32)[None, :]
    tile_expert = jnp.minimum(jnp.sum(expert_end[None, :] <= tile_first, axis=1), n_exp - 1)
    opens = jnp.sum(used & (expert_start == tile_first), axis=1) > 0
    buffer = (jnp.sum(used & (expert_start <= tile_first), axis=1) - 1) % 2
    next_expert = jnp.min(jnp.where(used & (expert_start > tile_first), ids, n_exp), axis=1)
    next_expert = jnp.where(next_expert == n_exp, -1, next_expert)
    tile_plan = tuple(a.astype(jnp.int32) for a in (tile_expert, tile_valid, opens, buffer, next_expert))
    return src_win.astype(jnp.int32), back_win.astype(jnp.int32), tile_plan


EXPERT_TILES_PER_STEP = 2


def _experts_kernel(te_ref, tv_ref, first_ref, par_ref, ne_ref, src_ref, nxt_ref, xl_ref, wgu_ref, bgu_a, bgu_b,
                    wd_ref, bd_a, bd_b, y_ref, xbuf, wgu_f, wd_f, wgu_s, wd_s, sem, wsem, *, d_exp):
    step = pl.program_id(0)
    n_steps = pl.num_programs(0)
    slot = step % 2
    ta, tb = 2 * step, 2 * step + 1
    tmm = WINS_PER_TILE * WIN_ROWS
    n_win = EXPERT_TILES_PER_STEP * WINS_PER_TILE

    def window_copy(win, dst_slot, w):
        return pltpu.make_async_copy(xl_ref.at[win], xbuf.at[dst_slot, w], sem.at[dst_slot])

    def start_gather(idx_ref, dst_slot):
        for w in range(n_win):
            window_copy(idx_ref[0, 0, w], dst_slot, w).start()

    def weight_copies(expert, buf):
        return (pltpu.make_async_copy(wgu_ref.at[expert], wgu_f.at[buf], wsem.at[buf, 0]),
                pltpu.make_async_copy(wd_ref.at[expert], wd_f.at[buf], wsem.at[buf, 1]))

    @pl.when((step == 0) & (tv_ref[0] > 0))
    def _():
        for cp in weight_copies(te_ref[0], par_ref[0]):
            cp.start()
        start_gather(src_ref, 0)

    @pl.when((step + 1 < n_steps) & (tv_ref[jnp.minimum(ta + 2, 2 * n_steps - 1)] > 0))
    def _():
        start_gather(nxt_ref, 1 - slot)

    def switch_weights(t):
        @pl.when((first_ref[t] > 0) & (tv_ref[t] > 0))
        def _():
            buf = par_ref[t]

            @pl.when(ne_ref[t] >= 0)
            def _():
                for cp in weight_copies(ne_ref[t], 1 - buf):
                    cp.start()

            for cp in weight_copies(te_ref[t], buf):
                cp.wait()
            wgu_s[buf] = wgu_f[buf].astype(BF16)
            wd_s[buf] = wd_f[buf].astype(BF16)

    switch_weights(ta)
    switch_weights(tb)

    def ffn(tiles):
        xs = [xbuf[slot, w0:w0 + WINS_PER_TILE].reshape(tmm, xbuf.shape[-1]) for w0, _, _, _ in tiles]
        hgu = [jnp.dot(x, wgu_s[buf], preferred_element_type=F32) + bias[0][0]
               for x, (_, buf, bias, _) in zip(xs, tiles)]
        acts = []
        for h in hgu:
            gate = jnp.minimum(h[:, :d_exp], SWIGLU_LIMIT)
            up = jnp.clip(h[:, d_exp:], -SWIGLU_LIMIT, SWIGLU_LIMIT)
            acts.append(((up + 1.0) * (gate * (1.0 / (1.0 + jnp.exp(-SWIGLU_ALPHA * gate))))).astype(BF16))
        for act, (_, buf, bias, row0) in zip(acts, tiles):
            y = jnp.dot(act, wd_s[buf], preferred_element_type=F32) + bias[1][0]
            y_ref[row0:row0 + tmm] = y.astype(y_ref.dtype)

    def wait_windows():
        pltpu.make_async_copy(xl_ref.at[pl.ds(0, n_win)], xbuf.at[slot], sem.at[slot]).wait()

    tile_a = (0, par_ref[ta], (bgu_a, bd_a), 0)
    tile_b = (WINS_PER_TILE, par_ref[tb], (bgu_b, bd_b), tmm)

    @pl.when(tv_ref[tb] > 0)
    def _():
        wait_windows()
        ffn([tile_a, tile_b])

    @pl.when((tv_ref[ta] > 0) & (tv_ref[tb] == 0))
    def _():
        wait_windows()
        ffn([tile_a])
        y_ref[tmm:] = jnp.zeros((tmm, y_ref.shape[-1]), y_ref.dtype)

    @pl.when(tv_ref[ta] == 0)
    def _():
        y_ref[...] = jnp.zeros_like(y_ref)


def _experts(xl, src_win, tile_plan, wgu, bgu, wd, bd):
    D = xl.shape[1]
    E, _, d2 = wgu.shape
    d_exp = d2 // 2
    sub = EXPERT_TILES_PER_STEP
    n_steps = tile_plan[0].shape[0] // sub
    tmm = WINS_PER_TILE * WIN_ROWS
    src3 = src_win.reshape(n_steps, 1, sub * WINS_PER_TILE)
    grid_spec = pltpu.PrefetchScalarGridSpec(
        num_scalar_prefetch=len(tile_plan),
        grid=(n_steps,),
        in_specs=[pl.BlockSpec((1, 1, sub * WINS_PER_TILE), lambda s, *_: (s, 0, 0), memory_space=pltpu.SMEM),
                  pl.BlockSpec((1, 1, sub * WINS_PER_TILE), lambda s, *_: (jnp.minimum(s + 1, n_steps - 1), 0, 0),
                               memory_space=pltpu.SMEM),
                  pl.BlockSpec(memory_space=pl.ANY),
                  pl.BlockSpec(memory_space=pl.ANY),
                  pl.BlockSpec((1, 1, d2), lambda s, te, *_: (te[2 * s], 0, 0)),
                  pl.BlockSpec((1, 1, d2), lambda s, te, *_: (te[2 * s + 1], 0, 0)),
                  pl.BlockSpec(memory_space=pl.ANY),
                  pl.BlockSpec((1, 1, D), lambda s, te, *_: (te[2 * s], 0, 0)),
                  pl.BlockSpec((1, 1, D), lambda s, te, *_: (te[2 * s + 1], 0, 0))],
        out_specs=pl.BlockSpec((sub * tmm, D), lambda s, *_: (s, 0)),
        scratch_shapes=[pltpu.VMEM((2, sub * WINS_PER_TILE, WIN_ROWS, D), BF16),
                        pltpu.VMEM((2, D, d2), F32), pltpu.VMEM((2, d_exp, D), F32),
                        pltpu.VMEM((2, D, d2), BF16), pltpu.VMEM((2, d_exp, D), BF16),
                        pltpu.SemaphoreType.DMA((2,)), pltpu.SemaphoreType.DMA((2, 2))],
    )
    bgu3, bd3 = bgu.reshape(E, 1, d2), bd.reshape(E, 1, D)
    return pl.pallas_call(
        functools.partial(_experts_kernel, d_exp=d_exp),
        grid_spec=grid_spec,
        out_shape=jax.ShapeDtypeStruct((n_steps * sub * tmm, D), BF16),
        compiler_params=pltpu.CompilerParams(dimension_semantics=("arbitrary",),
                                             vmem_limit_bytes=VMEM_LIMIT),
        name="experts",
    )(*tile_plan, src3, src3, xl.reshape(-1, WIN_ROWS, D), wgu, bgu3, bgu3, wd, bd3, bd3)


def _combine_kernel(back_ref, nxt_ref, y_ref, h_ref, route_ref, g_ref, o_ref, ybuf, sem, *, tm, chunk):
    i = pl.program_id(0)
    slot = i % 2
    n_win = ybuf.shape[1]
    tile_win = n_win // COMBINE_TILES_PER_STEP

    def window_copy(win, dst_slot, w):
        return pltpu.make_async_copy(y_ref.at[win], ybuf.at[dst_slot, w], sem.at[dst_slot])

    def start_fetch(idx_ref, dst_slot):
        for w in range(n_win):
            window_copy(idx_ref[0, 0, w], dst_slot, w).start()

    @pl.when(i == 0)
    def _():
        start_fetch(back_ref, 0)

    @pl.when(i + 1 < pl.num_programs(0))
    def _():
        start_fetch(nxt_ref, 1 - slot)

    tiles = [slice(t * tm, (t + 1) * tm) for t in range(COMBINE_TILES_PER_STEP)]
    routes = [route_ref[rows] for rows in tiles]
    weights = [[route[:, k:k + 1] for k in range(TOP_K)] for route in routes]
    slots = [[route[:, TOP_K + k:TOP_K + k + 1] for k in range(TOP_K)] for route in routes]
    pltpu.make_async_copy(y_ref.at[pl.ds(0, n_win)], ybuf.at[slot], sem.at[slot]).wait()
    acc = [h_ref[rows] for rows in tiles]
    for ci in range(tile_win * WIN_ROWS // chunk):
        sid = (lax.broadcasted_iota(jnp.int32, (1, chunk), 1) + ci * chunk).astype(F32)
        for t in range(COMBINE_TILES_PER_STEP):
            wmat = jnp.zeros((tm, chunk), F32)
            for k in range(TOP_K):
                wmat = jnp.where(sid == slots[t][k], weights[t][k], wmat)
            hi, lo = _split_bf16(wmat)
            first = t * tile_win + ci * chunk // WIN_ROWS
            y_rows = ybuf[slot, first:first + chunk // WIN_ROWS].reshape(chunk, ybuf.shape[-1])
            acc[t] = (acc[t] + jnp.dot(hi, y_rows, preferred_element_type=F32)
                      + jnp.dot(lo, y_rows, preferred_element_type=F32))
    for t, rows in enumerate(tiles):
        var = jnp.mean(acc[t] * acc[t], axis=-1, keepdims=True)
        o_ref[rows] = (acc[t] * lax.rsqrt(var + RMS_EPS)) * g_ref[...]


COMBINE_TILES_PER_STEP = 2


def _combine(y, back_win, h, route, g, *, tm):
    T, D = h.shape
    sub = COMBINE_TILES_PER_STEP
    n_steps, n_win = back_win.shape[0] // sub, back_win.shape[1] * sub
    back3 = back_win.reshape(n_steps, 1, n_win)
    return pl.pallas_call(
        functools.partial(_combine_kernel, tm=tm, chunk=256),
        grid=(n_steps,),
        in_specs=[pl.BlockSpec((1, 1, n_win), lambda i: (i, 0, 0), memory_space=pltpu.SMEM),
                  pl.BlockSpec((1, 1, n_win), lambda i: (jnp.minimum(i + 1, n_steps - 1), 0, 0),
                               memory_space=pltpu.SMEM),
                  pl.BlockSpec(memory_space=pl.ANY),
                  pl.BlockSpec((sub * tm, D), lambda i: (i, 0)),
                  pl.BlockSpec((sub * tm, LANES), lambda i: (i, 0)),
                  pl.BlockSpec((1, D), lambda i: (0, 0))],
        out_specs=pl.BlockSpec((sub * tm, D), lambda i: (i, 0)),
        out_shape=jax.ShapeDtypeStruct((T, D), F32),
        scratch_shapes=[pltpu.VMEM((2, n_win, WIN_ROWS, D), BF16), pltpu.SemaphoreType.DMA((2,))],
        compiler_params=pltpu.CompilerParams(dimension_semantics=("arbitrary",),
                                             vmem_limit_bytes=VMEM_LIMIT),
        name="combine",
    )(back3, back3, y.reshape(-1, WIN_ROWS, D), h, route, g.reshape(1, D))


def kernel(x, norm_mix_g, w_in, w_proj_sb, w_proj_moba, w_out, rel_bias, norm_ffn_g, w_router, b_router,
           w_gate_up, b_gate_up, w_down, b_down, norm_final_g):
    batch, seq, D = x.shape
    assert norm_mix_g.shape[0] == 1, "single-layer trunk"
    tm = 256
    xt = x.reshape(batch * seq, D)
    col_scale = np.ones((w_in.shape[-1],), np.float32)
    col_scale[0:D_MIX] = QUERY_SCALE
    col_scale[3 * D_MIX:4 * D_MIX] = QUERY_SCALE
    proj = _in_proj(xt, norm_mix_g[0], (w_in[0] * col_scale).astype(BF16))
    ysb = _sb_attention(proj, batch=batch, seq=seq)
    ymb = _moba_attention(proj, _moba_bias(rel_bias), rel_bias, batch=batch, seq=seq)
    wr_t = jnp.pad(w_router[0].T, ((0, LANES - N_EXPERTS), (0, 0)))
    br_t = b_router[0].reshape(N_EXPERTS, 1)
    h, xl, route, cnt = _post_attn(ysb, ymb, proj, xt, w_proj_sb[0].astype(BF16), w_proj_moba[0].astype(BF16),
                                   w_out[0].astype(BF16), norm_ffn_g[0], wr_t, br_t, tm=tm)
    src_win, back_win, tile_plan = _window_plan(cnt[:, 0, :N_EXPERTS].astype(jnp.int32), tm)
    y = _experts(xl, src_win, tile_plan, w_gate_up[0], b_gate_up[0], w_down[0], b_down[0])
    out = _combine(y, back_win, h, route, norm_final_g, tm=tm)
    return out.reshape(batch, seq, D)
```

```python
import functools
import math

import numpy as np
import jax
import jax.numpy as jnp
from jax import lax
from jax.experimental import pallas as pl
from jax.experimental.pallas import tpu as pltpu

HEAD_DIM = 64
N_HEADS = 8
D_MIX = N_HEADS * HEAD_DIM
MOBA_BLOCK = 256
MOBA_TOPK = 3
REL_BUCKETS = 32
REL_MAX_DIST = 128
N_EXPERTS = 32
TOP_K = 4
SWIGLU_LIMIT = 7.0
SWIGLU_ALPHA = 1.702
RMS_EPS = 1e-6

LANES = 128
SUBLANES = 8
BF16_TILE_ROWS = 16
NEG_BIG = -1e30
LOG2_E = math.log2(math.e)
QUERY_SCALE = LOG2_E / math.sqrt(HEAD_DIM)
VMEM_LIMIT = 56 * 1024 * 1024

F32 = jnp.float32
BF16 = jnp.bfloat16


def _split_bf16(a):
    hi = a.astype(BF16)
    lo = (a - hi.astype(F32)).astype(BF16)
    return hi, lo


def _dot_nt(a, b):
    return lax.dot_general(a, b, (((1,), (1,)), ((), ())), preferred_element_type=F32)


def _in_proj_kernel(x_ref, g_ref, w_ref, o_ref, *, n_slab, slab):
    x = x_ref[...]
    var = jnp.mean(x * x, axis=-1, keepdims=True)
    xn = ((x * lax.rsqrt(var + RMS_EPS)) * g_ref[...]).astype(BF16)
    for j in range(n_slab):
        o_ref[j] = jnp.dot(xn, w_ref[:, j * slab:(j + 1) * slab],
                           preferred_element_type=F32).astype(o_ref.dtype)


def _in_proj(xt, g, w_bf16, *, tm=1024, slab=D_MIX):
    T, D = xt.shape
    n_slab = w_bf16.shape[1] // slab
    return pl.pallas_call(
        functools.partial(_in_proj_kernel, n_slab=n_slab, slab=slab),
        grid=(T // tm,),
        in_specs=[pl.BlockSpec((tm, D), lambda i: (i, 0)),
                  pl.BlockSpec((1, D), lambda i: (0, 0)),
                  pl.BlockSpec(w_bf16.shape, lambda i: (0, 0))],
        out_specs=pl.BlockSpec((n_slab, tm, slab), lambda i: (0, i, 0)),
        out_shape=jax.ShapeDtypeStruct((n_slab, T, slab), BF16),
        compiler_params=pltpu.CompilerParams(dimension_semantics=("arbitrary",),
                                             vmem_limit_bytes=VMEM_LIMIT),
        name="in_proj",
    )(xt, g.reshape(1, D), w_bf16)


def _sb_kernel(q_ref, k_ref, v_ref, o_ref, *, tq, tk):
    i = pl.program_id(2)
    lane = lax.broadcasted_iota(jnp.int32, (1, LANES), 1)
    jj = lax.broadcasted_iota(jnp.int32, (tk, tk), 0)
    ss = lax.broadcasted_iota(jnp.int32, (tk, tk), 1)
    later = (jj >= ss).astype(BF16)
    n_sub = tq // tk
    streams = range(SB_HEADS_PER_STEP)

    def lanes_of(head):
        return slice((head // 2) * LANES, (head // 2 + 1) * LANES)

    qhs = []
    for head in streams:
        q = q_ref[0, :, lanes_of(head)]
        qhs.append(jnp.where(lane // HEAD_DIM == head % 2, q, jnp.zeros((), q.dtype)))

    def q_tile_keys(base, carry, diagonal):
        tiles = [(h, sub) for sub in reversed(range(n_sub)) for h in streams]
        starts = {sub: pl.multiple_of(base + sub * tk, tk) for sub in range(n_sub)}
        c = [list(carry[h][0]) for h in streams]
        acc = [carry[h][1] for h in streams]
        z2, drop, inner, pv = {}, {}, {}, {}

        def first_row(t):
            return t[1] * tk if diagonal else 0

        def causal(t):
            shape = (tq - first_row(t), tk)
            return lax.broadcasted_iota(jnp.int32, shape, 1) < lax.broadcasted_iota(jnp.int32, shape, 0)

        def scores(t):
            z2[t] = _dot_nt(qhs[t[0]][first_row(t):], k_ref[0, pl.ds(starts[t[1]], tk), lanes_of(t[0])])

        def drops(t):
            d = jnp.maximum(z2[t], 0.0) + jnp.log2(1.0 + jnp.exp2(-jnp.abs(z2[t])))
            if diagonal:
                d = jnp.where(causal(t), d, 0.0)
            drop[t] = d
            inner[t] = jnp.dot(d.astype(BF16), later, preferred_element_type=F32)

        def weights(t):
            h, b0 = t[0], first_row(t) // tk
            blocks = [slice((b - b0) * tk, (b - b0 + 1) * tk) for b in range(b0, n_sub)]
            w = [jnp.exp2((z2[t][rows] - c[h][b0 + n]) - inner[t][rows]) for n, rows in enumerate(blocks)]
            w = w[0] if len(w) == 1 else jnp.concatenate(w, axis=0)
            if diagonal:
                w = jnp.where(causal(t), w, 0.0)
            for n, rows in enumerate(blocks):
                c[h][b0 + n] = c[h][b0 + n] + jnp.sum(drop[t][rows], axis=-1, keepdims=True)
            pv[t] = jnp.dot(w.astype(BF16), v_ref[0, pl.ds(starts[t[1]], tk), lanes_of(h)],
                            preferred_element_type=F32)

        def accumulate(t):
            h, r0 = t[0], first_row(t)
            acc[h] = acc[h] + pv[t] if r0 == 0 else jnp.concatenate([acc[h][:r0], acc[h][r0:] + pv[t]], axis=0)

        stages = (scores, drops, weights, accumulate)
        for step in range(len(tiles) + len(stages) - 1):
            for k, stage in enumerate(stages):
                if 0 <= step - k < len(tiles):
                    stage(tiles[step - k])
        return tuple((tuple(c[h]), acc[h]) for h in streams)

    init = tuple((tuple(jnp.zeros((tk, 1), F32) for _ in range(n_sub)), jnp.zeros((tq, LANES), F32))
                 for _ in streams)
    carry = q_tile_keys(i * tq, init, True)
    carry = lax.fori_loop(0, i, lambda it, cr: q_tile_keys((i - 1 - it) * tq, cr, False), carry)
    for g in range(SB_HEADS_PER_STEP // 2):
        o_ref[:, g * LANES:(g + 1) * LANES] = jnp.where(lane < HEAD_DIM, carry[2 * g][1],
                                                        carry[2 * g + 1][1]).astype(o_ref.dtype)


SB_HEADS_PER_STEP = 4


def _sb_attention(proj, *, batch, seq, tq=1024, tk=256):
    T = batch * seq
    nq = seq // tq
    width = SB_HEADS_PER_STEP * HEAD_DIM
    return pl.pallas_call(
        functools.partial(_sb_kernel, tq=tq, tk=tk),
        grid=(batch, D_MIX // width, nq),
        in_specs=[pl.BlockSpec((1, tq, width), lambda b, p, i: (0, b * nq + i, p)),
                  pl.BlockSpec((1, seq, width), lambda b, p, i: (1, b, p)),
                  pl.BlockSpec((1, seq, width), lambda b, p, i: (2, b, p))],
        out_specs=pl.BlockSpec((tq, width), lambda b, p, i: (b * nq + i, p)),
        out_shape=jax.ShapeDtypeStruct((T, D_MIX), BF16),
        compiler_params=pltpu.CompilerParams(
            dimension_semantics=("arbitrary", "arbitrary", "arbitrary"), vmem_limit_bytes=VMEM_LIMIT),
        name="sb_attn",
    )(proj, proj, proj)


def _rel_bucket_np(dist):
    n = np.maximum(dist, 0)
    max_exact = REL_BUCKETS // 2
    nf = np.maximum(n, 1).astype(np.float64)
    large = max_exact + (np.log(nf / max_exact) / math.log(REL_MAX_DIST / max_exact)
                         * (REL_BUCKETS - max_exact)).astype(np.int32)
    large = np.minimum(large, REL_BUCKETS - 1)
    return np.where(n < max_exact, n, large).astype(np.int32)


def _bias_kernel(rel_ref, bkt_ref, o_ref):
    h = pl.program_id(0)
    for d in range(2):
        bkt = bkt_ref[d]
        acc = jnp.zeros(bkt.shape, F32)
        for b in range(REL_BUCKETS):
            acc = jnp.where(bkt == b, rel_ref[b, h], acc)
        o_ref[0, d] = acc * LOG2_E


def _moba_bias(rel_bias):
    L = MOBA_BLOCK
    r = np.arange(L)[:, None]
    c = np.arange(L)[None, :]
    bkt = np.stack([_rel_bucket_np(r - c), _rel_bucket_np(L + r - c)])
    return pl.pallas_call(
        _bias_kernel,
        grid=(N_HEADS,),
        in_specs=[pl.BlockSpec(memory_space=pltpu.SMEM),
                  pl.BlockSpec((2, L, L), lambda h: (0, 0, 0))],
        out_specs=pl.BlockSpec((1, 2, L, L), lambda h: (h, 0, 0, 0)),
        out_shape=jax.ShapeDtypeStruct((N_HEADS, 2, L, L), F32),
        name="moba_bias",
    )(rel_bias, jnp.asarray(bkt))


def _moba_kernel(rel_ref, q_ref, k_ref, v_ref, bias_ref, o_ref, km_ref, *, L, nblk):
    p = pl.program_id(1)
    i = pl.program_id(2)
    lane = lax.broadcasted_iota(jnp.int32, (1, LANES), 1)
    row = lax.broadcasted_iota(jnp.int32, (L, L), 0)
    col = lax.broadcasted_iota(jnp.int32, (L, L), 1)

    @pl.when(i == 0)
    def _():
        km_ref[...] = jnp.mean(k_ref[0].astype(F32).reshape(nblk, L, k_ref.shape[-1]), axis=1)

    def attend(n_off):
        rr = lax.broadcasted_iota(jnp.int32, (n_off, 1), 0)
        km = jnp.concatenate([km_ref[pl.ds(jnp.maximum(i - r, 0), 1), :] for r in range(n_off)], axis=0)
        km = jnp.concatenate([km, jnp.zeros((BF16_TILE_ROWS - n_off, km.shape[-1]), F32)], axis=0)
        km_hi, km_lo = _split_bf16(km)
        km_lo2 = (km - km_hi.astype(F32) - km_lo.astype(F32)).astype(BF16)
        past = (rr >= 1) & (rr <= i)
        outs, all_scores, all_values = [], [], []
        for head in range(MOBA_HEADS_PER_STEP):
            h = head % 2
            group = slice((head // 2) * LANES, (head // 2 + 1) * LANES)
            q = q_ref[0, :, group]
            own = lane // HEAD_DIM == h
            aux0 = (1 - h) * HEAD_DIM
            qh = jnp.where(own, q, jnp.zeros((), q.dtype))
            gs = (_dot_nt(km_hi[:, group], qh) + _dot_nt(km_lo[:, group], qh)
                  + _dot_nt(km_lo2[:, group], qh))[:n_off]
            pen = jnp.zeros((n_off, L), F32)
            for r in range(1, n_off):
                g_r = gs[r:r + 1, :]
                beats = past & ((gs > g_r) | ((gs == g_r) & (rr > r)))
                cnt = jnp.sum(beats.astype(F32), axis=0, keepdims=True)
                keep = (cnt < MOBA_TOPK) & (r <= i)
                pen = jnp.where((rr == r) & jnp.logical_not(keep), NEG_BIG, pen)
            far = jnp.full((n_off, L), rel_ref[REL_BUCKETS - 1, MOBA_HEADS_PER_STEP * p + head] * LOG2_E, F32)
            f_hi, f_lo = _split_bf16(far)
            f_lo2 = far - f_hi.astype(F32) - f_lo.astype(F32)
            far3 = jnp.where(rr == 0, f_hi.astype(F32), jnp.where(rr == 1, f_lo.astype(F32),
                                                                  jnp.where(rr == 2, f_lo2, 0.0)))
            aux = [pen, far3, jnp.zeros((LANES - aux0 - 2 * n_off, L), F32)]
            aux = jnp.concatenate(([jnp.zeros((aux0, L), F32)] if aux0 else []) + aux, axis=0)
            q_aug = jnp.where(own, q, aux.T.astype(q.dtype))
            scores, values = [], []
            for r in range(n_off):
                ks = pl.multiple_of(jnp.maximum(i - r, 0) * L, L)
                kblk = k_ref[0, pl.ds(ks, L), group]
                values.append(v_ref[0, pl.ds(ks, L), group])
                if r == 0:
                    s = _dot_nt(qh, kblk) + bias_ref[head, 0]
                    s = jnp.where(col <= row, s, NEG_BIG)
                else:
                    hot = (lane == aux0 + r)
                    if r >= 2:
                        hot = hot | ((lane >= aux0 + n_off) & (lane < aux0 + n_off + 3))
                    k_aug = jnp.where(own, kblk, jnp.where(hot, 1.0, 0.0).astype(kblk.dtype))
                    s = _dot_nt(q_aug, k_aug)
                    if r == 1:
                        s = s + bias_ref[head, 1]
                scores.append(s)
            all_scores.append(scores)
            all_values.append(values)
        for scores, values in zip(all_scores, all_values):
            m = scores[0]
            for s in scores[1:]:
                m = jnp.maximum(m, s)
            m = jnp.max(m, axis=-1, keepdims=True)
            probs = [jnp.exp2(s - m) for s in scores]
            l = probs[0]
            for pr in probs[1:]:
                l = l + pr
            l = jnp.sum(l, axis=-1, keepdims=True)
            acc = jnp.dot(probs[0].astype(BF16), values[0], preferred_element_type=F32)
            for pr, vblk in zip(probs[1:], values[1:]):
                acc = acc + jnp.dot(pr.astype(BF16), vblk, preferred_element_type=F32)
            outs.append(acc / l)
        for g in range(MOBA_HEADS_PER_STEP // 2):
            o_ref[:, g * LANES:(g + 1) * LANES] = jnp.where(lane < HEAD_DIM, outs[2 * g],
                                                            outs[2 * g + 1]).astype(o_ref.dtype)

    for n_off in range(1, nblk + 1):
        pl.when(i == n_off - 1)(functools.partial(attend, n_off))


MOBA_HEADS_PER_STEP = 4


def _moba_attention(proj, bias, rel_bias, *, batch, seq):
    T = batch * seq
    L = MOBA_BLOCK
    nblk = seq // L
    assert seq % (2 * L) == 0, "query blocks are handled in pairs"
    width = MOBA_HEADS_PER_STEP * HEAD_DIM
    return pl.pallas_call(
        functools.partial(_moba_kernel, L=L, nblk=nblk),
        grid=(batch, D_MIX // width, nblk),
        in_specs=[pl.BlockSpec(memory_space=pltpu.SMEM),
                  pl.BlockSpec((1, L, width), lambda b, p, i: (3, b * nblk + i, p)),
                  pl.BlockSpec((1, seq, width), lambda b, p, i: (4, b, p)),
                  pl.BlockSpec((1, seq, width), lambda b, p, i: (5, b, p)),
                  pl.BlockSpec((MOBA_HEADS_PER_STEP, 2, L, L), lambda b, p, i: (p, 0, 0, 0))],
        out_specs=pl.BlockSpec((L, width), lambda b, p, i: (b * nblk + i, p)),
        out_shape=jax.ShapeDtypeStruct((T, D_MIX), BF16),
        scratch_shapes=[pltpu.VMEM((nblk, width), F32)],
        compiler_params=pltpu.CompilerParams(
            dimension_semantics=("arbitrary", "arbitrary", "arbitrary"), vmem_limit_bytes=VMEM_LIMIT),
        name="moba_attn",
    )(rel_bias, proj, proj, proj, bias)


WIN_ROWS = BF16_TILE_ROWS
WINS_PER_TILE = 32


def _local_slots(tm):
    return tm * TOP_K + N_EXPERTS * WIN_ROWS


def _post_attn_kernel(ysb_ref, ymb_ref, gsb_ref, gmb_ref, x_ref, wsb_ref, wmb_ref, wo_ref, nrm_ref,
                      wt_ref, bt_ref, h_ref, xl_ref, route_ref, cnt_ref, *, tm, chunk):
    ls = _local_slots(tm)
    wt_both = jnp.concatenate(_split_bf16(wt_ref[...]), axis=0)
    hn_of, slots_of = {}, {}

    def dense(sub):
        rows = slice(sub * tm, (sub + 1) * tm)

        def gate(g_ref):
            g = jnp.concatenate([g_ref[0, rows], g_ref[1, rows]], axis=1).astype(F32)
            return 1.0 / (1.0 + jnp.exp(-g))

        merged = (gate(gsb_ref) * jnp.dot(ysb_ref[rows], wsb_ref[...], preferred_element_type=F32)
                  + gate(gmb_ref) * jnp.dot(ymb_ref[rows], wmb_ref[...], preferred_element_type=F32))
        hres = x_ref[rows] + jnp.dot(merged.astype(BF16), wo_ref[...], preferred_element_type=F32)
        h_ref[rows] = hres
        var = jnp.mean(hres * hres, axis=-1, keepdims=True)
        hn_of[sub] = (hres * lax.rsqrt(var + RMS_EPS)) * nrm_ref[...]

    def route(sub):
        hn = hn_of[sub]
        a_hi, a_lo = _split_bf16(hn)
        both = _dot_nt(wt_both, a_hi)
        logits = (both[:LANES] + both[LANES:] + _dot_nt(wt_both[:LANES], a_lo))[:N_EXPERTS] + bt_ref[...]
        eid = lax.broadcasted_iota(jnp.int32, (N_EXPERTS, 1), 0).astype(F32)
        work = logits
        vals, hots = [], []
        for _ in range(TOP_K):
            mx = jnp.max(work, axis=0, keepdims=True)
            idx = jnp.min(jnp.where(work == mx, eid, float(N_EXPERTS)), axis=0, keepdims=True)
            hot = eid == idx
            work = jnp.where(hot, NEG_BIG, work)
            vals.append(mx)
            hots.append(hot)
        exps = [jnp.exp(v - vals[0]) for v in vals]
        denom = exps[0] + exps[1] + exps[2] + exps[3]
        sel = jnp.zeros((N_EXPERTS, tm), F32)
        for hot in hots:
            sel = jnp.where(hot, 1.0, sel)
        sel_bf = sel.astype(BF16)
        r = lax.broadcasted_iota(jnp.int32, (tm, tm), 0)
        c = lax.broadcasted_iota(jnp.int32, (tm, tm), 1)
        earlier = jnp.dot(sel_bf, (r < c).astype(BF16), preferred_element_type=F32)
        cnt = jnp.sum(sel, axis=1, keepdims=True)
        padded = jnp.floor((cnt + (WIN_ROWS - 1)) * (1.0 / WIN_ROWS)) * WIN_ROWS
        er = lax.broadcasted_iota(jnp.int32, (N_EXPERTS, N_EXPERTS), 0)
        ec = lax.broadcasted_iota(jnp.int32, (N_EXPERTS, N_EXPERTS), 1)
        lower = jnp.dot((ec < er).astype(BF16), jnp.broadcast_to(padded, (N_EXPERTS, LANES)).astype(BF16),
                        preferred_element_type=F32)[:, 0:1]
        slot = lower + earlier
        slot_rows = [jnp.sum(jnp.where(hot, slot, 0.0), axis=0, keepdims=True) for hot in hots]
        rows = [e / denom for e in exps] + slot_rows
        route_t = jnp.concatenate(rows + [jnp.zeros((LANES - 2 * TOP_K, tm), F32)], axis=0)
        route_ref[sub * tm:(sub + 1) * tm] = route_t.T
        cnt_ref[sub] = _dot_nt(jnp.ones((SUBLANES, tm), BF16),
                               jnp.concatenate([sel_bf, jnp.zeros((LANES - N_EXPERTS, tm), BF16)], axis=0))
        slots_of[sub] = slot_rows

    def sort_rows(sub):
        slots_t = slots_of[sub]
        hn_bf = hn_of[sub].astype(BF16)
        for ci in range(ls // chunk):
            rid = (lax.broadcasted_iota(jnp.int32, (chunk, 1), 0) + ci * chunk).astype(F32)
            onehot = rid == slots_t[0]
            for k in range(1, TOP_K):
                onehot = onehot | (rid == slots_t[k])
            xl_ref[sub * ls + ci * chunk:sub * ls + (ci + 1) * chunk, :] = jnp.dot(
                jnp.where(onehot, 1.0, 0.0).astype(BF16), hn_bf, preferred_element_type=F32).astype(xl_ref.dtype)

    stages = (dense, route, sort_rows)
    for step in range(POST_TILES_PER_STEP + len(stages) - 1):
        for k, stage in enumerate(stages):
            if 0 <= step - k < POST_TILES_PER_STEP:
                stage(step - k)


POST_TILES_PER_STEP = 2


def _post_attn(ysb, ymb, proj, xt, wsb, wmb, wo, nrm_g, wr_t, br_t, *, tm):
    T, D = xt.shape
    half = D // 2
    n_tt = T // tm
    ls = _local_slots(tm)
    sub = POST_TILES_PER_STEP
    const = lambda i: (0, 0)
    return pl.pallas_call(
        functools.partial(_post_attn_kernel, tm=tm, chunk=128),
        grid=(n_tt // sub,),
        in_specs=[pl.BlockSpec((sub * tm, D_MIX), lambda i: (i, 0)),
                  pl.BlockSpec((sub * tm, D_MIX), lambda i: (i, 0)),
                  pl.BlockSpec((2, sub * tm, half), lambda i: (3, i, 0)),
                  pl.BlockSpec((2, sub * tm, half), lambda i: (4, i, 0)),
                  pl.BlockSpec((sub * tm, D), lambda i: (i, 0)),
                  pl.BlockSpec(wsb.shape, const),
                  pl.BlockSpec(wmb.shape, const),
                  pl.BlockSpec(wo.shape, const),
                  pl.BlockSpec((1, D), const),
                  pl.BlockSpec(wr_t.shape, const),
                  pl.BlockSpec(br_t.shape, const)],
        out_specs=[pl.BlockSpec((sub * tm, D), lambda i: (i, 0)),
                   pl.BlockSpec((sub * ls, D), lambda i: (i, 0)),
                   pl.BlockSpec((sub * tm, LANES), lambda i: (i, 0)),
                   pl.BlockSpec((sub, SUBLANES, LANES), lambda i: (i, 0, 0))],
        out_shape=[jax.ShapeDtypeStruct((T, D), F32),
                   jax.ShapeDtypeStruct((n_tt * ls, D), BF16),
                   jax.ShapeDtypeStruct((T, LANES), F32),
                   jax.ShapeDtypeStruct((n_tt, SUBLANES, LANES), F32)],
        compiler_params=pltpu.CompilerParams(dimension_semantics=("arbitrary",),
                                             vmem_limit_bytes=VMEM_LIMIT),
        name="post_attn",
    )(ysb, ymb, proj, proj, xt, wsb, wmb, wo, nrm_g.reshape(1, D), wr_t, br_t)


def _window_plan(counts, tm):
    n_tt, n_exp = counts.shape
    lw_tile = _local_slots(tm) // WIN_ROWS
    n_tiles = -(-(n_tt * (tm * TOP_K // WIN_ROWS + n_exp)) // WINS_PER_TILE) + n_exp
    n_tiles += n_tiles % 2
    nw = (counts + WIN_ROWS - 1) // WIN_ROWS
    local_start = jnp.cumsum(nw, axis=1) - nw
    per_expert = jnp.sum(nw, axis=0)
    per_expert_pad = -(-per_expert // WINS_PER_TILE) * WINS_PER_TILE
    expert_end = jnp.cumsum(per_expert_pad)
    sorted_start = (expert_end - per_expert_pad)[None, :] + jnp.cumsum(nw, axis=0) - nw
    run_start = sorted_start.T.reshape(-1)
    run_len = nw.T.reshape(-1)
    run_src = (jnp.arange(n_tt, dtype=jnp.int32)[:, None] * lw_tile + local_start).T.reshape(-1)
    g = jnp.arange(n_tiles * WINS_PER_TILE, dtype=jnp.int32)[:, None]
    off = g - run_start[None, :]
    src_win = jnp.sum(jnp.where((off >= 0) & (off < run_len[None, :]), run_src[None, :] + off, 0), axis=1)
    off_l = jnp.arange(lw_tile, dtype=jnp.int32)[None, :, None] - local_start[:, None, :]
    back_win = jnp.sum(jnp.where((off_l >= 0) & (off_l < nw[:, None, :]), sorted_start[:, None, :] + off_l, 0),
                       axis=2)
    tile_first = (jnp.arange(n_tiles, dtype=jnp.int32) * WINS_PER_TILE)[:, None]
    tile_valid = tile_first[:, 0] < expert_end[-1]
    used = (per_expert > 0)[None, :]
    expert_start = (expert_end - per_expert_pad)[None, :]
    ids = jnp.arange(n_exp, dtype=jnp.int32)[None, :]
    tile_expert = jnp.minimum(jnp.sum(expert_end[None, :] <= tile_first, axis=1), n_exp - 1)
    opens = jnp.sum(used & (expert_start == tile_first), axis=1) > 0
    buffer = (jnp.sum(used & (expert_start <= tile_first), axis=1) - 1) % 2
    next_expert = jnp.min(jnp.where(used & (expert_start > tile_first), ids, n_exp), axis=1)
    next_expert = jnp.where(next_expert == n_exp, -1, next_expert)
    tile_plan = tuple(a.astype(jnp.int32) for a in (tile_expert, tile_valid, opens, buffer, next_expert))
    return src_win.astype(jnp.int32), back_win.astype(jnp.int32), tile_plan


EXPERT_TILES_PER_STEP = 2


def _experts_kernel(te_ref, tv_ref, first_ref, par_ref, ne_ref, src_ref, nxt_ref, xl_ref, wgu_ref, bgu_a, bgu_b,
                    wd_ref, bd_a, bd_b, y_ref, xbuf, wgu_f, wd_f, wgu_s, wd_s, sem, wsem, *, d_exp):
    step = pl.program_id(0)
    n_steps = pl.num_programs(0)
    slot = step % 2
    ta, tb = 2 * step, 2 * step + 1
    tmm = WINS_PER_TILE * WIN_ROWS
    n_win = EXPERT_TILES_PER_STEP * WINS_PER_TILE

    def window_copy(win, dst_slot, w):
        return pltpu.make_async_copy(xl_ref.at[win], xbuf.at[dst_slot, w], sem.at[dst_slot])

    def start_gather(idx_ref, dst_slot):
        for w in range(n_win):
            window_copy(idx_ref[0, 0, w], dst_slot, w).start()

    def weight_copies(expert, buf):
        return (pltpu.make_async_copy(wgu_ref.at[expert], wgu_f.at[buf], wsem.at[buf, 0]),
                pltpu.make_async_copy(wd_ref.at[expert], wd_f.at[buf], wsem.at[buf, 1]))

    @pl.when((step == 0) & (tv_ref[0] > 0))
    def _():
        for cp in weight_copies(te_ref[0], par_ref[0]):
            cp.start()
        start_gather(src_ref, 0)

    @pl.when((step + 1 < n_steps) & (tv_ref[jnp.minimum(ta + 2, 2 * n_steps - 1)] > 0))
    def _():
        start_gather(nxt_ref, 1 - slot)

    def switch_weights(t):
        @pl.when((first_ref[t] > 0) & (tv_ref[t] > 0))
        def _():
            buf = par_ref[t]

            @pl.when(ne_ref[t] >= 0)
            def _():
                for cp in weight_copies(ne_ref[t], 1 - buf):
                    cp.start()

            for cp in weight_copies(te_ref[t], buf):
                cp.wait()
            wgu_s[buf] = wgu_f[buf].astype(BF16)
            wd_s[buf] = wd_f[buf].astype(BF16)

    switch_weights(ta)
    switch_weights(tb)

    def ffn(tiles):
        xs = [xbuf[slot, w0:w0 + WINS_PER_TILE].reshape(tmm, xbuf.shape[-1]) for w0, _, _, _ in tiles]
        hgu = [jnp.dot(x, wgu_s[buf], preferred_element_type=F32) + bias[0][0]
               for x, (_, buf, bias, _) in zip(xs, tiles)]
        acts = []
        for h in hgu:
            gate = jnp.minimum(h[:, :d_exp], SWIGLU_LIMIT)
            up = jnp.clip(h[:, d_exp:], -SWIGLU_LIMIT, SWIGLU_LIMIT)
            acts.append(((up + 1.0) * (gate * (1.0 / (1.0 + jnp.exp(-SWIGLU_ALPHA * gate))))).astype(BF16))
        for act, (_, buf, bias, row0) in zip(acts, tiles):
            y = jnp.dot(act, wd_s[buf], preferred_element_type=F32) + bias[1][0]
            y_ref[row0:row0 + tmm] = y.astype(y_ref.dtype)

    def wait_windows():
        pltpu.make_async_copy(xl_ref.at[pl.ds(0, n_win)], xbuf.at[slot], sem.at[slot]).wait()

    tile_a = (0, par_ref[ta], (bgu_a, bd_a), 0)
    tile_b = (WINS_PER_TILE, par_ref[tb], (bgu_b, bd_b), tmm)

    @pl.when(tv_ref[tb] > 0)
    def _():
        wait_windows()
        ffn([tile_a, tile_b])

    @pl.when((tv_ref[ta] > 0) & (tv_ref[tb] == 0))
    def _():
        wait_windows()
        ffn([tile_a])
        y_ref[tmm:] = jnp.zeros((tmm, y_ref.shape[-1]), y_ref.dtype)

    @pl.when(tv_ref[ta] == 0)
    def _():
        y_ref[...] = jnp.zeros_like(y_ref)


def _experts(xl, src_win, tile_plan, wgu, bgu, wd, bd):
    D = xl.shape[1]
    E, _, d2 = wgu.shape
    d_exp = d2 // 2
    sub = EXPERT_TILES_PER_STEP
    n_steps = tile_plan[0].shape[0] // sub
    tmm = WINS_PER_TILE * WIN_ROWS
    src3 = src_win.reshape(n_steps, 1, sub * WINS_PER_TILE)
    grid_spec = pltpu.PrefetchScalarGridSpec(
        num_scalar_prefetch=len(tile_plan),
        grid=(n_steps,),
        in_specs=[pl.BlockSpec((1, 1, sub * WINS_PER_TILE), lambda s, *_: (s, 0, 0), memory_space=pltpu.SMEM),
                  pl.BlockSpec((1, 1, sub * WINS_PER_TILE), lambda s, *_: (jnp.minimum(s + 1, n_steps - 1), 0, 0),
                               memory_space=pltpu.SMEM),
                  pl.BlockSpec(memory_space=pl.ANY),
                  pl.BlockSpec(memory_space=pl.ANY),
                  pl.BlockSpec((1, 1, d2), lambda s, te, *_: (te[2 * s], 0, 0)),
                  pl.BlockSpec((1, 1, d2), lambda s, te, *_: (te[2 * s + 1], 0, 0)),
                  pl.BlockSpec(memory_space=pl.ANY),
                  pl.BlockSpec((1, 1, D), lambda s, te, *_: (te[2 * s], 0, 0)),
                  pl.BlockSpec((1, 1, D), lambda s, te, *_: (te[2 * s + 1], 0, 0))],
        out_specs=pl.BlockSpec((sub * tmm, D), lambda s, *_: (s, 0)),
        scratch_shapes=[pltpu.VMEM((2, sub * WINS_PER_TILE, WIN_ROWS, D), BF16),
                        pltpu.VMEM((2, D, d2), F32), pltpu.VMEM((2, d_exp, D), F32),
                        pltpu.VMEM((2, D, d2), BF16), pltpu.VMEM((2, d_exp, D), BF16),
                        pltpu.SemaphoreType.DMA((2,)), pltpu.SemaphoreType.DMA((2, 2))],
    )
    bgu3, bd3 = bgu.reshape(E, 1, d2), bd.reshape(E, 1, D)
    return pl.pallas_call(
        functools.partial(_experts_kernel, d_exp=d_exp),
        grid_spec=grid_spec,
        out_shape=jax.ShapeDtypeStruct((n_steps * sub * tmm, D), BF16),
        compiler_params=pltpu.CompilerParams(dimension_semantics=("arbitrary",),
                                             vmem_limit_bytes=VMEM_LIMIT),
        name="experts",
    )(*tile_plan, src3, src3, xl.reshape(-1, WIN_ROWS, D), wgu, bgu3, bgu3, wd, bd3, bd3)


def _combine_kernel(back_ref, nxt_ref, y_ref, h_ref, route_ref, g_ref, o_ref, ybuf, sem, *, tm, chunk):
    i = pl.program_id(0)
    slot = i % 2
    n_win = ybuf.shape[1]
    tile_win = n_win // COMBINE_TILES_PER_STEP

    def window_copy(win, dst_slot, w):
        return pltpu.make_async_copy(y_ref.at[win], ybuf.at[dst_slot, w], sem.at[dst_slot])

    def start_fetch(idx_ref, dst_slot):
        for w in range(n_win):
            window_copy(idx_ref[0, 0, w], dst_slot, w).start(priority=w % 2)

    @pl.when(i == 0)
    def _():
        start_fetch(back_ref, 0)

    @pl.when(i + 1 < pl.num_programs(0))
    def _():
        start_fetch(nxt_ref, 1 - slot)

    tiles = [slice(t * tm, (t + 1) * tm) for t in range(COMBINE_TILES_PER_STEP)]
    routes = [route_ref[rows] for rows in tiles]
    weights = [[route[:, k:k + 1] for k in range(TOP_K)] for route in routes]
    slots = [[route[:, TOP_K + k:TOP_K + k + 1] for k in range(TOP_K)] for route in routes]
    pltpu.make_async_copy(y_ref.at[pl.ds(0, n_win)], ybuf.at[slot], sem.at[slot]).wait()
    acc = [h_ref[rows] for rows in tiles]
    for ci in range(tile_win * WIN_ROWS // chunk):
        sid = (lax.broadcasted_iota(jnp.int32, (1, chunk), 1) + ci * chunk).astype(F32)
        for t in range(COMBINE_TILES_PER_STEP):
            wmat = jnp.zeros((tm, chunk), F32)
            for k in range(TOP_K):
                wmat = jnp.where(sid == slots[t][k], weights[t][k], wmat)
            hi, lo = _split_bf16(wmat)
            first = t * tile_win + ci * chunk // WIN_ROWS
            y_rows = ybuf[slot, first:first + chunk // WIN_ROWS].reshape(chunk, ybuf.shape[-1])
            acc[t] = (acc[t] + jnp.dot(hi, y_rows, preferred_element_type=F32)
                      + jnp.dot(lo, y_rows, preferred_element_type=F32))
    for t, rows in enumerate(tiles):
        var = jnp.mean(acc[t] * acc[t], axis=-1, keepdims=True)
        o_ref[rows] = (acc[t] * lax.rsqrt(var + RMS_EPS)) * g_ref[...]


COMBINE_TILES_PER_STEP = 2


def _combine(y, back_win, h, route, g, *, tm):
    T, D = h.shape
    sub = COMBINE_TILES_PER_STEP
    n_steps, n_win = back_win.shape[0] // sub, back_win.shape[1] * sub
    back3 = back_win.reshape(n_steps, 1, n_win)
    return pl.pallas_call(
        functools.partial(_combine_kernel, tm=tm, chunk=256),
        grid=(n_steps,),
        in_specs=[pl.BlockSpec((1, 1, n_win), lambda i: (i, 0, 0), memory_space=pltpu.SMEM),
                  pl.BlockSpec((1, 1, n_win), lambda i: (jnp.minimum(i + 1, n_steps - 1), 0, 0),
                               memory_space=pltpu.SMEM),
                  pl.BlockSpec(memory_space=pl.ANY),
                  pl.BlockSpec((sub * tm, D), lambda i: (i, 0)),
                  pl.BlockSpec((sub * tm, LANES), lambda i: (i, 0)),
                  pl.BlockSpec((1, D), lambda i: (0, 0))],
        out_specs=pl.BlockSpec((sub * tm, D), lambda i: (i, 0)),
        out_shape=jax.ShapeDtypeStruct((T, D), F32),
        scratch_shapes=[pltpu.VMEM((2, n_win, WIN_ROWS, D), BF16), pltpu.SemaphoreType.DMA((2,))],
        compiler_params=pltpu.CompilerParams(dimension_semantics=("arbitrary",),
                                             vmem_limit_bytes=VMEM_LIMIT),
        name="combine",
    )(back3, back3, y.reshape(-1, WIN_ROWS, D), h, route, g.reshape(1, D))


def kernel(x, norm_mix_g, w_in, w_proj_sb, w_proj_moba, w_out, rel_bias, norm_ffn_g, w_router, b_router,
           w_gate_up, b_gate_up, w_down, b_down, norm_final_g):
    batch, seq, D = x.shape
    assert norm_mix_g.shape[0] == 1, "single-layer trunk"
    tm = 256
    xt = x.reshape(batch * seq, D)
    col_scale = np.ones((w_in.shape[-1],), np.float32)
    col_scale[0:D_MIX] = QUERY_SCALE
    col_scale[3 * D_MIX:4 * D_MIX] = QUERY_SCALE
    proj = _in_proj(xt, norm_mix_g[0], (w_in[0] * col_scale).astype(BF16))
    ysb = _sb_attention(proj, batch=batch, seq=seq)
    ymb = _moba_attention(proj, _moba_bias(rel_bias), rel_bias, batch=batch, seq=seq)
    wr_t = jnp.pad(w_router[0].T, ((0, LANES - N_EXPERTS), (0, 0)))
    br_t = b_router[0].reshape(N_EXPERTS, 1)
    h, xl, route, cnt = _post_attn(ysb, ymb, proj, xt, w_proj_sb[0].astype(BF16), w_proj_moba[0].astype(BF16),
                                   w_out[0].astype(BF16), norm_ffn_g[0], wr_t, br_t, tm=tm)
    src_win, back_win, tile_plan = _window_plan(cnt[:, 0, :N_EXPERTS].astype(jnp.int32), tm)
    y = _experts(xl, src_win, tile_plan, w_gate_up[0], b_gate_up[0], w_down[0], b_down[0])
    out = _combine(y, back_win, h, route, norm_final_g, tm=tm)
    return out.reshape(batch, seq, D)
```
